```python
import math
import jax
import jax.numpy as jnp
from jax import lax
import numpy as np

D_MODEL = 2048
BATCH = 8
SEQ = 4096
DEPTH = 1
DEC_BATCH = 8
DEC_SEQ = 16
PAST_LEN = 1024

CHUNK = 64
N_META = 16
Q_BLOCK = 128
GDN_HEADS = 8
GDN_HEAD_DIM = 128
GDN_WIDTH = GDN_HEADS * GDN_HEAD_DIM
CONV_WIDTH = 4
FOX_HEADS = 8
FOX_HEAD_DIM = 128
FOX_WIDTH = FOX_HEADS * FOX_HEAD_DIM
D_MIX = GDN_WIDTH + FOX_WIDTH
PROJ_SIZES = (3 * GDN_WIDTH, GDN_WIDTH, GDN_HEADS, GDN_HEADS, 3 * FOX_WIDTH, FOX_WIDTH, FOX_HEADS)
D_PROJ = 4 * GDN_WIDTH + 2 * GDN_HEADS + 4 * FOX_WIDTH + FOX_HEADS
N_GROUPS = 4
EXPERTS_PER_GROUP = 8
N_EXPERTS = N_GROUPS * EXPERTS_PER_GROUP
TOP_K = 2
D_EXPERT = D_MODEL // 8
F_BIAS_INIT = 3.0
EPS = 1e-6
F32 = jnp.float32

kernel_name = 'hymba_gdn_fox_hiermoe_stream_step'


def rmsnorm(x, w):
    xf = x.astype(F32)
    xf = xf * lax.rsqrt(jnp.mean(xf * xf, axis=-1, keepdims=True) + EPS)
    return (xf * w.astype(F32)).astype(x.dtype)


def l2norm(x):
    xf = x.astype(F32)
    return xf * lax.rsqrt(jnp.sum(xf * xf, axis=-1, keepdims=True) + EPS)


def split_projection(p):
    offs = [int(o) for o in np.cumsum(PROJ_SIZES)[:-1]]
    return jnp.split(p, offs, axis=-1)


def causal_conv(x, buf, w):
    L = x.shape[1]
    xp = jnp.concatenate([buf.astype(x.dtype), x], axis=1)
    out = w[0] * xp[:, 0:L]
    for i in range(1, CONV_WIDTH):
        out = out + w[i] * xp[:, i:i + L]
    return jax.nn.silu(out), xp[:, L:]


def gdn_features(qkv, a, b, A_log, dt_bias):
    B, L, _ = qkv.shape
    q, k, v = jnp.split(qkv.astype(F32), 3, axis=-1)
    shp = (B, L, GDN_HEADS, GDN_HEAD_DIM)
    q = l2norm(q.reshape(shp)) * (GDN_HEAD_DIM ** -0.5)
    k = l2norm(k.reshape(shp))
    v = v.reshape(shp)
    beta = jax.nn.sigmoid(b.astype(F32))
    g = -jnp.exp(A_log.astype(F32)) * jax.nn.softplus(a.astype(F32) + dt_bias.astype(F32))
    return q, k, v, beta, g


def gdn_chunk(S, q, k, v, beta, g):
    L = q.shape[1]
    Gl = jnp.cumsum(g, axis=1)
    Gh = jnp.transpose(Gl, (0, 2, 1))
    incl = jnp.tril(jnp.ones((L, L), dtype=bool))
    strict = jnp.tril(jnp.ones((L, L), dtype=bool), k=-1)
    diff = Gh[..., :, None] - Gh[..., None, :]
    decay = jnp.where(incl, jnp.exp(jnp.where(incl, diff, 0.0)), 0.0)
    kb = k * beta[..., None]
    a_mat = jnp.where(strict, jnp.einsum('blhd,bshd->bhls', kb, k) * decay, 0.0)
    eye = jnp.eye(L, dtype=F32)
    t_mat = lax.linalg.triangular_solve(a_mat + eye, jnp.broadcast_to(eye, a_mat.shape),
                                        left_side=True, lower=True, unit_diagonal=True)
    eg = jnp.exp(Gl)[..., None]
    w_mat = jnp.einsum('bhls,bshd->bhld', t_mat, kb * eg)
    u = jnp.einsum('bhls,bshd->bhld', t_mat, v * beta[..., None]) - jnp.einsum('bhld,bhde->bhle', w_mat, S)
    qk = jnp.einsum('blhd,bshd->bhls', q, k) * decay
    o = jnp.einsum('blhd,bhde->bhle', q * eg, S) + jnp.einsum('bhls,bhse->bhle', qk, u)
    g_last = Gh[..., -1]
    k_dec = k * jnp.exp(g_last[:, None, :] - Gl)[..., None]
    S_new = S * jnp.exp(g_last)[..., None, None] + jnp.einsum('blhd,bhle->bhde', k_dec, u)
    return S_new, jnp.transpose(o, (0, 2, 1, 3))


def gdn_prompt(q, k, v, beta, g):
    B, Lp = q.shape[:2]
    pad = CHUNK - N_META
    n_chunks = (Lp + pad) // CHUNK

    def chunks(t):
        t = jnp.pad(t, [(0, 0), (pad, 0)] + [(0, 0)] * (t.ndim - 2))
        return jnp.moveaxis(t.reshape((B, n_chunks, CHUNK) + t.shape[2:]), 1, 0)

    S0 = jnp.zeros((B, GDN_HEADS, GDN_HEAD_DIM, GDN_HEAD_DIM), F32)
    S, o = lax.scan(lambda S_, xs: gdn_chunk(S_, *xs), S0,
                    (chunks(q), chunks(k), chunks(v), chunks(beta), chunks(g)))
    o = jnp.moveaxis(o, 0, 1).reshape(B, n_chunks * CHUNK, GDN_HEADS, GDN_HEAD_DIM)[:, pad:]
    return o, S


def gdn_output(o, z, norm_w):
    B, L = o.shape[:2]
    z = z.reshape(B, L, GDN_HEADS, GDN_HEAD_DIM).astype(F32)
    return (rmsnorm(o, norm_w) * jax.nn.silu(z)).reshape(B, L, GDN_WIDTH)


def fox_features(qkv, f, q_norm_w, k_norm_w, f_bias):
    B, L, _ = qkv.shape
    q, k, v = jnp.split(qkv, 3, axis=-1)
    shp = (B, L, FOX_HEADS, FOX_HEAD_DIM)
    q = rmsnorm(q.reshape(shp), q_norm_w)
    k = rmsnorm(k.reshape(shp), k_norm_w)
    v = v.reshape(shp)
    logf = jax.nn.log_sigmoid(f.astype(F32) + f_bias.astype(F32))
    return q, k, v, logf


def fox_attend(q, Fq, qpos, k, v, Fk, kpos):
    s = jnp.einsum('bqhd,bkhd->bhqk', q, k, preferred_element_type=F32) * (FOX_HEAD_DIM ** -0.5)
    s = s + jnp.transpose(Fq, (0, 2, 1))[..., :, None] - jnp.transpose(Fk, (0, 2, 1))[..., None, :]
    s = jnp.where(kpos[None, :] <= qpos[:, None], s, -jnp.inf)
    p = jax.nn.softmax(s, axis=-1)
    return jnp.einsum('bhqk,bkhd->bqhd', p.astype(v.dtype), v)


def fox_prompt(q, k, v, logf):
    B, Lp = q.shape[:2]
    Lpad = -(-Lp // Q_BLOCK) * Q_BLOCK
    pad = Lpad - Lp
    F = jnp.cumsum(logf, axis=1)

    def padr(t):
        return jnp.pad(t, [(0, 0), (0, pad)] + [(0, 0)] * (t.ndim - 2))

    qp, kp, vp, Fp = padr(q), padr(k), padr(v), padr(F)
    pos = jnp.arange(Lpad)
    n_blocks = Lpad // Q_BLOCK
    qb = jnp.moveaxis(qp.reshape(B, n_blocks, Q_BLOCK, FOX_HEADS, FOX_HEAD_DIM), 1, 0)
    Fb = jnp.moveaxis(Fp.reshape(B, n_blocks, Q_BLOCK, FOX_HEADS), 1, 0)
    pb = pos.reshape(n_blocks, Q_BLOCK)
    o = lax.map(lambda xs: fox_attend(xs[0], xs[1], xs[2], kp, vp, Fp, pos), (qb, Fb, pb))
    return jnp.moveaxis(o, 0, 1).reshape(B, Lpad, FOX_HEADS, FOX_HEAD_DIM)[:, :Lp]


def fox_sample(q, k, v, logf, cache_k, cache_v, cache_logf):
    P, L = cache_k.shape[1], q.shape[1]
    k_all = jnp.concatenate([cache_k.astype(k.dtype), k], axis=1)
    v_all = jnp.concatenate([cache_v.astype(v.dtype), v], axis=1)
    F_all = jnp.cumsum(jnp.concatenate([cache_logf.astype(F32), logf], axis=1), axis=1)
    kpos = jnp.arange(P + L)
    qpos = P + jnp.arange(L)
    return fox_attend(q, F_all[:, P:], qpos, k_all, v_all, F_all, kpos)


def mixer(xn, lw, state):
    w_in, conv_w, A_log, dt_bias, gdn_norm_w, q_norm_w, k_norm_w, f_bias, w_out = lw
    B, L = xn.shape[:2]
    g_qkv, g_z, g_a, g_b, f_qkv, f_og, f_f = split_projection(xn @ w_in)
    if state is None:
        conv_buf = jnp.zeros((B, CONV_WIDTH - 1, 3 * GDN_WIDTH), xn.dtype)
    else:
        cache_k, cache_v, cache_logf, gdn_s, conv_buf = state
    g_qkv_c, new_conv = causal_conv(g_qkv, conv_buf, conv_w)
    q, k, v, beta, g = gdn_features(g_qkv_c, g_a, g_b, A_log, dt_bias)
    fq, fk, fv, logf = fox_features(f_qkv, f_f, q_norm_w, k_norm_w, f_bias)
    if state is None:
        o_g, new_s = gdn_prompt(q, k, v, beta, g)
        o_f = fox_prompt(fq, fk, fv, logf)
    else:
        new_s, o_g = gdn_chunk(gdn_s.astype(F32), q, k, v, beta, g)
        o_f = fox_sample(fq, fk, fv, logf, cache_k, cache_v, cache_logf)
    o_g = gdn_output(o_g, g_z, gdn_norm_w).astype(xn.dtype)
    o_f = (o_f.reshape(B, L, FOX_WIDTH) * jax.nn.sigmoid(f_og)).astype(xn.dtype)
    out = jnp.concatenate([o_g, o_f], axis=-1) @ w_out
    return out, (fk, fv, logf, new_s, new_conv)


def hier_moe(x, w_rg, b_rg, w_re, b_re, w1, w3, w2):
    B, L, D = x.shape
    xt = x.reshape(B * L, D)
    gl = (xt @ w_rg).astype(F32) + b_rg.astype(F32)
    pg = jax.nn.softmax(gl, axis=-1)
    g_onehot = jax.nn.one_hot(jnp.argmax(gl, axis=-1), N_GROUPS, dtype=F32)
    p_top = jnp.sum(pg * g_onehot, axis=-1)
    el = ((xt @ w_re).astype(F32) + b_re.astype(F32)).reshape(-1, N_GROUPS, EXPERTS_PER_GROUP)
    el_sel = jnp.einsum('nge,ng->ne', el, g_onehot)
    top_v, top_i = lax.top_k(el_sel, TOP_K)
    top_w = jax.nn.softmax(top_v, axis=-1) * p_top[:, None]
    e_w = jnp.einsum('nk,nke->ne', top_w, jax.nn.one_hot(top_i, EXPERTS_PER_GROUP, dtype=F32))
    combine = (g_onehot[:, :, None] * e_w[:, None, :]).reshape(-1, N_EXPERTS).astype(x.dtype)
    y = jnp.zeros_like(xt)
    for e in range(N_EXPERTS):
        h = jax.nn.silu(xt @ w1[e]) * (xt @ w3[e])
        y = y + combine[:, e:e + 1] * (h @ w2[e])
    return y.reshape(B, L, D)


def block(x, norm_mix, lw, norm_ffn, moe_w, state):
    h, new_state = mixer(rmsnorm(x, norm_mix), lw, state)
    x = x + h
    x = x + hier_moe(rmsnorm(x, norm_ffn), *moe_w)
    return x, new_state


def stack_layers(states, i):
    return jnp.stack([st[i] for st in states], axis=0)


def setup_inputs(seed: int = 0) -> dict:
    key = jax.random.key(seed)
    ks = jax.random.split(key, 32)

    def nrm(k, shape, scale=1.0):
        return scale * jax.random.normal(k, shape, F32)

    dt = jnp.exp(jax.random.uniform(ks[12], (DEPTH, GDN_HEADS), F32, math.log(1e-3), math.log(1e-1)))
    return {
        'x_prompt': nrm(ks[0], (BATCH, SEQ, D_MODEL)),
        'x_sample': nrm(ks[1], (DEC_BATCH, DEC_SEQ, D_MODEL)),
        'cache_fox_k': nrm(ks[2], (DEPTH, DEC_BATCH, PAST_LEN, FOX_HEADS, FOX_HEAD_DIM)),
        'cache_fox_v': nrm(ks[3], (DEPTH, DEC_BATCH, PAST_LEN, FOX_HEADS, FOX_HEAD_DIM)),
        'cache_fox_logf': jax.nn.log_sigmoid(F_BIAS_INIT + nrm(ks[4], (DEPTH, DEC_BATCH, PAST_LEN, FOX_HEADS))),
        'state_gdn': nrm(ks[5], (DEPTH, DEC_BATCH, GDN_HEADS, GDN_HEAD_DIM, GDN_HEAD_DIM), GDN_HEAD_DIM ** -0.5),
        'state_gdn_conv': nrm(ks[6], (DEPTH, DEC_BATCH, CONV_WIDTH - 1, 3 * GDN_WIDTH)),
        'meta_tokens': nrm(ks[7], (N_META, D_MODEL)),
        'norm_mix_w': 1.0 + nrm(ks[8], (DEPTH, D_MODEL), 0.02),
        'w_in': nrm(ks[9], (DEPTH, D_MODEL, D_PROJ), D_MODEL ** -0.5),
        'gdn_conv_w': nrm(ks[10], (DEPTH, CONV_WIDTH, 3 * GDN_WIDTH), CONV_WIDTH ** -0.5),
        'gdn_A_log': jnp.log(jax.random.uniform(ks[11], (DEPTH, GDN_HEADS), F32, 1.0, 16.0)),
        'gdn_dt_bias': dt + jnp.log(-jnp.expm1(-dt)),
        'gdn_norm_w': 1.0 + nrm(ks[13], (DEPTH, GDN_HEAD_DIM), 0.02),
        'fox_q_norm_w': 1.0 + nrm(ks[14], (DEPTH, FOX_HEAD_DIM), 0.02),
        'fox_k_norm_w': 1.0 + nrm(ks[15], (DEPTH, FOX_HEAD_DIM), 0.02),
        'fox_f_bias': F_BIAS_INIT + nrm(ks[16], (DEPTH, FOX_HEADS), 0.1),
        'w_out': nrm(ks[17], (DEPTH, D_MIX, D_MODEL), D_MIX ** -0.5),
        'norm_ffn_w': 1.0 + nrm(ks[18], (DEPTH, D_MODEL), 0.02),
        'w_router_group': nrm(ks[19], (DEPTH, D_MODEL, N_GROUPS), D_MODEL ** -0.5),
        'b_router_group': nrm(ks[20], (DEPTH, N_GROUPS), 0.01),
        'w_router_expert': nrm(ks[21], (DEPTH, D_MODEL, N_EXPERTS), D_MODEL ** -0.5),
        'b_router_expert': nrm(ks[22], (DEPTH, N_EXPERTS), 0.01),
        'w_gate': nrm(ks[23], (DEPTH, N_EXPERTS, D_MODEL, D_EXPERT), D_MODEL ** -0.5),
        'w_up': nrm(ks[24], (DEPTH, N_EXPERTS, D_MODEL, D_EXPERT), D_MODEL ** -0.5),
        'w_down': nrm(ks[25], (DEPTH, N_EXPERTS, D_EXPERT, D_MODEL), D_EXPERT ** -0.5),
        'norm_final_w': 1.0 + nrm(ks[26], (D_MODEL,), 0.02),
    }


def reference(x_prompt, x_sample, cache_fox_k, cache_fox_v, cache_fox_logf, state_gdn, state_gdn_conv,
              meta_tokens, norm_mix_w, w_in, gdn_conv_w, gdn_A_log, gdn_dt_bias, gdn_norm_w,
              fox_q_norm_w, fox_k_norm_w, fox_f_bias, w_out, norm_ffn_w, w_router_group, b_router_group,
              w_router_expert, b_router_expert, w_gate, w_up, w_down, norm_final_w):
    B = x_prompt.shape[0]
    meta = jnp.broadcast_to(meta_tokens.astype(x_prompt.dtype)[None], (B, N_META, D_MODEL))
    xp = jnp.concatenate([meta, x_prompt], axis=1)
    xs = x_sample
    p_states = []
    s_states = []
    for l in range(DEPTH):
        lw = (w_in[l], gdn_conv_w[l], gdn_A_log[l], gdn_dt_bias[l], gdn_norm_w[l],
              fox_q_norm_w[l], fox_k_norm_w[l], fox_f_bias[l], w_out[l])
        moe_w = (w_router_group[l], b_router_group[l], w_router_expert[l], b_router_expert[l],
                 w_gate[l], w_up[l], w_down[l])
        xp, st_p = block(xp, norm_mix_w[l], lw, norm_ffn_w[l], moe_w, None)
        st_in = (cache_fox_k[l], cache_fox_v[l], cache_fox_logf[l], state_gdn[l], state_gdn_conv[l])
        xs, st_s = block(xs, norm_mix_w[l], lw, norm_ffn_w[l], moe_w, st_in)
        p_states.append(st_p)
        s_states.append(st_s)
    y_prompt = rmsnorm(xp, norm_final_w)[:, N_META:]
    y_sample = rmsnorm(xs, norm_final_w)
    return (y_prompt, y_sample,
            stack_layers(p_states, 0), stack_layers(p_states, 1), stack_layers(p_states, 2),
            stack_layers(p_states, 3), stack_layers(p_states, 4),
            stack_layers(s_states, 0), stack_layers(s_states, 1), stack_layers(s_states, 2),
            stack_layers(s_states, 3), stack_layers(s_states, 4))
```

```python
import functools

import jax
import jax.numpy as jnp
from jax import lax
from jax.experimental import pallas as pl
from jax.experimental.pallas import tpu as pltpu

F32 = jnp.float32
BF16 = jnp.bfloat16
EPS = 1e-6
NEG = -1e30

N_META = 16
HEADS = 8
HEAD_DIM = 128
WIDTH = HEADS * HEAD_DIM
CONV_WIDTH = 4
N_GROUPS = 4
EXPERTS_PER_GROUP = 8
N_EXPERTS = N_GROUPS * EXPERTS_PER_GROUP
GDN_CHUNK = 128
LANES = 128
SM_F, SM_A, SM_B = 0, 8, 16
PB_FQ, PB_FK, PB_FV, PB_GZ, PB_FOG = 3, 4, 5, 6, 7
RT_E0 = N_GROUPS

VMEM_LIMIT = 56 * 1024 * 1024


def _cparams(sem):
    return pltpu.CompilerParams(dimension_semantics=sem, vmem_limit_bytes=VMEM_LIMIT)


def _pick(n, prefs):
    for p in prefs:
        if n % p == 0:
            return p
    raise ValueError(f"no tile in {prefs} divides {n}")


def _dot(a, b):
    return jnp.dot(a, b, preferred_element_type=F32)


def _dot_nt(a, b):
    return lax.dot_general(a, b, (((1,), (1,)), ((), ())), preferred_element_type=F32)


def _dot_tn(a, b):
    return lax.dot_general(a, b, (((0,), (0,)), ((), ())), preferred_element_type=F32)


def _split3(x):
    x1 = x.astype(BF16)
    r1 = x - x1.astype(F32)
    x2 = r1.astype(BF16)
    x3 = (r1 - x2.astype(F32)).astype(BF16)
    return x1, x2, x3


def _mask_dot(mask_bf16, x):
    x1, x2, x3 = _split3(x)
    return _dot(mask_bf16, x1) + _dot(mask_bf16, x2) + _dot(mask_bf16, x3)


def _dot_mask(x, mask_bf16):
    x1, x2, x3 = _split3(x)
    return _dot(x1, mask_bf16) + _dot(x2, mask_bf16) + _dot(x3, mask_bf16)


def _softplus(x):
    return jnp.maximum(x, 0.0) + jnp.log1p(jnp.exp(-jnp.abs(x)))


def _sigmoid(x):
    return 1.0 / (1.0 + jnp.exp(-x))


def _silu(x):
    return x * _sigmoid(x)


def _iota2(shape, dim):
    return lax.broadcasted_iota(jnp.int32, shape, dim)


def _inproj_kernel(x_ref, nw_ref, wbig_ref, wsm_ref, wsmt_ref, p_ref, psm_ref, psmt_ref, xn_scr):
    @pl.when(pl.program_id(1) == 0)
    def _():
        x = x_ref[...]
        xn = x * lax.rsqrt(jnp.mean(x * x, axis=-1, keepdims=True) + EPS) * nw_ref[...]
        xnb = xn.astype(BF16)
        xn_scr[...] = xnb
        psm_ref[...] = _dot(xnb, wsm_ref[...])
        psmt_ref[...] = _dot_nt(wsmt_ref[...], xnb)

    p_ref[...] = _dot(xn_scr[...], wbig_ref[...]).astype(BF16)


def _inproj(x, norm_w, wbig, wsm, wsmt):
    n, d = x.shape
    tm = _pick(n, (1024, 1152, 512, 384, 256, 128))
    tn = 1024
    nproj = wbig.shape[1]
    return pl.pallas_call(
        _inproj_kernel,
        grid=(n // tm, nproj // tn),
        in_specs=[
            pl.BlockSpec((tm, d), lambda i, j: (i, 0)),
            pl.BlockSpec((1, d), lambda i, j: (0, 0)),
            pl.BlockSpec((d, tn), lambda i, j: (0, j)),
            pl.BlockSpec((d, LANES), lambda i, j: (0, 0)),
            pl.BlockSpec((32, d), lambda i, j: (0, 0)),
        ],
        out_specs=[
            pl.BlockSpec((tm, tn), lambda i, j: (i, j)),
            pl.BlockSpec((tm, LANES), lambda i, j: (i, 0)),
            pl.BlockSpec((32, tm), lambda i, j: (0, i)),
        ],
        out_shape=[
            jax.ShapeDtypeStruct((n, nproj), BF16),
            jax.ShapeDtypeStruct((n, LANES), F32),
            jax.ShapeDtypeStruct((32, n), F32),
        ],
        scratch_shapes=[pltpu.VMEM((tm, d), BF16)],
        compiler_params=_cparams(("arbitrary", "arbitrary")),
        name="inproj",
    )(x, norm_w, wbig, wsm, wsmt)


def _gdn_kernel(qkv_ref, z_ref, sm_ref, smt_ref, s0_ref, cb_ref, convw_ref, alog_c_ref, dtb_c_ref,
                alog_r_ref, dtb_r_ref, normw_ref, o_ref, snew_ref, cnew_ref, s_scr, xp_scr, *, l_valid):
    C = GDN_CHUNK

    @pl.when(pl.program_id(1) == 0)
    def _():
        s_scr[...] = s0_ref[0]
        xp_scr[5:8, :] = cb_ref[0]

    x = qkv_ref[...].astype(F32)
    xp_scr[8:8 + C, :] = x
    w = convw_ref[...]
    conv = (w[0:1] * xp_scr[5:5 + C, :] + w[1:2] * xp_scr[6:6 + C, :]
            + w[2:3] * xp_scr[7:7 + C, :] + w[3:4] * x)
    act = _silu(conv)
    tail = xp_scr[8 + l_valid - 3:8 + l_valid, :]
    xp_scr[5:8, :] = tail
    cnew_ref[0] = tail

    row = _iota2((C, C), 0)
    col = _iota2((C, C), 1)
    incl = row >= col
    strict = row > col
    tril_b = jnp.where(incl, 1.0, 0.0).astype(BF16)
    triu_b = jnp.where(row <= col, 1.0, 0.0).astype(BF16)

    sm = sm_ref[...]
    g_col = -jnp.exp(alog_c_ref[...]) * _softplus(sm + dtb_c_ref[...])
    g_row = -jnp.exp(alog_r_ref[...]) * _softplus(smt_ref[...] + dtb_r_ref[...])
    beta_col = _sigmoid(sm)
    if l_valid < C:
        g_col = jnp.where(_iota2((C, LANES), 0) < l_valid, g_col, 0.0)
        beta_col = jnp.where(_iota2((C, LANES), 0) < l_valid, beta_col, 0.0)
        g_row = jnp.where(_iota2((HEADS, C), 1) < l_valid, g_row, 0.0)
    gl_col = _mask_dot(tril_b, g_col)
    gl_row = _dot_mask(g_row, triu_b)

    inv_levels = []
    sh = 1
    while (1 << sh) < C:
        m = (((row >> sh) & 1) == 1) & ((col >> (sh + 1)) == (row >> (sh + 1))) & (((col >> sh) & 1) == 0)
        inv_levels.append(m)
        sh += 1
    m1 = ((row & 1) == 1) & (col == row - 1)
    eye = jnp.where(row == col, 1.0, 0.0)

    normw = normw_ref[...]
    for h in range(HEADS):
        q = act[:, h * HEAD_DIM:(h + 1) * HEAD_DIM]
        k = act[:, WIDTH + h * HEAD_DIM:WIDTH + (h + 1) * HEAD_DIM]
        v = act[:, 2 * WIDTH + h * HEAD_DIM:2 * WIDTH + (h + 1) * HEAD_DIM]
        qn = q * lax.rsqrt(jnp.sum(q * q, axis=-1, keepdims=True) + EPS) * (HEAD_DIM ** -0.5)
        kn = k * lax.rsqrt(jnp.sum(k * k, axis=-1, keepdims=True) + EPS)
        beta = beta_col[:, SM_B + h:SM_B + h + 1]
        gc = gl_col[:, SM_A + h:SM_A + h + 1]
        gr = gl_row[h:h + 1, :]
        decay = jnp.exp(jnp.where(incl, gc - gr, NEG))
        kb = kn * beta
        knb = kn.astype(BF16)
        a_mat = jnp.where(strict, _dot_nt(kb.astype(BF16), knb) * decay, 0.0)
        qk = _dot_nt(qn.astype(BF16), knb) * decay

        t = eye - jnp.where(m1, a_mat, 0.0)
        for m in inv_levels:
            tb = t.astype(BF16)
            y = _dot(tb, jnp.where(m, a_mat, 0.0).astype(BF16))
            t = t - _dot(y.astype(BF16), tb)

        eg = jnp.exp(gc)
        s = s_scr[h]
        sb = s.astype(BF16)
        r = v * beta - _dot((kb * eg).astype(BF16), sb)
        u = _dot(t.astype(BF16), r.astype(BF16))
        ub = u.astype(BF16)
        o = _dot((qn * eg).astype(BF16), sb) + _dot(qk.astype(BF16), ub)
        g_last = gc[C - 1:C, :]
        k_dec = kn * jnp.exp(g_last - gc)
        s_scr[h] = s * jnp.exp(g_last) + _dot_tn(k_dec.astype(BF16), ub)

        on = o * lax.rsqrt(jnp.mean(o * o, axis=-1, keepdims=True) + EPS) * normw
        z = z_ref[:, h * HEAD_DIM:(h + 1) * HEAD_DIM].astype(F32)
        o_ref[:, h * HEAD_DIM:(h + 1) * HEAD_DIM] = (on * _silu(z)).astype(BF16)

    snew_ref[0] = s_scr[...]


def _gdn(p, psm, psmt, s0, cb, conv_w, alog, dtb, norm_w, *, n_seq, l_valid, bcast_state):
    n = p.shape[0]
    C = GDN_CHUNK
    nc = n // (n_seq * C)
    alog_c = jnp.zeros((1, LANES), F32).at[0, SM_A:SM_A + HEADS].set(alog)
    dtb_c = jnp.zeros((1, LANES), F32).at[0, SM_A:SM_A + HEADS].set(dtb)
    st_idx = (lambda b, c: (0, 0, 0, 0)) if bcast_state else (lambda b, c: (b, 0, 0, 0))
    cb_idx = (lambda b, c: (0, 0, 0)) if bcast_state else (lambda b, c: (b, 0, 0))
    return pl.pallas_call(
        functools.partial(_gdn_kernel, l_valid=l_valid),
        grid=(n_seq, nc),
        in_specs=[
            pl.BlockSpec((C, 3 * WIDTH), lambda b, c: (b * nc + c, 0)),
            pl.BlockSpec((C, WIDTH), lambda b, c: (b * nc + c, PB_GZ)),
            pl.BlockSpec((C, LANES), lambda b, c: (b * nc + c, 0)),
            pl.BlockSpec((HEADS, C), lambda b, c: (SM_A // HEADS, b * nc + c)),
            pl.BlockSpec((1, HEADS, HEAD_DIM, HEAD_DIM), st_idx),
            pl.BlockSpec((1, CONV_WIDTH - 1, 3 * WIDTH), cb_idx),
            pl.BlockSpec((CONV_WIDTH, 3 * WIDTH), lambda b, c: (0, 0)),
            pl.BlockSpec((1, LANES), lambda b, c: (0, 0)),
            pl.BlockSpec((1, LANES), lambda b, c: (0, 0)),
            pl.BlockSpec((HEADS, 1), lambda b, c: (0, 0)),
            pl.BlockSpec((HEADS, 1), lambda b, c: (0, 0)),
            pl.BlockSpec((1, HEAD_DIM), lambda b, c: (0, 0)),
        ],
        out_specs=[
            pl.BlockSpec((C, WIDTH), lambda b, c: (b * nc + c, 0)),
            pl.BlockSpec((1, HEADS, HEAD_DIM, HEAD_DIM), lambda b, c: (b, 0, 0, 0)),
            pl.BlockSpec((1, CONV_WIDTH - 1, 3 * WIDTH), lambda b, c: (b, 0, 0)),
        ],
        out_shape=[
            jax.ShapeDtypeStruct((n, WIDTH), BF16),
            jax.ShapeDtypeStruct((n_seq, HEADS, HEAD_DIM, HEAD_DIM), F32),
            jax.ShapeDtypeStruct((n_seq, CONV_WIDTH - 1, 3 * WIDTH), F32),
        ],
        scratch_shapes=[
            pltpu.VMEM((HEADS, HEAD_DIM, HEAD_DIM), F32),
            pltpu.VMEM((C + 8, 3 * WIDTH), F32),
        ],
        compiler_params=_cparams(("arbitrary", "arbitrary")),
        name="gdn",
    )(p, p, psm, psmt, s0, cb, conv_w, alog_c, dtb_c, alog.reshape(HEADS, 1), dtb.reshape(HEADS, 1),
      norm_w.reshape(1, HEAD_DIM))


def _foxprep_kernel(qkv_ref, sm_ref, smt_ref, c0c_ref, c0r_ref, qw_ref, kw_ref, fb_c_ref, fb_r_ref,
                    q_ref, k32_ref, kb_ref, v32_ref, lfc_ref, lfr_ref, fc_ref, fr_ref, cc_scr, cr_scr):
    tm = qkv_ref.shape[0]

    @pl.when(pl.program_id(1) == 0)
    def _():
        cc_scr[...] = c0c_ref[0]
        cr_scr[...] = c0r_ref[0]

    qw = qw_ref[...]
    kw = kw_ref[...]
    for h in range(HEADS):
        q = qkv_ref[:, h * HEAD_DIM:(h + 1) * HEAD_DIM].astype(F32)
        k = qkv_ref[:, WIDTH + h * HEAD_DIM:WIDTH + (h + 1) * HEAD_DIM].astype(F32)
        qn = q * lax.rsqrt(jnp.mean(q * q, axis=-1, keepdims=True) + EPS) * qw
        kn = k * lax.rsqrt(jnp.mean(k * k, axis=-1, keepdims=True) + EPS) * kw
        q_ref[:, h * HEAD_DIM:(h + 1) * HEAD_DIM] = (qn * (HEAD_DIM ** -0.5)).astype(BF16)
        k32_ref[:, h * HEAD_DIM:(h + 1) * HEAD_DIM] = kn
        kb_ref[:, h * HEAD_DIM:(h + 1) * HEAD_DIM] = kn.astype(BF16)
    v32_ref[...] = qkv_ref[:, 2 * WIDTH:3 * WIDTH].astype(F32)

    lf_col = -_softplus(-(sm_ref[...] + fb_c_ref[...]))
    lf_row = -_softplus(-(smt_ref[...] + fb_r_ref[...]))
    lfc_ref[...] = lf_col
    lfr_ref[...] = lf_row
    row = _iota2((tm, tm), 0)
    col = _iota2((tm, tm), 1)
    tril_b = jnp.where(row >= col, 1.0, 0.0).astype(BF16)
    triu_b = jnp.where(row <= col, 1.0, 0.0).astype(BF16)
    f_col = cc_scr[...] + _mask_dot(tril_b, lf_col)
    f_row = cr_scr[...] + _dot_mask(lf_row, triu_b)
    fc_ref[...] = f_col
    fr_ref[...] = f_row
    cc_scr[...] = f_col[tm - 1:tm, :]
    cr_scr[...] = f_row[:, tm - 1:tm]


def _foxprep(p, psm, psmt, c0c, c0r, q_norm_w, k_norm_w, f_bias, *, n_seq, bcast_carry):
    n = p.shape[0]
    rows = n // n_seq
    tm = _pick(rows, (512, 256, 128))
    nt = rows // tm
    fb_c = jnp.zeros((1, LANES), F32).at[0, SM_F:SM_F + HEADS].set(f_bias)
    c_idx = (lambda b, t: (0, 0, 0)) if bcast_carry else (lambda b, t: (b, 0, 0))
    rowblk = lambda b, t: (b * nt + t, 0)
    return pl.pallas_call(
        _foxprep_kernel,
        grid=(n_seq, nt),
        in_specs=[
            pl.BlockSpec((tm, 3 * WIDTH), lambda b, t: (b * nt + t, 1)),
            pl.BlockSpec((tm, LANES), rowblk),
            pl.BlockSpec((HEADS, tm), lambda b, t: (SM_F // HEADS, b * nt + t)),
            pl.BlockSpec((1, 1, LANES), c_idx),
            pl.BlockSpec((1, HEADS, 1), c_idx),
            pl.BlockSpec((1, HEAD_DIM), lambda b, t: (0, 0)),
            pl.BlockSpec((1, HEAD_DIM), lambda b, t: (0, 0)),
            pl.BlockSpec((1, LANES), lambda b, t: (0, 0)),
            pl.BlockSpec((HEADS, 1), lambda b, t: (0, 0)),
        ],
        out_specs=[
            pl.BlockSpec((tm, WIDTH), rowblk),
            pl.BlockSpec((tm, WIDTH), rowblk),
            pl.BlockSpec((tm, WIDTH), rowblk),
            pl.BlockSpec((tm, WIDTH), rowblk),
            pl.BlockSpec((tm, LANES), rowblk),
            pl.BlockSpec((HEADS, tm), lambda b, t: (0, b * nt + t)),
            pl.BlockSpec((tm, LANES), rowblk),
            pl.BlockSpec((HEADS, tm), lambda b, t: (0, b * nt + t)),
        ],
        out_shape=[
            jax.ShapeDtypeStruct((n, WIDTH), BF16),
            jax.ShapeDtypeStruct((n, WIDTH), F32),
            jax.ShapeDtypeStruct((n, WIDTH), BF16),
            jax.ShapeDtypeStruct((n, WIDTH), F32),
            jax.ShapeDtypeStruct((n, LANES), F32),
            jax.ShapeDtypeStruct((HEADS, n), F32),
            jax.ShapeDtypeStruct((n, LANES), F32),
            jax.ShapeDtypeStruct((HEADS, n), F32),
        ],
        scratch_shapes=[pltpu.VMEM((1, LANES), F32), pltpu.VMEM((HEADS, 1), F32)],
        compiler_params=_cparams(("arbitrary", "arbitrary")),
        name="foxprep",
    )(p, psm, psmt, c0c, c0r, q_norm_w.reshape(1, HEAD_DIM), k_norm_w.reshape(1, HEAD_DIM), fb_c,
      f_bias.reshape(HEADS, 1))


def _fox_kernel(qi_ref, ki_ref, q_ref, k_ref, v_ref, fc_ref, fr_ref, og_ref, km_ref, vm_ref, frm_ref,
                o_ref, m_scr, l_scr, acc_scr):
    pair = pl.program_id(1)
    qi = qi_ref[pair]
    ki = ki_ref[pair]
    tq = q_ref.shape[0]
    tk = k_ref.shape[0]
    tmeta = km_ref.shape[0]

    def step(h, kb, vb, f_row, mask):
        q = q_ref[:, h * HEAD_DIM:(h + 1) * HEAD_DIM]
        s = _dot_nt(q, kb) + (fc_ref[:, SM_F + h:SM_F + h + 1] - f_row)
        s = jnp.where(mask, s, NEG)
        m_prev = m_scr[h]
        m_new = jnp.maximum(m_prev, jnp.max(s, axis=-1, keepdims=True))
        alpha = jnp.exp(m_prev - m_new)
        p = jnp.exp(s - m_new)
        l_scr[h] = alpha * l_scr[h] + jnp.sum(p, axis=-1, keepdims=True)
        acc_scr[h] = alpha * acc_scr[h] + _dot(p.astype(BF16), vb)
        m_scr[h] = m_new

    @pl.when(ki == 0)
    def _():
        mask = _iota2((tq, tmeta), 1) < N_META
        for h in range(HEADS):
            m_scr[h] = jnp.full((tq, 1), NEG, F32)
            l_scr[h] = jnp.zeros((tq, 1), F32)
            acc_scr[h] = jnp.zeros((tq, HEAD_DIM), F32)
            step(h, km_ref[:, h * HEAD_DIM:(h + 1) * HEAD_DIM], vm_ref[:, h * HEAD_DIM:(h + 1) * HEAD_DIM],
                 frm_ref[h:h + 1, :], mask)

    mask = (_iota2((tq, tk), 1) + ki * tk) <= (_iota2((tq, tk), 0) + qi * tq)
    for h in range(HEADS):
        step(h, k_ref[:, h * HEAD_DIM:(h + 1) * HEAD_DIM], v_ref[:, h * HEAD_DIM:(h + 1) * HEAD_DIM],
             fr_ref[h:h + 1, :], mask)

    @pl.when(ki == qi)
    def _():
        for h in range(HEADS):
            gate = _sigmoid(og_ref[:, h * HEAD_DIM:(h + 1) * HEAD_DIM].astype(F32))
            o_ref[:, h * HEAD_DIM:(h + 1) * HEAD_DIM] = (acc_scr[h] / l_scr[h] * gate).astype(BF16)


def _fox_prompt(q, kb, p, f_col, f_row, k_meta, p_small, f_row_meta, *, n_seq):
    n = q.shape[0]
    rows = n // n_seq
    tq = _pick(rows, (512, 256, 128))
    nq = rows // tq
    pairs = [(i, j) for i in range(nq) for j in range(i + 1)]
    qi = jnp.asarray([a for a, _ in pairs], jnp.int32)
    ki = jnp.asarray([b for _, b in pairs], jnp.int32)
    qblk = lambda b, t, qi_r, ki_r: (b * nq + qi_r[t], 0)
    grid_spec = pltpu.PrefetchScalarGridSpec(
        num_scalar_prefetch=2,
        grid=(n_seq, len(pairs)),
        in_specs=[
            pl.BlockSpec((tq, WIDTH), qblk),
            pl.BlockSpec((tq, WIDTH), lambda b, t, qi_r, ki_r: (b * nq + ki_r[t], 0)),
            pl.BlockSpec((tq, WIDTH), lambda b, t, qi_r, ki_r: (b * nq + ki_r[t], PB_FV)),
            pl.BlockSpec((tq, LANES), qblk),
            pl.BlockSpec((HEADS, tq), lambda b, t, qi_r, ki_r: (0, b * nq + ki_r[t])),
            pl.BlockSpec((tq, WIDTH), lambda b, t, qi_r, ki_r: (b * nq + qi_r[t], PB_FOG)),
            pl.BlockSpec((GDN_CHUNK, WIDTH), lambda b, t, qi_r, ki_r: (0, 0)),
            pl.BlockSpec((GDN_CHUNK, WIDTH), lambda b, t, qi_r, ki_r: (0, PB_FV)),
            pl.BlockSpec((HEADS, GDN_CHUNK), lambda b, t, qi_r, ki_r: (0, 0)),
        ],
        out_specs=pl.BlockSpec((tq, WIDTH), qblk),
        scratch_shapes=[
            pltpu.VMEM((HEADS, tq, 1), F32),
            pltpu.VMEM((HEADS, tq, 1), F32),
            pltpu.VMEM((HEADS, tq, HEAD_DIM), F32),
        ],
    )
    return pl.pallas_call(
        _fox_kernel,
        grid_spec=grid_spec,
        out_shape=jax.ShapeDtypeStruct((n, WIDTH), BF16),
        compiler_params=_cparams(("arbitrary", "arbitrary")),
        name="fox_prompt",
    )(qi, ki, q, kb, p, f_col, f_row, p, k_meta, p_small, f_row_meta)


def _fox_sample_kernel(q_ref, kn_ref, vn_ref, og_ref, lfc_ref, lfr_ref, ck_ref, cv_ref, clc_ref, clr_ref,
                       o_ref, *, l_valid):
    C = q_ref.shape[0]
    P = ck_ref.shape[1]
    rowp = _iota2((P, P), 0)
    colp = _iota2((P, P), 1)
    triu_p = jnp.where(rowp <= colp, 1.0, 0.0).astype(BF16)
    row = _iota2((C, C), 0)
    col = _iota2((C, C), 1)
    tril_b = jnp.where(row >= col, 1.0, 0.0).astype(BF16)
    triu_b = jnp.where(row <= col, 1.0, 0.0).astype(BF16)

    f_cache = _dot_mask(clr_ref[0], triu_p)
    carry_r = f_cache[:, P - 1:P]
    carry_c = jnp.sum(clc_ref[0], axis=0, keepdims=True)
    f_new_r = carry_r + _dot_mask(lfr_ref[...], triu_b)
    f_new_c = carry_c + _mask_dot(tril_b, lfc_ref[...])[:, SM_F:SM_F + HEADS]
    mask_new = (col <= row) & (col < l_valid)

    for h in range(HEADS):
        hs = slice(h * HEAD_DIM, (h + 1) * HEAD_DIM)
        q = q_ref[:, hs]
        fq = f_new_c[:, h:h + 1]
        s_c = _dot_nt(q, ck_ref[0, :, hs].astype(BF16)) + (fq - f_cache[h:h + 1, :])
        s_n = jnp.where(mask_new, _dot_nt(q, kn_ref[:, hs]) + (fq - f_new_r[h:h + 1, :]), NEG)
        m = jnp.maximum(jnp.max(s_c, axis=-1, keepdims=True), jnp.max(s_n, axis=-1, keepdims=True))
        p_c = jnp.exp(s_c - m)
        p_n = jnp.exp(s_n - m)
        l = jnp.sum(p_c, axis=-1, keepdims=True) + jnp.sum(p_n, axis=-1, keepdims=True)
        o = _dot(p_c.astype(BF16), cv_ref[0, :, hs].astype(BF16)) + _dot(p_n.astype(BF16), vn_ref[:, hs])
        gate = _sigmoid(og_ref[:, hs].astype(F32))
        o_ref[:, hs] = (o / l * gate).astype(BF16)


def _fox_sample(q, kb, p, lf_col, lf_row, cache_k, cache_v, cache_lf, *, l_valid):
    bs, past = cache_k.shape[0], cache_k.shape[1]
    C = GDN_CHUNK
    n = q.shape[0]
    blk = lambda b: (b + 1, 0)
    return pl.pallas_call(
        functools.partial(_fox_sample_kernel, l_valid=l_valid),
        grid=(bs,),
        in_specs=[
            pl.BlockSpec((C, WIDTH), blk),
            pl.BlockSpec((C, WIDTH), blk),
            pl.BlockSpec((C, WIDTH), lambda b: (b + 1, PB_FV)),
            pl.BlockSpec((C, WIDTH), lambda b: (b + 1, PB_FOG)),
            pl.BlockSpec((C, LANES), blk),
            pl.BlockSpec((HEADS, C), lambda b: (0, b + 1)),
            pl.BlockSpec((1, past, WIDTH), lambda b: (b, 0, 0)),
            pl.BlockSpec((1, past, WIDTH), lambda b: (b, 0, 0)),
            pl.BlockSpec((1, past, HEADS), lambda b: (b, 0, 0)),
            pl.BlockSpec((1, HEADS, past), lambda b: (b, 0, 0)),
        ],
        out_specs=pl.BlockSpec((C, WIDTH), blk),
        out_shape=jax.ShapeDtypeStruct((n, WIDTH), BF16),
        compiler_params=_cparams(("arbitrary",)),
        name="fox_sample",
    )(q, kb, p, p, lf_col, lf_row, cache_k.reshape(bs, past, WIDTH), cache_v.reshape(bs, past, WIDTH),
      cache_lf, jnp.swapaxes(cache_lf, 1, 2))


def _outproj_kernel(og_ref, of_ref, x_ref, wo_ref, nw_ref, wr_ref, br_ref, x1_ref, xn_ref, cw_ref):
    half = og_ref.shape[1]
    h = _dot(og_ref[...], wo_ref[0:half, :]) + _dot(of_ref[...], wo_ref[half:2 * half, :])
    x1 = x_ref[...] + h
    x1_ref[...] = x1
    xn = x1 * lax.rsqrt(jnp.mean(x1 * x1, axis=-1, keepdims=True) + EPS) * nw_ref[...]
    xn_ref[...] = xn.astype(BF16)

    x_hi = xn.astype(BF16)
    x_lo = (xn - x_hi.astype(F32)).astype(BF16)
    logits = (_dot(x_hi, wr_ref[0]) + _dot(x_lo, wr_ref[0]) + _dot(x_hi, wr_ref[1])) + br_ref[...]

    tm = logits.shape[0]
    lane = _iota2((tm, LANES), 1).astype(F32)
    big = float(LANES)
    gl = jnp.where(lane < N_GROUPS, logits, NEG)
    gmax = jnp.max(gl, axis=-1, keepdims=True)
    gidx = jnp.min(jnp.where(gl == gmax, lane, big), axis=-1, keepdims=True)
    p_top = 1.0 / jnp.sum(jnp.exp(gl - gmax), axis=-1, keepdims=True)
    e = lane - RT_E0
    sel = (e >= 0) & (e < N_EXPERTS) & (jnp.floor(e * (1.0 / EXPERTS_PER_GROUP)) == gidx)
    el = jnp.where(sel, logits, NEG)
    v1 = jnp.max(el, axis=-1, keepdims=True)
    i1 = jnp.min(jnp.where(el == v1, lane, big), axis=-1, keepdims=True)
    el2 = jnp.where(lane == i1, NEG, el)
    v2 = jnp.max(el2, axis=-1, keepdims=True)
    i2 = jnp.min(jnp.where(el2 == v2, lane, big), axis=-1, keepdims=True)
    e2 = jnp.exp(v2 - v1)
    w1 = p_top / (1.0 + e2)
    w2 = p_top * e2 / (1.0 + e2)
    cw_ref[...] = jnp.where(lane == i1, w1, 0.0) + jnp.where(lane == i2, w2, 0.0)


def _outproj(og, of, x, wo, norm_w, wr, br):
    n, d = x.shape
    tm = _pick(n, (512, 384, 256, 128))
    rowblk = lambda i: (i, 0)
    return pl.pallas_call(
        _outproj_kernel,
        grid=(n // tm,),
        in_specs=[
            pl.BlockSpec((tm, WIDTH), rowblk),
            pl.BlockSpec((tm, WIDTH), rowblk),
            pl.BlockSpec((tm, d), rowblk),
            pl.BlockSpec((2 * WIDTH, d), lambda i: (0, 0)),
            pl.BlockSpec((1, d), lambda i: (0, 0)),
            pl.BlockSpec((2, d, LANES), lambda i: (0, 0, 0)),
            pl.BlockSpec((1, LANES), lambda i: (0, 0)),
        ],
        out_specs=[
            pl.BlockSpec((tm, d), rowblk),
            pl.BlockSpec((tm, d), rowblk),
            pl.BlockSpec((tm, LANES), rowblk),
        ],
        out_shape=[
            jax.ShapeDtypeStruct((n, d), F32),
            jax.ShapeDtypeStruct((n, d), BF16),
            jax.ShapeDtypeStruct((n, LANES), F32),
        ],
        compiler_params=_cparams(("arbitrary",)),
        name="outproj_router",
    )(og, of, x, wo, norm_w, wr, br)


def _moe_kernel(xn_ref, cw_ref, x1_ref, w1_ref, w3_ref, w2_ref, nw_ref, y_ref, acc_scr):
    e = pl.program_id(1)

    @pl.when(e == 0)
    def _():
        acc_scr[...] = jnp.zeros_like(acc_scr)

    xn = xn_ref[...]
    hid = _silu(_dot(xn, w1_ref[0])) * _dot(xn, w3_ref[0])
    lane = _iota2(cw_ref.shape, 1)
    c = jnp.sum(jnp.where(lane == RT_E0 + e, cw_ref[...], 0.0), axis=-1, keepdims=True)
    acc_scr[...] += c * _dot(hid.astype(BF16), w2_ref[0])

    @pl.when(e == pl.num_programs(1) - 1)
    def _():
        x2 = x1_ref[...] + acc_scr[...]
        y_ref[...] = x2 * lax.rsqrt(jnp.mean(x2 * x2, axis=-1, keepdims=True) + EPS) * nw_ref[...]


def _moe(xn, cw, x1, w1, w3, w2, norm_w):
    n, d = x1.shape
    de = w1.shape[2]
    tm = _pick(n, (512, 384, 256, 128))
    rowblk = lambda i, e: (i, 0)
    return pl.pallas_call(
        _moe_kernel,
        grid=(n // tm, N_EXPERTS),
        in_specs=[
            pl.BlockSpec((tm, d), rowblk),
            pl.BlockSpec((tm, LANES), rowblk),
            pl.BlockSpec((tm, d), rowblk),
            pl.BlockSpec((1, d, de), lambda i, e: (e, 0, 0)),
            pl.BlockSpec((1, d, de), lambda i, e: (e, 0, 0)),
            pl.BlockSpec((1, de, d), lambda i, e: (e, 0, 0)),
            pl.BlockSpec((1, d), lambda i, e: (0, 0)),
        ],
        out_specs=pl.BlockSpec((tm, d), rowblk),
        out_shape=jax.ShapeDtypeStruct((n, d), F32),
        scratch_shapes=[pltpu.VMEM((tm, d), F32)],
        compiler_params=_cparams(("arbitrary", "arbitrary")),
        name="moe",
    )(xn, cw, x1, w1, w3, w2, norm_w)


def kernel(x_prompt, x_sample, cache_fox_k, cache_fox_v, cache_fox_logf, state_gdn, state_gdn_conv, meta_tokens, norm_mix_w, w_in, gdn_conv_w, gdn_A_log, gdn_dt_bias, gdn_norm_w, fox_q_norm_w, fox_k_norm_w, fox_f_bias, w_out, norm_ffn_w, w_router_group, b_router_group, w_router_expert, b_router_expert, w_gate, w_up, w_down, norm_final_w):
    B, S, D = x_prompt.shape
    BS, LS, _ = x_sample.shape
    C = GDN_CHUNK
    assert w_in.shape[0] == 1, "single-layer step only"
    assert S % C == 0 and LS <= C and N_META <= C and meta_tokens.shape[0] == N_META

    wi = w_in[0]
    o = 0
    parts = {}
    for name, size in (("g_qkv", 3 * WIDTH), ("g_z", WIDTH), ("g_a", HEADS), ("g_b", HEADS),
                       ("f_qkv", 3 * WIDTH), ("f_og", WIDTH), ("f_f", HEADS)):
        parts[name] = wi[:, o:o + size]
        o += size
    wbig = jnp.concatenate([parts["g_qkv"], parts["f_qkv"], parts["g_z"], parts["f_og"]], axis=1).astype(BF16)
    wsm_cols = jnp.concatenate([parts["f_f"], parts["g_a"], parts["g_b"]], axis=1)
    wsm = jnp.pad(wsm_cols, ((0, 0), (0, LANES - 3 * HEADS))).astype(BF16)
    wsmt = jnp.pad(wsm_cols.T, ((0, 32 - 3 * HEADS), (0, 0))).astype(BF16)
    wo = w_out[0].astype(BF16)
    wr32 = jnp.pad(jnp.concatenate([w_router_group[0], w_router_expert[0]], axis=1),
                   ((0, 0), (0, LANES - N_GROUPS - N_EXPERTS)))
    wr_hi = wr32.astype(BF16)
    wr = jnp.stack([wr_hi, (wr32 - wr_hi.astype(F32)).astype(BF16)])
    br = jnp.pad(jnp.concatenate([b_router_group[0], b_router_expert[0]]),
                 (0, LANES - N_GROUPS - N_EXPERTS)).reshape(1, LANES)
    w1 = w_gate[0].astype(BF16)
    w3 = w_up[0].astype(BF16)
    w2 = w_down[0].astype(BF16)
    nmix = norm_mix_w[0].reshape(1, D)
    nffn = norm_ffn_w[0].reshape(1, D)
    nfin = norm_final_w.reshape(1, D)

    x_small = jnp.concatenate([
        jnp.pad(meta_tokens.astype(F32), ((0, C - N_META), (0, 0))),
        jnp.pad(x_sample, ((0, 0), (0, C - LS), (0, 0))).reshape(BS * C, D)], axis=0)
    xp = x_prompt.reshape(B * S, D)

    p_s, psm_s, psmt_s = _inproj(x_small, nmix, wbig, wsm, wsmt)
    p_p, psm_p, psmt_p = _inproj(xp, nmix, wbig, wsm, wsmt)

    s0_s = jnp.concatenate([jnp.zeros((1,) + state_gdn.shape[2:], F32), state_gdn[0]], axis=0)
    cb_s = jnp.concatenate([jnp.zeros((1,) + state_gdn_conv.shape[2:], F32), state_gdn_conv[0]], axis=0)
    gdn_args = (gdn_conv_w[0], gdn_A_log[0], gdn_dt_bias[0], gdn_norm_w[0])
    og_s, st_s, cv_s = _gdn(p_s, psm_s, psmt_s, s0_s, cb_s, *gdn_args, n_seq=1 + BS, l_valid=LS,
                            bcast_state=False)
    og_p, st_p, cv_p = _gdn(p_p, psm_p, psmt_p, st_s[0:1], cv_s[0:1], *gdn_args, n_seq=B, l_valid=C,
                            bcast_state=True)

    fox_args = (fox_q_norm_w[0], fox_k_norm_w[0], fox_f_bias[0])
    zc = jnp.zeros((1, 1, LANES), F32)
    zr = jnp.zeros((1, HEADS, 1), F32)
    q_s, k32_s, kb_s, v32_s, lfc_s, lfr_s, fc_s, fr_s = _foxprep(
        p_s, psm_s, psmt_s, zc, zr, *fox_args, n_seq=1 + BS, bcast_carry=True)
    c0c = fc_s[N_META - 1:N_META, :].reshape(1, 1, LANES)
    c0r = fr_s[:, N_META - 1:N_META].reshape(1, HEADS, 1)
    q_p, k32_p, kb_p, v32_p, lfc_p, lfr_p, fc_p, fr_p = _foxprep(
        p_p, psm_p, psmt_p, c0c, c0r, *fox_args, n_seq=B, bcast_carry=True)
    of_p = _fox_prompt(q_p, kb_p, p_p, fc_p, fr_p, kb_s, p_s, fr_s, n_seq=B)
    of_s = _fox_sample(q_s, kb_s, p_s, lfc_s, lfr_s, cache_fox_k[0], cache_fox_v[0], cache_fox_logf[0],
                       l_valid=LS)

    x1_p, xn_p, cw_p = _outproj(og_p, of_p, xp, wo, nffn, wr, br)
    x1_s, xn_s, cw_s = _outproj(og_s, of_s, x_small, wo, nffn, wr, br)
    y_p = _moe(xn_p, cw_p, x1_p, w1, w3, w2, nfin)
    y_s = _moe(xn_s, cw_s, x1_s, w1, w3, w2, nfin)

    def with_meta(meta_rows, real):
        w_ = real.shape[-1]
        m = jnp.broadcast_to(meta_rows[None], (B, N_META, w_))
        return jnp.concatenate([m, real.reshape(B, S, w_)], axis=1)[None]

    def sample_rows(a):
        return a[C:].reshape(BS, C, a.shape[-1])[:, :LS][None]

    hd = (HEADS, HEAD_DIM)
    y_prompt = y_p.reshape(B, S, D)
    y_sample = sample_rows(y_s)[0]
    fk_p = with_meta(k32_s[:N_META], k32_p).reshape(1, B, N_META + S, *hd)
    fv_p = with_meta(v32_s[:N_META], v32_p).reshape(1, B, N_META + S, *hd)
    lf_p = with_meta(lfc_s[:N_META, SM_F:SM_F + HEADS], lfc_p[:, SM_F:SM_F + HEADS])
    fk_s = sample_rows(k32_s).reshape(1, BS, LS, *hd)
    fv_s = sample_rows(v32_s).reshape(1, BS, LS, *hd)
    lf_s = sample_rows(lfc_s[:, SM_F:SM_F + HEADS])
    return (y_prompt, y_sample, fk_p, fv_p, lf_p, st_p[None], cv_p[None],
            fk_s, fv_s, lf_s, st_s[1:][None], cv_s[1:][None])
```

```python
import functools

import jax
import jax.numpy as jnp
from jax import lax
from jax.experimental import pallas as pl
from jax.experimental.pallas import tpu as pltpu

F32 = jnp.float32
BF16 = jnp.bfloat16
EPS = 1e-6
NEG = -1e30
LOG2E = 1.4426950408889634

N_META = 16
HEADS = 8
HEAD_DIM = 128
WIDTH = HEADS * HEAD_DIM
CONV_WIDTH = 4
N_GROUPS = 4
EXPERTS_PER_GROUP = 8
N_EXPERTS = N_GROUPS * EXPERTS_PER_GROUP
GDN_CHUNK = 128
LANES = 128
SM_F, SM_A, SM_B = 0, 8, 16
PB_FQ, PB_FK, PB_FV, PB_GZ, PB_FOG = 3, 4, 5, 6, 7
RT_E0 = N_GROUPS

VMEM_LIMIT = 56 * 1024 * 1024


def _cparams(sem):
    return pltpu.CompilerParams(dimension_semantics=sem, vmem_limit_bytes=VMEM_LIMIT)


def _pick(n, prefs):
    for p in prefs:
        if n % p == 0:
            return p
    raise ValueError(f"no tile in {prefs} divides {n}")


def _dot(a, b):
    return jnp.dot(a, b, preferred_element_type=F32)


def _dot_nt(a, b):
    return lax.dot_general(a, b, (((1,), (1,)), ((), ())), preferred_element_type=F32)


def _dot_tn(a, b):
    return lax.dot_general(a, b, (((0,), (0,)), ((), ())), preferred_element_type=F32)


def _split3(x):
    x1 = x.astype(BF16)
    r1 = x - x1.astype(F32)
    x2 = r1.astype(BF16)
    x3 = (r1 - x2.astype(F32)).astype(BF16)
    return x1, x2, x3


def _mask_dot(mask_bf16, x):
    x1, x2, x3 = _split3(x)
    return _dot(mask_bf16, x1) + _dot(mask_bf16, x2) + _dot(mask_bf16, x3)


def _dot_mask(x, mask_bf16):
    x1, x2, x3 = _split3(x)
    return _dot(x1, mask_bf16) + _dot(x2, mask_bf16) + _dot(x3, mask_bf16)


def _softplus(x):
    return jnp.maximum(x, 0.0) + jnp.log1p(jnp.exp(-jnp.abs(x)))


def _sigmoid(x):
    return 1.0 / (1.0 + jnp.exp(-x))


def _silu(x):
    return x * _sigmoid(x)


def _iota2(shape, dim):
    return lax.broadcasted_iota(jnp.int32, shape, dim)


def _inproj_kernel(x_ref, nw_ref, wbig_ref, wsm_ref, wsmt_ref, p_ref, psm_ref, psmt_ref, xn_scr):
    @pl.when(pl.program_id(1) == 0)
    def _():
        x = x_ref[...]
        xn = x * lax.rsqrt(jnp.mean(x * x, axis=-1, keepdims=True) + EPS) * nw_ref[...]
        xnb = xn.astype(BF16)
        xn_scr[...] = xnb
        psm_ref[...] = _dot(xnb, wsm_ref[...])
        psmt_ref[...] = _dot_nt(wsmt_ref[...], xnb)

    p_ref[...] = _dot(xn_scr[...], wbig_ref[...]).astype(BF16)


def _inproj(x, norm_w, wbig, wsm, wsmt):
    n, d = x.shape
    tm = _pick(n, (1024, 1152, 512, 384, 256, 128))
    tn = 1024
    nproj = wbig.shape[1]
    return pl.pallas_call(
        _inproj_kernel,
        grid=(n // tm, nproj // tn),
        in_specs=[
            pl.BlockSpec((tm, d), lambda i, j: (i, 0)),
            pl.BlockSpec((1, d), lambda i, j: (0, 0)),
            pl.BlockSpec((d, tn), lambda i, j: (0, j)),
            pl.BlockSpec((d, LANES), lambda i, j: (0, 0)),
            pl.BlockSpec((32, d), lambda i, j: (0, 0)),
        ],
        out_specs=[
            pl.BlockSpec((tm, tn), lambda i, j: (i, j)),
            pl.BlockSpec((tm, LANES), lambda i, j: (i, 0)),
            pl.BlockSpec((32, tm), lambda i, j: (0, i)),
        ],
        out_shape=[
            jax.ShapeDtypeStruct((n, nproj), BF16),
            jax.ShapeDtypeStruct((n, LANES), F32),
            jax.ShapeDtypeStruct((32, n), F32),
        ],
        scratch_shapes=[pltpu.VMEM((tm, d), BF16)],
        compiler_params=_cparams(("arbitrary", "arbitrary")),
        name="inproj",
    )(x, norm_w, wbig, wsm, wsmt)


def _gdn_kernel(qkv_ref, z_ref, sm_ref, smt_ref, s0_ref, cb_ref, convw_ref, alog_c_ref, dtb_c_ref,
                alog_r_ref, dtb_r_ref, normw_ref, o_ref, snew_ref, cnew_ref, s_scr, xp_scr, *, l_valid):
    C = GDN_CHUNK

    @pl.when(pl.program_id(1) == 0)
    def _():
        s_scr[...] = s0_ref[0]
        xp_scr[5:8, :] = cb_ref[0]

    x = qkv_ref[...].astype(F32)
    xp_scr[8:8 + C, :] = x
    w = convw_ref[...]
    conv = (w[0:1] * xp_scr[5:5 + C, :] + w[1:2] * xp_scr[6:6 + C, :]
            + w[2:3] * xp_scr[7:7 + C, :] + w[3:4] * x)
    act = _silu(conv)
    tail = xp_scr[8 + l_valid - 3:8 + l_valid, :]
    xp_scr[5:8, :] = tail
    cnew_ref[0] = tail

    row = _iota2((C, C), 0)
    col = _iota2((C, C), 1)
    incl = row >= col
    strict = row > col
    tril_b = jnp.where(incl, 1.0, 0.0).astype(BF16)
    triu_b = jnp.where(row <= col, 1.0, 0.0).astype(BF16)

    sm = sm_ref[...]
    g_col = -jnp.exp(alog_c_ref[...]) * _softplus(sm + dtb_c_ref[...])
    g_row = -jnp.exp(alog_r_ref[...]) * _softplus(smt_ref[...] + dtb_r_ref[...])
    beta_col = _sigmoid(sm)
    if l_valid < C:
        g_col = jnp.where(_iota2((C, LANES), 0) < l_valid, g_col, 0.0)
        beta_col = jnp.where(_iota2((C, LANES), 0) < l_valid, beta_col, 0.0)
        g_row = jnp.where(_iota2((HEADS, C), 1) < l_valid, g_row, 0.0)
    gl_col = _mask_dot(tril_b, g_col)
    gl_row = _dot_mask(g_row, triu_b)

    inv_levels = []
    sh = 1
    while (1 << sh) < C:
        m = (((row >> sh) & 1) == 1) & ((col >> (sh + 1)) == (row >> (sh + 1))) & (((col >> sh) & 1) == 0)
        inv_levels.append(m)
        sh += 1
    m1 = ((row & 1) == 1) & (col == row - 1)
    eye = jnp.where(row == col, 1.0, 0.0)

    normw = normw_ref[...]
    H = range(HEADS)
    hs = [slice(h * HEAD_DIM, (h + 1) * HEAD_DIM) for h in H]
    gc = [gl_col[:, SM_A + h:SM_A + h + 1] for h in H]
    beta = [beta_col[:, SM_B + h:SM_B + h + 1] for h in H]
    qn, kn, kb, knb = [], [], [], []
    for h in H:
        q = act[:, hs[h]]
        k = act[:, WIDTH + h * HEAD_DIM:WIDTH + (h + 1) * HEAD_DIM]
        qn.append(q * lax.rsqrt(jnp.sum(q * q, axis=-1, keepdims=True) + EPS) * (HEAD_DIM ** -0.5))
        kn.append(k * lax.rsqrt(jnp.sum(k * k, axis=-1, keepdims=True) + EPS))
        kb.append(kn[h] * beta[h])
        knb.append(kn[h].astype(BF16))
    decay = [jnp.exp(jnp.where(incl, gc[h] - gl_row[h:h + 1, :], NEG)) for h in H]
    a_mat = [jnp.where(strict, _dot_nt(kb[h].astype(BF16), knb[h]) * decay[h], 0.0) for h in H]
    qk = [(_dot_nt(qn[h].astype(BF16), knb[h]) * decay[h]).astype(BF16) for h in H]

    t = [eye - jnp.where(m1, a_mat[h], 0.0) for h in H]
    for m in inv_levels:
        tb = [t[h].astype(BF16) for h in H]
        y = [_dot(tb[h], jnp.where(m, a_mat[h], 0.0).astype(BF16)).astype(BF16) for h in H]
        t = [t[h] - _dot(y[h], tb[h]) for h in H]
    tb = [t[h].astype(BF16) for h in H]

    eg = [jnp.exp(gc[h]) for h in H]
    g_last = [gc[h][C - 1:C, :] for h in H]
    s = [s_scr[h] for h in H]
    sb = [s[h].astype(BF16) for h in H]
    r = [(act[:, 2 * WIDTH + h * HEAD_DIM:2 * WIDTH + (h + 1) * HEAD_DIM] * beta[h]
          - _dot((kb[h] * eg[h]).astype(BF16), sb[h])).astype(BF16) for h in H]
    ub = [_dot(tb[h], r[h]).astype(BF16) for h in H]
    o = [_dot((qn[h] * eg[h]).astype(BF16), sb[h]) + _dot(qk[h], ub[h]) for h in H]
    for h in H:
        k_dec = (kn[h] * jnp.exp(g_last[h] - gc[h])).astype(BF16)
        s_scr[h] = s[h] * jnp.exp(g_last[h]) + _dot_tn(k_dec, ub[h])
    for h in H:
        on = o[h] * lax.rsqrt(jnp.mean(o[h] * o[h], axis=-1, keepdims=True) + EPS) * normw
        z = z_ref[:, hs[h]].astype(F32)
        o_ref[:, hs[h]] = (on * _silu(z)).astype(BF16)

    snew_ref[0] = s_scr[...]


def _gdn(p, psm, psmt, s0, cb, conv_w, alog, dtb, norm_w, *, n_seq, l_valid, bcast_state):
    n = p.shape[0]
    C = GDN_CHUNK
    nc = n // (n_seq * C)
    alog_c = jnp.zeros((1, LANES), F32).at[0, SM_A:SM_A + HEADS].set(alog)
    dtb_c = jnp.zeros((1, LANES), F32).at[0, SM_A:SM_A + HEADS].set(dtb)
    st_idx = (lambda b, c: (0, 0, 0, 0)) if bcast_state else (lambda b, c: (b, 0, 0, 0))
    cb_idx = (lambda b, c: (0, 0, 0)) if bcast_state else (lambda b, c: (b, 0, 0))
    return pl.pallas_call(
        functools.partial(_gdn_kernel, l_valid=l_valid),
        grid=(n_seq, nc),
        in_specs=[
            pl.BlockSpec((C, 3 * WIDTH), lambda b, c: (b * nc + c, 0)),
            pl.BlockSpec((C, WIDTH), lambda b, c: (b * nc + c, PB_GZ)),
            pl.BlockSpec((C, LANES), lambda b, c: (b * nc + c, 0)),
            pl.BlockSpec((HEADS, C), lambda b, c: (SM_A // HEADS, b * nc + c)),
            pl.BlockSpec((1, HEADS, HEAD_DIM, HEAD_DIM), st_idx),
            pl.BlockSpec((1, CONV_WIDTH - 1, 3 * WIDTH), cb_idx),
            pl.BlockSpec((CONV_WIDTH, 3 * WIDTH), lambda b, c: (0, 0)),
            pl.BlockSpec((1, LANES), lambda b, c: (0, 0)),
            pl.BlockSpec((1, LANES), lambda b, c: (0, 0)),
            pl.BlockSpec((HEADS, 1), lambda b, c: (0, 0)),
            pl.BlockSpec((HEADS, 1), lambda b, c: (0, 0)),
            pl.BlockSpec((1, HEAD_DIM), lambda b, c: (0, 0)),
        ],
        out_specs=[
            pl.BlockSpec((C, WIDTH), lambda b, c: (b * nc + c, 0)),
            pl.BlockSpec((1, HEADS, HEAD_DIM, HEAD_DIM), lambda b, c: (b, 0, 0, 0)),
            pl.BlockSpec((1, CONV_WIDTH - 1, 3 * WIDTH), lambda b, c: (b, 0, 0)),
        ],
        out_shape=[
            jax.ShapeDtypeStruct((n, WIDTH), BF16),
            jax.ShapeDtypeStruct((n_seq, HEADS, HEAD_DIM, HEAD_DIM), F32),
            jax.ShapeDtypeStruct((n_seq, CONV_WIDTH - 1, 3 * WIDTH), F32),
        ],
        scratch_shapes=[
            pltpu.VMEM((HEADS, HEAD_DIM, HEAD_DIM), F32),
            pltpu.VMEM((C + 8, 3 * WIDTH), F32),
        ],
        compiler_params=_cparams(("arbitrary", "arbitrary")),
        name="gdn",
    )(p, p, psm, psmt, s0, cb, conv_w, alog_c, dtb_c, alog.reshape(HEADS, 1), dtb.reshape(HEADS, 1),
      norm_w.reshape(1, HEAD_DIM))


def _foxprep_kernel(qkv_ref, sm_ref, smt_ref, c0c_ref, c0r_ref, qw_ref, kw_ref, fb_c_ref, fb_r_ref,
                    k32_ref, v32_ref, lfc_ref, lfr_ref, fc_ref, fr_ref, qt_ref, kcat_ref, vt_ref, *rest,
                    row_major):
    if row_major:
        q_ref, kb_ref, cc_scr, cr_scr = rest
    else:
        cc_scr, cr_scr = rest
    tm = qkv_ref.shape[0]

    @pl.when(pl.program_id(1) == 0)
    def _():
        cc_scr[...] = c0c_ref[0]
        cr_scr[...] = c0r_ref[0]

    lf_col = -_softplus(-(sm_ref[...] + fb_c_ref[...]))
    lf_row = -_softplus(-(smt_ref[...] + fb_r_ref[...]))
    lfc_ref[...] = lf_col
    lfr_ref[...] = lf_row
    row = _iota2((tm, tm), 0)
    col = _iota2((tm, tm), 1)
    tril_b = jnp.where(row >= col, 1.0, 0.0).astype(BF16)
    triu_b = jnp.where(row <= col, 1.0, 0.0).astype(BF16)
    f_col = cc_scr[...] + _mask_dot(tril_b, lf_col)
    f_row = cr_scr[...] + _dot_mask(lf_row, triu_b)
    fc_ref[...] = f_col
    fr_ref[...] = f_row
    cc_scr[...] = f_col[tm - 1:tm, :]
    cr_scr[...] = f_row[:, tm - 1:tm]

    qw = qw_ref[...]
    kw = kw_ref[...]
    lane = _iota2((tm, HEAD_DIM), 1)
    sub = _iota2((16, tm), 0)
    zeros_t = jnp.zeros((HEAD_DIM - 16, tm), BF16)
    for h in range(HEADS):
        hs = slice(h * HEAD_DIM, (h + 1) * HEAD_DIM)
        q = qkv_ref[:, hs].astype(F32)
        k = qkv_ref[:, WIDTH + h * HEAD_DIM:WIDTH + (h + 1) * HEAD_DIM].astype(F32)
        v = qkv_ref[:, 2 * WIDTH + h * HEAD_DIM:2 * WIDTH + (h + 1) * HEAD_DIM].astype(F32)
        qn = q * lax.rsqrt(jnp.mean(q * q, axis=-1, keepdims=True) + EPS) * qw
        kn = k * lax.rsqrt(jnp.mean(k * k, axis=-1, keepdims=True) + EPS) * kw
        k32_ref[:, hs] = kn
        v32_ref[:, hs] = v
        if row_major:
            q_ref[:, hs] = (qn * (HEAD_DIM ** -0.5)).astype(BF16)
            kb_ref[:, hs] = kn.astype(BF16)
        base = 2 * h * HEAD_DIM
        qt_ref[base:base + HEAD_DIM, :] = (qn * (LOG2E * HEAD_DIM ** -0.5)).T.astype(BF16)
        r1, r2, r3 = _split3(f_row[h:h + 1, :] * LOG2E)
        aug_q = jnp.where(sub == 0, r1.astype(F32), jnp.where(sub == 1, r2.astype(F32), jnp.where(
            sub == 2, r3.astype(F32), jnp.where(sub < 6, 1.0, 0.0))))
        qt_ref[base + HEAD_DIM:base + HEAD_DIM + 16, :] = aug_q.astype(BF16)
        qt_ref[base + HEAD_DIM + 16:base + 2 * HEAD_DIM, :] = zeros_t
        kcat_ref[:, base:base + HEAD_DIM] = kn.astype(BF16)
        c1, c2, c3 = _split3(f_col[:, SM_F + h:SM_F + h + 1] * LOG2E)
        aug_k = jnp.where(lane < 3, 1.0, jnp.where(lane == 3, -c1.astype(F32), jnp.where(
            lane == 4, -c2.astype(F32), jnp.where(lane == 5, -c3.astype(F32), 0.0))))
        kcat_ref[:, base + HEAD_DIM:base + 2 * HEAD_DIM] = aug_k.astype(BF16)
        vt_ref[hs, :] = v.T.astype(BF16)


def _foxprep(p, psm, psmt, c0c, c0r, q_norm_w, k_norm_w, f_bias, *, n_seq, bcast_carry, row_major):
    n = p.shape[0]
    rows = n // n_seq
    tm = _pick(rows, (512, 256, 128))
    nt = rows // tm
    fb_c = jnp.zeros((1, LANES), F32).at[0, SM_F:SM_F + HEADS].set(f_bias)
    c_idx = (lambda b, t: (0, 0, 0)) if bcast_carry else (lambda b, t: (b, 0, 0))
    rowblk = lambda b, t: (b * nt + t, 0)
    colblk = lambda b, t: (0, b * nt + t)
    out_specs = [
        pl.BlockSpec((tm, WIDTH), rowblk),
        pl.BlockSpec((tm, WIDTH), rowblk),
        pl.BlockSpec((tm, LANES), rowblk),
        pl.BlockSpec((HEADS, tm), colblk),
        pl.BlockSpec((tm, LANES), rowblk),
        pl.BlockSpec((HEADS, tm), colblk),
        pl.BlockSpec((2 * WIDTH, tm), colblk),
        pl.BlockSpec((tm, 2 * WIDTH), rowblk),
        pl.BlockSpec((WIDTH, tm), colblk),
    ]
    out_shape = [
        jax.ShapeDtypeStruct((n, WIDTH), F32),
        jax.ShapeDtypeStruct((n, WIDTH), F32),
        jax.ShapeDtypeStruct((n, LANES), F32),
        jax.ShapeDtypeStruct((HEADS, n), F32),
        jax.ShapeDtypeStruct((n, LANES), F32),
        jax.ShapeDtypeStruct((HEADS, n), F32),
        jax.ShapeDtypeStruct((2 * WIDTH, n), BF16),
        jax.ShapeDtypeStruct((n, 2 * WIDTH), BF16),
        jax.ShapeDtypeStruct((WIDTH, n), BF16),
    ]
    if row_major:
        out_specs += [pl.BlockSpec((tm, WIDTH), rowblk), pl.BlockSpec((tm, WIDTH), rowblk)]
        out_shape += [jax.ShapeDtypeStruct((n, WIDTH), BF16),
                      jax.ShapeDtypeStruct((n, WIDTH), BF16)]
    return pl.pallas_call(
        functools.partial(_foxprep_kernel, row_major=row_major),
        grid=(n_seq, nt),
        in_specs=[
            pl.BlockSpec((tm, 3 * WIDTH), lambda b, t: (b * nt + t, 1)),
            pl.BlockSpec((tm, LANES), rowblk),
            pl.BlockSpec((HEADS, tm), lambda b, t: (SM_F // HEADS, b * nt + t)),
            pl.BlockSpec((1, 1, LANES), c_idx),
            pl.BlockSpec((1, HEADS, 1), c_idx),
            pl.BlockSpec((1, HEAD_DIM), lambda b, t: (0, 0)),
            pl.BlockSpec((1, HEAD_DIM), lambda b, t: (0, 0)),
            pl.BlockSpec((1, LANES), lambda b, t: (0, 0)),
            pl.BlockSpec((HEADS, 1), lambda b, t: (0, 0)),
        ],
        out_specs=out_specs,
        out_shape=out_shape,
        scratch_shapes=[pltpu.VMEM((1, LANES), F32), pltpu.VMEM((HEADS, 1), F32)],
        compiler_params=_cparams(("arbitrary", "arbitrary")),
        name="foxprep",
    )(p, psm, psmt, c0c, c0r, q_norm_w.reshape(1, HEAD_DIM), k_norm_w.reshape(1, HEAD_DIM), fb_c,
      f_bias.reshape(HEADS, 1))


FOX_UNIT = 256


def _fox_kernel(qi_ref, ki_ref, qt_ref, kcat_ref, vt_ref, og_ref, km_ref, vtm_ref, o_ref, m_scr, l_scr, acc_scr):
    pair = pl.program_id(1)
    qi = qi_ref[pair]
    ki = ki_ref[pair]
    tq = qt_ref.shape[1]
    tk = kcat_ref.shape[0]
    U = FOX_UNIT
    H = range(HEADS)

    def unit(qh, k_rows, vt_cols, mask):
        qs = slice(qh * U, (qh + 1) * U)
        s = [_dot(k_rows(h), qt_ref[2 * h * HEAD_DIM:2 * (h + 1) * HEAD_DIM, qs]) for h in H]
        if mask is not None:
            s = [jnp.where(mask, s[h], NEG) for h in H]
        m_old = [m_scr[h:h + 1, qs] for h in H]
        m_new = [jnp.maximum(m_old[h], jnp.max(s[h], axis=0, keepdims=True)) for h in H]
        alpha = [jnp.exp2(m_old[h] - m_new[h]) for h in H]
        p = [jnp.exp2(s[h] - m_new[h]) for h in H]
        for h in H:
            l_scr[h:h + 1, qs] = alpha[h] * l_scr[h:h + 1, qs] + jnp.sum(p[h], axis=0, keepdims=True)
            m_scr[h:h + 1, qs] = m_new[h]
        for h in H:
            acc_scr[h, :, qs] = alpha[h] * acc_scr[h, :, qs] + _dot(vt_cols(h), p[h].astype(BF16))

    def k_unit(ku):
        return lambda h: kcat_ref[ku * U:(ku + 1) * U, 2 * h * HEAD_DIM:2 * (h + 1) * HEAD_DIM]

    def vt_unit(ku):
        return lambda h: vt_ref[h * HEAD_DIM:(h + 1) * HEAD_DIM, ku * U:(ku + 1) * U]

    @pl.when(ki == 0)
    def _():
        m_scr[...] = jnp.full(m_scr.shape, NEG, F32)
        l_scr[...] = jnp.zeros(l_scr.shape, F32)
        acc_scr[...] = jnp.zeros(acc_scr.shape, F32)
        tmeta = km_ref.shape[0]
        mask = _iota2((tmeta, U), 0) < N_META
        for qh in range(tq // U):
            unit(qh, lambda h: km_ref[:, 2 * h * HEAD_DIM:2 * (h + 1) * HEAD_DIM],
                 lambda h: vtm_ref[h * HEAD_DIM:(h + 1) * HEAD_DIM, :], mask)

    @pl.when(ki < qi)
    def _():
        for qh in range(tq // U):
            for ku in range(tk // U):
                unit(qh, k_unit(ku), vt_unit(ku), None)

    @pl.when(ki == qi)
    def _():
        diag = _iota2((U, U), 0) <= _iota2((U, U), 1)
        for qh in range(tq // U):
            for ku in range(qh + 1):
                unit(qh, k_unit(ku), vt_unit(ku), diag if ku == qh else None)
        for h in H:
            hs = slice(h * HEAD_DIM, (h + 1) * HEAD_DIM)
            gate = _sigmoid(og_ref[:, hs].astype(F32))
            o_ref[:, hs] = ((acc_scr[h] / l_scr[h:h + 1, :]).T * gate).astype(BF16)


def _fox_prompt(qt, kcat, vt, p, kcat_small, vt_small, *, n_seq):
    n = p.shape[0]
    rows = n // n_seq
    tq = _pick(rows, (512, 256))
    nq = rows // tq
    pairs = [(i, j) for i in range(nq) for j in range(i + 1)]
    qi = jnp.asarray([a for a, _ in pairs], jnp.int32)
    ki = jnp.asarray([b for _, b in pairs], jnp.int32)
    grid_spec = pltpu.PrefetchScalarGridSpec(
        num_scalar_prefetch=2,
        grid=(n_seq, len(pairs)),
        in_specs=[
            pl.BlockSpec((2 * WIDTH, tq), lambda b, t, qi_r, ki_r: (0, b * nq + qi_r[t])),
            pl.BlockSpec((tq, 2 * WIDTH), lambda b, t, qi_r, ki_r: (b * nq + ki_r[t], 0)),
            pl.BlockSpec((WIDTH, tq), lambda b, t, qi_r, ki_r: (0, b * nq + ki_r[t])),
            pl.BlockSpec((tq, WIDTH), lambda b, t, qi_r, ki_r: (b * nq + qi_r[t], PB_FOG)),
            pl.BlockSpec((GDN_CHUNK, 2 * WIDTH), lambda b, t, qi_r, ki_r: (0, 0)),
            pl.BlockSpec((WIDTH, GDN_CHUNK), lambda b, t, qi_r, ki_r: (0, 0)),
        ],
        out_specs=pl.BlockSpec((tq, WIDTH), lambda b, t, qi_r, ki_r: (b * nq + qi_r[t], 0)),
        scratch_shapes=[
            pltpu.VMEM((HEADS, tq), F32),
            pltpu.VMEM((HEADS, tq), F32),
            pltpu.VMEM((HEADS, HEAD_DIM, tq), F32),
        ],
    )
    return pl.pallas_call(
        _fox_kernel,
        grid_spec=grid_spec,
        out_shape=jax.ShapeDtypeStruct((n, WIDTH), BF16),
        compiler_params=_cparams(("arbitrary", "arbitrary")),
        name="fox_prompt",
    )(qi, ki, qt, kcat, vt, p, kcat_small, vt_small)


def _fox_sample_kernel(q_ref, kn_ref, vn_ref, og_ref, lfc_ref, lfr_ref, ck_ref, cv_ref, clc_ref, clr_ref,
                       o_ref, *, l_valid):
    C = q_ref.shape[0]
    P = ck_ref.shape[1]
    rowp = _iota2((P, P), 0)
    colp = _iota2((P, P), 1)
    triu_p = jnp.where(rowp <= colp, 1.0, 0.0).astype(BF16)
    row = _iota2((C, C), 0)
    col = _iota2((C, C), 1)
    tril_b = jnp.where(row >= col, 1.0, 0.0).astype(BF16)
    triu_b = jnp.where(row <= col, 1.0, 0.0).astype(BF16)

    f_cache = _dot_mask(clr_ref[0], triu_p)
    carry_r = f_cache[:, P - 1:P]
    carry_c = jnp.sum(clc_ref[0], axis=0, keepdims=True)
    f_new_r = carry_r + _dot_mask(lfr_ref[...], triu_b)
    f_new_c = carry_c + _mask_dot(tril_b, lfc_ref[...])[:, SM_F:SM_F + HEADS]
    mask_new = (col <= row) & (col < l_valid)

    for h in range(HEADS):
        hs = slice(h * HEAD_DIM, (h + 1) * HEAD_DIM)
        q = q_ref[:, hs]
        fq = f_new_c[:, h:h + 1]
        s_c = _dot_nt(q, ck_ref[0, :, hs].astype(BF16)) + (fq - f_cache[h:h + 1, :])
        s_n = jnp.where(mask_new, _dot_nt(q, kn_ref[:, hs]) + (fq - f_new_r[h:h + 1, :]), NEG)
        m = jnp.maximum(jnp.max(s_c, axis=-1, keepdims=True), jnp.max(s_n, axis=-1, keepdims=True))
        p_c = jnp.exp(s_c - m)
        p_n = jnp.exp(s_n - m)
        l = jnp.sum(p_c, axis=-1, keepdims=True) + jnp.sum(p_n, axis=-1, keepdims=True)
        o = _dot(p_c.astype(BF16), cv_ref[0, :, hs].astype(BF16)) + _dot(p_n.astype(BF16), vn_ref[:, hs])
        gate = _sigmoid(og_ref[:, hs].astype(F32))
        o_ref[:, hs] = (o / l * gate).astype(BF16)


def _fox_sample(q, kb, p, lf_col, lf_row, cache_k, cache_v, cache_lf, *, l_valid):
    bs, past = cache_k.shape[0], cache_k.shape[1]
    C = GDN_CHUNK
    n = q.shape[0]
    blk = lambda b: (b + 1, 0)
    return pl.pallas_call(
        functools.partial(_fox_sample_kernel, l_valid=l_valid),
        grid=(bs,),
        in_specs=[
            pl.BlockSpec((C, WIDTH), blk),
            pl.BlockSpec((C, WIDTH), blk),
            pl.BlockSpec((C, WIDTH), lambda b: (b + 1, PB_FV)),
            pl.BlockSpec((C, WIDTH), lambda b: (b + 1, PB_FOG)),
            pl.BlockSpec((C, LANES), blk),
            pl.BlockSpec((HEADS, C), lambda b: (0, b + 1)),
            pl.BlockSpec((1, past, WIDTH), lambda b: (b, 0, 0)),
            pl.BlockSpec((1, past, WIDTH), lambda b: (b, 0, 0)),
            pl.BlockSpec((1, past, HEADS), lambda b: (b, 0, 0)),
            pl.BlockSpec((1, HEADS, past), lambda b: (b, 0, 0)),
        ],
        out_specs=pl.BlockSpec((C, WIDTH), blk),
        out_shape=jax.ShapeDtypeStruct((n, WIDTH), BF16),
        compiler_params=_cparams(("arbitrary",)),
        name="fox_sample",
    )(q, kb, p, p, lf_col, lf_row, cache_k.reshape(bs, past, WIDTH), cache_v.reshape(bs, past, WIDTH),
      cache_lf, jnp.swapaxes(cache_lf, 1, 2))


def _outproj_kernel(og_ref, of_ref, x_ref, wo_ref, nw_ref, wr_ref, br_ref, x1_ref, xn_ref, cw_ref):
    half = og_ref.shape[1]
    h = _dot(og_ref[...], wo_ref[0:half, :]) + _dot(of_ref[...], wo_ref[half:2 * half, :])
    x1 = x_ref[...] + h
    x1_ref[...] = x1
    xn = x1 * lax.rsqrt(jnp.mean(x1 * x1, axis=-1, keepdims=True) + EPS) * nw_ref[...]
    xn_ref[...] = xn.astype(BF16)

    x_hi = xn.astype(BF16)
    x_lo = (xn - x_hi.astype(F32)).astype(BF16)
    logits = (_dot(x_hi, wr_ref[0]) + _dot(x_lo, wr_ref[0]) + _dot(x_hi, wr_ref[1])) + br_ref[...]

    tm = logits.shape[0]
    lane = _iota2((tm, LANES), 1).astype(F32)
    big = float(LANES)
    gl = jnp.where(lane < N_GROUPS, logits, NEG)
    gmax = jnp.max(gl, axis=-1, keepdims=True)
    gidx = jnp.min(jnp.where(gl == gmax, lane, big), axis=-1, keepdims=True)
    p_top = 1.0 / jnp.sum(jnp.exp(gl - gmax), axis=-1, keepdims=True)
    e = lane - RT_E0
    sel = (e >= 0) & (e < N_EXPERTS) & (jnp.floor(e * (1.0 / EXPERTS_PER_GROUP)) == gidx)
    el = jnp.where(sel, logits, NEG)
    v1 = jnp.max(el, axis=-1, keepdims=True)
    i1 = jnp.min(jnp.where(el == v1, lane, big), axis=-1, keepdims=True)
    el2 = jnp.where(lane == i1, NEG, el)
    v2 = jnp.max(el2, axis=-1, keepdims=True)
    i2 = jnp.min(jnp.where(el2 == v2, lane, big), axis=-1, keepdims=True)
    e2 = jnp.exp(v2 - v1)
    w1 = p_top / (1.0 + e2)
    w2 = p_top * e2 / (1.0 + e2)
    cw_ref[...] = jnp.where(lane == i1, w1, 0.0) + jnp.where(lane == i2, w2, 0.0)


def _outproj(og, of, x, wo, norm_w, wr, br):
    n, d = x.shape
    tm = _pick(n, (512, 384, 256, 128))
    rowblk = lambda i: (i, 0)
    return pl.pallas_call(
        _outproj_kernel,
        grid=(n // tm,),
        in_specs=[
            pl.BlockSpec((tm, WIDTH), rowblk),
            pl.BlockSpec((tm, WIDTH), rowblk),
            pl.BlockSpec((tm, d), rowblk),
            pl.BlockSpec((2 * WIDTH, d), lambda i: (0, 0)),
            pl.BlockSpec((1, d), lambda i: (0, 0)),
            pl.BlockSpec((2, d, LANES), lambda i: (0, 0, 0)),
            pl.BlockSpec((1, LANES), lambda i: (0, 0)),
        ],
        out_specs=[
            pl.BlockSpec((tm, d), rowblk),
            pl.BlockSpec((tm, d), rowblk),
            pl.BlockSpec((tm, LANES), rowblk),
        ],
        out_shape=[
            jax.ShapeDtypeStruct((n, d), F32),
            jax.ShapeDtypeStruct((n, d), BF16),
            jax.ShapeDtypeStruct((n, LANES), F32),
        ],
        compiler_params=_cparams(("arbitrary",)),
        name="outproj_router",
    )(og, of, x, wo, norm_w, wr, br)


def _moe_kernel(xn_ref, cw_ref, x1_ref, w1_ref, w3_ref, w2_ref, nw_ref, y_ref, acc_scr):
    e = pl.program_id(1)

    @pl.when(e == 0)
    def _():
        acc_scr[...] = jnp.zeros_like(acc_scr)

    xn = xn_ref[...]
    hid = _silu(_dot(xn, w1_ref[0])) * _dot(xn, w3_ref[0])
    lane = _iota2(cw_ref.shape, 1)
    c = jnp.sum(jnp.where(lane == RT_E0 + e, cw_ref[...], 0.0), axis=-1, keepdims=True)
    acc_scr[...] += c * _dot(hid.astype(BF16), w2_ref[0])

    @pl.when(e == pl.num_programs(1) - 1)
    def _():
        x2 = x1_ref[...] + acc_scr[...]
        y_ref[...] = x2 * lax.rsqrt(jnp.mean(x2 * x2, axis=-1, keepdims=True) + EPS) * nw_ref[...]


def _moe(xn, cw, x1, w1, w3, w2, norm_w):
    n, d = x1.shape
    de = w1.shape[2]
    tm = _pick(n, (512, 384, 256, 128))
    rowblk = lambda i, e: (i, 0)
    return pl.pallas_call(
        _moe_kernel,
        grid=(n // tm, N_EXPERTS),
        in_specs=[
            pl.BlockSpec((tm, d), rowblk),
            pl.BlockSpec((tm, LANES), rowblk),
            pl.BlockSpec((tm, d), rowblk),
            pl.BlockSpec((1, d, de), lambda i, e: (e, 0, 0)),
            pl.BlockSpec((1, d, de), lambda i, e: (e, 0, 0)),
            pl.BlockSpec((1, de, d), lambda i, e: (e, 0, 0)),
            pl.BlockSpec((1, d), lambda i, e: (0, 0)),
        ],
        out_specs=pl.BlockSpec((tm, d), rowblk),
        out_shape=jax.ShapeDtypeStruct((n, d), F32),
        scratch_shapes=[pltpu.VMEM((tm, d), F32)],
        compiler_params=_cparams(("arbitrary", "arbitrary")),
        name="moe",
    )(xn, cw, x1, w1, w3, w2, norm_w)


def kernel(x_prompt, x_sample, cache_fox_k, cache_fox_v, cache_fox_logf, state_gdn, state_gdn_conv, meta_tokens, norm_mix_w, w_in, gdn_conv_w, gdn_A_log, gdn_dt_bias, gdn_norm_w, fox_q_norm_w, fox_k_norm_w, fox_f_bias, w_out, norm_ffn_w, w_router_group, b_router_group, w_router_expert, b_router_expert, w_gate, w_up, w_down, norm_final_w):
    B, S, D = x_prompt.shape
    BS, LS, _ = x_sample.shape
    C = GDN_CHUNK
    assert w_in.shape[0] == 1, "single-layer step only"
    assert S % C == 0 and LS <= C and N_META <= C and meta_tokens.shape[0] == N_META

    wi = w_in[0]
    o = 0
    parts = {}
    for name, size in (("g_qkv", 3 * WIDTH), ("g_z", WIDTH), ("g_a", HEADS), ("g_b", HEADS),
                       ("f_qkv", 3 * WIDTH), ("f_og", WIDTH), ("f_f", HEADS)):
        parts[name] = wi[:, o:o + size]
        o += size
    wbig = jnp.concatenate([parts["g_qkv"], parts["f_qkv"], parts["g_z"], parts["f_og"]], axis=1).astype(BF16)
    wsm_cols = jnp.concatenate([parts["f_f"], parts["g_a"], parts["g_b"]], axis=1)
    wsm = jnp.pad(wsm_cols, ((0, 0), (0, LANES - 3 * HEADS))).astype(BF16)
    wsmt = jnp.pad(wsm_cols.T, ((0, 32 - 3 * HEADS), (0, 0))).astype(BF16)
    wo = w_out[0].astype(BF16)
    wr32 = jnp.pad(jnp.concatenate([w_router_group[0], w_router_expert[0]], axis=1),
                   ((0, 0), (0, LANES - N_GROUPS - N_EXPERTS)))
    wr_hi = wr32.astype(BF16)
    wr = jnp.stack([wr_hi, (wr32 - wr_hi.astype(F32)).astype(BF16)])
    br = jnp.pad(jnp.concatenate([b_router_group[0], b_router_expert[0]]),
                 (0, LANES - N_GROUPS - N_EXPERTS)).reshape(1, LANES)
    w1 = w_gate[0].astype(BF16)
    w3 = w_up[0].astype(BF16)
    w2 = w_down[0].astype(BF16)
    nmix = norm_mix_w[0].reshape(1, D)
    nffn = norm_ffn_w[0].reshape(1, D)
    nfin = norm_final_w.reshape(1, D)

    x_small = jnp.concatenate([
        jnp.pad(meta_tokens.astype(F32), ((0, C - N_META), (0, 0))),
        jnp.pad(x_sample, ((0, 0), (0, C - LS), (0, 0))).reshape(BS * C, D)], axis=0)
    xp = x_prompt.reshape(B * S, D)

    p_s, psm_s, psmt_s = _inproj(x_small, nmix, wbig, wsm, wsmt)
    p_p, psm_p, psmt_p = _inproj(xp, nmix, wbig, wsm, wsmt)

    s0_s = jnp.concatenate([jnp.zeros((1,) + state_gdn.shape[2:], F32), state_gdn[0]], axis=0)
    cb_s = jnp.concatenate([jnp.zeros((1,) + state_gdn_conv.shape[2:], F32), state_gdn_conv[0]], axis=0)
    gdn_args = (gdn_conv_w[0], gdn_A_log[0], gdn_dt_bias[0], gdn_norm_w[0])
    og_s, st_s, cv_s = _gdn(p_s, psm_s, psmt_s, s0_s, cb_s, *gdn_args, n_seq=1 + BS, l_valid=LS,
                            bcast_state=False)
    og_p, st_p, cv_p = _gdn(p_p, psm_p, psmt_p, st_s[0:1], cv_s[0:1], *gdn_args, n_seq=B, l_valid=C,
                            bcast_state=True)

    fox_args = (fox_q_norm_w[0], fox_k_norm_w[0], fox_f_bias[0])
    zc = jnp.zeros((1, 1, LANES), F32)
    zr = jnp.zeros((1, HEADS, 1), F32)
    k32_s, v32_s, lfc_s, lfr_s, fc_s, fr_s, qt_s, kcat_s, vt_s, q_s, kb_s = _foxprep(
        p_s, psm_s, psmt_s, zc, zr, *fox_args, n_seq=1 + BS, bcast_carry=True, row_major=True)
    c0c = fc_s[N_META - 1:N_META, :].reshape(1, 1, LANES)
    c0r = fr_s[:, N_META - 1:N_META].reshape(1, HEADS, 1)
    k32_p, v32_p, lfc_p, lfr_p, fc_p, fr_p, qt_p, kcat_p, vt_p = _foxprep(
        p_p, psm_p, psmt_p, c0c, c0r, *fox_args, n_seq=B, bcast_carry=True, row_major=False)
    of_p = _fox_prompt(qt_p, kcat_p, vt_p, p_p, kcat_s, vt_s, n_seq=B)
    of_s = _fox_sample(q_s, kb_s, p_s, lfc_s, lfr_s, cache_fox_k[0], cache_fox_v[0], cache_fox_logf[0],
                       l_valid=LS)

    x1_p, xn_p, cw_p = _outproj(og_p, of_p, xp, wo, nffn, wr, br)
    x1_s, xn_s, cw_s = _outproj(og_s, of_s, x_small, wo, nffn, wr, br)
    y_p = _moe(xn_p, cw_p, x1_p, w1, w3, w2, nfin)
    y_s = _moe(xn_s, cw_s, x1_s, w1, w3, w2, nfin)

    def with_meta(meta_rows, real):
        w_ = real.shape[-1]
        m = jnp.broadcast_to(meta_rows[None], (B, N_META, w_))
        return jnp.concatenate([m, real.reshape(B, S, w_)], axis=1)[None]

    def sample_rows(a):
        return a[C:].reshape(BS, C, a.shape[-1])[:, :LS][None]

    hd = (HEADS, HEAD_DIM)
    y_prompt = y_p.reshape(B, S, D)
    y_sample = sample_rows(y_s)[0]
    fk_p = with_meta(k32_s[:N_META], k32_p).reshape(1, B, N_META + S, *hd)
    fv_p = with_meta(v32_s[:N_META], v32_p).reshape(1, B, N_META + S, *hd)
    lf_p = with_meta(lfc_s[:N_META, SM_F:SM_F + HEADS], lfc_p[:, SM_F:SM_F + HEADS])
    fk_s = sample_rows(k32_s).reshape(1, BS, LS, *hd)
    fv_s = sample_rows(v32_s).reshape(1, BS, LS, *hd)
    lf_s = sample_rows(lfc_s[:, SM_F:SM_F + HEADS])
    return (y_prompt, y_sample, fk_p, fv_p, lf_p, st_p[None], cv_p[None],
            fk_s, fv_s, lf_s, st_s[1:][None], cv_s[1:][None])
```

```python
import functools

import jax
import jax.numpy as jnp
from jax import lax
from jax.experimental import pallas as pl
from jax.experimental.pallas import tpu as pltpu

F32 = jnp.float32
BF16 = jnp.bfloat16
EPS = 1e-6
NEG = -1e30
LOG2E = 1.4426950408889634

N_META = 16
HEADS = 8
HEAD_DIM = 128
WIDTH = HEADS * HEAD_DIM
CONV_WIDTH = 4
N_GROUPS = 4
EXPERTS_PER_GROUP = 8
N_EXPERTS = N_GROUPS * EXPERTS_PER_GROUP
GDN_CHUNK = 128
LANES = 128
SM_F, SM_A, SM_B = 0, 8, 16
PB_FQ, PB_FK, PB_FV, PB_GZ, PB_FOG = 3, 4, 5, 6, 7
RT_E0 = N_GROUPS
PAIRS_PER_GROUP = EXPERTS_PER_GROUP * (EXPERTS_PER_GROUP - 1) // 2
N_CLASSES = N_GROUPS * PAIRS_PER_GROUP
RT_CLS, RT_RANK, RT_WA, RT_WB = 0, 1, 2, 3
MOE_TS = 256

VMEM_LIMIT = 56 * 1024 * 1024


def _cparams(sem):
    return pltpu.CompilerParams(dimension_semantics=sem, vmem_limit_bytes=VMEM_LIMIT)


def _pick(n, prefs):
    for p in prefs:
        if n % p == 0:
            return p
    raise ValueError(f"no tile in {prefs} divides {n}")


def _dot(a, b):
    return jnp.dot(a, b, preferred_element_type=F32)


def _dot_nt(a, b):
    return lax.dot_general(a, b, (((1,), (1,)), ((), ())), preferred_element_type=F32)


def _dot_tn(a, b):
    return lax.dot_general(a, b, (((0,), (0,)), ((), ())), preferred_element_type=F32)


def _split3(x):
    x1 = x.astype(BF16)
    r1 = x - x1.astype(F32)
    x2 = r1.astype(BF16)
    x3 = (r1 - x2.astype(F32)).astype(BF16)
    return x1, x2, x3


def _mask_dot(mask_bf16, x):
    x1, x2, x3 = _split3(x)
    return _dot(mask_bf16, x1) + _dot(mask_bf16, x2) + _dot(mask_bf16, x3)


def _dot_mask(x, mask_bf16):
    x1, x2, x3 = _split3(x)
    return _dot(x1, mask_bf16) + _dot(x2, mask_bf16) + _dot(x3, mask_bf16)


def _softplus(x):
    return jnp.maximum(x, 0.0) + jnp.log1p(jnp.exp(-jnp.abs(x)))


def _sigmoid(x):
    return 1.0 / (1.0 + jnp.exp(-x))


def _silu(x):
    return x * _sigmoid(x)


def _iota2(shape, dim):
    return lax.broadcasted_iota(jnp.int32, shape, dim)


def _inproj_kernel(x_ref, nw_ref, wbig_ref, wsm_ref, wsmt_ref, p_ref, psm_ref, psmt_ref, xn_scr):
    @pl.when(pl.program_id(1) == 0)
    def _():
        x = x_ref[...]
        xn = x * lax.rsqrt(jnp.mean(x * x, axis=-1, keepdims=True) + EPS) * nw_ref[...]
        xnb = xn.astype(BF16)
        xn_scr[...] = xnb
        psm_ref[...] = _dot(xnb, wsm_ref[...])
        psmt_ref[...] = _dot_nt(wsmt_ref[...], xnb)

    p_ref[...] = _dot(xn_scr[...], wbig_ref[...]).astype(BF16)


def _inproj(x, norm_w, wbig, wsm, wsmt):
    n, d = x.shape
    tm = _pick(n, (1024, 1152, 512, 384, 256, 128))
    tn = 1024
    nproj = wbig.shape[1]
    return pl.pallas_call(
        _inproj_kernel,
        grid=(n // tm, nproj // tn),
        in_specs=[
            pl.BlockSpec((tm, d), lambda i, j: (i, 0)),
            pl.BlockSpec((1, d), lambda i, j: (0, 0)),
            pl.BlockSpec((d, tn), lambda i, j: (0, j)),
            pl.BlockSpec((d, LANES), lambda i, j: (0, 0)),
            pl.BlockSpec((32, d), lambda i, j: (0, 0)),
        ],
        out_specs=[
            pl.BlockSpec((tm, tn), lambda i, j: (i, j)),
            pl.BlockSpec((tm, LANES), lambda i, j: (i, 0)),
            pl.BlockSpec((32, tm), lambda i, j: (0, i)),
        ],
        out_shape=[
            jax.ShapeDtypeStruct((n, nproj), BF16),
            jax.ShapeDtypeStruct((n, LANES), F32),
            jax.ShapeDtypeStruct((32, n), F32),
        ],
        scratch_shapes=[pltpu.VMEM((tm, d), BF16)],
        compiler_params=_cparams(("arbitrary", "arbitrary")),
        name="inproj",
    )(x, norm_w, wbig, wsm, wsmt)


def _gdn_kernel(qkv_ref, z_ref, sm_ref, smt_ref, s0_ref, cb_ref, convw_ref, alog_c_ref, dtb_c_ref,
                alog_r_ref, dtb_r_ref, normw_ref, o_ref, snew_ref, cnew_ref, s_scr, xp_scr, *, l_valid):
    C = GDN_CHUNK

    @pl.when(pl.program_id(1) == 0)
    def _():
        s_scr[...] = s0_ref[0]
        xp_scr[5:8, :] = cb_ref[0]

    x = qkv_ref[...].astype(F32)
    xp_scr[8:8 + C, :] = x
    w = convw_ref[...]
    conv = (w[0:1] * xp_scr[5:5 + C, :] + w[1:2] * xp_scr[6:6 + C, :]
            + w[2:3] * xp_scr[7:7 + C, :] + w[3:4] * x)
    act = _silu(conv)
    tail = xp_scr[8 + l_valid - 3:8 + l_valid, :]
    xp_scr[5:8, :] = tail
    cnew_ref[0] = tail

    row = _iota2((C, C), 0)
    col = _iota2((C, C), 1)
    incl = row >= col
    strict = row > col
    tril_b = jnp.where(incl, 1.0, 0.0).astype(BF16)
    triu_b = jnp.where(row <= col, 1.0, 0.0).astype(BF16)

    sm = sm_ref[...]
    g_col = -jnp.exp(alog_c_ref[...]) * _softplus(sm + dtb_c_ref[...])
    g_row = -jnp.exp(alog_r_ref[...]) * _softplus(smt_ref[...] + dtb_r_ref[...])
    beta_col = _sigmoid(sm)
    if l_valid < C:
        g_col = jnp.where(_iota2((C, LANES), 0) < l_valid, g_col, 0.0)
        beta_col = jnp.where(_iota2((C, LANES), 0) < l_valid, beta_col, 0.0)
        g_row = jnp.where(_iota2((HEADS, C), 1) < l_valid, g_row, 0.0)
    gl_col = _mask_dot(tril_b, g_col)
    gl_row = _dot_mask(g_row, triu_b)

    inv_levels = []
    sh = 1
    while (1 << sh) < C:
        m = (((row >> sh) & 1) == 1) & ((col >> (sh + 1)) == (row >> (sh + 1))) & (((col >> sh) & 1) == 0)
        inv_levels.append(m)
        sh += 1
    m1 = ((row & 1) == 1) & (col == row - 1)
    eye = jnp.where(row == col, 1.0, 0.0)

    normw = normw_ref[...]
    H = range(HEADS)
    hs = [slice(h * HEAD_DIM, (h + 1) * HEAD_DIM) for h in H]
    gc = [gl_col[:, SM_A + h:SM_A + h + 1] for h in H]
    beta = [beta_col[:, SM_B + h:SM_B + h + 1] for h in H]
    qn, kn, kb, knb = [], [], [], []
    for h in H:
        q = act[:, hs[h]]
        k = act[:, WIDTH + h * HEAD_DIM:WIDTH + (h + 1) * HEAD_DIM]
        qn.append(q * lax.rsqrt(jnp.sum(q * q, axis=-1, keepdims=True) + EPS) * (HEAD_DIM ** -0.5))
        kn.append(k * lax.rsqrt(jnp.sum(k * k, axis=-1, keepdims=True) + EPS))
        kb.append(kn[h] * beta[h])
        knb.append(kn[h].astype(BF16))
    decay = [jnp.exp(jnp.where(incl, gc[h] - gl_row[h:h + 1, :], NEG)) for h in H]
    a_mat = [jnp.where(strict, _dot_nt(kb[h].astype(BF16), knb[h]) * decay[h], 0.0) for h in H]
    qk = [(_dot_nt(qn[h].astype(BF16), knb[h]) * decay[h]).astype(BF16) for h in H]

    t = [eye - jnp.where(m1, a_mat[h], 0.0) for h in H]
    for m in inv_levels:
        tb = [t[h].astype(BF16) for h in H]
        y = [_dot(tb[h], jnp.where(m, a_mat[h], 0.0).astype(BF16)).astype(BF16) for h in H]
        t = [t[h] - _dot(y[h], tb[h]) for h in H]
    tb = [t[h].astype(BF16) for h in H]

    eg = [jnp.exp(gc[h]) for h in H]
    g_last = [gc[h][C - 1:C, :] for h in H]
    s = [s_scr[h] for h in H]
    sb = [s[h].astype(BF16) for h in H]
    r = [(act[:, 2 * WIDTH + h * HEAD_DIM:2 * WIDTH + (h + 1) * HEAD_DIM] * beta[h]
          - _dot((kb[h] * eg[h]).astype(BF16), sb[h])).astype(BF16) for h in H]
    ub = [_dot(tb[h], r[h]).astype(BF16) for h in H]
    o = [_dot((qn[h] * eg[h]).astype(BF16), sb[h]) + _dot(qk[h], ub[h]) for h in H]
    for h in H:
        k_dec = (kn[h] * jnp.exp(g_last[h] - gc[h])).astype(BF16)
        s_scr[h] = s[h] * jnp.exp(g_last[h]) + _dot_tn(k_dec, ub[h])
    for h in H:
        on = o[h] * lax.rsqrt(jnp.mean(o[h] * o[h], axis=-1, keepdims=True) + EPS) * normw
        z = z_ref[:, hs[h]].astype(F32)
        o_ref[:, hs[h]] = (on * _silu(z)).astype(BF16)

    snew_ref[0] = s_scr[...]


def _gdn(p, psm, psmt, s0, cb, conv_w, alog, dtb, norm_w, *, n_seq, l_valid, bcast_state):
    n = p.shape[0]
    C = GDN_CHUNK
    nc = n // (n_seq * C)
    alog_c = jnp.zeros((1, LANES), F32).at[0, SM_A:SM_A + HEADS].set(alog)
    dtb_c = jnp.zeros((1, LANES), F32).at[0, SM_A:SM_A + HEADS].set(dtb)
    st_idx = (lambda b, c: (0, 0, 0, 0)) if bcast_state else (lambda b, c: (b, 0, 0, 0))
    cb_idx = (lambda b, c: (0, 0, 0)) if bcast_state else (lambda b, c: (b, 0, 0))
    return pl.pallas_call(
        functools.partial(_gdn_kernel, l_valid=l_valid),
        grid=(n_seq, nc),
        in_specs=[
            pl.BlockSpec((C, 3 * WIDTH), lambda b, c: (b * nc + c, 0)),
            pl.BlockSpec((C, WIDTH), lambda b, c: (b * nc + c, PB_GZ)),
            pl.BlockSpec((C, LANES), lambda b, c: (b * nc + c, 0)),
            pl.BlockSpec((HEADS, C), lambda b, c: (SM_A // HEADS, b * nc + c)),
            pl.BlockSpec((1, HEADS, HEAD_DIM, HEAD_DIM), st_idx),
            pl.BlockSpec((1, CONV_WIDTH - 1, 3 * WIDTH), cb_idx),
            pl.BlockSpec((CONV_WIDTH, 3 * WIDTH), lambda b, c: (0, 0)),
            pl.BlockSpec((1, LANES), lambda b, c: (0, 0)),
            pl.BlockSpec((1, LANES), lambda b, c: (0, 0)),
            pl.BlockSpec((HEADS, 1), lambda b, c: (0, 0)),
            pl.BlockSpec((HEADS, 1), lambda b, c: (0, 0)),
            pl.BlockSpec((1, HEAD_DIM), lambda b, c: (0, 0)),
        ],
        out_specs=[
            pl.BlockSpec((C, WIDTH), lambda b, c: (b * nc + c, 0)),
            pl.BlockSpec((1, HEADS, HEAD_DIM, HEAD_DIM), lambda b, c: (b, 0, 0, 0)),
            pl.BlockSpec((1, CONV_WIDTH - 1, 3 * WIDTH), lambda b, c: (b, 0, 0)),
        ],
        out_shape=[
            jax.ShapeDtypeStruct((n, WIDTH), BF16),
            jax.ShapeDtypeStruct((n_seq, HEADS, HEAD_DIM, HEAD_DIM), F32),
            jax.ShapeDtypeStruct((n_seq, CONV_WIDTH - 1, 3 * WIDTH), F32),
        ],
        scratch_shapes=[
            pltpu.VMEM((HEADS, HEAD_DIM, HEAD_DIM), F32),
            pltpu.VMEM((C + 8, 3 * WIDTH), F32),
        ],
        compiler_params=_cparams(("arbitrary", "arbitrary")),
        name="gdn",
    )(p, p, psm, psmt, s0, cb, conv_w, alog_c, dtb_c, alog.reshape(HEADS, 1), dtb.reshape(HEADS, 1),
      norm_w.reshape(1, HEAD_DIM))


def _foxprep_kernel(qkv_ref, sm_ref, smt_ref, c0c_ref, c0r_ref, qw_ref, kw_ref, fb_c_ref, fb_r_ref,
                    k32_ref, v32_ref, lfc_ref, lfr_ref, fc_ref, fr_ref, qt_ref, kcat_ref, vt_ref, *rest,
                    row_major):
    if row_major:
        q_ref, kb_ref, cc_scr, cr_scr = rest
    else:
        cc_scr, cr_scr = rest
    tm = qkv_ref.shape[0]

    @pl.when(pl.program_id(1) == 0)
    def _():
        cc_scr[...] = c0c_ref[0]
        cr_scr[...] = c0r_ref[0]

    lf_col = -_softplus(-(sm_ref[...] + fb_c_ref[...]))
    lf_row = -_softplus(-(smt_ref[...] + fb_r_ref[...]))
    lfc_ref[...] = lf_col
    lfr_ref[...] = lf_row
    row = _iota2((tm, tm), 0)
    col = _iota2((tm, tm), 1)
    tril_b = jnp.where(row >= col, 1.0, 0.0).astype(BF16)
    triu_b = jnp.where(row <= col, 1.0, 0.0).astype(BF16)
    f_col = cc_scr[...] + _mask_dot(tril_b, lf_col)
    f_row = cr_scr[...] + _dot_mask(lf_row, triu_b)
    fc_ref[...] = f_col
    fr_ref[...] = f_row
    cc_scr[...] = f_col[tm - 1:tm, :]
    cr_scr[...] = f_row[:, tm - 1:tm]

    qw = qw_ref[...]
    kw = kw_ref[...]
    lane = _iota2((tm, HEAD_DIM), 1)
    sub = _iota2((16, tm), 0)
    zeros_t = jnp.zeros((HEAD_DIM - 16, tm), BF16)
    for h in range(HEADS):
        hs = slice(h * HEAD_DIM, (h + 1) * HEAD_DIM)
        q = qkv_ref[:, hs].astype(F32)
        k = qkv_ref[:, WIDTH + h * HEAD_DIM:WIDTH + (h + 1) * HEAD_DIM].astype(F32)
        v = qkv_ref[:, 2 * WIDTH + h * HEAD_DIM:2 * WIDTH + (h + 1) * HEAD_DIM].astype(F32)
        qn = q * lax.rsqrt(jnp.mean(q * q, axis=-1, keepdims=True) + EPS) * qw
        kn = k * lax.rsqrt(jnp.mean(k * k, axis=-1, keepdims=True) + EPS) * kw
        k32_ref[:, hs] = kn
        v32_ref[:, hs] = v
        if row_major:
            q_ref[:, hs] = (qn * (HEAD_DIM ** -0.5)).astype(BF16)
            kb_ref[:, hs] = kn.astype(BF16)
        base = 2 * h * HEAD_DIM
        qt_ref[base:base + HEAD_DIM, :] = (qn * (LOG2E * HEAD_DIM ** -0.5)).T.astype(BF16)
        r1, r2, r3 = _split3(f_row[h:h + 1, :] * LOG2E)
        aug_q = jnp.where(sub == 0, r1.astype(F32), jnp.where(sub == 1, r2.astype(F32), jnp.where(
            sub == 2, r3.astype(F32), jnp.where(sub < 6, 1.0, 0.0))))
        qt_ref[base + HEAD_DIM:base + HEAD_DIM + 16, :] = aug_q.astype(BF16)
        qt_ref[base + HEAD_DIM + 16:base + 2 * HEAD_DIM, :] = zeros_t
        kcat_ref[:, base:base + HEAD_DIM] = kn.astype(BF16)
        c1, c2, c3 = _split3(f_col[:, SM_F + h:SM_F + h + 1] * LOG2E)
        aug_k = jnp.where(lane < 3, 1.0, jnp.where(lane == 3, -c1.astype(F32), jnp.where(
            lane == 4, -c2.astype(F32), jnp.where(lane == 5, -c3.astype(F32), 0.0))))
        kcat_ref[:, base + HEAD_DIM:base + 2 * HEAD_DIM] = aug_k.astype(BF16)
        vt_ref[hs, :] = v.T.astype(BF16)


def _foxprep(p, psm, psmt, c0c, c0r, q_norm_w, k_norm_w, f_bias, *, n_seq, bcast_carry, row_major):
    n = p.shape[0]
    rows = n // n_seq
    tm = _pick(rows, (512, 256, 128))
    nt = rows // tm
    fb_c = jnp.zeros((1, LANES), F32).at[0, SM_F:SM_F + HEADS].set(f_bias)
    c_idx = (lambda b, t: (0, 0, 0)) if bcast_carry else (lambda b, t: (b, 0, 0))
    rowblk = lambda b, t: (b * nt + t, 0)
    colblk = lambda b, t: (0, b * nt + t)
    out_specs = [
        pl.BlockSpec((tm, WIDTH), rowblk),
        pl.BlockSpec((tm, WIDTH), rowblk),
        pl.BlockSpec((tm, LANES), rowblk),
        pl.BlockSpec((HEADS, tm), colblk),
        pl.BlockSpec((tm, LANES), rowblk),
        pl.BlockSpec((HEADS, tm), colblk),
        pl.BlockSpec((2 * WIDTH, tm), colblk),
        pl.BlockSpec((tm, 2 * WIDTH), rowblk),
        pl.BlockSpec((WIDTH, tm), colblk),
    ]
    out_shape = [
        jax.ShapeDtypeStruct((n, WIDTH), F32),
        jax.ShapeDtypeStruct((n, WIDTH), F32),
        jax.ShapeDtypeStruct((n, LANES), F32),
        jax.ShapeDtypeStruct((HEADS, n), F32),
        jax.ShapeDtypeStruct((n, LANES), F32),
        jax.ShapeDtypeStruct((HEADS, n), F32),
        jax.ShapeDtypeStruct((2 * WIDTH, n), BF16),
        jax.ShapeDtypeStruct((n, 2 * WIDTH), BF16),
        jax.ShapeDtypeStruct((WIDTH, n), BF16),
    ]
    if row_major:
        out_specs += [pl.BlockSpec((tm, WIDTH), rowblk), pl.BlockSpec((tm, WIDTH), rowblk)]
        out_shape += [jax.ShapeDtypeStruct((n, WIDTH), BF16),
                      jax.ShapeDtypeStruct((n, WIDTH), BF16)]
    return pl.pallas_call(
        functools.partial(_foxprep_kernel, row_major=row_major),
        grid=(n_seq, nt),
        in_specs=[
            pl.BlockSpec((tm, 3 * WIDTH), lambda b, t: (b * nt + t, 1)),
            pl.BlockSpec((tm, LANES), rowblk),
            pl.BlockSpec((HEADS, tm), lambda b, t: (SM_F // HEADS, b * nt + t)),
            pl.BlockSpec((1, 1, LANES), c_idx),
            pl.BlockSpec((1, HEADS, 1), c_idx),
            pl.BlockSpec((1, HEAD_DIM), lambda b, t: (0, 0)),
            pl.BlockSpec((1, HEAD_DIM), lambda b, t: (0, 0)),
            pl.BlockSpec((1, LANES), lambda b, t: (0, 0)),
            pl.BlockSpec((HEADS, 1), lambda b, t: (0, 0)),
        ],
        out_specs=out_specs,
        out_shape=out_shape,
        scratch_shapes=[pltpu.VMEM((1, LANES), F32), pltpu.VMEM((HEADS, 1), F32)],
        compiler_params=_cparams(("arbitrary", "arbitrary")),
        name="foxprep",
    )(p, psm, psmt, c0c, c0r, q_norm_w.reshape(1, HEAD_DIM), k_norm_w.reshape(1, HEAD_DIM), fb_c,
      f_bias.reshape(HEADS, 1))


FOX_UNIT = 256


def _fox_kernel(qi_ref, ki_ref, qt_ref, kcat_ref, vt_ref, og_ref, km_ref, vtm_ref, o_ref, m_scr, l_scr, acc_scr):
    pair = pl.program_id(1)
    qi = qi_ref[pair]
    ki = ki_ref[pair]
    tq = qt_ref.shape[1]
    tk = kcat_ref.shape[0]
    U = FOX_UNIT
    H = range(HEADS)

    def unit(qh, k_rows, vt_cols, mask):
        qs = slice(qh * U, (qh + 1) * U)
        s = [_dot(k_rows(h), qt_ref[2 * h * HEAD_DIM:2 * (h + 1) * HEAD_DIM, qs]) for h in H]
        if mask is not None:
            s = [jnp.where(mask, s[h], NEG) for h in H]
        m_old = [m_scr[h:h + 1, qs] for h in H]
        m_new = [jnp.maximum(m_old[h], jnp.max(s[h], axis=0, keepdims=True)) for h in H]
        alpha = [jnp.exp2(m_old[h] - m_new[h]) for h in H]
        p = [jnp.exp2(s[h] - m_new[h]) for h in H]
        for h in H:
            l_scr[h:h + 1, qs] = alpha[h] * l_scr[h:h + 1, qs] + jnp.sum(p[h], axis=0, keepdims=True)
            m_scr[h:h + 1, qs] = m_new[h]
        for h in H:
            acc_scr[h, :, qs] = alpha[h] * acc_scr[h, :, qs] + _dot(vt_cols(h), p[h].astype(BF16))

    def k_unit(ku):
        return lambda h: kcat_ref[ku * U:(ku + 1) * U, 2 * h * HEAD_DIM:2 * (h + 1) * HEAD_DIM]

    def vt_unit(ku):
        return lambda h: vt_ref[h * HEAD_DIM:(h + 1) * HEAD_DIM, ku * U:(ku + 1) * U]

    @pl.when(ki == 0)
    def _():
        m_scr[...] = jnp.full(m_scr.shape, NEG, F32)
        l_scr[...] = jnp.zeros(l_scr.shape, F32)
        acc_scr[...] = jnp.zeros(acc_scr.shape, F32)
        tmeta = km_ref.shape[0]
        mask = _iota2((tmeta, U), 0) < N_META
        for qh in range(tq // U):
            unit(qh, lambda h: km_ref[:, 2 * h * HEAD_DIM:2 * (h + 1) * HEAD_DIM],
                 lambda h: vtm_ref[h * HEAD_DIM:(h + 1) * HEAD_DIM, :], mask)

    @pl.when(ki < qi)
    def _():
        for qh in range(tq // U):
            for ku in range(tk // U):
                unit(qh, k_unit(ku), vt_unit(ku), None)

    @pl.when(ki == qi)
    def _():
        diag = _iota2((U, U), 0) <= _iota2((U, U), 1)
        for qh in range(tq // U):
            for ku in range(qh + 1):
                unit(qh, k_unit(ku), vt_unit(ku), diag if ku == qh else None)
        for h in H:
            hs = slice(h * HEAD_DIM, (h + 1) * HEAD_DIM)
            gate = _sigmoid(og_ref[:, hs].astype(F32))
            o_ref[:, hs] = ((acc_scr[h] / l_scr[h:h + 1, :]).T * gate).astype(BF16)


def _fox_prompt(qt, kcat, vt, p, kcat_small, vt_small, *, n_seq):
    n = p.shape[0]
    rows = n // n_seq
    tq = _pick(rows, (512, 256))
    nq = rows // tq
    pairs = [(i, j) for i in range(nq) for j in range(i + 1)]
    qi = jnp.asarray([a for a, _ in pairs], jnp.int32)
    ki = jnp.asarray([b for _, b in pairs], jnp.int32)
    grid_spec = pltpu.PrefetchScalarGridSpec(
        num_scalar_prefetch=2,
        grid=(n_seq, len(pairs)),
        in_specs=[
            pl.BlockSpec((2 * WIDTH, tq), lambda b, t, qi_r, ki_r: (0, b * nq + qi_r[t])),
            pl.BlockSpec((tq, 2 * WIDTH), lambda b, t, qi_r, ki_r: (b * nq + ki_r[t], 0)),
            pl.BlockSpec((WIDTH, tq), lambda b, t, qi_r, ki_r: (0, b * nq + ki_r[t])),
            pl.BlockSpec((tq, WIDTH), lambda b, t, qi_r, ki_r: (b * nq + qi_r[t], PB_FOG)),
            pl.BlockSpec((GDN_CHUNK, 2 * WIDTH), lambda b, t, qi_r, ki_r: (0, 0)),
            pl.BlockSpec((WIDTH, GDN_CHUNK), lambda b, t, qi_r, ki_r: (0, 0)),
        ],
        out_specs=pl.BlockSpec((tq, WIDTH), lambda b, t, qi_r, ki_r: (b * nq + qi_r[t], 0)),
        scratch_shapes=[
            pltpu.VMEM((HEADS, tq), F32),
            pltpu.VMEM((HEADS, tq), F32),
            pltpu.VMEM((HEADS, HEAD_DIM, tq), F32),
        ],
    )
    return pl.pallas_call(
        _fox_kernel,
        grid_spec=grid_spec,
        out_shape=jax.ShapeDtypeStruct((n, WIDTH), BF16),
        compiler_params=_cparams(("arbitrary", "arbitrary")),
        name="fox_prompt",
    )(qi, ki, qt, kcat, vt, p, kcat_small, vt_small)


def _fox_sample_kernel(q_ref, kn_ref, vn_ref, og_ref, lfc_ref, lfr_ref, ck_ref, cv_ref, clc_ref, clr_ref,
                       o_ref, *, l_valid):
    C = q_ref.shape[0]
    P = ck_ref.shape[1]
    rowp = _iota2((P, P), 0)
    colp = _iota2((P, P), 1)
    triu_p = jnp.where(rowp <= colp, 1.0, 0.0).astype(BF16)
    row = _iota2((C, C), 0)
    col = _iota2((C, C), 1)
    tril_b = jnp.where(row >= col, 1.0, 0.0).astype(BF16)
    triu_b = jnp.where(row <= col, 1.0, 0.0).astype(BF16)

    f_cache = _dot_mask(clr_ref[0], triu_p)
    carry_r = f_cache[:, P - 1:P]
    carry_c = jnp.sum(clc_ref[0], axis=0, keepdims=True)
    f_new_r = carry_r + _dot_mask(lfr_ref[...], triu_b)
    f_new_c = carry_c + _mask_dot(tril_b, lfc_ref[...])[:, SM_F:SM_F + HEADS]
    mask_new = (col <= row) & (col < l_valid)

    for h in range(HEADS):
        hs = slice(h * HEAD_DIM, (h + 1) * HEAD_DIM)
        q = q_ref[:, hs]
        fq = f_new_c[:, h:h + 1]
        s_c = _dot_nt(q, ck_ref[0, :, hs].astype(BF16)) + (fq - f_cache[h:h + 1, :])
        s_n = jnp.where(mask_new, _dot_nt(q, kn_ref[:, hs]) + (fq - f_new_r[h:h + 1, :]), NEG)
        m = jnp.maximum(jnp.max(s_c, axis=-1, keepdims=True), jnp.max(s_n, axis=-1, keepdims=True))
        p_c = jnp.exp(s_c - m)
        p_n = jnp.exp(s_n - m)
        l = jnp.sum(p_c, axis=-1, keepdims=True) + jnp.sum(p_n, axis=-1, keepdims=True)
        o = _dot(p_c.astype(BF16), cv_ref[0, :, hs].astype(BF16)) + _dot(p_n.astype(BF16), vn_ref[:, hs])
        gate = _sigmoid(og_ref[:, hs].astype(F32))
        o_ref[:, hs] = (o / l * gate).astype(BF16)


def _fox_sample(q, kb, p, lf_col, lf_row, cache_k, cache_v, cache_lf, *, l_valid):
    bs, past = cache_k.shape[0], cache_k.shape[1]
    C = GDN_CHUNK
    n = q.shape[0]
    blk = lambda b: (b + 1, 0)
    return pl.pallas_call(
        functools.partial(_fox_sample_kernel, l_valid=l_valid),
        grid=(bs,),
        in_specs=[
            pl.BlockSpec((C, WIDTH), blk),
            pl.BlockSpec((C, WIDTH), blk),
            pl.BlockSpec((C, WIDTH), lambda b: (b + 1, PB_FV)),
            pl.BlockSpec((C, WIDTH), lambda b: (b + 1, PB_FOG)),
            pl.BlockSpec((C, LANES), blk),
            pl.BlockSpec((HEADS, C), lambda b: (0, b + 1)),
            pl.BlockSpec((1, past, WIDTH), lambda b: (b, 0, 0)),
            pl.BlockSpec((1, past, WIDTH), lambda b: (b, 0, 0)),
            pl.BlockSpec((1, past, HEADS), lambda b: (b, 0, 0)),
            pl.BlockSpec((1, HEADS, past), lambda b: (b, 0, 0)),
        ],
        out_specs=pl.BlockSpec((C, WIDTH), lambda b: (b, 0)),
        out_shape=jax.ShapeDtypeStruct((bs * C, WIDTH), BF16),
        compiler_params=_cparams(("arbitrary",)),
        name="fox_sample",
    )(q, kb, p, p, lf_col, lf_row, cache_k.reshape(bs, past, WIDTH), cache_v.reshape(bs, past, WIDTH),
      cache_lf, jnp.swapaxes(cache_lf, 1, 2))


def _outproj_kernel(og_ref, of_ref, x_ref, wo_ref, nw_ref, wr_ref, br_ref, cnt0_ref, xr_ref, rt_ref, cnt_ref,
                    cnt_scr):
    d = x_ref.shape[1]

    @pl.when(pl.program_id(0) == 0)
    def _():
        cnt_scr[...] = cnt0_ref[...]

    half = og_ref.shape[1]
    h = _dot(og_ref[...], wo_ref[0:half, :]) + _dot(of_ref[...], wo_ref[half:2 * half, :])
    x1 = x_ref[...] + h
    xr_ref[:, 0:d] = x1
    xn = x1 * lax.rsqrt(jnp.mean(x1 * x1, axis=-1, keepdims=True) + EPS) * nw_ref[...]

    x_hi = xn.astype(BF16)
    x_lo = (xn - x_hi.astype(F32)).astype(BF16)
    logits = (_dot(x_hi, wr_ref[0]) + _dot(x_lo, wr_ref[0]) + _dot(x_hi, wr_ref[1])) + br_ref[...]

    tm = logits.shape[0]
    lane = _iota2((tm, LANES), 1).astype(F32)
    big = float(LANES)
    gl = jnp.where(lane < N_GROUPS, logits, NEG)
    gmax = jnp.max(gl, axis=-1, keepdims=True)
    gidx = jnp.min(jnp.where(gl == gmax, lane, big), axis=-1, keepdims=True)
    p_top = 1.0 / jnp.sum(jnp.exp(gl - gmax), axis=-1, keepdims=True)
    e = lane - RT_E0
    sel = (e >= 0) & (e < N_EXPERTS) & (jnp.floor(e * (1.0 / EXPERTS_PER_GROUP)) == gidx)
    el = jnp.where(sel, logits, NEG)
    v1 = jnp.max(el, axis=-1, keepdims=True)
    i1 = jnp.min(jnp.where(el == v1, lane, big), axis=-1, keepdims=True)
    el2 = jnp.where(lane == i1, NEG, el)
    v2 = jnp.max(el2, axis=-1, keepdims=True)
    i2 = jnp.min(jnp.where(el2 == v2, lane, big), axis=-1, keepdims=True)
    e2 = jnp.exp(v2 - v1)
    w1 = p_top / (1.0 + e2)
    w2 = p_top * e2 / (1.0 + e2)

    ex1 = i1 - RT_E0
    ex2 = i2 - RT_E0
    first = ex1 < ex2
    ea = jnp.where(first, ex1, ex2)
    eb = jnp.where(first, ex2, ex1)
    wa = jnp.where(first, w1, w2)
    wb = jnp.where(first, w2, w1)
    la = ea - gidx * EXPERTS_PER_GROUP
    lb = eb - gidx * EXPERTS_PER_GROUP
    cls = gidx * PAIRS_PER_GROUP + la * (2 * EXPERTS_PER_GROUP - 1 - la) * 0.5 + (lb - la - 1.0)

    onehot = lane == cls
    oh = jnp.where(onehot, 1.0, 0.0)
    strict_b = jnp.where(_iota2((tm, tm), 0) > _iota2((tm, tm), 1), 1.0, 0.0).astype(BF16)
    before = cnt_scr[...] + _dot(strict_b, oh.astype(BF16))
    rank = jnp.sum(jnp.where(onehot, before, 0.0), axis=-1, keepdims=True)
    cnt_scr[...] += jnp.sum(oh, axis=0, keepdims=True)
    cnt_ref[...] = cnt_scr[...]
    route = jnp.where(lane == RT_CLS, cls, jnp.where(lane == RT_RANK, rank, jnp.where(
        lane == RT_WA, wa, jnp.where(lane == RT_WB, wb, 0.0))))
    rt_ref[...] = route
    xr_ref[:, d:d + LANES] = route


def _outproj(og, of, x, wo, norm_w, wr, br, cnt0):
    n, d = x.shape
    tm = _pick(n, (512, 384, 256, 128))
    rowblk = lambda i: (i, 0)
    return pl.pallas_call(
        _outproj_kernel,
        grid=(n // tm,),
        in_specs=[
            pl.BlockSpec((tm, WIDTH), rowblk),
            pl.BlockSpec((tm, WIDTH), rowblk),
            pl.BlockSpec((tm, d), rowblk),
            pl.BlockSpec((2 * WIDTH, d), lambda i: (0, 0)),
            pl.BlockSpec((1, d), lambda i: (0, 0)),
            pl.BlockSpec((2, d, LANES), lambda i: (0, 0, 0)),
            pl.BlockSpec((1, LANES), lambda i: (0, 0)),
            pl.BlockSpec((1, LANES), lambda i: (0, 0)),
        ],
        out_specs=[
            pl.BlockSpec((tm, d + LANES), rowblk),
            pl.BlockSpec((tm, LANES), rowblk),
            pl.BlockSpec((1, LANES), lambda i: (0, 0)),
        ],
        out_shape=[
            jax.ShapeDtypeStruct((n, d + LANES), F32),
            jax.ShapeDtypeStruct((n, LANES), F32),
            jax.ShapeDtypeStruct((1, LANES), F32),
        ],
        scratch_shapes=[pltpu.VMEM((1, LANES), F32)],
        compiler_params=_cparams(("arbitrary",)),
        name="outproj_router",
    )(og, of, x, wo, norm_w, wr, br, cnt0)


def _pos_rows(tm):
    return -(-(-(-tm // LANES)) // 8) * 8


def _tile_positions(pos, tm):
    nt = pos.shape[0] // tm
    rows = _pos_rows(tm)
    p = jnp.pad(pos.reshape(nt, tm), ((0, 0), (0, rows * LANES - tm)))
    return p.reshape(nt * rows, LANES)


def _row_copy_loops(tm, pos_smem, make_copy):
    def start(t, carry):
        make_copy(t, pos_smem[t // LANES, t % LANES]).start()
        return carry

    def wait(t, carry):
        make_copy(0, 0).wait()
        return carry

    lax.fori_loop(0, tm, start, 0, unroll=8)
    lax.fori_loop(0, tm, wait, 0, unroll=8)


def _dispatch_kernel(ends_ref, xr_ref, pos_hbm, *rest, zero_fill):
    if zero_fill:
        xs_hbm, pos_smem, zbuf, sem_idx, sem_fill, sem_rows = rest
    else:
        _, xs_hbm, pos_smem, sem_idx, sem_rows = rest
    i = pl.program_id(0)
    tm = xr_ref.shape[0]
    rows = pos_smem.shape[0]
    idx_cp = pltpu.make_async_copy(pos_hbm.at[pl.ds(pl.multiple_of(i * rows, 8), rows)], pos_smem, sem_idx)
    idx_cp.start()

    if zero_fill:
        @pl.when(i == 0)
        def _():
            zbuf[...] = jnp.zeros(zbuf.shape, F32)

            def fill(start):
                def body(c, carry):
                    lo = jnp.where(c == 0, 0, ends_ref[jnp.maximum(c - 1, 0)])
                    hi = ends_ref[c]

                    @pl.when(hi > lo)
                    def _():
                        cp = pltpu.make_async_copy(
                            zbuf, xs_hbm.at[pl.ds(pl.multiple_of(hi - MOE_TS, MOE_TS), MOE_TS)], sem_fill)
                        if start:
                            cp.start()
                        else:
                            cp.wait()
                    return carry
                lax.fori_loop(0, N_CLASSES, body, 0)

            fill(True)
            fill(False)

    idx_cp.wait()
    _row_copy_loops(tm, pos_smem, lambda t, p: pltpu.make_async_copy(
        xr_ref.at[pl.ds(t, 1)], xs_hbm.at[pl.ds(p, 1)], sem_rows))


def _dispatch(ends, xr, pos, xs, *, n_sorted):
    n, dw = xr.shape
    tm = _pick(n, (1024, 1152, 512, 384, 256, 128))
    rows = _pos_rows(tm)
    zero_fill = xs is None
    any_spec = pl.BlockSpec(memory_space=pl.ANY)
    in_specs = [pl.BlockSpec((tm, dw), lambda i, ends_r: (i, 0)), any_spec]
    args = [xr, _tile_positions(pos, tm)]
    scratch = [pltpu.SMEM((rows, LANES), jnp.int32)]
    if zero_fill:
        scratch += [pltpu.VMEM((MOE_TS, dw), F32), pltpu.SemaphoreType.DMA, pltpu.SemaphoreType.DMA,
                    pltpu.SemaphoreType.DMA]
        aliases = {}
    else:
        in_specs.append(any_spec)
        args.append(xs)
        scratch += [pltpu.SemaphoreType.DMA, pltpu.SemaphoreType.DMA]
        aliases = {3: 0}
    return pl.pallas_call(
        functools.partial(_dispatch_kernel, zero_fill=zero_fill),
        grid_spec=pltpu.PrefetchScalarGridSpec(
            num_scalar_prefetch=1, grid=(n // tm,), in_specs=in_specs, out_specs=any_spec,
            scratch_shapes=scratch),
        out_shape=jax.ShapeDtypeStruct((n_sorted, dw), F32),
        input_output_aliases=aliases,
        compiler_params=pltpu.CompilerParams(dimension_semantics=("arbitrary",), vmem_limit_bytes=VMEM_LIMIT,
                                             has_side_effects=True),
        name="moe_dispatch",
    )(ends, *args)


def _ffn_kernel(ta_ref, tb_ref, nv_ref, xs_ref, w1a_ref, w3a_ref, w2a_ref, w1b_ref, w3b_ref, w2b_ref,
                nffn_ref, nfin_ref, ys_ref):
    @pl.when(pl.program_id(0) < nv_ref[0])
    def _():
        d = ys_ref.shape[1]
        x1 = xs_ref[:, 0:d]
        route = xs_ref[:, d:d + LANES]
        xn = (x1 * lax.rsqrt(jnp.mean(x1 * x1, axis=-1, keepdims=True) + EPS) * nffn_ref[...]).astype(BF16)

        def expert(w1_ref, w3_ref, w2_ref):
            hid = _silu(_dot(xn, w1_ref[0])) * _dot(xn, w3_ref[0])
            return _dot(hid.astype(BF16), w2_ref[0])

        x2 = (x1 + route[:, RT_WA:RT_WA + 1] * expert(w1a_ref, w3a_ref, w2a_ref)
              + route[:, RT_WB:RT_WB + 1] * expert(w1b_ref, w3b_ref, w2b_ref))
        ys_ref[...] = x2 * lax.rsqrt(jnp.mean(x2 * x2, axis=-1, keepdims=True) + EPS) * nfin_ref[...]


def _ffn(tile_a, tile_b, n_valid, xs, w1, w3, w2, nffn, nfin):
    ns, dw = xs.shape
    d = dw - LANES
    de = w1.shape[2]
    row = lambda t, ta, tb, nv: (jnp.minimum(t, nv[0] - 1), 0)
    wa = lambda t, ta, tb, nv: (ta[t], 0, 0)
    wb = lambda t, ta, tb, nv: (tb[t], 0, 0)
    const = lambda t, ta, tb, nv: (0, 0)
    return pl.pallas_call(
        _ffn_kernel,
        grid_spec=pltpu.PrefetchScalarGridSpec(
            num_scalar_prefetch=3,
            grid=(ns // MOE_TS,),
            in_specs=[
                pl.BlockSpec((MOE_TS, dw), row),
                pl.BlockSpec((1, d, de), wa), pl.BlockSpec((1, d, de), wa), pl.BlockSpec((1, de, d), wa),
                pl.BlockSpec((1, d, de), wb), pl.BlockSpec((1, d, de), wb), pl.BlockSpec((1, de, d), wb),
                pl.BlockSpec((1, d), const), pl.BlockSpec((1, d), const),
            ],
            out_specs=pl.BlockSpec((MOE_TS, d), row),
        ),
        out_shape=jax.ShapeDtypeStruct((ns, d), F32),
        compiler_params=_cparams(("arbitrary",)),
        name="moe_ffn",
    )(tile_a, tile_b, n_valid, xs, w1, w3, w2, w1, w3, w2, nffn, nfin)


def _unsort_kernel(pos_hbm, ys_hbm, y_ref, pos_smem, sem_idx, sem_rows):
    i = pl.program_id(0)
    tm = y_ref.shape[0]
    rows = pos_smem.shape[0]
    idx_cp = pltpu.make_async_copy(pos_hbm.at[pl.ds(pl.multiple_of(i * rows, 8), rows)], pos_smem, sem_idx)
    idx_cp.start()
    idx_cp.wait()
    _row_copy_loops(tm, pos_smem, lambda t, p: pltpu.make_async_copy(
        ys_hbm.at[pl.ds(p, 1)], y_ref.at[pl.ds(t, 1)], sem_rows))


def _unsort(ys, pos):
    n = pos.shape[0]
    d = ys.shape[1]
    tm = _pick(n, (1024, 1152, 512, 384, 256, 128))
    any_spec = pl.BlockSpec(memory_space=pl.ANY)
    return pl.pallas_call(
        _unsort_kernel,
        grid=(n // tm,),
        in_specs=[any_spec, any_spec],
        out_specs=pl.BlockSpec((tm, d), lambda i: (i, 0)),
        out_shape=jax.ShapeDtypeStruct((n, d), F32),
        scratch_shapes=[pltpu.SMEM((_pos_rows(tm), LANES), jnp.int32), pltpu.SemaphoreType.DMA,
                        pltpu.SemaphoreType.DMA],
        compiler_params=_cparams(("arbitrary",)),
        name="moe_unsort",
    )(_tile_positions(pos, tm), ys)


def _pair_tables():
    a, b = [], []
    for g in range(N_GROUPS):
        for la in range(EXPERTS_PER_GROUP):
            for lb in range(la + 1, EXPERTS_PER_GROUP):
                a.append(g * EXPERTS_PER_GROUP + la)
                b.append(g * EXPERTS_PER_GROUP + lb)
    return jnp.asarray(a, jnp.int32), jnp.asarray(b, jnp.int32)


def _moe(xr_list, rt_list, cnt, w1, w3, w2, nffn, nfin):
    n_total = sum(x.shape[0] for x in xr_list)
    n_sorted = (-(-n_total // MOE_TS) + N_CLASSES) * MOE_TS
    counts = cnt[0, :N_CLASSES].astype(jnp.int32)
    padded = (counts + MOE_TS - 1) // MOE_TS * MOE_TS
    ends = jnp.cumsum(padded)
    offs = ends - padded
    pos_list = [offs[rt[:, RT_CLS].astype(jnp.int32)] + rt[:, RT_RANK].astype(jnp.int32) for rt in rt_list]
    tile_start = jnp.arange(n_sorted // MOE_TS, dtype=jnp.int32) * MOE_TS
    tile_cls = jnp.minimum(jnp.searchsorted(ends, tile_start, side="right"), N_CLASSES - 1)
    pair_a, pair_b = _pair_tables()
    n_valid = (ends[N_CLASSES - 1] // MOE_TS).reshape(1)
    xs = None
    for xr, pos in zip(xr_list, pos_list):
        xs = _dispatch(ends, xr, pos, xs, n_sorted=n_sorted)
    ys = _ffn(pair_a[tile_cls], pair_b[tile_cls], n_valid, xs, w1, w3, w2, nffn, nfin)
    return [_unsort(ys, pos) for pos in pos_list]


def kernel(x_prompt, x_sample, cache_fox_k, cache_fox_v, cache_fox_logf, state_gdn, state_gdn_conv, meta_tokens, norm_mix_w, w_in, gdn_conv_w, gdn_A_log, gdn_dt_bias, gdn_norm_w, fox_q_norm_w, fox_k_norm_w, fox_f_bias, w_out, norm_ffn_w, w_router_group, b_router_group, w_router_expert, b_router_expert, w_gate, w_up, w_down, norm_final_w):
    B, S, D = x_prompt.shape
    BS, LS, _ = x_sample.shape
    C = GDN_CHUNK
    assert w_in.shape[0] == 1, "single-layer step only"
    assert S % C == 0 and LS <= C and N_META <= C and meta_tokens.shape[0] == N_META

    wi = w_in[0]
    o = 0
    parts = {}
    for name, size in (("g_qkv", 3 * WIDTH), ("g_z", WIDTH), ("g_a", HEADS), ("g_b", HEADS),
                       ("f_qkv", 3 * WIDTH), ("f_og", WIDTH), ("f_f", HEADS)):
        parts[name] = wi[:, o:o + size]
        o += size
    wbig = jnp.concatenate([parts["g_qkv"], parts["f_qkv"], parts["g_z"], parts["f_og"]], axis=1).astype(BF16)
    wsm_cols = jnp.concatenate([parts["f_f"], parts["g_a"], parts["g_b"]], axis=1)
    wsm = jnp.pad(wsm_cols, ((0, 0), (0, LANES - 3 * HEADS))).astype(BF16)
    wsmt = jnp.pad(wsm_cols.T, ((0, 32 - 3 * HEADS), (0, 0))).astype(BF16)
    wo = w_out[0].astype(BF16)
    wr32 = jnp.pad(jnp.concatenate([w_router_group[0], w_router_expert[0]], axis=1),
                   ((0, 0), (0, LANES - N_GROUPS - N_EXPERTS)))
    wr_hi = wr32.astype(BF16)
    wr = jnp.stack([wr_hi, (wr32 - wr_hi.astype(F32)).astype(BF16)])
    br = jnp.pad(jnp.concatenate([b_router_group[0], b_router_expert[0]]),
                 (0, LANES - N_GROUPS - N_EXPERTS)).reshape(1, LANES)
    w1 = w_gate[0].astype(BF16)
    w3 = w_up[0].astype(BF16)
    w2 = w_down[0].astype(BF16)
    nmix = norm_mix_w[0].reshape(1, D)
    nffn = norm_ffn_w[0].reshape(1, D)
    nfin = norm_final_w.reshape(1, D)

    x_small = jnp.concatenate([
        jnp.pad(meta_tokens.astype(F32), ((0, C - N_META), (0, 0))),
        jnp.pad(x_sample, ((0, 0), (0, C - LS), (0, 0))).reshape(BS * C, D)], axis=0)
    xp = x_prompt.reshape(B * S, D)

    p_s, psm_s, psmt_s = _inproj(x_small, nmix, wbig, wsm, wsmt)
    p_p, psm_p, psmt_p = _inproj(xp, nmix, wbig, wsm, wsmt)

    s0_s = jnp.concatenate([jnp.zeros((1,) + state_gdn.shape[2:], F32), state_gdn[0]], axis=0)
    cb_s = jnp.concatenate([jnp.zeros((1,) + state_gdn_conv.shape[2:], F32), state_gdn_conv[0]], axis=0)
    gdn_args = (gdn_conv_w[0], gdn_A_log[0], gdn_dt_bias[0], gdn_norm_w[0])
    og_s, st_s, cv_s = _gdn(p_s, psm_s, psmt_s, s0_s, cb_s, *gdn_args, n_seq=1 + BS, l_valid=LS,
                            bcast_state=False)
    og_p, st_p, cv_p = _gdn(p_p, psm_p, psmt_p, st_s[0:1], cv_s[0:1], *gdn_args, n_seq=B, l_valid=C,
                            bcast_state=True)

    fox_args = (fox_q_norm_w[0], fox_k_norm_w[0], fox_f_bias[0])
    zc = jnp.zeros((1, 1, LANES), F32)
    zr = jnp.zeros((1, HEADS, 1), F32)
    k32_s, v32_s, lfc_s, lfr_s, fc_s, fr_s, qt_s, kcat_s, vt_s, q_s, kb_s = _foxprep(
        p_s, psm_s, psmt_s, zc, zr, *fox_args, n_seq=1 + BS, bcast_carry=True, row_major=True)
    c0c = fc_s[N_META - 1:N_META, :].reshape(1, 1, LANES)
    c0r = fr_s[:, N_META - 1:N_META].reshape(1, HEADS, 1)
    k32_p, v32_p, lfc_p, lfr_p, fc_p, fr_p, qt_p, kcat_p, vt_p = _foxprep(
        p_p, psm_p, psmt_p, c0c, c0r, *fox_args, n_seq=B, bcast_carry=True, row_major=False)
    of_p = _fox_prompt(qt_p, kcat_p, vt_p, p_p, kcat_s, vt_s, n_seq=B)
    of_s = _fox_sample(q_s, kb_s, p_s, lfc_s, lfr_s, cache_fox_k[0], cache_fox_v[0], cache_fox_logf[0],
                       l_valid=LS)

    xr_p, rt_p, cnt_p = _outproj(og_p, of_p, xp, wo, nffn, wr, br, jnp.zeros((1, LANES), F32))
    xr_s, rt_s, cnt = _outproj(og_s[C:], of_s, x_small[C:], wo, nffn, wr, br, cnt_p)
    y_p, y_s = _moe([xr_p, xr_s], [rt_p, rt_s], cnt, w1, w3, w2, nffn, nfin)

    def with_meta(meta_rows, real):
        w_ = real.shape[-1]
        m = jnp.broadcast_to(meta_rows[None], (B, N_META, w_))
        return jnp.concatenate([m, real.reshape(B, S, w_)], axis=1)[None]

    def sample_rows(a):
        return a[C:].reshape(BS, C, a.shape[-1])[:, :LS][None]

    hd = (HEADS, HEAD_DIM)
    y_prompt = y_p.reshape(B, S, D)
    y_sample = y_s.reshape(BS, C, D)[:, :LS]
    fk_p = with_meta(k32_s[:N_META], k32_p).reshape(1, B, N_META + S, *hd)
    fv_p = with_meta(v32_s[:N_META], v32_p).reshape(1, B, N_META + S, *hd)
    lf_p = with_meta(lfc_s[:N_META, SM_F:SM_F + HEADS], lfc_p[:, SM_F:SM_F + HEADS])
    fk_s = sample_rows(k32_s).reshape(1, BS, LS, *hd)
    fv_s = sample_rows(v32_s).reshape(1, BS, LS, *hd)
    lf_s = sample_rows(lfc_s[:, SM_F:SM_F + HEADS])
    return (y_prompt, y_sample, fk_p, fv_p, lf_p, st_p[None], cv_p[None],
            fk_s, fv_s, lf_s, st_s[1:][None], cv_s[1:][None])
```

```python
import functools

import jax
import jax.numpy as jnp
from jax import lax
from jax.experimental import pallas as pl
from jax.experimental.pallas import tpu as pltpu

F32 = jnp.float32
BF16 = jnp.bfloat16
EPS = 1e-6
NEG = -1e30
LOG2E = 1.4426950408889634

N_META = 16
HEADS = 8
HEAD_DIM = 128
WIDTH = HEADS * HEAD_DIM
CONV_WIDTH = 4
N_GROUPS = 4
EXPERTS_PER_GROUP = 8
N_EXPERTS = N_GROUPS * EXPERTS_PER_GROUP
GDN_CHUNK = 128
LANES = 128
SM_F, SM_A, SM_B = 0, 8, 16
PB_FQ, PB_FK, PB_FV, PB_GZ, PB_FOG = 3, 4, 5, 6, 7
RT_E0 = N_GROUPS
PAIRS_PER_GROUP = EXPERTS_PER_GROUP * (EXPERTS_PER_GROUP - 1) // 2
N_CLASSES = N_GROUPS * PAIRS_PER_GROUP
RT_CLS, RT_RANK, RT_WA, RT_WB = 0, 1, 2, 3
MOE_TS = 256

VMEM_LIMIT = 56 * 1024 * 1024


def _cparams(sem):
    return pltpu.CompilerParams(dimension_semantics=sem, vmem_limit_bytes=VMEM_LIMIT)


def _pick(n, prefs):
    for p in prefs:
        if n % p == 0:
            return p
    raise ValueError(f"no tile in {prefs} divides {n}")


def _dot(a, b):
    return jnp.dot(a, b, preferred_element_type=F32)


def _dot_nt(a, b):
    return lax.dot_general(a, b, (((1,), (1,)), ((), ())), preferred_element_type=F32)


def _dot_tn(a, b):
    return lax.dot_general(a, b, (((0,), (0,)), ((), ())), preferred_element_type=F32)


def _split3(x):
    x1 = x.astype(BF16)
    r1 = x - x1.astype(F32)
    x2 = r1.astype(BF16)
    x3 = (r1 - x2.astype(F32)).astype(BF16)
    return x1, x2, x3


def _mask_dot(mask_bf16, x):
    x1, x2, x3 = _split3(x)
    return _dot(mask_bf16, x1) + _dot(mask_bf16, x2) + _dot(mask_bf16, x3)


def _dot_mask(x, mask_bf16):
    x1, x2, x3 = _split3(x)
    return _dot(x1, mask_bf16) + _dot(x2, mask_bf16) + _dot(x3, mask_bf16)


def _softplus(x):
    return jnp.maximum(x, 0.0) + jnp.log1p(jnp.exp(-jnp.abs(x)))


def _sigmoid(x):
    return 1.0 / (1.0 + jnp.exp(-x))


def _silu(x):
    return x * _sigmoid(x)


def _iota2(shape, dim):
    return lax.broadcasted_iota(jnp.int32, shape, dim)


def _inproj_kernel(x_ref, nw_ref, wbig_ref, wsm_ref, wsmt_ref, p_ref, psm_ref, psmt_ref, xn_scr):
    @pl.when(pl.program_id(1) == 0)
    def _():
        x = x_ref[...]
        xn = x * lax.rsqrt(jnp.mean(x * x, axis=-1, keepdims=True) + EPS) * nw_ref[...]
        xnb = xn.astype(BF16)
        xn_scr[...] = xnb
        psm_ref[...] = _dot(xnb, wsm_ref[...])
        psmt_ref[...] = _dot_nt(wsmt_ref[...], xnb)

    p_ref[...] = _dot(xn_scr[...], wbig_ref[...]).astype(BF16)


def _inproj(x, norm_w, wbig, wsm, wsmt):
    n, d = x.shape
    tm = _pick(n, (1024, 1152, 512, 384, 256, 128))
    tn = 1024
    nproj = wbig.shape[1]
    return pl.pallas_call(
        _inproj_kernel,
        grid=(n // tm, nproj // tn),
        in_specs=[
            pl.BlockSpec((tm, d), lambda i, j: (i, 0)),
            pl.BlockSpec((1, d), lambda i, j: (0, 0)),
            pl.BlockSpec((d, tn), lambda i, j: (0, j)),
            pl.BlockSpec((d, LANES), lambda i, j: (0, 0)),
            pl.BlockSpec((32, d), lambda i, j: (0, 0)),
        ],
        out_specs=[
            pl.BlockSpec((tm, tn), lambda i, j: (i, j)),
            pl.BlockSpec((tm, LANES), lambda i, j: (i, 0)),
            pl.BlockSpec((32, tm), lambda i, j: (0, i)),
        ],
        out_shape=[
            jax.ShapeDtypeStruct((n, nproj), BF16),
            jax.ShapeDtypeStruct((n, LANES), F32),
            jax.ShapeDtypeStruct((32, n), F32),
        ],
        scratch_shapes=[pltpu.VMEM((tm, d), BF16)],
        compiler_params=_cparams(("arbitrary", "arbitrary")),
        name="inproj",
    )(x, norm_w, wbig, wsm, wsmt)


def _gdn_kernel(qkv_ref, z_ref, sm_ref, smt_ref, s0_ref, cb_ref, convw_ref, alog_c_ref, dtb_c_ref,
                alog_r_ref, dtb_r_ref, normw_ref, o_ref, snew_ref, cnew_ref, s_scr, xp_scr, conv_scr, *,
                l_valid):
    C = GDN_CHUNK

    @pl.when(pl.program_id(1) == 0)
    def _():
        s_scr[...] = s0_ref[0]
        xp_scr[5:8, :] = cb_ref[0]

    row = _iota2((C, C), 0)
    col = _iota2((C, C), 1)

    xb = qkv_ref[...]
    x = xb.astype(F32)
    xp_scr[8:16, :] = x[0:8]
    w = convw_ref[...]
    shift_b = jnp.concatenate(
        [jnp.where(col == row - i, 1.0, 0.0).astype(BF16) for i in (1, 2, 3)], axis=0)
    sh = _dot(shift_b, xb)
    conv_scr[...] = w[3:4] * x + w[2:3] * sh[0:C] + w[1:2] * sh[C:2 * C] + w[0:1] * sh[2 * C:3 * C]
    conv_scr[0:8, :] = (w[0:1] * xp_scr[5:13, :] + w[1:2] * xp_scr[6:14, :]
                        + w[2:3] * xp_scr[7:15, :] + w[3:4] * x[0:8])
    act = _silu(conv_scr[...])
    tail = x[l_valid - 3:l_valid, :]
    xp_scr[5:8, :] = tail
    cnew_ref[0] = tail

    incl = row >= col
    strict = row > col
    tril_b = jnp.where(incl, 1.0, 0.0).astype(BF16)
    triu_b = jnp.where(row <= col, 1.0, 0.0).astype(BF16)

    sm = sm_ref[...]
    g_col = -jnp.exp(alog_c_ref[...]) * _softplus(sm + dtb_c_ref[...])
    g_row = -jnp.exp(alog_r_ref[...]) * _softplus(smt_ref[...] + dtb_r_ref[...])
    beta_col = _sigmoid(sm)
    if l_valid < C:
        g_col = jnp.where(_iota2((C, LANES), 0) < l_valid, g_col, 0.0)
        beta_col = jnp.where(_iota2((C, LANES), 0) < l_valid, beta_col, 0.0)
        g_row = jnp.where(_iota2((HEADS, C), 1) < l_valid, g_row, 0.0)
    gl_col = _mask_dot(tril_b, g_col)
    gl_row = _dot_mask(g_row, triu_b)

    inv_levels = []
    sh = 1
    while (1 << sh) < C:
        m = (((row >> sh) & 1) == 1) & ((col >> (sh + 1)) == (row >> (sh + 1))) & (((col >> sh) & 1) == 0)
        inv_levels.append(m)
        sh += 1
    m1 = ((row & 1) == 1) & (col == row - 1)
    eye = jnp.where(row == col, 1.0, 0.0)

    normw = normw_ref[...]
    H = range(HEADS)
    hs = [slice(h * HEAD_DIM, (h + 1) * HEAD_DIM) for h in H]
    gc = [gl_col[:, SM_A + h:SM_A + h + 1] for h in H]
    beta = [beta_col[:, SM_B + h:SM_B + h + 1] for h in H]
    qn, kn, kb, knb = [], [], [], []
    for h in H:
        q = act[:, hs[h]]
        k = act[:, WIDTH + h * HEAD_DIM:WIDTH + (h + 1) * HEAD_DIM]
        qn.append(q * lax.rsqrt(jnp.sum(q * q, axis=-1, keepdims=True) + EPS) * (HEAD_DIM ** -0.5))
        kn.append(k * lax.rsqrt(jnp.sum(k * k, axis=-1, keepdims=True) + EPS))
        kb.append(kn[h] * beta[h])
        knb.append(kn[h].astype(BF16))
    decay = [jnp.exp(jnp.where(incl, gc[h] - gl_row[h:h + 1, :], NEG)) for h in H]
    a_mat = [jnp.where(strict, _dot_nt(kb[h].astype(BF16), knb[h]) * decay[h], 0.0) for h in H]
    qk = [(_dot_nt(qn[h].astype(BF16), knb[h]) * decay[h]).astype(BF16) for h in H]

    t = [eye - jnp.where(m1, a_mat[h], 0.0) for h in H]
    for m in inv_levels:
        tb = [t[h].astype(BF16) for h in H]
        y = [_dot(tb[h], jnp.where(m, a_mat[h], 0.0).astype(BF16)).astype(BF16) for h in H]
        t = [t[h] - _dot(y[h], tb[h]) for h in H]
    tb = [t[h].astype(BF16) for h in H]

    eg = [jnp.exp(gc[h]) for h in H]
    g_last = [gc[h][C - 1:C, :] for h in H]
    s = [s_scr[h] for h in H]
    sb = [s[h].astype(BF16) for h in H]
    r = [(act[:, 2 * WIDTH + h * HEAD_DIM:2 * WIDTH + (h + 1) * HEAD_DIM] * beta[h]
          - _dot((kb[h] * eg[h]).astype(BF16), sb[h])).astype(BF16) for h in H]
    ub = [_dot(tb[h], r[h]).astype(BF16) for h in H]
    o = [_dot((qn[h] * eg[h]).astype(BF16), sb[h]) + _dot(qk[h], ub[h]) for h in H]
    for h in H:
        k_dec = (kn[h] * jnp.exp(g_last[h] - gc[h])).astype(BF16)
        s_scr[h] = s[h] * jnp.exp(g_last[h]) + _dot_tn(k_dec, ub[h])
    for h in H:
        on = o[h] * lax.rsqrt(jnp.mean(o[h] * o[h], axis=-1, keepdims=True) + EPS) * normw
        z = z_ref[:, hs[h]].astype(F32)
        o_ref[:, hs[h]] = (on * _silu(z)).astype(BF16)

    snew_ref[0] = s_scr[...]


def _gdn(p, psm, psmt, s0, cb, conv_w, alog, dtb, norm_w, *, n_seq, l_valid, bcast_state):
    n = p.shape[0]
    C = GDN_CHUNK
    nc = n // (n_seq * C)
    alog_c = jnp.zeros((1, LANES), F32).at[0, SM_A:SM_A + HEADS].set(alog)
    dtb_c = jnp.zeros((1, LANES), F32).at[0, SM_A:SM_A + HEADS].set(dtb)
    st_idx = (lambda b, c: (0, 0, 0, 0)) if bcast_state else (lambda b, c: (b, 0, 0, 0))
    cb_idx = (lambda b, c: (0, 0, 0)) if bcast_state else (lambda b, c: (b, 0, 0))
    return pl.pallas_call(
        functools.partial(_gdn_kernel, l_valid=l_valid),
        grid=(n_seq, nc),
        in_specs=[
            pl.BlockSpec((C, 3 * WIDTH), lambda b, c: (b * nc + c, 0)),
            pl.BlockSpec((C, WIDTH), lambda b, c: (b * nc + c, PB_GZ)),
            pl.BlockSpec((C, LANES), lambda b, c: (b * nc + c, 0)),
            pl.BlockSpec((HEADS, C), lambda b, c: (SM_A // HEADS, b * nc + c)),
            pl.BlockSpec((1, HEADS, HEAD_DIM, HEAD_DIM), st_idx),
            pl.BlockSpec((1, CONV_WIDTH - 1, 3 * WIDTH), cb_idx),
            pl.BlockSpec((CONV_WIDTH, 3 * WIDTH), lambda b, c: (0, 0)),
            pl.BlockSpec((1, LANES), lambda b, c: (0, 0)),
            pl.BlockSpec((1, LANES), lambda b, c: (0, 0)),
            pl.BlockSpec((HEADS, 1), lambda b, c: (0, 0)),
            pl.BlockSpec((HEADS, 1), lambda b, c: (0, 0)),
            pl.BlockSpec((1, HEAD_DIM), lambda b, c: (0, 0)),
        ],
        out_specs=[
            pl.BlockSpec((C, WIDTH), lambda b, c: (b * nc + c, 0)),
            pl.BlockSpec((1, HEADS, HEAD_DIM, HEAD_DIM), lambda b, c: (b, 0, 0, 0)),
            pl.BlockSpec((1, CONV_WIDTH - 1, 3 * WIDTH), lambda b, c: (b, 0, 0)),
        ],
        out_shape=[
            jax.ShapeDtypeStruct((n, WIDTH), BF16),
            jax.ShapeDtypeStruct((n_seq, HEADS, HEAD_DIM, HEAD_DIM), F32),
            jax.ShapeDtypeStruct((n_seq, CONV_WIDTH - 1, 3 * WIDTH), F32),
        ],
        scratch_shapes=[
            pltpu.VMEM((HEADS, HEAD_DIM, HEAD_DIM), F32),
            pltpu.VMEM((16, 3 * WIDTH), F32),
            pltpu.VMEM((C, 3 * WIDTH), F32),
        ],
        compiler_params=_cparams(("arbitrary", "arbitrary")),
        name="gdn",
    )(p, p, psm, psmt, s0, cb, conv_w, alog_c, dtb_c, alog.reshape(HEADS, 1), dtb.reshape(HEADS, 1),
      norm_w.reshape(1, HEAD_DIM))


def _foxprep_kernel(qkv_ref, sm_ref, smt_ref, c0c_ref, c0r_ref, qw_ref, kw_ref, fb_c_ref, fb_r_ref,
                    k32_ref, v32_ref, lfc_ref, lfr_ref, fc_ref, fr_ref, qt_ref, kcat_ref, vt_ref, *rest,
                    row_major):
    if row_major:
        q_ref, kb_ref, cc_scr, cr_scr = rest
    else:
        cc_scr, cr_scr = rest
    tm = qkv_ref.shape[0]

    @pl.when(pl.program_id(1) == 0)
    def _():
        cc_scr[...] = c0c_ref[0]
        cr_scr[...] = c0r_ref[0]

    lf_col = -_softplus(-(sm_ref[...] + fb_c_ref[...]))
    lf_row = -_softplus(-(smt_ref[...] + fb_r_ref[...]))
    lfc_ref[...] = lf_col
    lfr_ref[...] = lf_row
    row = _iota2((tm, tm), 0)
    col = _iota2((tm, tm), 1)
    tril_b = jnp.where(row >= col, 1.0, 0.0).astype(BF16)
    triu_b = jnp.where(row <= col, 1.0, 0.0).astype(BF16)
    f_col = cc_scr[...] + _mask_dot(tril_b, lf_col)
    f_row = cr_scr[...] + _dot_mask(lf_row, triu_b)
    fc_ref[...] = f_col
    fr_ref[...] = f_row
    cc_scr[...] = f_col[tm - 1:tm, :]
    cr_scr[...] = f_row[:, tm - 1:tm]

    qw = qw_ref[...]
    kw = kw_ref[...]
    lane = _iota2((tm, HEAD_DIM), 1)
    sub = _iota2((16, tm), 0)
    zeros_t = jnp.zeros((HEAD_DIM - 16, tm), BF16)
    for h in range(HEADS):
        hs = slice(h * HEAD_DIM, (h + 1) * HEAD_DIM)
        q = qkv_ref[:, hs].astype(F32)
        k = qkv_ref[:, WIDTH + h * HEAD_DIM:WIDTH + (h + 1) * HEAD_DIM].astype(F32)
        v = qkv_ref[:, 2 * WIDTH + h * HEAD_DIM:2 * WIDTH + (h + 1) * HEAD_DIM].astype(F32)
        qn = q * lax.rsqrt(jnp.mean(q * q, axis=-1, keepdims=True) + EPS) * qw
        kn = k * lax.rsqrt(jnp.mean(k * k, axis=-1, keepdims=True) + EPS) * kw
        k32_ref[:, hs] = kn
        v32_ref[:, hs] = v
        if row_major:
            q_ref[:, hs] = (qn * (HEAD_DIM ** -0.5)).astype(BF16)
            kb_ref[:, hs] = kn.astype(BF16)
        base = 2 * h * HEAD_DIM
        qt_ref[base:base + HEAD_DIM, :] = (qn * (LOG2E * HEAD_DIM ** -0.5)).T.astype(BF16)
        r1, r2, r3 = _split3(f_row[h:h + 1, :] * LOG2E)
        aug_q = jnp.where(sub == 0, r1.astype(F32), jnp.where(sub == 1, r2.astype(F32), jnp.where(
            sub == 2, r3.astype(F32), jnp.where(sub < 6, 1.0, 0.0))))
        qt_ref[base + HEAD_DIM:base + HEAD_DIM + 16, :] = aug_q.astype(BF16)
        qt_ref[base + HEAD_DIM + 16:base + 2 * HEAD_DIM, :] = zeros_t
        kcat_ref[:, base:base + HEAD_DIM] = kn.astype(BF16)
        c1, c2, c3 = _split3(f_col[:, SM_F + h:SM_F + h + 1] * LOG2E)
        aug_k = jnp.where(lane < 3, 1.0, jnp.where(lane == 3, -c1.astype(F32), jnp.where(
            lane == 4, -c2.astype(F32), jnp.where(lane == 5, -c3.astype(F32), 0.0))))
        kcat_ref[:, base + HEAD_DIM:base + 2 * HEAD_DIM] = aug_k.astype(BF16)
        vt_ref[hs, :] = v.T.astype(BF16)


def _foxprep(p, psm, psmt, c0c, c0r, q_norm_w, k_norm_w, f_bias, *, n_seq, bcast_carry, row_major):
    n = p.shape[0]
    rows = n // n_seq
    tm = _pick(rows, (512, 256, 128))
    nt = rows // tm
    fb_c = jnp.zeros((1, LANES), F32).at[0, SM_F:SM_F + HEADS].set(f_bias)
    c_idx = (lambda b, t: (0, 0, 0)) if bcast_carry else (lambda b, t: (b, 0, 0))
    rowblk = lambda b, t: (b * nt + t, 0)
    colblk = lambda b, t: (0, b * nt + t)
    out_specs = [
        pl.BlockSpec((tm, WIDTH), rowblk),
        pl.BlockSpec((tm, WIDTH), rowblk),
        pl.BlockSpec((tm, LANES), rowblk),
        pl.BlockSpec((HEADS, tm), colblk),
        pl.BlockSpec((tm, LANES), rowblk),
        pl.BlockSpec((HEADS, tm), colblk),
        pl.BlockSpec((2 * WIDTH, tm), colblk),
        pl.BlockSpec((tm, 2 * WIDTH), rowblk),
        pl.BlockSpec((WIDTH, tm), colblk),
    ]
    out_shape = [
        jax.ShapeDtypeStruct((n, WIDTH), F32),
        jax.ShapeDtypeStruct((n, WIDTH), F32),
        jax.ShapeDtypeStruct((n, LANES), F32),
        jax.ShapeDtypeStruct((HEADS, n), F32),
        jax.ShapeDtypeStruct((n, LANES), F32),
        jax.ShapeDtypeStruct((HEADS, n), F32),
        jax.ShapeDtypeStruct((2 * WIDTH, n), BF16),
        jax.ShapeDtypeStruct((n, 2 * WIDTH), BF16),
        jax.ShapeDtypeStruct((WIDTH, n), BF16),
    ]
    if row_major:
        out_specs += [pl.BlockSpec((tm, WIDTH), rowblk), pl.BlockSpec((tm, WIDTH), rowblk)]
        out_shape += [jax.ShapeDtypeStruct((n, WIDTH), BF16),
                      jax.ShapeDtypeStruct((n, WIDTH), BF16)]
    return pl.pallas_call(
        functools.partial(_foxprep_kernel, row_major=row_major),
        grid=(n_seq, nt),
        in_specs=[
            pl.BlockSpec((tm, 3 * WIDTH), lambda b, t: (b * nt + t, 1)),
            pl.BlockSpec((tm, LANES), rowblk),
            pl.BlockSpec((HEADS, tm), lambda b, t: (SM_F // HEADS, b * nt + t)),
            pl.BlockSpec((1, 1, LANES), c_idx),
            pl.BlockSpec((1, HEADS, 1), c_idx),
            pl.BlockSpec((1, HEAD_DIM), lambda b, t: (0, 0)),
            pl.BlockSpec((1, HEAD_DIM), lambda b, t: (0, 0)),
            pl.BlockSpec((1, LANES), lambda b, t: (0, 0)),
            pl.BlockSpec((HEADS, 1), lambda b, t: (0, 0)),
        ],
        out_specs=out_specs,
        out_shape=out_shape,
        scratch_shapes=[pltpu.VMEM((1, LANES), F32), pltpu.VMEM((HEADS, 1), F32)],
        compiler_params=_cparams(("arbitrary", "arbitrary")),
        name="foxprep",
    )(p, psm, psmt, c0c, c0r, q_norm_w.reshape(1, HEAD_DIM), k_norm_w.reshape(1, HEAD_DIM), fb_c,
      f_bias.reshape(HEADS, 1))


FOX_UNIT = 256


def _fox_kernel(qi_ref, ki_ref, qt_ref, kcat_ref, vt_ref, og_ref, km_ref, vtm_ref, o_ref, m_scr, l_scr, acc_scr):
    pair = pl.program_id(1)
    qi = qi_ref[pair]
    ki = ki_ref[pair]
    tq = qt_ref.shape[1]
    tk = kcat_ref.shape[0]
    U = FOX_UNIT
    H = range(HEADS)

    def unit(qh, k_rows, vt_cols, mask):
        qs = slice(qh * U, (qh + 1) * U)
        s = [_dot(k_rows(h), qt_ref[2 * h * HEAD_DIM:2 * (h + 1) * HEAD_DIM, qs]) for h in H]
        if mask is not None:
            s = [jnp.where(mask, s[h], NEG) for h in H]
        m_old = [m_scr[h:h + 1, qs] for h in H]
        m_new = [jnp.maximum(m_old[h], jnp.max(s[h], axis=0, keepdims=True)) for h in H]
        alpha = [jnp.exp2(m_old[h] - m_new[h]) for h in H]
        p = [jnp.exp2(s[h] - m_new[h]) for h in H]
        for h in H:
            l_scr[h:h + 1, qs] = alpha[h] * l_scr[h:h + 1, qs] + jnp.sum(p[h], axis=0, keepdims=True)
            m_scr[h:h + 1, qs] = m_new[h]
        for h in H:
            acc_scr[h, :, qs] = alpha[h] * acc_scr[h, :, qs] + _dot(vt_cols(h), p[h].astype(BF16))

    def k_unit(ku):
        return lambda h: kcat_ref[ku * U:(ku + 1) * U, 2 * h * HEAD_DIM:2 * (h + 1) * HEAD_DIM]

    def vt_unit(ku):
        return lambda h: vt_ref[h * HEAD_DIM:(h + 1) * HEAD_DIM, ku * U:(ku + 1) * U]

    @pl.when(ki == 0)
    def _():
        m_scr[...] = jnp.full(m_scr.shape, NEG, F32)
        l_scr[...] = jnp.zeros(l_scr.shape, F32)
        acc_scr[...] = jnp.zeros(acc_scr.shape, F32)
        tmeta = km_ref.shape[0]
        mask = _iota2((tmeta, U), 0) < N_META
        for qh in range(tq // U):
            unit(qh, lambda h: km_ref[:, 2 * h * HEAD_DIM:2 * (h + 1) * HEAD_DIM],
                 lambda h: vtm_ref[h * HEAD_DIM:(h + 1) * HEAD_DIM, :], mask)

    @pl.when(ki < qi)
    def _():
        for qh in range(tq // U):
            for ku in range(tk // U):
                unit(qh, k_unit(ku), vt_unit(ku), None)

    @pl.when(ki == qi)
    def _():
        diag = _iota2((U, U), 0) <= _iota2((U, U), 1)
        for qh in range(tq // U):
            for ku in range(qh + 1):
                unit(qh, k_unit(ku), vt_unit(ku), diag if ku == qh else None)
        for h in H:
            hs = slice(h * HEAD_DIM, (h + 1) * HEAD_DIM)
            gate = _sigmoid(og_ref[:, hs].astype(F32))
            o_ref[:, hs] = ((acc_scr[h] / l_scr[h:h + 1, :]).T * gate).astype(BF16)


def _fox_prompt(qt, kcat, vt, p, kcat_small, vt_small, *, n_seq):
    n = p.shape[0]
    rows = n // n_seq
    tq = _pick(rows, (512, 256))
    nq = rows // tq
    pairs = [(i, j) for i in range(nq) for j in range(i + 1)]
    qi = jnp.asarray([a for a, _ in pairs], jnp.int32)
    ki = jnp.asarray([b for _, b in pairs], jnp.int32)
    grid_spec = pltpu.PrefetchScalarGridSpec(
        num_scalar_prefetch=2,
        grid=(n_seq, len(pairs)),
        in_specs=[
            pl.BlockSpec((2 * WIDTH, tq), lambda b, t, qi_r, ki_r: (0, b * nq + qi_r[t])),
            pl.BlockSpec((tq, 2 * WIDTH), lambda b, t, qi_r, ki_r: (b * nq + ki_r[t], 0)),
            pl.BlockSpec((WIDTH, tq), lambda b, t, qi_r, ki_r: (0, b * nq + ki_r[t])),
            pl.BlockSpec((tq, WIDTH), lambda b, t, qi_r, ki_r: (b * nq + qi_r[t], PB_FOG)),
            pl.BlockSpec((GDN_CHUNK, 2 * WIDTH), lambda b, t, qi_r, ki_r: (0, 0)),
            pl.BlockSpec((WIDTH, GDN_CHUNK), lambda b, t, qi_r, ki_r: (0, 0)),
        ],
        out_specs=pl.BlockSpec((tq, WIDTH), lambda b, t, qi_r, ki_r: (b * nq + qi_r[t], 0)),
        scratch_shapes=[
            pltpu.VMEM((HEADS, tq), F32),
            pltpu.VMEM((HEADS, tq), F32),
            pltpu.VMEM((HEADS, HEAD_DIM, tq), F32),
        ],
    )
    return pl.pallas_call(
        _fox_kernel,
        grid_spec=grid_spec,
        out_shape=jax.ShapeDtypeStruct((n, WIDTH), BF16),
        compiler_params=_cparams(("arbitrary", "arbitrary")),
        name="fox_prompt",
    )(qi, ki, qt, kcat, vt, p, kcat_small, vt_small)


def _fox_sample_kernel(q_ref, kn_ref, vn_ref, og_ref, lfc_ref, lfr_ref, ck_ref, cv_ref, clc_ref, clr_ref,
                       o_ref, *, l_valid):
    C = q_ref.shape[0]
    P = ck_ref.shape[1]
    rowp = _iota2((P, P), 0)
    colp = _iota2((P, P), 1)
    triu_p = jnp.where(rowp <= colp, 1.0, 0.0).astype(BF16)
    row = _iota2((C, C), 0)
    col = _iota2((C, C), 1)
    tril_b = jnp.where(row >= col, 1.0, 0.0).astype(BF16)
    triu_b = jnp.where(row <= col, 1.0, 0.0).astype(BF16)

    f_cache = _dot_mask(clr_ref[0], triu_p)
    carry_r = f_cache[:, P - 1:P]
    carry_c = jnp.sum(clc_ref[0], axis=0, keepdims=True)
    f_new_r = carry_r + _dot_mask(lfr_ref[...], triu_b)
    f_new_c = carry_c + _mask_dot(tril_b, lfc_ref[...])[:, SM_F:SM_F + HEADS]
    mask_new = (col <= row) & (col < l_valid)

    for h in range(HEADS):
        hs = slice(h * HEAD_DIM, (h + 1) * HEAD_DIM)
        q = q_ref[:, hs]
        fq = f_new_c[:, h:h + 1]
        s_c = _dot_nt(q, ck_ref[0, :, hs].astype(BF16)) + (fq - f_cache[h:h + 1, :])
        s_n = jnp.where(mask_new, _dot_nt(q, kn_ref[:, hs]) + (fq - f_new_r[h:h + 1, :]), NEG)
        m = jnp.maximum(jnp.max(s_c, axis=-1, keepdims=True), jnp.max(s_n, axis=-1, keepdims=True))
        p_c = jnp.exp(s_c - m)
        p_n = jnp.exp(s_n - m)
        l = jnp.sum(p_c, axis=-1, keepdims=True) + jnp.sum(p_n, axis=-1, keepdims=True)
        o = _dot(p_c.astype(BF16), cv_ref[0, :, hs].astype(BF16)) + _dot(p_n.astype(BF16), vn_ref[:, hs])
        gate = _sigmoid(og_ref[:, hs].astype(F32))
        o_ref[:, hs] = (o / l * gate).astype(BF16)


def _fox_sample(q, kb, p, lf_col, lf_row, cache_k, cache_v, cache_lf, *, l_valid):
    bs, past = cache_k.shape[0], cache_k.shape[1]
    C = GDN_CHUNK
    n = q.shape[0]
    blk = lambda b: (b + 1, 0)
    return pl.pallas_call(
        functools.partial(_fox_sample_kernel, l_valid=l_valid),
        grid=(bs,),
        in_specs=[
            pl.BlockSpec((C, WIDTH), blk),
            pl.BlockSpec((C, WIDTH), blk),
            pl.BlockSpec((C, WIDTH), lambda b: (b + 1, PB_FV)),
            pl.BlockSpec((C, WIDTH), lambda b: (b + 1, PB_FOG)),
            pl.BlockSpec((C, LANES), blk),
            pl.BlockSpec((HEADS, C), lambda b: (0, b + 1)),
            pl.BlockSpec((1, past, WIDTH), lambda b: (b, 0, 0)),
            pl.BlockSpec((1, past, WIDTH), lambda b: (b, 0, 0)),
            pl.BlockSpec((1, past, HEADS), lambda b: (b, 0, 0)),
            pl.BlockSpec((1, HEADS, past), lambda b: (b, 0, 0)),
        ],
        out_specs=pl.BlockSpec((C, WIDTH), lambda b: (b, 0)),
        out_shape=jax.ShapeDtypeStruct((bs * C, WIDTH), BF16),
        compiler_params=_cparams(("arbitrary",)),
        name="fox_sample",
    )(q, kb, p, p, lf_col, lf_row, cache_k.reshape(bs, past, WIDTH), cache_v.reshape(bs, past, WIDTH),
      cache_lf, jnp.swapaxes(cache_lf, 1, 2))


def _outproj_kernel(og_ref, of_ref, x_ref, wo_ref, nw_ref, wr_ref, br_ref, cnt0_ref, xr_ref, rt_ref, cnt_ref,
                    cnt_scr):
    d = x_ref.shape[1]

    @pl.when(pl.program_id(0) == 0)
    def _():
        cnt_scr[...] = cnt0_ref[...]

    half = og_ref.shape[1]
    h = _dot(og_ref[...], wo_ref[0:half, :]) + _dot(of_ref[...], wo_ref[half:2 * half, :])
    x1 = x_ref[...] + h
    xr_ref[:, 0:d] = x1
    xn = x1 * lax.rsqrt(jnp.mean(x1 * x1, axis=-1, keepdims=True) + EPS) * nw_ref[...]

    x_hi = xn.astype(BF16)
    x_lo = (xn - x_hi.astype(F32)).astype(BF16)
    logits = (_dot(x_hi, wr_ref[0]) + _dot(x_lo, wr_ref[0]) + _dot(x_hi, wr_ref[1])) + br_ref[...]

    tm = logits.shape[0]
    lane = _iota2((tm, LANES), 1).astype(F32)
    big = float(LANES)
    gl = jnp.where(lane < N_GROUPS, logits, NEG)
    gmax = jnp.max(gl, axis=-1, keepdims=True)
    gidx = jnp.min(jnp.where(gl == gmax, lane, big), axis=-1, keepdims=True)
    p_top = 1.0 / jnp.sum(jnp.exp(gl - gmax), axis=-1, keepdims=True)
    e = lane - RT_E0
    sel = (e >= 0) & (e < N_EXPERTS) & (jnp.floor(e * (1.0 / EXPERTS_PER_GROUP)) == gidx)
    el = jnp.where(sel, logits, NEG)
    v1 = jnp.max(el, axis=-1, keepdims=True)
    i1 = jnp.min(jnp.where(el == v1, lane, big), axis=-1, keepdims=True)
    el2 = jnp.where(lane == i1, NEG, el)
    v2 = jnp.max(el2, axis=-1, keepdims=True)
    i2 = jnp.min(jnp.where(el2 == v2, lane, big), axis=-1, keepdims=True)
    e2 = jnp.exp(v2 - v1)
    w1 = p_top / (1.0 + e2)
    w2 = p_top * e2 / (1.0 + e2)

    ex1 = i1 - RT_E0
    ex2 = i2 - RT_E0
    first = ex1 < ex2
    ea = jnp.where(first, ex1, ex2)
    eb = jnp.where(first, ex2, ex1)
    wa = jnp.where(first, w1, w2)
    wb = jnp.where(first, w2, w1)
    la = ea - gidx * EXPERTS_PER_GROUP
    lb = eb - gidx * EXPERTS_PER_GROUP
    cls = gidx * PAIRS_PER_GROUP + la * (2 * EXPERTS_PER_GROUP - 1 - la) * 0.5 + (lb - la - 1.0)

    onehot = lane == cls
    oh = jnp.where(onehot, 1.0, 0.0)
    strict_b = jnp.where(_iota2((tm, tm), 0) > _iota2((tm, tm), 1), 1.0, 0.0).astype(BF16)
    before = cnt_scr[...] + _dot(strict_b, oh.astype(BF16))
    rank = jnp.sum(jnp.where(onehot, before, 0.0), axis=-1, keepdims=True)
    cnt_scr[...] += jnp.sum(oh, axis=0, keepdims=True)
    cnt_ref[...] = cnt_scr[...]
    route = jnp.where(lane == RT_CLS, cls, jnp.where(lane == RT_RANK, rank, jnp.where(
        lane == RT_WA, wa, jnp.where(lane == RT_WB, wb, 0.0))))
    rt_ref[...] = route
    xr_ref[:, d:d + LANES] = route


def _outproj(og, of, x, wo, norm_w, wr, br, cnt0):
    n, d = x.shape
    tm = _pick(n, (512, 384, 256, 128))
    rowblk = lambda i: (i, 0)
    return pl.pallas_call(
        _outproj_kernel,
        grid=(n // tm,),
        in_specs=[
            pl.BlockSpec((tm, WIDTH), rowblk),
            pl.BlockSpec((tm, WIDTH), rowblk),
            pl.BlockSpec((tm, d), rowblk),
            pl.BlockSpec((2 * WIDTH, d), lambda i: (0, 0)),
            pl.BlockSpec((1, d), lambda i: (0, 0)),
            pl.BlockSpec((2, d, LANES), lambda i: (0, 0, 0)),
            pl.BlockSpec((1, LANES), lambda i: (0, 0)),
            pl.BlockSpec((1, LANES), lambda i: (0, 0)),
        ],
        out_specs=[
            pl.BlockSpec((tm, d + LANES), rowblk),
            pl.BlockSpec((tm, LANES), rowblk),
            pl.BlockSpec((1, LANES), lambda i: (0, 0)),
        ],
        out_shape=[
            jax.ShapeDtypeStruct((n, d + LANES), F32),
            jax.ShapeDtypeStruct((n, LANES), F32),
            jax.ShapeDtypeStruct((1, LANES), F32),
        ],
        scratch_shapes=[pltpu.VMEM((1, LANES), F32)],
        compiler_params=_cparams(("arbitrary",)),
        name="outproj_router",
    )(og, of, x, wo, norm_w, wr, br, cnt0)


def _pos_rows(tm):
    return -(-(-(-tm // LANES)) // 8) * 8


def _tile_positions(pos, tm):
    nt = pos.shape[0] // tm
    rows = _pos_rows(tm)
    p = jnp.pad(pos.reshape(nt, tm), ((0, 0), (0, rows * LANES - tm)))
    return p.reshape(nt * rows, LANES)


def _row_copy_loops(tm, pos_smem, make_copy):
    def start_row(r, carry):
        def start(c, carry):
            make_copy(r * LANES + c, pos_smem[r, c]).start()
            return carry
        return lax.fori_loop(0, LANES, start, carry, unroll=8)

    def wait(t, carry):
        make_copy(0, 0).wait()
        return carry

    lax.fori_loop(0, tm // LANES, start_row, 0)
    lax.fori_loop(0, tm, wait, 0, unroll=8)


def _dispatch_kernel(ends_ref, xr_ref, pos_hbm, *rest, zero_fill):
    if zero_fill:
        xs_hbm, pos_smem, zbuf, sem_idx, sem_fill, sem_rows = rest
    else:
        _, xs_hbm, pos_smem, sem_idx, sem_rows = rest
    i = pl.program_id(0)
    tm = xr_ref.shape[0]
    rows = pos_smem.shape[0]
    idx_cp = pltpu.make_async_copy(pos_hbm.at[pl.ds(pl.multiple_of(i * rows, 8), rows)], pos_smem, sem_idx)
    idx_cp.start()

    if zero_fill:
        @pl.when(i == 0)
        def _():
            zbuf[...] = jnp.zeros(zbuf.shape, F32)

            def fill(start):
                def body(c, carry):
                    lo = jnp.where(c == 0, 0, ends_ref[jnp.maximum(c - 1, 0)])
                    hi = ends_ref[c]

                    @pl.when(hi > lo)
                    def _():
                        cp = pltpu.make_async_copy(
                            zbuf, xs_hbm.at[pl.ds(pl.multiple_of(hi - MOE_TS, MOE_TS), MOE_TS)], sem_fill)
                        if start:
                            cp.start()
                        else:
                            cp.wait()
                    return carry
                lax.fori_loop(0, N_CLASSES, body, 0)

            fill(True)
            fill(False)

    idx_cp.wait()
    _row_copy_loops(tm, pos_smem, lambda t, p: pltpu.make_async_copy(
        xr_ref.at[pl.ds(t, 1)], xs_hbm.at[pl.ds(p, 1)], sem_rows))


def _dispatch(ends, xr, pos, xs, *, n_sorted):
    n, dw = xr.shape
    tm = _pick(n, (1024, 1152, 512, 384, 256, 128))
    rows = _pos_rows(tm)
    zero_fill = xs is None
    any_spec = pl.BlockSpec(memory_space=pl.ANY)
    in_specs = [pl.BlockSpec((tm, dw), lambda i, ends_r: (i, 0)), any_spec]
    args = [xr, _tile_positions(pos, tm)]
    scratch = [pltpu.SMEM((rows, LANES), jnp.int32)]
    if zero_fill:
        scratch += [pltpu.VMEM((MOE_TS, dw), F32), pltpu.SemaphoreType.DMA, pltpu.SemaphoreType.DMA,
                    pltpu.SemaphoreType.DMA]
        aliases = {}
    else:
        in_specs.append(any_spec)
        args.append(xs)
        scratch += [pltpu.SemaphoreType.DMA, pltpu.SemaphoreType.DMA]
        aliases = {3: 0}
    return pl.pallas_call(
        functools.partial(_dispatch_kernel, zero_fill=zero_fill),
        grid_spec=pltpu.PrefetchScalarGridSpec(
            num_scalar_prefetch=1, grid=(n // tm,), in_specs=in_specs, out_specs=any_spec,
            scratch_shapes=scratch),
        out_shape=jax.ShapeDtypeStruct((n_sorted, dw), F32),
        input_output_aliases=aliases,
        compiler_params=pltpu.CompilerParams(dimension_semantics=("arbitrary",), vmem_limit_bytes=VMEM_LIMIT,
                                             has_side_effects=True),
        name="moe_dispatch",
    )(ends, *args)


def _ffn_kernel(ta_ref, tb_ref, nv_ref, xs_ref, w1a_ref, w3a_ref, w2a_ref, w1b_ref, w3b_ref, w2b_ref,
                nffn_ref, nfin_ref, ys_ref):
    @pl.when(pl.program_id(0) < nv_ref[0])
    def _():
        d = ys_ref.shape[1]
        x1 = xs_ref[:, 0:d]
        route = xs_ref[:, d:d + LANES]
        xn = (x1 * lax.rsqrt(jnp.mean(x1 * x1, axis=-1, keepdims=True) + EPS) * nffn_ref[...]).astype(BF16)

        ga, gb = _dot(xn, w1a_ref[0]), _dot(xn, w1b_ref[0])
        ua, ub = _dot(xn, w3a_ref[0]), _dot(xn, w3b_ref[0])
        ha = (_silu(ga) * ua * route[:, RT_WA:RT_WA + 1]).astype(BF16)
        hb = (_silu(gb) * ub * route[:, RT_WB:RT_WB + 1]).astype(BF16)
        x2 = x1 + _dot(ha, w2a_ref[0]) + _dot(hb, w2b_ref[0])
        ys_ref[...] = x2 * lax.rsqrt(jnp.mean(x2 * x2, axis=-1, keepdims=True) + EPS) * nfin_ref[...]


def _ffn(tile_a, tile_b, n_valid, xs, w1, w3, w2, nffn, nfin):
    ns, dw = xs.shape
    d = dw - LANES
    de = w1.shape[2]
    row = lambda t, ta, tb, nv: (jnp.minimum(t, nv[0] - 1), 0)
    wa = lambda t, ta, tb, nv: (ta[t], 0, 0)
    wb = lambda t, ta, tb, nv: (tb[t], 0, 0)
    const = lambda t, ta, tb, nv: (0, 0)
    wspec = pl.BlockSpec
    return pl.pallas_call(
        _ffn_kernel,
        grid_spec=pltpu.PrefetchScalarGridSpec(
            num_scalar_prefetch=3,
            grid=(ns // MOE_TS,),
            in_specs=[
                pl.BlockSpec((MOE_TS, dw), row),
                wspec((1, d, de), wa), wspec((1, d, de), wa), wspec((1, de, d), wa),
                wspec((1, d, de), wb), wspec((1, d, de), wb), wspec((1, de, d), wb),
                pl.BlockSpec((1, d), const), pl.BlockSpec((1, d), const),
            ],
            out_specs=pl.BlockSpec((MOE_TS, d), row),
        ),
        out_shape=jax.ShapeDtypeStruct((ns, d), F32),
        compiler_params=_cparams(("arbitrary",)),
        name="moe_ffn",
    )(tile_a, tile_b, n_valid, xs, w1, w3, w2, w1, w3, w2, nffn, nfin)


def _unsort_kernel(pos_hbm, ys_hbm, y_ref, pos_smem, sem_idx, sem_rows):
    i = pl.program_id(0)
    tm = y_ref.shape[0]
    rows = pos_smem.shape[0]
    idx_cp = pltpu.make_async_copy(pos_hbm.at[pl.ds(pl.multiple_of(i * rows, 8), rows)], pos_smem, sem_idx)
    idx_cp.start()
    idx_cp.wait()
    _row_copy_loops(tm, pos_smem, lambda t, p: pltpu.make_async_copy(
        ys_hbm.at[pl.ds(p, 1)], y_ref.at[pl.ds(t, 1)], sem_rows))


def _unsort(ys, pos):
    n = pos.shape[0]
    d = ys.shape[1]
    tm = _pick(n, (1024, 1152, 512, 384, 256, 128))
    any_spec = pl.BlockSpec(memory_space=pl.ANY)
    return pl.pallas_call(
        _unsort_kernel,
        grid=(n // tm,),
        in_specs=[any_spec, any_spec],
        out_specs=pl.BlockSpec((tm, d), lambda i: (i, 0)),
        out_shape=jax.ShapeDtypeStruct((n, d), F32),
        scratch_shapes=[pltpu.SMEM((_pos_rows(tm), LANES), jnp.int32), pltpu.SemaphoreType.DMA,
                        pltpu.SemaphoreType.DMA],
        compiler_params=_cparams(("arbitrary",)),
        name="moe_unsort",
    )(_tile_positions(pos, tm), ys)


def _pair_tables():
    a, b = [], []
    for g in range(N_GROUPS):
        for la in range(EXPERTS_PER_GROUP):
            for lb in range(la + 1, EXPERTS_PER_GROUP):
                a.append(g * EXPERTS_PER_GROUP + la)
                b.append(g * EXPERTS_PER_GROUP + lb)
    return jnp.asarray(a, jnp.int32), jnp.asarray(b, jnp.int32)


def _moe(xr_list, rt_list, cnt, w1, w3, w2, nffn, nfin):
    n_total = sum(x.shape[0] for x in xr_list)
    n_sorted = (-(-n_total // MOE_TS) + N_CLASSES) * MOE_TS
    counts = cnt[0, :N_CLASSES].astype(jnp.int32)
    padded = (counts + MOE_TS - 1) // MOE_TS * MOE_TS
    ends = jnp.cumsum(padded)
    offs = ends - padded
    classes = jnp.arange(N_CLASSES, dtype=jnp.int32)

    def position(rt):
        cls = rt[:, RT_CLS].astype(jnp.int32)
        return jnp.sum(jnp.where(cls[:, None] == classes, offs, 0), axis=1) + rt[:, RT_RANK].astype(jnp.int32)

    pos_list = [position(rt) for rt in rt_list]
    tile_start = jnp.arange(n_sorted // MOE_TS, dtype=jnp.int32) * MOE_TS
    tile_cls = jnp.minimum(jnp.sum((ends <= tile_start[:, None]).astype(jnp.int32), axis=1), N_CLASSES - 1)
    pair_a, pair_b = _pair_tables()
    n_valid = (ends[N_CLASSES - 1] // MOE_TS).reshape(1)
    xs = None
    for xr, pos in zip(xr_list, pos_list):
        xs = _dispatch(ends, xr, pos, xs, n_sorted=n_sorted)
    tile_is = tile_cls[:, None] == classes
    tile_a = jnp.sum(jnp.where(tile_is, pair_a, 0), axis=1)
    tile_b = jnp.sum(jnp.where(tile_is, pair_b, 0), axis=1)
    ys = _ffn(tile_a, tile_b, n_valid, xs, w1, w3, w2, nffn, nfin)
    return [_unsort(ys, pos) for pos in pos_list]


def kernel(x_prompt, x_sample, cache_fox_k, cache_fox_v, cache_fox_logf, state_gdn, state_gdn_conv, meta_tokens, norm_mix_w, w_in, gdn_conv_w, gdn_A_log, gdn_dt_bias, gdn_norm_w, fox_q_norm_w, fox_k_norm_w, fox_f_bias, w_out, norm_ffn_w, w_router_group, b_router_group, w_router_expert, b_router_expert, w_gate, w_up, w_down, norm_final_w):
    B, S, D = x_prompt.shape
    BS, LS, _ = x_sample.shape
    C = GDN_CHUNK
    assert w_in.shape[0] == 1, "single-layer step only"
    assert S % C == 0 and LS <= C and N_META <= C and meta_tokens.shape[0] == N_META

    wi = w_in[0]
    o = 0
    parts = {}
    for name, size in (("g_qkv", 3 * WIDTH), ("g_z", WIDTH), ("g_a", HEADS), ("g_b", HEADS),
                       ("f_qkv", 3 * WIDTH), ("f_og", WIDTH), ("f_f", HEADS)):
        parts[name] = wi[:, o:o + size]
        o += size
    wbig = jnp.concatenate([parts["g_qkv"], parts["f_qkv"], parts["g_z"], parts["f_og"]], axis=1).astype(BF16)
    wsm_cols = jnp.concatenate([parts["f_f"], parts["g_a"], parts["g_b"]], axis=1)
    wsm = jnp.pad(wsm_cols, ((0, 0), (0, LANES - 3 * HEADS))).astype(BF16)
    wsmt = jnp.pad(wsm_cols.T, ((0, 32 - 3 * HEADS), (0, 0))).astype(BF16)
    wo = w_out[0].astype(BF16)
    wr32 = jnp.pad(jnp.concatenate([w_router_group[0], w_router_expert[0]], axis=1),
                   ((0, 0), (0, LANES - N_GROUPS - N_EXPERTS)))
    wr_hi = wr32.astype(BF16)
    wr = jnp.stack([wr_hi, (wr32 - wr_hi.astype(F32)).astype(BF16)])
    br = jnp.pad(jnp.concatenate([b_router_group[0], b_router_expert[0]]),
                 (0, LANES - N_GROUPS - N_EXPERTS)).reshape(1, LANES)
    w1 = w_gate[0].astype(BF16)
    w3 = w_up[0].astype(BF16)
    w2 = w_down[0].astype(BF16)
    nmix = norm_mix_w[0].reshape(1, D)
    nffn = norm_ffn_w[0].reshape(1, D)
    nfin = norm_final_w.reshape(1, D)

    x_small = jnp.concatenate([
        jnp.pad(meta_tokens.astype(F32), ((0, C - N_META), (0, 0))),
        jnp.pad(x_sample, ((0, 0), (0, C - LS), (0, 0))).reshape(BS * C, D)], axis=0)
    xp = x_prompt.reshape(B * S, D)

    p_s, psm_s, psmt_s = _inproj(x_small, nmix, wbig, wsm, wsmt)
    p_p, psm_p, psmt_p = _inproj(xp, nmix, wbig, wsm, wsmt)

    s0_s = jnp.concatenate([jnp.zeros((1,) + state_gdn.shape[2:], F32), state_gdn[0]], axis=0)
    cb_s = jnp.concatenate([jnp.zeros((1,) + state_gdn_conv.shape[2:], F32), state_gdn_conv[0]], axis=0)
    gdn_args = (gdn_conv_w[0], gdn_A_log[0], gdn_dt_bias[0], gdn_norm_w[0])
    og_s, st_s, cv_s = _gdn(p_s, psm_s, psmt_s, s0_s, cb_s, *gdn_args, n_seq=1 + BS, l_valid=LS,
                            bcast_state=False)
    og_p, st_p, cv_p = _gdn(p_p, psm_p, psmt_p, st_s[0:1], cv_s[0:1], *gdn_args, n_seq=B, l_valid=C,
                            bcast_state=True)

    fox_args = (fox_q_norm_w[0], fox_k_norm_w[0], fox_f_bias[0])
    zc = jnp.zeros((1, 1, LANES), F32)
    zr = jnp.zeros((1, HEADS, 1), F32)
    k32_s, v32_s, lfc_s, lfr_s, fc_s, fr_s, qt_s, kcat_s, vt_s, q_s, kb_s = _foxprep(
        p_s, psm_s, psmt_s, zc, zr, *fox_args, n_seq=1 + BS, bcast_carry=True, row_major=True)
    c0c = fc_s[N_META - 1:N_META, :].reshape(1, 1, LANES)
    c0r = fr_s[:, N_META - 1:N_META].reshape(1, HEADS, 1)
    k32_p, v32_p, lfc_p, lfr_p, fc_p, fr_p, qt_p, kcat_p, vt_p = _foxprep(
        p_p, psm_p, psmt_p, c0c, c0r, *fox_args, n_seq=B, bcast_carry=True, row_major=False)
    of_p = _fox_prompt(qt_p, kcat_p, vt_p, p_p, kcat_s, vt_s, n_seq=B)
    of_s = _fox_sample(q_s, kb_s, p_s, lfc_s, lfr_s, cache_fox_k[0], cache_fox_v[0], cache_fox_logf[0],
                       l_valid=LS)

    xr_p, rt_p, cnt_p = _outproj(og_p, of_p, xp, wo, nffn, wr, br, jnp.zeros((1, LANES), F32))
    xr_s, rt_s, cnt = _outproj(og_s[C:], of_s, x_small[C:], wo, nffn, wr, br, cnt_p)
    y_p, y_s = _moe([xr_p, xr_s], [rt_p, rt_s], cnt, w1, w3, w2, nffn, nfin)

    def with_meta(meta_rows, real):
        w_ = real.shape[-1]
        m = jnp.broadcast_to(meta_rows[None], (B, N_META, w_))
        return jnp.concatenate([m, real.reshape(B, S, w_)], axis=1)[None]

    def sample_rows(a):
        return a[C:].reshape(BS, C, a.shape[-1])[:, :LS][None]

    hd = (HEADS, HEAD_DIM)
    y_prompt = y_p.reshape(B, S, D)
    y_sample = y_s.reshape(BS, C, D)[:, :LS]
    fk_p = with_meta(k32_s[:N_META], k32_p).reshape(1, B, N_META + S, *hd)
    fv_p = with_meta(v32_s[:N_META], v32_p).reshape(1, B, N_META + S, *hd)
    lf_p = with_meta(lfc_s[:N_META, SM_F:SM_F + HEADS], lfc_p[:, SM_F:SM_F + HEADS])
    fk_s = sample_rows(k32_s).reshape(1, BS, LS, *hd)
    fv_s = sample_rows(v32_s).reshape(1, BS, LS, *hd)
    lf_s = sample_rows(lfc_s[:, SM_F:SM_F + HEADS])
    return (y_prompt, y_sample, fk_p, fv_p, lf_p, st_p[None], cv_p[None],
            fk_s, fv_s, lf_s, st_s[1:][None], cv_s[1:][None])
```

```python
import functools

import jax
import jax.numpy as jnp
from jax import lax
from jax.experimental import pallas as pl
from jax.experimental.pallas import tpu as pltpu

F32 = jnp.float32
BF16 = jnp.bfloat16
EPS = 1e-6
NEG = -1e30
LOG2E = 1.4426950408889634

N_META = 16
HEADS = 8
HEAD_DIM = 128
WIDTH = HEADS * HEAD_DIM
CONV_WIDTH = 4
N_GROUPS = 4
EXPERTS_PER_GROUP = 8
N_EXPERTS = N_GROUPS * EXPERTS_PER_GROUP
GDN_CHUNK = 128
LANES = 128
SM_F, SM_A, SM_B = 0, 8, 16
PB_FQ, PB_FK, PB_FV, PB_GZ, PB_FOG = 3, 4, 5, 6, 7
RT_E0 = N_GROUPS
PAIRS_PER_GROUP = EXPERTS_PER_GROUP * (EXPERTS_PER_GROUP - 1) // 2
N_CLASSES = N_GROUPS * PAIRS_PER_GROUP
RT_CLS, RT_RANK, RT_WA, RT_WB = 0, 1, 2, 3
MOE_TS = 256

VMEM_LIMIT = 56 * 1024 * 1024


def _cparams(sem):
    return pltpu.CompilerParams(dimension_semantics=sem, vmem_limit_bytes=VMEM_LIMIT)


def _pick(n, prefs):
    for p in prefs:
        if n % p == 0:
            return p
    raise ValueError(f"no tile in {prefs} divides {n}")


def _dot(a, b):
    return jnp.dot(a, b, preferred_element_type=F32)


def _dot_nt(a, b):
    return lax.dot_general(a, b, (((1,), (1,)), ((), ())), preferred_element_type=F32)


def _dot_tn(a, b):
    return lax.dot_general(a, b, (((0,), (0,)), ((), ())), preferred_element_type=F32)


def _split3(x):
    x1 = x.astype(BF16)
    r1 = x - x1.astype(F32)
    x2 = r1.astype(BF16)
    x3 = (r1 - x2.astype(F32)).astype(BF16)
    return x1, x2, x3


def _mask_dot(mask_bf16, x):
    x1, x2, x3 = _split3(x)
    return _dot(mask_bf16, x1) + _dot(mask_bf16, x2) + _dot(mask_bf16, x3)


def _dot_mask(x, mask_bf16):
    x1, x2, x3 = _split3(x)
    return _dot(x1, mask_bf16) + _dot(x2, mask_bf16) + _dot(x3, mask_bf16)


def _softplus(x):
    return jnp.maximum(x, 0.0) + jnp.log1p(jnp.exp(-jnp.abs(x)))


def _sigmoid(x):
    return 1.0 / (1.0 + jnp.exp(-x))


def _silu(x):
    return x * _sigmoid(x)


def _iota2(shape, dim):
    return lax.broadcasted_iota(jnp.int32, shape, dim)


def _inproj_kernel(x_ref, nw_ref, wbig_ref, wsm_ref, wsmt_ref, p_ref, psm_ref, psmt_ref, xn_scr):
    @pl.when(pl.program_id(1) == 0)
    def _():
        x = x_ref[...]
        xn = x * lax.rsqrt(jnp.mean(x * x, axis=-1, keepdims=True) + EPS) * nw_ref[...]
        xnb = xn.astype(BF16)
        xn_scr[...] = xnb
        psm_ref[...] = _dot(xnb, wsm_ref[...])
        psmt_ref[...] = _dot_nt(wsmt_ref[...], xnb)

    p_ref[...] = _dot(xn_scr[...], wbig_ref[...]).astype(BF16)


def _inproj(x, norm_w, wbig, wsm, wsmt):
    n, d = x.shape
    tm = _pick(n, (1024, 1152, 512, 384, 256, 128))
    tn = 1024
    nproj = wbig.shape[1]
    return pl.pallas_call(
        _inproj_kernel,
        grid=(n // tm, nproj // tn),
        in_specs=[
            pl.BlockSpec((tm, d), lambda i, j: (i, 0)),
            pl.BlockSpec((1, d), lambda i, j: (0, 0)),
            pl.BlockSpec((d, tn), lambda i, j: (0, j)),
            pl.BlockSpec((d, LANES), lambda i, j: (0, 0)),
            pl.BlockSpec((32, d), lambda i, j: (0, 0)),
        ],
        out_specs=[
            pl.BlockSpec((tm, tn), lambda i, j: (i, j)),
            pl.BlockSpec((tm, LANES), lambda i, j: (i, 0)),
            pl.BlockSpec((32, tm), lambda i, j: (0, i)),
        ],
        out_shape=[
            jax.ShapeDtypeStruct((n, nproj), BF16),
            jax.ShapeDtypeStruct((n, LANES), F32),
            jax.ShapeDtypeStruct((32, n), F32),
        ],
        scratch_shapes=[pltpu.VMEM((tm, d), BF16)],
        compiler_params=_cparams(("arbitrary", "arbitrary")),
        name="inproj",
    )(x, norm_w, wbig, wsm, wsmt)


def _gdn_kernel(qkv_ref, z_ref, sm_ref, smt_ref, s0_ref, cb_ref, convw_ref, alog_c_ref, dtb_c_ref,
                alog_r_ref, dtb_r_ref, normw_ref, o_ref, snew_ref, cnew_ref, s_scr, xp_scr, conv_scr, *,
                l_valid):
    C = GDN_CHUNK

    @pl.when(pl.program_id(1) == 0)
    def _():
        s_scr[...] = s0_ref[0]
        xp_scr[5:8, :] = cb_ref[0]

    row = _iota2((C, C), 0)
    col = _iota2((C, C), 1)

    xb = qkv_ref[...]
    x = xb.astype(F32)
    xp_scr[8:16, :] = x[0:8]
    w = convw_ref[...]
    shift_b = jnp.concatenate(
        [jnp.where(col == row - i, 1.0, 0.0).astype(BF16) for i in (1, 2, 3)], axis=0)
    sh = _dot(shift_b, xb)
    conv_scr[...] = w[3:4] * x + w[2:3] * sh[0:C] + w[1:2] * sh[C:2 * C] + w[0:1] * sh[2 * C:3 * C]
    conv_scr[0:8, :] = (w[0:1] * xp_scr[5:13, :] + w[1:2] * xp_scr[6:14, :]
                        + w[2:3] * xp_scr[7:15, :] + w[3:4] * x[0:8])
    act = _silu(conv_scr[...])
    tail = x[l_valid - 3:l_valid, :]
    xp_scr[5:8, :] = tail
    cnew_ref[0] = tail

    incl = row >= col
    strict = row > col
    tril_b = jnp.where(incl, 1.0, 0.0).astype(BF16)
    triu_b = jnp.where(row <= col, 1.0, 0.0).astype(BF16)

    sm = sm_ref[...]
    g_col = -jnp.exp(alog_c_ref[...]) * _softplus(sm + dtb_c_ref[...])
    g_row = -jnp.exp(alog_r_ref[...]) * _softplus(smt_ref[...] + dtb_r_ref[...])
    beta_col = _sigmoid(sm)
    if l_valid < C:
        g_col = jnp.where(_iota2((C, LANES), 0) < l_valid, g_col, 0.0)
        beta_col = jnp.where(_iota2((C, LANES), 0) < l_valid, beta_col, 0.0)
        g_row = jnp.where(_iota2((HEADS, C), 1) < l_valid, g_row, 0.0)
    gl_col = _mask_dot(tril_b, g_col)
    gl_row = _dot_mask(g_row, triu_b)

    inv_levels = []
    sh = 1
    while (1 << sh) < C:
        m = (((row >> sh) & 1) == 1) & ((col >> (sh + 1)) == (row >> (sh + 1))) & (((col >> sh) & 1) == 0)
        inv_levels.append(m)
        sh += 1
    m1 = ((row & 1) == 1) & (col == row - 1)
    eye = jnp.where(row == col, 1.0, 0.0)

    normw = normw_ref[...]
    H = range(HEADS)
    hs = [slice(h * HEAD_DIM, (h + 1) * HEAD_DIM) for h in H]
    gc = [gl_col[:, SM_A + h:SM_A + h + 1] for h in H]
    beta = [beta_col[:, SM_B + h:SM_B + h + 1] for h in H]
    qn, kn, kb, knb = [], [], [], []
    for h in H:
        q = act[:, hs[h]]
        k = act[:, WIDTH + h * HEAD_DIM:WIDTH + (h + 1) * HEAD_DIM]
        qn.append(q * lax.rsqrt(jnp.sum(q * q, axis=-1, keepdims=True) + EPS) * (HEAD_DIM ** -0.5))
        kn.append(k * lax.rsqrt(jnp.sum(k * k, axis=-1, keepdims=True) + EPS))
        kb.append(kn[h] * beta[h])
        knb.append(kn[h].astype(BF16))
    decay = [jnp.exp(jnp.where(incl, gc[h] - gl_row[h:h + 1, :], NEG)) for h in H]
    a_mat = [jnp.where(strict, _dot_nt(kb[h].astype(BF16), knb[h]) * decay[h], 0.0) for h in H]
    qk = [(_dot_nt(qn[h].astype(BF16), knb[h]) * decay[h]).astype(BF16) for h in H]

    t = [eye - jnp.where(m1, a_mat[h], 0.0) for h in H]
    for m in inv_levels:
        tb = [t[h].astype(BF16) for h in H]
        y = [_dot(tb[h], jnp.where(m, a_mat[h], 0.0).astype(BF16)).astype(BF16) for h in H]
        t = [t[h] - _dot(y[h], tb[h]) for h in H]
    tb = [t[h].astype(BF16) for h in H]

    eg = [jnp.exp(gc[h]) for h in H]
    g_last = [gc[h][C - 1:C, :] for h in H]
    s = [s_scr[h] for h in H]
    sb = [s[h].astype(BF16) for h in H]
    r = [(act[:, 2 * WIDTH + h * HEAD_DIM:2 * WIDTH + (h + 1) * HEAD_DIM] * beta[h]
          - _dot((kb[h] * eg[h]).astype(BF16), sb[h])).astype(BF16) for h in H]
    ub = [_dot(tb[h], r[h]).astype(BF16) for h in H]
    o = [_dot((qn[h] * eg[h]).astype(BF16), sb[h]) + _dot(qk[h], ub[h]) for h in H]
    for h in H:
        k_dec = (kn[h] * jnp.exp(g_last[h] - gc[h])).astype(BF16)
        s_scr[h] = s[h] * jnp.exp(g_last[h]) + _dot_tn(k_dec, ub[h])
    for h in H:
        on = o[h] * lax.rsqrt(jnp.mean(o[h] * o[h], axis=-1, keepdims=True) + EPS) * normw
        z = z_ref[:, hs[h]].astype(F32)
        o_ref[:, hs[h]] = (on * _silu(z)).astype(BF16)

    snew_ref[0] = s_scr[...]


def _gdn(p, psm, psmt, s0, cb, conv_w, alog, dtb, norm_w, *, n_seq, l_valid, bcast_state):
    n = p.shape[0]
    C = GDN_CHUNK
    nc = n // (n_seq * C)
    alog_c = jnp.zeros((1, LANES), F32).at[0, SM_A:SM_A + HEADS].set(alog)
    dtb_c = jnp.zeros((1, LANES), F32).at[0, SM_A:SM_A + HEADS].set(dtb)
    st_idx = (lambda b, c: (0, 0, 0, 0)) if bcast_state else (lambda b, c: (b, 0, 0, 0))
    cb_idx = (lambda b, c: (0, 0, 0)) if bcast_state else (lambda b, c: (b, 0, 0))
    return pl.pallas_call(
        functools.partial(_gdn_kernel, l_valid=l_valid),
        grid=(n_seq, nc),
        in_specs=[
            pl.BlockSpec((C, 3 * WIDTH), lambda b, c: (b * nc + c, 0)),
            pl.BlockSpec((C, WIDTH), lambda b, c: (b * nc + c, PB_GZ)),
            pl.BlockSpec((C, LANES), lambda b, c: (b * nc + c, 0)),
            pl.BlockSpec((HEADS, C), lambda b, c: (SM_A // HEADS, b * nc + c)),
            pl.BlockSpec((1, HEADS, HEAD_DIM, HEAD_DIM), st_idx),
            pl.BlockSpec((1, CONV_WIDTH - 1, 3 * WIDTH), cb_idx),
            pl.BlockSpec((CONV_WIDTH, 3 * WIDTH), lambda b, c: (0, 0)),
            pl.BlockSpec((1, LANES), lambda b, c: (0, 0)),
            pl.BlockSpec((1, LANES), lambda b, c: (0, 0)),
            pl.BlockSpec((HEADS, 1), lambda b, c: (0, 0)),
            pl.BlockSpec((HEADS, 1), lambda b, c: (0, 0)),
            pl.BlockSpec((1, HEAD_DIM), lambda b, c: (0, 0)),
        ],
        out_specs=[
            pl.BlockSpec((C, WIDTH), lambda b, c: (b * nc + c, 0)),
            pl.BlockSpec((1, HEADS, HEAD_DIM, HEAD_DIM), lambda b, c: (b, 0, 0, 0)),
            pl.BlockSpec((1, CONV_WIDTH - 1, 3 * WIDTH), lambda b, c: (b, 0, 0)),
        ],
        out_shape=[
            jax.ShapeDtypeStruct((n, WIDTH), BF16),
            jax.ShapeDtypeStruct((n_seq, HEADS, HEAD_DIM, HEAD_DIM), F32),
            jax.ShapeDtypeStruct((n_seq, CONV_WIDTH - 1, 3 * WIDTH), F32),
        ],
        scratch_shapes=[
            pltpu.VMEM((HEADS, HEAD_DIM, HEAD_DIM), F32),
            pltpu.VMEM((16, 3 * WIDTH), F32),
            pltpu.VMEM((C, 3 * WIDTH), F32),
        ],
        compiler_params=_cparams(("arbitrary", "arbitrary")),
        name="gdn",
    )(p, p, psm, psmt, s0, cb, conv_w, alog_c, dtb_c, alog.reshape(HEADS, 1), dtb.reshape(HEADS, 1),
      norm_w.reshape(1, HEAD_DIM))


def _foxprep_kernel(qkv_ref, sm_ref, smt_ref, c0c_ref, c0r_ref, qw_ref, kw_ref, fb_c_ref, fb_r_ref,
                    k32_ref, v32_ref, lfc_ref, lfr_ref, fc_ref, fr_ref, qt_ref, kcat_ref, vt_ref, *rest,
                    row_major):
    if row_major:
        q_ref, kb_ref, cc_scr, cr_scr = rest
    else:
        cc_scr, cr_scr = rest
    tm = qkv_ref.shape[0]

    @pl.when(pl.program_id(1) == 0)
    def _():
        cc_scr[...] = c0c_ref[0]
        cr_scr[...] = c0r_ref[0]

    lf_col = -_softplus(-(sm_ref[...] + fb_c_ref[...]))
    lf_row = -_softplus(-(smt_ref[...] + fb_r_ref[...]))
    lfc_ref[...] = lf_col
    lfr_ref[...] = lf_row
    row = _iota2((tm, tm), 0)
    col = _iota2((tm, tm), 1)
    tril_b = jnp.where(row >= col, 1.0, 0.0).astype(BF16)
    triu_b = jnp.where(row <= col, 1.0, 0.0).astype(BF16)
    f_col = cc_scr[...] + _mask_dot(tril_b, lf_col)
    f_row = cr_scr[...] + _dot_mask(lf_row, triu_b)
    fc_ref[...] = f_col
    fr_ref[...] = f_row
    cc_scr[...] = f_col[tm - 1:tm, :]
    cr_scr[...] = f_row[:, tm - 1:tm]

    qw = qw_ref[...]
    kw = kw_ref[...]
    lane = _iota2((tm, HEAD_DIM), 1)
    sub = _iota2((16, tm), 0)
    zeros_t = jnp.zeros((HEAD_DIM - 16, tm), BF16)
    for h in range(HEADS):
        hs = slice(h * HEAD_DIM, (h + 1) * HEAD_DIM)
        q = qkv_ref[:, hs].astype(F32)
        k = qkv_ref[:, WIDTH + h * HEAD_DIM:WIDTH + (h + 1) * HEAD_DIM].astype(F32)
        v = qkv_ref[:, 2 * WIDTH + h * HEAD_DIM:2 * WIDTH + (h + 1) * HEAD_DIM].astype(F32)
        qn = q * lax.rsqrt(jnp.mean(q * q, axis=-1, keepdims=True) + EPS) * qw
        kn = k * lax.rsqrt(jnp.mean(k * k, axis=-1, keepdims=True) + EPS) * kw
        k32_ref[pl.ds(h, tm, stride=HEADS), :] = kn
        v32_ref[pl.ds(h, tm, stride=HEADS), :] = v
        if row_major:
            q_ref[:, hs] = (qn * (HEAD_DIM ** -0.5)).astype(BF16)
            kb_ref[:, hs] = kn.astype(BF16)
        base = 2 * h * HEAD_DIM
        qt_ref[base:base + HEAD_DIM, :] = (qn * (LOG2E * HEAD_DIM ** -0.5)).T.astype(BF16)
        r1, r2, r3 = _split3(f_row[h:h + 1, :] * LOG2E)
        aug_q = jnp.where(sub == 0, r1.astype(F32), jnp.where(sub == 1, r2.astype(F32), jnp.where(
            sub == 2, r3.astype(F32), jnp.where(sub < 6, 1.0, 0.0))))
        qt_ref[base + HEAD_DIM:base + HEAD_DIM + 16, :] = aug_q.astype(BF16)
        qt_ref[base + HEAD_DIM + 16:base + 2 * HEAD_DIM, :] = zeros_t
        kcat_ref[:, base:base + HEAD_DIM] = kn.astype(BF16)
        c1, c2, c3 = _split3(f_col[:, SM_F + h:SM_F + h + 1] * LOG2E)
        aug_k = jnp.where(lane < 3, 1.0, jnp.where(lane == 3, -c1.astype(F32), jnp.where(
            lane == 4, -c2.astype(F32), jnp.where(lane == 5, -c3.astype(F32), 0.0))))
        kcat_ref[:, base + HEAD_DIM:base + 2 * HEAD_DIM] = aug_k.astype(BF16)
        vt_ref[hs, :] = v.T.astype(BF16)


def _foxprep(p, psm, psmt, c0c, c0r, q_norm_w, k_norm_w, f_bias, *, n_seq, bcast_carry, row_major, lead=0):
    n = p.shape[0]
    rows = n // n_seq
    tm = _pick(rows, (512, 256, 128))
    nt = rows // tm
    fb_c = jnp.zeros((1, LANES), F32).at[0, SM_F:SM_F + HEADS].set(f_bias)
    c_idx = (lambda b, t: (0, 0, 0)) if bcast_carry else (lambda b, t: (b, 0, 0))
    rowblk = lambda b, t: (b * nt + t, 0)
    colblk = lambda b, t: (0, b * nt + t)
    kv_spec = pl.BlockSpec((pl.Element(tm * HEADS), pl.Element(HEAD_DIM)),
                           lambda b, t: ((b * (rows + lead) + lead + t * tm) * HEADS, 0))
    kv_shape = jax.ShapeDtypeStruct((n_seq * (rows + lead) * HEADS, HEAD_DIM), F32)
    out_specs = [
        kv_spec,
        kv_spec,
        pl.BlockSpec((tm, LANES), rowblk),
        pl.BlockSpec((HEADS, tm), colblk),
        pl.BlockSpec((tm, LANES), rowblk),
        pl.BlockSpec((HEADS, tm), colblk),
        pl.BlockSpec((2 * WIDTH, tm), colblk),
        pl.BlockSpec((tm, 2 * WIDTH), rowblk),
        pl.BlockSpec((WIDTH, tm), colblk),
    ]
    out_shape = [
        kv_shape,
        kv_shape,
        jax.ShapeDtypeStruct((n, LANES), F32),
        jax.ShapeDtypeStruct((HEADS, n), F32),
        jax.ShapeDtypeStruct((n, LANES), F32),
        jax.ShapeDtypeStruct((HEADS, n), F32),
        jax.ShapeDtypeStruct((2 * WIDTH, n), BF16),
        jax.ShapeDtypeStruct((n, 2 * WIDTH), BF16),
        jax.ShapeDtypeStruct((WIDTH, n), BF16),
    ]
    if row_major:
        out_specs += [pl.BlockSpec((tm, WIDTH), rowblk), pl.BlockSpec((tm, WIDTH), rowblk)]
        out_shape += [jax.ShapeDtypeStruct((n, WIDTH), BF16),
                      jax.ShapeDtypeStruct((n, WIDTH), BF16)]
    return pl.pallas_call(
        functools.partial(_foxprep_kernel, row_major=row_major),
        grid=(n_seq, nt),
        in_specs=[
            pl.BlockSpec((tm, 3 * WIDTH), lambda b, t: (b * nt + t, 1)),
            pl.BlockSpec((tm, LANES), rowblk),
            pl.BlockSpec((HEADS, tm), lambda b, t: (SM_F // HEADS, b * nt + t)),
            pl.BlockSpec((1, 1, LANES), c_idx),
            pl.BlockSpec((1, HEADS, 1), c_idx),
            pl.BlockSpec((1, HEAD_DIM), lambda b, t: (0, 0)),
            pl.BlockSpec((1, HEAD_DIM), lambda b, t: (0, 0)),
            pl.BlockSpec((1, LANES), lambda b, t: (0, 0)),
            pl.BlockSpec((HEADS, 1), lambda b, t: (0, 0)),
        ],
        out_specs=out_specs,
        out_shape=out_shape,
        scratch_shapes=[pltpu.VMEM((1, LANES), F32), pltpu.VMEM((HEADS, 1), F32)],
        compiler_params=_cparams(("arbitrary", "arbitrary")),
        name="foxprep",
    )(p, psm, psmt, c0c, c0r, q_norm_w.reshape(1, HEAD_DIM), k_norm_w.reshape(1, HEAD_DIM), fb_c,
      f_bias.reshape(HEADS, 1))


def _fill_lead_kernel(src_ref, big_ref, o_ref):
    o_ref[...] = src_ref[...]


def _fill_lead(big, src, *, n_seq):
    r = src.shape[0]
    seq_rows = big.shape[0] // n_seq
    assert seq_rows % r == 0
    return pl.pallas_call(
        _fill_lead_kernel,
        grid=(n_seq,),
        in_specs=[pl.BlockSpec((r, HEAD_DIM), lambda b: (0, 0)), pl.BlockSpec(memory_space=pl.ANY)],
        out_specs=pl.BlockSpec((r, HEAD_DIM), lambda b: (b * (seq_rows // r), 0)),
        out_shape=jax.ShapeDtypeStruct(big.shape, big.dtype),
        input_output_aliases={1: 0},
        compiler_params=_cparams(("arbitrary",)),
        name="fill_lead",
    )(src, big)


def _logf_kernel(meta_ref, lf_ref, o_ref):
    o_ref[0, 0:N_META, :] = meta_ref[0:N_META, SM_F:SM_F + HEADS]
    o_ref[0, N_META:, :] = lf_ref[:, SM_F:SM_F + HEADS]


def _assemble_logf(lf_small, lf_prompt, *, n_seq):
    s = lf_prompt.shape[0] // n_seq
    return pl.pallas_call(
        _logf_kernel,
        grid=(n_seq,),
        in_specs=[pl.BlockSpec((GDN_CHUNK, LANES), lambda b: (0, 0)), pl.BlockSpec((s, LANES), lambda b: (b, 0))],
        out_specs=pl.BlockSpec((1, N_META + s, HEADS), lambda b: (b, 0, 0)),
        out_shape=jax.ShapeDtypeStruct((n_seq, N_META + s, HEADS), F32),
        compiler_params=_cparams(("arbitrary",)),
        name="assemble_logf",
    )(lf_small, lf_prompt)


FOX_UNIT = 256


def _fox_kernel(qi_ref, ki_ref, qt_ref, kcat_ref, vt_ref, og_ref, km_ref, vtm_ref, o_ref, m_scr, l_scr, acc_scr):
    pair = pl.program_id(1)
    qi = qi_ref[pair]
    ki = ki_ref[pair]
    tq = qt_ref.shape[1]
    tk = kcat_ref.shape[0]
    U = FOX_UNIT
    H = range(HEADS)

    def unit(qh, k_rows, vt_cols, mask):
        qs = slice(qh * U, (qh + 1) * U)
        s = [_dot(k_rows(h), qt_ref[2 * h * HEAD_DIM:2 * (h + 1) * HEAD_DIM, qs]) for h in H]
        if mask is not None:
            s = [jnp.where(mask, s[h], NEG) for h in H]
        m_old = [m_scr[h:h + 1, qs] for h in H]
        m_new = [jnp.maximum(m_old[h], jnp.max(s[h], axis=0, keepdims=True)) for h in H]
        alpha = [jnp.exp2(m_old[h] - m_new[h]) for h in H]
        p = [jnp.exp2(s[h] - m_new[h]) for h in H]
        for h in H:
            l_scr[h:h + 1, qs] = alpha[h] * l_scr[h:h + 1, qs] + jnp.sum(p[h], axis=0, keepdims=True)
            m_scr[h:h + 1, qs] = m_new[h]
        for h in H:
            acc_scr[h, :, qs] = alpha[h] * acc_scr[h, :, qs] + _dot(vt_cols(h), p[h].astype(BF16))

    def k_unit(ku):
        return lambda h: kcat_ref[ku * U:(ku + 1) * U, 2 * h * HEAD_DIM:2 * (h + 1) * HEAD_DIM]

    def vt_unit(ku):
        return lambda h: vt_ref[h * HEAD_DIM:(h + 1) * HEAD_DIM, ku * U:(ku + 1) * U]

    @pl.when(ki == 0)
    def _():
        m_scr[...] = jnp.full(m_scr.shape, NEG, F32)
        l_scr[...] = jnp.zeros(l_scr.shape, F32)
        acc_scr[...] = jnp.zeros(acc_scr.shape, F32)
        tmeta = km_ref.shape[0]
        mask = _iota2((tmeta, U), 0) < N_META
        for qh in range(tq // U):
            unit(qh, lambda h: km_ref[:, 2 * h * HEAD_DIM:2 * (h + 1) * HEAD_DIM],
                 lambda h: vtm_ref[h * HEAD_DIM:(h + 1) * HEAD_DIM, :], mask)

    @pl.when(ki < qi)
    def _():
        for qh in range(tq // U):
            for ku in range(tk // U):
                unit(qh, k_unit(ku), vt_unit(ku), None)

    @pl.when(ki == qi)
    def _():
        diag = _iota2((U, U), 0) <= _iota2((U, U), 1)
        for qh in range(tq // U):
            for ku in range(qh + 1):
                unit(qh, k_unit(ku), vt_unit(ku), diag if ku == qh else None)
        for h in H:
            hs = slice(h * HEAD_DIM, (h + 1) * HEAD_DIM)
            gate = _sigmoid(og_ref[:, hs].astype(F32))
            o_ref[:, hs] = ((acc_scr[h] / l_scr[h:h + 1, :]).T * gate).astype(BF16)


def _fox_prompt(qt, kcat, vt, p, kcat_small, vt_small, *, n_seq):
    n = p.shape[0]
    rows = n // n_seq
    tq = _pick(rows, (512, 256))
    nq = rows // tq
    pairs = [(i, j) for i in range(nq) for j in range(i + 1)]
    qi = jnp.asarray([a for a, _ in pairs], jnp.int32)
    ki = jnp.asarray([b for _, b in pairs], jnp.int32)
    grid_spec = pltpu.PrefetchScalarGridSpec(
        num_scalar_prefetch=2,
        grid=(n_seq, len(pairs)),
        in_specs=[
            pl.BlockSpec((2 * WIDTH, tq), lambda b, t, qi_r, ki_r: (0, b * nq + qi_r[t])),
            pl.BlockSpec((tq, 2 * WIDTH), lambda b, t, qi_r, ki_r: (b * nq + ki_r[t], 0)),
            pl.BlockSpec((WIDTH, tq), lambda b, t, qi_r, ki_r: (0, b * nq + ki_r[t])),
            pl.BlockSpec((tq, WIDTH), lambda b, t, qi_r, ki_r: (b * nq + qi_r[t], PB_FOG)),
            pl.BlockSpec((GDN_CHUNK, 2 * WIDTH), lambda b, t, qi_r, ki_r: (0, 0)),
            pl.BlockSpec((WIDTH, GDN_CHUNK), lambda b, t, qi_r, ki_r: (0, 0)),
        ],
        out_specs=pl.BlockSpec((tq, WIDTH), lambda b, t, qi_r, ki_r: (b * nq + qi_r[t], 0)),
        scratch_shapes=[
            pltpu.VMEM((HEADS, tq), F32),
            pltpu.VMEM((HEADS, tq), F32),
            pltpu.VMEM((HEADS, HEAD_DIM, tq), F32),
        ],
    )
    return pl.pallas_call(
        _fox_kernel,
        grid_spec=grid_spec,
        out_shape=jax.ShapeDtypeStruct((n, WIDTH), BF16),
        compiler_params=_cparams(("arbitrary", "arbitrary")),
        name="fox_prompt",
    )(qi, ki, qt, kcat, vt, p, kcat_small, vt_small)


def _fox_sample_kernel(q_ref, kn_ref, vn_ref, og_ref, lfc_ref, lfr_ref, ck_ref, cv_ref, clc_ref, clr_ref,
                       o_ref, *, l_valid):
    C = q_ref.shape[0]
    P = ck_ref.shape[1] // HEADS
    rowp = _iota2((P, P), 0)
    colp = _iota2((P, P), 1)
    triu_p = jnp.where(rowp <= colp, 1.0, 0.0).astype(BF16)
    row = _iota2((C, C), 0)
    col = _iota2((C, C), 1)
    tril_b = jnp.where(row >= col, 1.0, 0.0).astype(BF16)
    triu_b = jnp.where(row <= col, 1.0, 0.0).astype(BF16)

    f_cache = _dot_mask(clr_ref[0], triu_p)
    carry_r = f_cache[:, P - 1:P]
    carry_c = jnp.sum(clc_ref[0], axis=0, keepdims=True)
    f_new_r = carry_r + _dot_mask(lfr_ref[...], triu_b)
    f_new_c = carry_c + _mask_dot(tril_b, lfc_ref[...])[:, SM_F:SM_F + HEADS]
    mask_new = (col <= row) & (col < l_valid)

    for h in range(HEADS):
        hs = slice(h * HEAD_DIM, (h + 1) * HEAD_DIM)
        q = q_ref[:, hs]
        fq = f_new_c[:, h:h + 1]
        ck = ck_ref[0, pl.ds(h, P, stride=HEADS), :].astype(BF16)
        cv = cv_ref[0, pl.ds(h, P, stride=HEADS), :].astype(BF16)
        s_c = _dot_nt(q, ck) + (fq - f_cache[h:h + 1, :])
        s_n = jnp.where(mask_new, _dot_nt(q, kn_ref[:, hs]) + (fq - f_new_r[h:h + 1, :]), NEG)
        m = jnp.maximum(jnp.max(s_c, axis=-1, keepdims=True), jnp.max(s_n, axis=-1, keepdims=True))
        p_c = jnp.exp(s_c - m)
        p_n = jnp.exp(s_n - m)
        l = jnp.sum(p_c, axis=-1, keepdims=True) + jnp.sum(p_n, axis=-1, keepdims=True)
        o = _dot(p_c.astype(BF16), cv) + _dot(p_n.astype(BF16), vn_ref[:, hs])
        gate = _sigmoid(og_ref[:, hs].astype(F32))
        o_ref[:, hs] = (o / l * gate).astype(BF16)


def _fox_sample(q, kb, p, lf_col, lf_row, cache_k, cache_v, cache_lf, *, l_valid):
    bs, past = cache_k.shape[0], cache_k.shape[1]
    C = GDN_CHUNK
    n = q.shape[0]
    blk = lambda b: (b + 1, 0)
    return pl.pallas_call(
        functools.partial(_fox_sample_kernel, l_valid=l_valid),
        grid=(bs,),
        in_specs=[
            pl.BlockSpec((C, WIDTH), blk),
            pl.BlockSpec((C, WIDTH), blk),
            pl.BlockSpec((C, WIDTH), lambda b: (b + 1, PB_FV)),
            pl.BlockSpec((C, WIDTH), lambda b: (b + 1, PB_FOG)),
            pl.BlockSpec((C, LANES), blk),
            pl.BlockSpec((HEADS, C), lambda b: (0, b + 1)),
            pl.BlockSpec((1, past * HEADS, HEAD_DIM), lambda b: (b, 0, 0)),
            pl.BlockSpec((1, past * HEADS, HEAD_DIM), lambda b: (b, 0, 0)),
            pl.BlockSpec((1, past, HEADS), lambda b: (b, 0, 0)),
            pl.BlockSpec((1, HEADS, past), lambda b: (b, 0, 0)),
        ],
        out_specs=pl.BlockSpec((C, WIDTH), lambda b: (b, 0)),
        out_shape=jax.ShapeDtypeStruct((bs * C, WIDTH), BF16),
        compiler_params=_cparams(("arbitrary",)),
        name="fox_sample",
    )(q, kb, p, p, lf_col, lf_row, cache_k.reshape(bs, past * HEADS, HEAD_DIM),
      cache_v.reshape(bs, past * HEADS, HEAD_DIM), cache_lf, jnp.swapaxes(cache_lf, 1, 2))


def _outproj_kernel(og_ref, of_ref, x_ref, wo_ref, nw_ref, wr_ref, br_ref, cnt0_ref, xr_ref, rt_ref, cnt_ref,
                    cnt_scr):
    d = x_ref.shape[1]

    @pl.when(pl.program_id(0) == 0)
    def _():
        cnt_scr[...] = cnt0_ref[...]

    half = og_ref.shape[1]
    h = _dot(og_ref[...], wo_ref[0:half, :]) + _dot(of_ref[...], wo_ref[half:2 * half, :])
    x1 = x_ref[...] + h
    xr_ref[:, 0:d] = x1
    xn = x1 * lax.rsqrt(jnp.mean(x1 * x1, axis=-1, keepdims=True) + EPS) * nw_ref[...]

    x_hi = xn.astype(BF16)
    x_lo = (xn - x_hi.astype(F32)).astype(BF16)
    logits = (_dot(x_hi, wr_ref[0]) + _dot(x_lo, wr_ref[0]) + _dot(x_hi, wr_ref[1])) + br_ref[...]

    tm = logits.shape[0]
    lane = _iota2((tm, LANES), 1).astype(F32)
    big = float(LANES)
    gl = jnp.where(lane < N_GROUPS, logits, NEG)
    gmax = jnp.max(gl, axis=-1, keepdims=True)
    gidx = jnp.min(jnp.where(gl == gmax, lane, big), axis=-1, keepdims=True)
    p_top = 1.0 / jnp.sum(jnp.exp(gl - gmax), axis=-1, keepdims=True)
    e = lane - RT_E0
    sel = (e >= 0) & (e < N_EXPERTS) & (jnp.floor(e * (1.0 / EXPERTS_PER_GROUP)) == gidx)
    el = jnp.where(sel, logits, NEG)
    v1 = jnp.max(el, axis=-1, keepdims=True)
    i1 = jnp.min(jnp.where(el == v1, lane, big), axis=-1, keepdims=True)
    el2 = jnp.where(lane == i1, NEG, el)
    v2 = jnp.max(el2, axis=-1, keepdims=True)
    i2 = jnp.min(jnp.where(el2 == v2, lane, big), axis=-1, keepdims=True)
    e2 = jnp.exp(v2 - v1)
    w1 = p_top / (1.0 + e2)
    w2 = p_top * e2 / (1.0 + e2)

    ex1 = i1 - RT_E0
    ex2 = i2 - RT_E0
    first = ex1 < ex2
    ea = jnp.where(first, ex1, ex2)
    eb = jnp.where(first, ex2, ex1)
    wa = jnp.where(first, w1, w2)
    wb = jnp.where(first, w2, w1)
    la = ea - gidx * EXPERTS_PER_GROUP
    lb = eb - gidx * EXPERTS_PER_GROUP
    cls = gidx * PAIRS_PER_GROUP + la * (2 * EXPERTS_PER_GROUP - 1 - la) * 0.5 + (lb - la - 1.0)

    onehot = lane == cls
    oh = jnp.where(onehot, 1.0, 0.0)
    strict_b = jnp.where(_iota2((tm, tm), 0) > _iota2((tm, tm), 1), 1.0, 0.0).astype(BF16)
    before = cnt_scr[...] + _dot(strict_b, oh.astype(BF16))
    rank = jnp.sum(jnp.where(onehot, before, 0.0), axis=-1, keepdims=True)
    cnt_scr[...] += jnp.sum(oh, axis=0, keepdims=True)
    cnt_ref[...] = cnt_scr[...]
    route = jnp.where(lane == RT_CLS, cls, jnp.where(lane == RT_RANK, rank, jnp.where(
        lane == RT_WA, wa, jnp.where(lane == RT_WB, wb, 0.0))))
    rt_ref[...] = route
    xr_ref[:, d:d + LANES] = route


def _outproj(og, of, x, wo, norm_w, wr, br, cnt0):
    n, d = x.shape
    tm = _pick(n, (512, 384, 256, 128))
    rowblk = lambda i: (i, 0)
    return pl.pallas_call(
        _outproj_kernel,
        grid=(n // tm,),
        in_specs=[
            pl.BlockSpec((tm, WIDTH), rowblk),
            pl.BlockSpec((tm, WIDTH), rowblk),
            pl.BlockSpec((tm, d), rowblk),
            pl.BlockSpec((2 * WIDTH, d), lambda i: (0, 0)),
            pl.BlockSpec((1, d), lambda i: (0, 0)),
            pl.BlockSpec((2, d, LANES), lambda i: (0, 0, 0)),
            pl.BlockSpec((1, LANES), lambda i: (0, 0)),
            pl.BlockSpec((1, LANES), lambda i: (0, 0)),
        ],
        out_specs=[
            pl.BlockSpec((tm, d + LANES), rowblk),
            pl.BlockSpec((tm, LANES), rowblk),
            pl.BlockSpec((1, LANES), lambda i: (0, 0)),
        ],
        out_shape=[
            jax.ShapeDtypeStruct((n, d + LANES), F32),
            jax.ShapeDtypeStruct((n, LANES), F32),
            jax.ShapeDtypeStruct((1, LANES), F32),
        ],
        scratch_shapes=[pltpu.VMEM((1, LANES), F32)],
        compiler_params=_cparams(("arbitrary",)),
        name="outproj_router",
    )(og, of, x, wo, norm_w, wr, br, cnt0)


def _pos_rows(tm):
    return -(-(-(-tm // LANES)) // 8) * 8


def _tile_positions(pos, tm):
    nt = pos.shape[0] // tm
    rows = _pos_rows(tm)
    p = jnp.pad(pos.reshape(nt, tm), ((0, 0), (0, rows * LANES - tm)))
    return p.reshape(nt * rows, LANES)


def _row_copy_loops(tm, pos_smem, make_copy):
    def start_row(r, carry):
        base = pl.multiple_of(r * LANES, LANES)
        for c in range(LANES):
            make_copy(base + c, pos_smem[r, c]).start()
        return carry

    def wait(t, carry):
        make_copy(0, 0).wait()
        return carry

    lax.fori_loop(0, tm // LANES, start_row, 0)
    lax.fori_loop(0, tm, wait, 0, unroll=8)


def _dispatch_kernel(ends_ref, xr_ref, pos_hbm, *rest, zero_fill):
    if zero_fill:
        xs_hbm, pos_smem, zbuf, sem_idx, sem_fill, sem_rows = rest
    else:
        _, xs_hbm, pos_smem, sem_idx, sem_rows = rest
    i = pl.program_id(0)
    tm = xr_ref.shape[0]
    rows = pos_smem.shape[0]
    idx_cp = pltpu.make_async_copy(pos_hbm.at[pl.ds(pl.multiple_of(i * rows, 8), rows)], pos_smem, sem_idx)
    idx_cp.start()

    if zero_fill:
        @pl.when(i == 0)
        def _():
            zbuf[...] = jnp.zeros(zbuf.shape, F32)

            def fill(start):
                def body(c, carry):
                    lo = jnp.where(c == 0, 0, ends_ref[jnp.maximum(c - 1, 0)])
                    hi = ends_ref[c]

                    @pl.when(hi > lo)
                    def _():
                        cp = pltpu.make_async_copy(
                            zbuf, xs_hbm.at[pl.ds(pl.multiple_of(hi - MOE_TS, MOE_TS), MOE_TS)], sem_fill)
                        if start:
                            cp.start()
                        else:
                            cp.wait()
                    return carry
                lax.fori_loop(0, N_CLASSES, body, 0)

            fill(True)
            fill(False)

    idx_cp.wait()
    _row_copy_loops(tm, pos_smem, lambda t, p: pltpu.make_async_copy(
        xr_ref.at[pl.ds(t, 1)], xs_hbm.at[pl.ds(p, 1)], sem_rows))


def _dispatch(ends, xr, pos, xs, *, n_sorted):
    n, dw = xr.shape
    tm = _pick(n, (1024, 1152, 512, 384, 256, 128))
    rows = _pos_rows(tm)
    zero_fill = xs is None
    any_spec = pl.BlockSpec(memory_space=pl.ANY)
    in_specs = [pl.BlockSpec((tm, dw), lambda i, ends_r: (i, 0)), any_spec]
    args = [xr, _tile_positions(pos, tm)]
    scratch = [pltpu.SMEM((rows, LANES), jnp.int32)]
    if zero_fill:
        scratch += [pltpu.VMEM((MOE_TS, dw), F32), pltpu.SemaphoreType.DMA, pltpu.SemaphoreType.DMA,
                    pltpu.SemaphoreType.DMA]
        aliases = {}
    else:
        in_specs.append(any_spec)
        args.append(xs)
        scratch += [pltpu.SemaphoreType.DMA, pltpu.SemaphoreType.DMA]
        aliases = {3: 0}
    return pl.pallas_call(
        functools.partial(_dispatch_kernel, zero_fill=zero_fill),
        grid_spec=pltpu.PrefetchScalarGridSpec(
            num_scalar_prefetch=1, grid=(n // tm,), in_specs=in_specs, out_specs=any_spec,
            scratch_shapes=scratch),
        out_shape=jax.ShapeDtypeStruct((n_sorted, dw), F32),
        input_output_aliases=aliases,
        compiler_params=pltpu.CompilerParams(dimension_semantics=("arbitrary",), vmem_limit_bytes=VMEM_LIMIT,
                                             has_side_effects=True),
        name="moe_dispatch",
    )(ends, *args)


def _ffn_kernel(ta_ref, tb_ref, nv_ref, xs_ref, w1a_ref, w3a_ref, w2a_ref, w1b_ref, w3b_ref, w2b_ref,
                nffn_ref, nfin_ref, ys_ref):
    @pl.when(pl.program_id(0) < nv_ref[0])
    def _():
        d = ys_ref.shape[1]
        x1 = xs_ref[:, 0:d]
        route = xs_ref[:, d:d + LANES]
        xn = (x1 * lax.rsqrt(jnp.mean(x1 * x1, axis=-1, keepdims=True) + EPS) * nffn_ref[...]).astype(BF16)

        ga, gb = _dot(xn, w1a_ref[0]), _dot(xn, w1b_ref[0])
        ua, ub = _dot(xn, w3a_ref[0]), _dot(xn, w3b_ref[0])
        ha = (_silu(ga) * ua * route[:, RT_WA:RT_WA + 1]).astype(BF16)
        hb = (_silu(gb) * ub * route[:, RT_WB:RT_WB + 1]).astype(BF16)
        x2 = x1 + _dot(ha, w2a_ref[0]) + _dot(hb, w2b_ref[0])
        ys_ref[...] = x2 * lax.rsqrt(jnp.mean(x2 * x2, axis=-1, keepdims=True) + EPS) * nfin_ref[...]


def _ffn(tile_a, tile_b, n_valid, xs, w1, w3, w2, nffn, nfin):
    ns, dw = xs.shape
    d = dw - LANES
    de = w1.shape[2]
    row = lambda t, ta, tb, nv: (jnp.minimum(t, nv[0] - 1), 0)
    wa = lambda t, ta, tb, nv: (ta[t], 0, 0)
    wb = lambda t, ta, tb, nv: (tb[t], 0, 0)
    const = lambda t, ta, tb, nv: (0, 0)
    wspec = pl.BlockSpec
    return pl.pallas_call(
        _ffn_kernel,
        grid_spec=pltpu.PrefetchScalarGridSpec(
            num_scalar_prefetch=3,
            grid=(ns // MOE_TS,),
            in_specs=[
                pl.BlockSpec((MOE_TS, dw), row),
                wspec((1, d, de), wa), wspec((1, d, de), wa), wspec((1, de, d), wa),
                wspec((1, d, de), wb), wspec((1, d, de), wb), wspec((1, de, d), wb),
                pl.BlockSpec((1, d), const), pl.BlockSpec((1, d), const),
            ],
            out_specs=pl.BlockSpec((MOE_TS, d), row),
        ),
        out_shape=jax.ShapeDtypeStruct((ns, d), F32),
        compiler_params=_cparams(("arbitrary",)),
        name="moe_ffn",
    )(tile_a, tile_b, n_valid, xs, w1, w3, w2, w1, w3, w2, nffn, nfin)


def _unsort_kernel(pos_hbm, ys_hbm, y_ref, pos_smem, sem_idx, sem_rows):
    i = pl.program_id(0)
    tm = y_ref.shape[0]
    rows = pos_smem.shape[0]
    idx_cp = pltpu.make_async_copy(pos_hbm.at[pl.ds(pl.multiple_of(i * rows, 8), rows)], pos_smem, sem_idx)
    idx_cp.start()
    idx_cp.wait()
    _row_copy_loops(tm, pos_smem, lambda t, p: pltpu.make_async_copy(
        ys_hbm.at[pl.ds(p, 1)], y_ref.at[pl.ds(t, 1)], sem_rows))


def _unsort(ys, pos):
    n = pos.shape[0]
    d = ys.shape[1]
    tm = _pick(n, (1024, 1152, 512, 384, 256, 128))
    any_spec = pl.BlockSpec(memory_space=pl.ANY)
    return pl.pallas_call(
        _unsort_kernel,
        grid=(n // tm,),
        in_specs=[any_spec, any_spec],
        out_specs=pl.BlockSpec((tm, d), lambda i: (i, 0)),
        out_shape=jax.ShapeDtypeStruct((n, d), F32),
        scratch_shapes=[pltpu.SMEM((_pos_rows(tm), LANES), jnp.int32), pltpu.SemaphoreType.DMA,
                        pltpu.SemaphoreType.DMA],
        compiler_params=_cparams(("arbitrary",)),
        name="moe_unsort",
    )(_tile_positions(pos, tm), ys)


def _pair_tables():
    a, b = [], []
    for g in range(N_GROUPS):
        for la in range(EXPERTS_PER_GROUP):
            for lb in range(la + 1, EXPERTS_PER_GROUP):
                a.append(g * EXPERTS_PER_GROUP + la)
                b.append(g * EXPERTS_PER_GROUP + lb)
    return jnp.asarray(a, jnp.int32), jnp.asarray(b, jnp.int32)


def _moe(xr_list, rt_list, cnt, w1, w3, w2, nffn, nfin):
    n_total = sum(x.shape[0] for x in xr_list)
    n_sorted = (-(-n_total // MOE_TS) + N_CLASSES) * MOE_TS
    counts = cnt[0, :N_CLASSES].astype(jnp.int32)
    padded = (counts + MOE_TS - 1) // MOE_TS * MOE_TS
    ends = jnp.cumsum(padded)
    offs = ends - padded
    classes = jnp.arange(N_CLASSES, dtype=jnp.int32)

    def position(rt):
        cls = rt[:, RT_CLS].astype(jnp.int32)
        return jnp.sum(jnp.where(cls[:, None] == classes, offs, 0), axis=1) + rt[:, RT_RANK].astype(jnp.int32)

    pos_list = [position(rt) for rt in rt_list]
    tile_start = jnp.arange(n_sorted // MOE_TS, dtype=jnp.int32) * MOE_TS
    tile_cls = jnp.minimum(jnp.sum((ends <= tile_start[:, None]).astype(jnp.int32), axis=1), N_CLASSES - 1)
    pair_a, pair_b = _pair_tables()
    n_valid = (ends[N_CLASSES - 1] // MOE_TS).reshape(1)
    xs = None
    for xr, pos in zip(xr_list, pos_list):
        xs = _dispatch(ends, xr, pos, xs, n_sorted=n_sorted)
    tile_is = tile_cls[:, None] == classes
    tile_a = jnp.sum(jnp.where(tile_is, pair_a, 0), axis=1)
    tile_b = jnp.sum(jnp.where(tile_is, pair_b, 0), axis=1)
    ys = _ffn(tile_a, tile_b, n_valid, xs, w1, w3, w2, nffn, nfin)
    return [_unsort(ys, pos) for pos in pos_list]


def kernel(x_prompt, x_sample, cache_fox_k, cache_fox_v, cache_fox_logf, state_gdn, state_gdn_conv, meta_tokens, norm_mix_w, w_in, gdn_conv_w, gdn_A_log, gdn_dt_bias, gdn_norm_w, fox_q_norm_w, fox_k_norm_w, fox_f_bias, w_out, norm_ffn_w, w_router_group, b_router_group, w_router_expert, b_router_expert, w_gate, w_up, w_down, norm_final_w):
    B, S, D = x_prompt.shape
    BS, LS, _ = x_sample.shape
    C = GDN_CHUNK
    assert w_in.shape[0] == 1, "single-layer step only"
    assert S % C == 0 and LS <= C and N_META <= C and meta_tokens.shape[0] == N_META

    wi = w_in[0]
    o = 0
    parts = {}
    for name, size in (("g_qkv", 3 * WIDTH), ("g_z", WIDTH), ("g_a", HEADS), ("g_b", HEADS),
                       ("f_qkv", 3 * WIDTH), ("f_og", WIDTH), ("f_f", HEADS)):
        parts[name] = wi[:, o:o + size]
        o += size
    wbig = jnp.concatenate([parts["g_qkv"], parts["f_qkv"], parts["g_z"], parts["f_og"]], axis=1).astype(BF16)
    wsm_cols = jnp.concatenate([parts["f_f"], parts["g_a"], parts["g_b"]], axis=1)
    wsm = jnp.pad(wsm_cols, ((0, 0), (0, LANES - 3 * HEADS))).astype(BF16)
    wsmt = jnp.pad(wsm_cols.T, ((0, 32 - 3 * HEADS), (0, 0))).astype(BF16)
    wo = w_out[0].astype(BF16)
    wr32 = jnp.pad(jnp.concatenate([w_router_group[0], w_router_expert[0]], axis=1),
                   ((0, 0), (0, LANES - N_GROUPS - N_EXPERTS)))
    wr_hi = wr32.astype(BF16)
    wr = jnp.stack([wr_hi, (wr32 - wr_hi.astype(F32)).astype(BF16)])
    br = jnp.pad(jnp.concatenate([b_router_group[0], b_router_expert[0]]),
                 (0, LANES - N_GROUPS - N_EXPERTS)).reshape(1, LANES)
    w1 = w_gate[0].astype(BF16)
    w3 = w_up[0].astype(BF16)
    w2 = w_down[0].astype(BF16)
    nmix = norm_mix_w[0].reshape(1, D)
    nffn = norm_ffn_w[0].reshape(1, D)
    nfin = norm_final_w.reshape(1, D)

    x_small = jnp.concatenate([
        jnp.pad(meta_tokens.astype(F32), ((0, C - N_META), (0, 0))),
        jnp.pad(x_sample, ((0, 0), (0, C - LS), (0, 0))).reshape(BS * C, D)], axis=0)
    xp = x_prompt.reshape(B * S, D)

    p_s, psm_s, psmt_s = _inproj(x_small, nmix, wbig, wsm, wsmt)
    p_p, psm_p, psmt_p = _inproj(xp, nmix, wbig, wsm, wsmt)

    s0_s = jnp.concatenate([jnp.zeros((1,) + state_gdn.shape[2:], F32), state_gdn[0]], axis=0)
    cb_s = jnp.concatenate([jnp.zeros((1,) + state_gdn_conv.shape[2:], F32), state_gdn_conv[0]], axis=0)
    gdn_args = (gdn_conv_w[0], gdn_A_log[0], gdn_dt_bias[0], gdn_norm_w[0])
    og_s, st_s, cv_s = _gdn(p_s, psm_s, psmt_s, s0_s, cb_s, *gdn_args, n_seq=1 + BS, l_valid=LS,
                            bcast_state=False)
    og_p, st_p, cv_p = _gdn(p_p, psm_p, psmt_p, st_s[0:1], cv_s[0:1], *gdn_args, n_seq=B, l_valid=C,
                            bcast_state=True)

    fox_args = (fox_q_norm_w[0], fox_k_norm_w[0], fox_f_bias[0])
    zc = jnp.zeros((1, 1, LANES), F32)
    zr = jnp.zeros((1, HEADS, 1), F32)
    k32_s, v32_s, lfc_s, lfr_s, fc_s, fr_s, qt_s, kcat_s, vt_s, q_s, kb_s = _foxprep(
        p_s, psm_s, psmt_s, zc, zr, *fox_args, n_seq=1 + BS, bcast_carry=True, row_major=True)
    c0c = fc_s[N_META - 1:N_META, :].reshape(1, 1, LANES)
    c0r = fr_s[:, N_META - 1:N_META].reshape(1, HEADS, 1)
    k32_p, v32_p, lfc_p, lfr_p, fc_p, fr_p, qt_p, kcat_p, vt_p = _foxprep(
        p_p, psm_p, psmt_p, c0c, c0r, *fox_args, n_seq=B, bcast_carry=True, row_major=False, lead=N_META)
    of_p = _fox_prompt(qt_p, kcat_p, vt_p, p_p, kcat_s, vt_s, n_seq=B)
    of_s = _fox_sample(q_s, kb_s, p_s, lfc_s, lfr_s, cache_fox_k[0], cache_fox_v[0], cache_fox_logf[0],
                       l_valid=LS)

    xr_p, rt_p, cnt_p = _outproj(og_p, of_p, xp, wo, nffn, wr, br, jnp.zeros((1, LANES), F32))
    xr_s, rt_s, cnt = _outproj(og_s[C:], of_s, x_small[C:], wo, nffn, wr, br, cnt_p)
    y_p, y_s = _moe([xr_p, xr_s], [rt_p, rt_s], cnt, w1, w3, w2, nffn, nfin)

    def prompt_kv(tab_p, tab_s):
        tab = _fill_lead(tab_p, tab_s[:N_META * HEADS], n_seq=B)
        return tab.reshape(1, B, N_META + S, HEADS, HEAD_DIM)

    def sample_kv(tab_s):
        return tab_s.reshape(1 + BS, C, HEADS, HEAD_DIM)[1:, :LS][None]

    y_prompt = y_p.reshape(B, S, D)
    y_sample = y_s.reshape(BS, C, D)[:, :LS]
    fk_p = prompt_kv(k32_p, k32_s)
    fv_p = prompt_kv(v32_p, v32_s)
    lf_p = _assemble_logf(lfc_s, lfc_p, n_seq=B)[None]
    fk_s = sample_kv(k32_s)
    fv_s = sample_kv(v32_s)
    lf_s = lfc_s[C:, SM_F:SM_F + HEADS].reshape(BS, C, HEADS)[:, :LS][None]
    return (y_prompt, y_sample, fk_p, fv_p, lf_p, st_p[None], cv_p[None],
            fk_s, fv_s, lf_s, st_s[1:][None], cv_s[1:][None])
```

```python
import functools

import jax
import jax.numpy as jnp
from jax import lax
from jax.experimental import pallas as pl
from jax.experimental.pallas import tpu as pltpu

F32 = jnp.float32
BF16 = jnp.bfloat16
EPS = 1e-6
NEG = -1e30
LOG2E = 1.4426950408889634

N_META = 16
HEADS = 8
HEAD_DIM = 128
WIDTH = HEADS * HEAD_DIM
CONV_WIDTH = 4
N_GROUPS = 4
EXPERTS_PER_GROUP = 8
N_EXPERTS = N_GROUPS * EXPERTS_PER_GROUP
GDN_CHUNK = 128
LANES = 128
SM_F, SM_A, SM_B = 0, 8, 16
PB_FQ, PB_FK, PB_FV, PB_GZ, PB_FOG = 3, 4, 5, 6, 7
RT_E0 = N_GROUPS
PAIRS_PER_GROUP = EXPERTS_PER_GROUP * (EXPERTS_PER_GROUP - 1) // 2
N_CLASSES = N_GROUPS * PAIRS_PER_GROUP
RT_CLS, RT_RANK, RT_WA, RT_WB = 0, 1, 2, 3
MOE_TS = 256

VMEM_LIMIT = 56 * 1024 * 1024


def _cparams(sem):
    return pltpu.CompilerParams(dimension_semantics=sem, vmem_limit_bytes=VMEM_LIMIT)


def _pick(n, prefs):
    for p in prefs:
        if n % p == 0:
            return p
    raise ValueError(f"no tile in {prefs} divides {n}")


def _dot(a, b):
    return jnp.dot(a, b, preferred_element_type=F32)


def _dot_nt(a, b):
    return lax.dot_general(a, b, (((1,), (1,)), ((), ())), preferred_element_type=F32)


def _dot_tn(a, b):
    return lax.dot_general(a, b, (((0,), (0,)), ((), ())), preferred_element_type=F32)


def _split3(x):
    x1 = x.astype(BF16)
    r1 = x - x1.astype(F32)
    x2 = r1.astype(BF16)
    x3 = (r1 - x2.astype(F32)).astype(BF16)
    return x1, x2, x3


def _mask_dot(mask_bf16, x):
    x1, x2, x3 = _split3(x)
    return _dot(mask_bf16, x1) + _dot(mask_bf16, x2) + _dot(mask_bf16, x3)


def _dot_mask(x, mask_bf16):
    x1, x2, x3 = _split3(x)
    return _dot(x1, mask_bf16) + _dot(x2, mask_bf16) + _dot(x3, mask_bf16)


def _softplus(x):
    return jnp.maximum(x, 0.0) + jnp.log1p(jnp.exp(-jnp.abs(x)))


def _sigmoid(x):
    return 1.0 / (1.0 + jnp.exp(-x))


def _silu(x):
    return x * _sigmoid(x)


def _iota2(shape, dim):
    return lax.broadcasted_iota(jnp.int32, shape, dim)


def _inproj_kernel(x_ref, nw_ref, wbig_ref, wsm_ref, wsmt_ref, p_ref, psm_ref, psmt_ref, xn_scr):
    @pl.when(pl.program_id(1) == 0)
    def _():
        x = x_ref[...]
        xn = x * lax.rsqrt(jnp.mean(x * x, axis=-1, keepdims=True) + EPS) * nw_ref[...]
        xnb = xn.astype(BF16)
        xn_scr[...] = xnb
        psm_ref[...] = _dot(xnb, wsm_ref[...])
        psmt_ref[...] = _dot_nt(wsmt_ref[...], xnb)

    p_ref[...] = _dot(xn_scr[...], wbig_ref[...]).astype(BF16)


def _inproj(x, norm_w, wbig, wsm, wsmt):
    n, d = x.shape
    tm = _pick(n, (1024, 1152, 512, 384, 256, 128))
    tn = 2048
    nproj = wbig.shape[1]
    return pl.pallas_call(
        _inproj_kernel,
        grid=(n // tm, nproj // tn),
        in_specs=[
            pl.BlockSpec((tm, d), lambda i, j: (i, 0)),
            pl.BlockSpec((1, d), lambda i, j: (0, 0)),
            pl.BlockSpec((d, tn), lambda i, j: (0, j)),
            pl.BlockSpec((d, LANES), lambda i, j: (0, 0)),
            pl.BlockSpec((32, d), lambda i, j: (0, 0)),
        ],
        out_specs=[
            pl.BlockSpec((tm, tn), lambda i, j: (i, j)),
            pl.BlockSpec((tm, LANES), lambda i, j: (i, 0)),
            pl.BlockSpec((32, tm), lambda i, j: (0, i)),
        ],
        out_shape=[
            jax.ShapeDtypeStruct((n, nproj), BF16),
            jax.ShapeDtypeStruct((n, LANES), F32),
            jax.ShapeDtypeStruct((32, n), F32),
        ],
        scratch_shapes=[pltpu.VMEM((tm, d), BF16)],
        compiler_params=_cparams(("arbitrary", "arbitrary")),
        name="inproj",
    )(x, norm_w, wbig, wsm, wsmt)


def _gdn_kernel(qkv_ref, z_ref, sm_ref, smt_ref, s0_ref, cb_ref, convw_ref, alog_c_ref, dtb_c_ref,
                alog_r_ref, dtb_r_ref, normw_ref, o_ref, snew_ref, cnew_ref, s_scr, xp_scr, conv_scr, *,
                l_valid):
    C = GDN_CHUNK
    n_sub = qkv_ref.shape[0] // C

    @pl.when(pl.program_id(1) == 0)
    def _():
        s_scr[...] = s0_ref[0]
        xp_scr[5:8, :] = cb_ref[0]

    row = _iota2((C, C), 0)
    col = _iota2((C, C), 1)
    incl = row >= col
    strict = row > col
    tril_b = jnp.where(incl, 1.0, 0.0).astype(BF16)
    triu_b = jnp.where(row <= col, 1.0, 0.0).astype(BF16)
    shift_b = jnp.concatenate(
        [jnp.where(col == row - i, 1.0, 0.0).astype(BF16) for i in (1, 2, 3)], axis=0)
    inv_levels = []
    sh = 1
    while (1 << sh) < C:
        inv_levels.append((((row >> sh) & 1) == 1) & ((col >> (sh + 1)) == (row >> (sh + 1)))
                          & (((col >> sh) & 1) == 0))
        sh += 1
    m1 = ((row & 1) == 1) & (col == row - 1)
    eye = jnp.where(row == col, 1.0, 0.0)
    w = convw_ref[...]
    normw = normw_ref[...]

    act, gl_col, gl_row, beta_col = [], [], [], []
    for sub in range(n_sub):
        rows = slice(sub * C, (sub + 1) * C)
        xb = qkv_ref[rows, :]
        x = xb.astype(F32)
        xp_scr[8:16, :] = x[0:8]
        shifted = _dot(shift_b, xb)
        conv_scr[rows, :] = (w[3:4] * x + w[2:3] * shifted[0:C] + w[1:2] * shifted[C:2 * C]
                             + w[0:1] * shifted[2 * C:3 * C])
        conv_scr[sub * C:sub * C + 8, :] = (w[0:1] * xp_scr[5:13, :] + w[1:2] * xp_scr[6:14, :]
                                            + w[2:3] * xp_scr[7:15, :] + w[3:4] * x[0:8])
        act.append(_silu(conv_scr[rows, :]))
        tail = x[l_valid - 3:l_valid, :]
        xp_scr[5:8, :] = tail
        cnew_ref[0] = tail

        sm = sm_ref[rows, :]
        g_col = -jnp.exp(alog_c_ref[...]) * _softplus(sm + dtb_c_ref[...])
        g_row = -jnp.exp(alog_r_ref[...]) * _softplus(smt_ref[:, rows] + dtb_r_ref[...])
        b_col = _sigmoid(sm)
        if l_valid < C:
            g_col = jnp.where(_iota2((C, LANES), 0) < l_valid, g_col, 0.0)
            b_col = jnp.where(_iota2((C, LANES), 0) < l_valid, b_col, 0.0)
            g_row = jnp.where(_iota2((HEADS, C), 1) < l_valid, g_row, 0.0)
        beta_col.append(b_col)
        gl_col.append(_mask_dot(tril_b, g_col))
        gl_row.append(_dot_mask(g_row, triu_b))

    hs = [slice(h * HEAD_DIM, (h + 1) * HEAD_DIM) for h in range(HEADS)]
    I = [(sub, h) for sub in range(n_sub) for h in range(HEADS)]
    gc = {(c, h): gl_col[c][:, SM_A + h:SM_A + h + 1] for c, h in I}
    beta = {(c, h): beta_col[c][:, SM_B + h:SM_B + h + 1] for c, h in I}
    qn, kn, kb, knb = {}, {}, {}, {}
    for c, h in I:
        q = act[c][:, hs[h]]
        k = act[c][:, WIDTH + h * HEAD_DIM:WIDTH + (h + 1) * HEAD_DIM]
        qn[c, h] = q * lax.rsqrt(jnp.sum(q * q, axis=-1, keepdims=True) + EPS) * (HEAD_DIM ** -0.5)
        kn[c, h] = k * lax.rsqrt(jnp.sum(k * k, axis=-1, keepdims=True) + EPS)
        kb[c, h] = kn[c, h] * beta[c, h]
        knb[c, h] = kn[c, h].astype(BF16)
    decay = {i: jnp.exp(jnp.where(incl, gc[i] - gl_row[i[0]][i[1]:i[1] + 1, :], NEG)) for i in I}
    a_mat = {i: jnp.where(strict, _dot_nt(kb[i].astype(BF16), knb[i]) * decay[i], 0.0) for i in I}
    qk = {i: (_dot_nt(qn[i].astype(BF16), knb[i]) * decay[i]).astype(BF16) for i in I}

    t = {i: eye - jnp.where(m1, a_mat[i], 0.0) for i in I}
    for m in inv_levels:
        tb = {i: t[i].astype(BF16) for i in I}
        y = {i: _dot(tb[i], jnp.where(m, a_mat[i], 0.0).astype(BF16)).astype(BF16) for i in I}
        t = {i: t[i] - _dot(y[i], tb[i]) for i in I}
    tb = {i: t[i].astype(BF16) for i in I}
    eg = {i: jnp.exp(gc[i]) for i in I}
    g_last = {i: gc[i][C - 1:C, :] for i in I}

    for c in range(n_sub):
        H = [(c, h) for h in range(HEADS)]
        s = {i: s_scr[i[1]] for i in H}
        sb = {i: s[i].astype(BF16) for i in H}
        r = {i: (act[c][:, 2 * WIDTH + i[1] * HEAD_DIM:2 * WIDTH + (i[1] + 1) * HEAD_DIM] * beta[i]
                 - _dot((kb[i] * eg[i]).astype(BF16), sb[i])).astype(BF16) for i in H}
        ub = {i: _dot(tb[i], r[i]).astype(BF16) for i in H}
        o = {i: _dot((qn[i] * eg[i]).astype(BF16), sb[i]) + _dot(qk[i], ub[i]) for i in H}
        for i in H:
            k_dec = (kn[i] * jnp.exp(g_last[i] - gc[i])).astype(BF16)
            s_scr[i[1]] = s[i] * jnp.exp(g_last[i]) + _dot_tn(k_dec, ub[i])
        for i in H:
            on = o[i] * lax.rsqrt(jnp.mean(o[i] * o[i], axis=-1, keepdims=True) + EPS) * normw
            z = z_ref[c * C:(c + 1) * C, hs[i[1]]].astype(F32)
            o_ref[c * C:(c + 1) * C, hs[i[1]]] = (on * _silu(z)).astype(BF16)

    snew_ref[0] = s_scr[...]


def _gdn(p, psm, psmt, s0, cb, conv_w, alog, dtb, norm_w, *, n_seq, l_valid, bcast_state):
    n = p.shape[0]
    C = GDN_CHUNK
    n_sub = _pick(n // (n_seq * C), (2, 1))
    R = n_sub * C
    nc = n // (n_seq * R)
    alog_c = jnp.zeros((1, LANES), F32).at[0, SM_A:SM_A + HEADS].set(alog)
    dtb_c = jnp.zeros((1, LANES), F32).at[0, SM_A:SM_A + HEADS].set(dtb)
    st_idx = (lambda b, c: (0, 0, 0, 0)) if bcast_state else (lambda b, c: (b, 0, 0, 0))
    cb_idx = (lambda b, c: (0, 0, 0)) if bcast_state else (lambda b, c: (b, 0, 0))
    return pl.pallas_call(
        functools.partial(_gdn_kernel, l_valid=l_valid),
        grid=(n_seq, nc),
        in_specs=[
            pl.BlockSpec((R, 3 * WIDTH), lambda b, c: (b * nc + c, 0)),
            pl.BlockSpec((R, WIDTH), lambda b, c: (b * nc + c, PB_GZ)),
            pl.BlockSpec((R, LANES), lambda b, c: (b * nc + c, 0)),
            pl.BlockSpec((HEADS, R), lambda b, c: (SM_A // HEADS, b * nc + c)),
            pl.BlockSpec((1, HEADS, HEAD_DIM, HEAD_DIM), st_idx),
            pl.BlockSpec((1, CONV_WIDTH - 1, 3 * WIDTH), cb_idx),
            pl.BlockSpec((CONV_WIDTH, 3 * WIDTH), lambda b, c: (0, 0)),
            pl.BlockSpec((1, LANES), lambda b, c: (0, 0)),
            pl.BlockSpec((1, LANES), lambda b, c: (0, 0)),
            pl.BlockSpec((HEADS, 1), lambda b, c: (0, 0)),
            pl.BlockSpec((HEADS, 1), lambda b, c: (0, 0)),
            pl.BlockSpec((1, HEAD_DIM), lambda b, c: (0, 0)),
        ],
        out_specs=[
            pl.BlockSpec((R, WIDTH), lambda b, c: (b * nc + c, 0)),
            pl.BlockSpec((1, HEADS, HEAD_DIM, HEAD_DIM), lambda b, c: (b, 0, 0, 0)),
            pl.BlockSpec((1, CONV_WIDTH - 1, 3 * WIDTH), lambda b, c: (b, 0, 0)),
        ],
        out_shape=[
            jax.ShapeDtypeStruct((n, WIDTH), BF16),
            jax.ShapeDtypeStruct((n_seq, HEADS, HEAD_DIM, HEAD_DIM), F32),
            jax.ShapeDtypeStruct((n_seq, CONV_WIDTH - 1, 3 * WIDTH), F32),
        ],
        scratch_shapes=[
            pltpu.VMEM((HEADS, HEAD_DIM, HEAD_DIM), F32),
            pltpu.VMEM((16, 3 * WIDTH), F32),
            pltpu.VMEM((R, 3 * WIDTH), F32),
        ],
        compiler_params=_cparams(("arbitrary", "arbitrary")),
        name="gdn",
    )(p, p, psm, psmt, s0, cb, conv_w, alog_c, dtb_c, alog.reshape(HEADS, 1), dtb.reshape(HEADS, 1),
      norm_w.reshape(1, HEAD_DIM))


def _foxprep_kernel(qkv_ref, sm_ref, smt_ref, c0c_ref, c0r_ref, qw_ref, kw_ref, fb_c_ref, fb_r_ref,
                    k32_ref, v32_ref, lfc_ref, lfr_ref, fc_ref, fr_ref, qt_ref, kcat_ref, vt_ref, *rest,
                    row_major):
    if row_major:
        q_ref, kb_ref, cc_scr, cr_scr = rest
    else:
        cc_scr, cr_scr = rest
    tm = qkv_ref.shape[0]

    @pl.when(pl.program_id(1) == 0)
    def _():
        cc_scr[...] = c0c_ref[0]
        cr_scr[...] = c0r_ref[0]

    lf_col = -_softplus(-(sm_ref[...] + fb_c_ref[...]))
    lf_row = -_softplus(-(smt_ref[...] + fb_r_ref[...]))
    lfc_ref[...] = lf_col
    lfr_ref[...] = lf_row
    row = _iota2((tm, tm), 0)
    col = _iota2((tm, tm), 1)
    tril_b = jnp.where(row >= col, 1.0, 0.0).astype(BF16)
    triu_b = jnp.where(row <= col, 1.0, 0.0).astype(BF16)
    f_col = cc_scr[...] + _mask_dot(tril_b, lf_col)
    f_row = cr_scr[...] + _dot_mask(lf_row, triu_b)
    fc_ref[...] = f_col
    fr_ref[...] = f_row
    cc_scr[...] = f_col[tm - 1:tm, :]
    cr_scr[...] = f_row[:, tm - 1:tm]

    qw = qw_ref[...]
    kw = kw_ref[...]
    lane = _iota2((tm, HEAD_DIM), 1)
    sub = _iota2((16, tm), 0)
    zeros_t = jnp.zeros((HEAD_DIM - 16, tm), BF16)
    for h in range(HEADS):
        hs = slice(h * HEAD_DIM, (h + 1) * HEAD_DIM)
        q = qkv_ref[:, hs].astype(F32)
        k = qkv_ref[:, WIDTH + h * HEAD_DIM:WIDTH + (h + 1) * HEAD_DIM].astype(F32)
        v = qkv_ref[:, 2 * WIDTH + h * HEAD_DIM:2 * WIDTH + (h + 1) * HEAD_DIM].astype(F32)
        qn = q * lax.rsqrt(jnp.mean(q * q, axis=-1, keepdims=True) + EPS) * qw
        kn = k * lax.rsqrt(jnp.mean(k * k, axis=-1, keepdims=True) + EPS) * kw
        k32_ref[pl.ds(h, tm, stride=HEADS), :] = kn
        v32_ref[pl.ds(h, tm, stride=HEADS), :] = v
        if row_major:
            q_ref[:, hs] = (qn * (HEAD_DIM ** -0.5)).astype(BF16)
            kb_ref[:, hs] = kn.astype(BF16)
        base = 2 * h * HEAD_DIM
        qt_ref[base:base + HEAD_DIM, :] = (qn * (LOG2E * HEAD_DIM ** -0.5)).T.astype(BF16)
        r1, r2, r3 = _split3(f_row[h:h + 1, :] * LOG2E)
        aug_q = jnp.where(sub == 0, r1.astype(F32), jnp.where(sub == 1, r2.astype(F32), jnp.where(
            sub == 2, r3.astype(F32), jnp.where(sub < 6, 1.0, 0.0))))
        qt_ref[base + HEAD_DIM:base + HEAD_DIM + 16, :] = aug_q.astype(BF16)
        qt_ref[base + HEAD_DIM + 16:base + 2 * HEAD_DIM, :] = zeros_t
        kcat_ref[:, base:base + HEAD_DIM] = kn.astype(BF16)
        c1, c2, c3 = _split3(f_col[:, SM_F + h:SM_F + h + 1] * LOG2E)
        aug_k = jnp.where(lane < 3, 1.0, jnp.where(lane == 3, -c1.astype(F32), jnp.where(
            lane == 4, -c2.astype(F32), jnp.where(lane == 5, -c3.astype(F32), 0.0))))
        kcat_ref[:, base + HEAD_DIM:base + 2 * HEAD_DIM] = aug_k.astype(BF16)
        vt_ref[hs, :] = v.T.astype(BF16)


def _foxprep(p, psm, psmt, c0c, c0r, q_norm_w, k_norm_w, f_bias, *, n_seq, bcast_carry, row_major, lead=0):
    n = p.shape[0]
    rows = n // n_seq
    tm = _pick(rows, (512, 256, 128))
    nt = rows // tm
    fb_c = jnp.zeros((1, LANES), F32).at[0, SM_F:SM_F + HEADS].set(f_bias)
    c_idx = (lambda b, t: (0, 0, 0)) if bcast_carry else (lambda b, t: (b, 0, 0))
    rowblk = lambda b, t: (b * nt + t, 0)
    colblk = lambda b, t: (0, b * nt + t)
    kv_spec = pl.BlockSpec((pl.Element(tm * HEADS), pl.Element(HEAD_DIM)),
                           lambda b, t: ((b * (rows + lead) + lead + t * tm) * HEADS, 0))
    kv_shape = jax.ShapeDtypeStruct((n_seq * (rows + lead) * HEADS, HEAD_DIM), F32)
    out_specs = [
        kv_spec,
        kv_spec,
        pl.BlockSpec((tm, LANES), rowblk),
        pl.BlockSpec((HEADS, tm), colblk),
        pl.BlockSpec((tm, LANES), rowblk),
        pl.BlockSpec((HEADS, tm), colblk),
        pl.BlockSpec((2 * WIDTH, tm), colblk),
        pl.BlockSpec((tm, 2 * WIDTH), rowblk),
        pl.BlockSpec((WIDTH, tm), colblk),
    ]
    out_shape = [
        kv_shape,
        kv_shape,
        jax.ShapeDtypeStruct((n, LANES), F32),
        jax.ShapeDtypeStruct((HEADS, n), F32),
        jax.ShapeDtypeStruct((n, LANES), F32),
        jax.ShapeDtypeStruct((HEADS, n), F32),
        jax.ShapeDtypeStruct((2 * WIDTH, n), BF16),
        jax.ShapeDtypeStruct((n, 2 * WIDTH), BF16),
        jax.ShapeDtypeStruct((WIDTH, n), BF16),
    ]
    if row_major:
        out_specs += [pl.BlockSpec((tm, WIDTH), rowblk), pl.BlockSpec((tm, WIDTH), rowblk)]
        out_shape += [jax.ShapeDtypeStruct((n, WIDTH), BF16),
                      jax.ShapeDtypeStruct((n, WIDTH), BF16)]
    return pl.pallas_call(
        functools.partial(_foxprep_kernel, row_major=row_major),
        grid=(n_seq, nt),
        in_specs=[
            pl.BlockSpec((tm, 3 * WIDTH), lambda b, t: (b * nt + t, 1)),
            pl.BlockSpec((tm, LANES), rowblk),
            pl.BlockSpec((HEADS, tm), lambda b, t: (SM_F // HEADS, b * nt + t)),
            pl.BlockSpec((1, 1, LANES), c_idx),
            pl.BlockSpec((1, HEADS, 1), c_idx),
            pl.BlockSpec((1, HEAD_DIM), lambda b, t: (0, 0)),
            pl.BlockSpec((1, HEAD_DIM), lambda b, t: (0, 0)),
            pl.BlockSpec((1, LANES), lambda b, t: (0, 0)),
            pl.BlockSpec((HEADS, 1), lambda b, t: (0, 0)),
        ],
        out_specs=out_specs,
        out_shape=out_shape,
        scratch_shapes=[pltpu.VMEM((1, LANES), F32), pltpu.VMEM((HEADS, 1), F32)],
        compiler_params=_cparams(("arbitrary", "arbitrary")),
        name="foxprep",
    )(p, psm, psmt, c0c, c0r, q_norm_w.reshape(1, HEAD_DIM), k_norm_w.reshape(1, HEAD_DIM), fb_c,
      f_bias.reshape(HEADS, 1))


def _fill_lead_kernel(src_ref, big_ref, o_ref):
    o_ref[...] = src_ref[...]


def _fill_lead(big, src, *, n_seq):
    r = src.shape[0]
    seq_rows = big.shape[0] // n_seq
    assert seq_rows % r == 0
    return pl.pallas_call(
        _fill_lead_kernel,
        grid=(n_seq,),
        in_specs=[pl.BlockSpec((r, HEAD_DIM), lambda b: (0, 0)), pl.BlockSpec(memory_space=pl.ANY)],
        out_specs=pl.BlockSpec((r, HEAD_DIM), lambda b: (b * (seq_rows // r), 0)),
        out_shape=jax.ShapeDtypeStruct(big.shape, big.dtype),
        input_output_aliases={1: 0},
        compiler_params=_cparams(("arbitrary",)),
        name="fill_lead",
    )(src, big)


def _logf_kernel(meta_ref, lf_ref, o_ref):
    o_ref[0, 0:N_META, :] = meta_ref[0:N_META, SM_F:SM_F + HEADS]
    o_ref[0, N_META:, :] = lf_ref[:, SM_F:SM_F + HEADS]


def _assemble_logf(lf_small, lf_prompt, *, n_seq):
    s = lf_prompt.shape[0] // n_seq
    return pl.pallas_call(
        _logf_kernel,
        grid=(n_seq,),
        in_specs=[pl.BlockSpec((GDN_CHUNK, LANES), lambda b: (0, 0)), pl.BlockSpec((s, LANES), lambda b: (b, 0))],
        out_specs=pl.BlockSpec((1, N_META + s, HEADS), lambda b: (b, 0, 0)),
        out_shape=jax.ShapeDtypeStruct((n_seq, N_META + s, HEADS), F32),
        compiler_params=_cparams(("arbitrary",)),
        name="assemble_logf",
    )(lf_small, lf_prompt)


FOX_UNIT = 256

def _fox_kernel(qi_ref, ki_ref, qt_ref, kcat_ref, vt_ref, og_ref, km_ref, vtm_ref, o_ref, m_scr, l_scr, acc_scr):
    pair = pl.program_id(1)
    qi = qi_ref[pair]
    ki = ki_ref[pair]
    tq = qt_ref.shape[1]
    tk = kcat_ref.shape[0]
    U = FOX_UNIT
    H = range(HEADS)

    def unit(qh, k_rows, vt_cols, mask):
        qs = slice(qh * U, (qh + 1) * U)
        s = [_dot(k_rows(h), qt_ref[2 * h * HEAD_DIM:2 * (h + 1) * HEAD_DIM, qs]) for h in H]
        if mask is not None:
            s = [jnp.where(mask, s[h], NEG) for h in H]
        m_old = [m_scr[h:h + 1, qs] for h in H]
        m_new = [jnp.maximum(m_old[h], jnp.max(s[h], axis=0, keepdims=True)) for h in H]
        alpha = [jnp.exp2(m_old[h] - m_new[h]) for h in H]
        p = [jnp.exp2(s[h] - m_new[h]) for h in H]
        for h in H:
            l_scr[h:h + 1, qs] = alpha[h] * l_scr[h:h + 1, qs] + jnp.sum(p[h], axis=0, keepdims=True)
            m_scr[h:h + 1, qs] = m_new[h]
        for h in H:
            acc_scr[h, :, qs] = alpha[h] * acc_scr[h, :, qs] + _dot(vt_cols(h), p[h].astype(BF16))

    def k_unit(ku):
        return lambda h: kcat_ref[ku * U:(ku + 1) * U, 2 * h * HEAD_DIM:2 * (h + 1) * HEAD_DIM]

    def vt_unit(ku):
        return lambda h: vt_ref[h * HEAD_DIM:(h + 1) * HEAD_DIM, ku * U:(ku + 1) * U]

    @pl.when(ki == 0)
    def _():
        m_scr[...] = jnp.full(m_scr.shape, NEG, F32)
        l_scr[...] = jnp.zeros(l_scr.shape, F32)
        acc_scr[...] = jnp.zeros(acc_scr.shape, F32)
        tmeta = km_ref.shape[0]
        mask = _iota2((tmeta, U), 0) < N_META
        for qh in range(tq // U):
            unit(qh, lambda h: km_ref[:, 2 * h * HEAD_DIM:2 * (h + 1) * HEAD_DIM],
                 lambda h: vtm_ref[h * HEAD_DIM:(h + 1) * HEAD_DIM, :], mask)

    @pl.when(ki < qi)
    def _():
        for qh in range(tq // U):
            for ku in range(tk // U):
                unit(qh, k_unit(ku), vt_unit(ku), None)

    @pl.when(ki == qi)
    def _():
        diag = _iota2((U, U), 0) <= _iota2((U, U), 1)
        for qh in range(tq // U):
            for ku in range(qh + 1):
                unit(qh, k_unit(ku), vt_unit(ku), diag if ku == qh else None)
        for h in H:
            hs = slice(h * HEAD_DIM, (h + 1) * HEAD_DIM)
            gate = _sigmoid(og_ref[:, hs].astype(F32))
            o_ref[:, hs] = ((acc_scr[h] / l_scr[h:h + 1, :]).T * gate).astype(BF16)


def _fox_prompt(qt, kcat, vt, p, kcat_small, vt_small, *, n_seq):
    n = p.shape[0]
    rows = n // n_seq
    tq = _pick(rows, (512, 256))
    nq = rows // tq
    pairs = [(i, j) for i in range(nq) for j in range(i + 1)]
    qi = jnp.asarray([a for a, _ in pairs], jnp.int32)
    ki = jnp.asarray([b for _, b in pairs], jnp.int32)
    grid_spec = pltpu.PrefetchScalarGridSpec(
        num_scalar_prefetch=2,
        grid=(n_seq, len(pairs)),
        in_specs=[
            pl.BlockSpec((2 * WIDTH, tq), lambda b, t, qi_r, ki_r: (0, b * nq + qi_r[t])),
            pl.BlockSpec((tq, 2 * WIDTH), lambda b, t, qi_r, ki_r: (b * nq + ki_r[t], 0)),
            pl.BlockSpec((WIDTH, tq), lambda b, t, qi_r, ki_r: (0, b * nq + ki_r[t])),
            pl.BlockSpec((tq, WIDTH), lambda b, t, qi_r, ki_r: (b * nq + qi_r[t], PB_FOG)),
            pl.BlockSpec((GDN_CHUNK, 2 * WIDTH), lambda b, t, qi_r, ki_r: (0, 0)),
            pl.BlockSpec((WIDTH, GDN_CHUNK), lambda b, t, qi_r, ki_r: (0, 0)),
        ],
        out_specs=pl.BlockSpec((tq, WIDTH), lambda b, t, qi_r, ki_r: (b * nq + qi_r[t], 0)),
        scratch_shapes=[
            pltpu.VMEM((HEADS, tq), F32),
            pltpu.VMEM((HEADS, tq), F32),
            pltpu.VMEM((HEADS, HEAD_DIM, tq), F32),
        ],
    )
    return pl.pallas_call(
        _fox_kernel,
        grid_spec=grid_spec,
        out_shape=jax.ShapeDtypeStruct((n, WIDTH), BF16),
        compiler_params=_cparams(("arbitrary", "arbitrary")),
        name="fox_prompt",
    )(qi, ki, qt, kcat, vt, p, kcat_small, vt_small)


def _fox_sample_kernel(q_ref, kn_ref, vn_ref, og_ref, lfc_ref, lfr_ref, ck_ref, cv_ref, clc_ref, clr_ref,
                       o_ref, *, l_valid):
    C = q_ref.shape[0]
    P = ck_ref.shape[1] // HEADS
    rowp = _iota2((P, P), 0)
    colp = _iota2((P, P), 1)
    triu_p = jnp.where(rowp <= colp, 1.0, 0.0).astype(BF16)
    row = _iota2((C, C), 0)
    col = _iota2((C, C), 1)
    tril_b = jnp.where(row >= col, 1.0, 0.0).astype(BF16)
    triu_b = jnp.where(row <= col, 1.0, 0.0).astype(BF16)

    f_cache = _dot_mask(clr_ref[0], triu_p)
    carry_r = f_cache[:, P - 1:P]
    carry_c = jnp.sum(clc_ref[0], axis=0, keepdims=True)
    f_new_r = carry_r + _dot_mask(lfr_ref[...], triu_b)
    f_new_c = carry_c + _mask_dot(tril_b, lfc_ref[...])[:, SM_F:SM_F + HEADS]
    mask_new = (col <= row) & (col < l_valid)

    for h in range(HEADS):
        hs = slice(h * HEAD_DIM, (h + 1) * HEAD_DIM)
        q = q_ref[:, hs]
        fq = f_new_c[:, h:h + 1]
        ck = ck_ref[0, pl.ds(h, P, stride=HEADS), :].astype(BF16)
        cv = cv_ref[0, pl.ds(h, P, stride=HEADS), :].astype(BF16)
        s_c = _dot_nt(q, ck) + (fq - f_cache[h:h + 1, :])
        s_n = jnp.where(mask_new, _dot_nt(q, kn_ref[:, hs]) + (fq - f_new_r[h:h + 1, :]), NEG)
        m = jnp.maximum(jnp.max(s_c, axis=-1, keepdims=True), jnp.max(s_n, axis=-1, keepdims=True))
        p_c = jnp.exp(s_c - m)
        p_n = jnp.exp(s_n - m)
        l = jnp.sum(p_c, axis=-1, keepdims=True) + jnp.sum(p_n, axis=-1, keepdims=True)
        o = _dot(p_c.astype(BF16), cv) + _dot(p_n.astype(BF16), vn_ref[:, hs])
        gate = _sigmoid(og_ref[:, hs].astype(F32))
        o_ref[:, hs] = (o / l * gate).astype(BF16)


def _fox_sample(q, kb, p, lf_col, lf_row, cache_k, cache_v, cache_lf, *, l_valid):
    bs, past = cache_k.shape[0], cache_k.shape[1]
    C = GDN_CHUNK
    n = q.shape[0]
    blk = lambda b: (b + 1, 0)
    return pl.pallas_call(
        functools.partial(_fox_sample_kernel, l_valid=l_valid),
        grid=(bs,),
        in_specs=[
            pl.BlockSpec((C, WIDTH), blk),
            pl.BlockSpec((C, WIDTH), blk),
            pl.BlockSpec((C, WIDTH), lambda b: (b + 1, PB_FV)),
            pl.BlockSpec((C, WIDTH), lambda b: (b + 1, PB_FOG)),
            pl.BlockSpec((C, LANES), blk),
            pl.BlockSpec((HEADS, C), lambda b: (0, b + 1)),
            pl.BlockSpec((1, past * HEADS, HEAD_DIM), lambda b: (b, 0, 0)),
            pl.BlockSpec((1, past * HEADS, HEAD_DIM), lambda b: (b, 0, 0)),
            pl.BlockSpec((1, past, HEADS), lambda b: (b, 0, 0)),
            pl.BlockSpec((1, HEADS, past), lambda b: (b, 0, 0)),
        ],
        out_specs=pl.BlockSpec((C, WIDTH), lambda b: (b, 0)),
        out_shape=jax.ShapeDtypeStruct((bs * C, WIDTH), BF16),
        compiler_params=_cparams(("arbitrary",)),
        name="fox_sample",
    )(q, kb, p, p, lf_col, lf_row, cache_k.reshape(bs, past * HEADS, HEAD_DIM),
      cache_v.reshape(bs, past * HEADS, HEAD_DIM), cache_lf, jnp.swapaxes(cache_lf, 1, 2))


def _outproj_kernel(og_ref, of_ref, x_ref, wo_ref, nw_ref, wr_ref, br_ref, cnt0_ref, xr_ref, rt_ref, cnt_ref,
                    cnt_scr):
    d = x_ref.shape[1]

    @pl.when(pl.program_id(0) == 0)
    def _():
        cnt_scr[...] = cnt0_ref[...]

    half = og_ref.shape[1]
    h = _dot(og_ref[...], wo_ref[0:half, :]) + _dot(of_ref[...], wo_ref[half:2 * half, :])
    x1 = x_ref[...] + h
    xr_ref[:, 0:d] = x1
    xn = x1 * lax.rsqrt(jnp.mean(x1 * x1, axis=-1, keepdims=True) + EPS) * nw_ref[...]

    x_hi = xn.astype(BF16)
    x_lo = (xn - x_hi.astype(F32)).astype(BF16)
    logits = (_dot(x_hi, wr_ref[0]) + _dot(x_lo, wr_ref[0]) + _dot(x_hi, wr_ref[1])) + br_ref[...]

    tm = logits.shape[0]
    lane = _iota2((tm, LANES), 1).astype(F32)
    big = float(LANES)
    gl = jnp.where(lane < N_GROUPS, logits, NEG)
    gmax = jnp.max(gl, axis=-1, keepdims=True)
    gidx = jnp.min(jnp.where(gl == gmax, lane, big), axis=-1, keepdims=True)
    p_top = 1.0 / jnp.sum(jnp.exp(gl - gmax), axis=-1, keepdims=True)
    e = lane - RT_E0
    sel = (e >= 0) & (e < N_EXPERTS) & (jnp.floor(e * (1.0 / EXPERTS_PER_GROUP)) == gidx)
    el = jnp.where(sel, logits, NEG)
    v1 = jnp.max(el, axis=-1, keepdims=True)
    i1 = jnp.min(jnp.where(el == v1, lane, big), axis=-1, keepdims=True)
    el2 = jnp.where(lane == i1, NEG, el)
    v2 = jnp.max(el2, axis=-1, keepdims=True)
    i2 = jnp.min(jnp.where(el2 == v2, lane, big), axis=-1, keepdims=True)
    e2 = jnp.exp(v2 - v1)
    w1 = p_top / (1.0 + e2)
    w2 = p_top * e2 / (1.0 + e2)

    ex1 = i1 - RT_E0
    ex2 = i2 - RT_E0
    first = ex1 < ex2
    ea = jnp.where(first, ex1, ex2)
    eb = jnp.where(first, ex2, ex1)
    wa = jnp.where(first, w1, w2)
    wb = jnp.where(first, w2, w1)
    la = ea - gidx * EXPERTS_PER_GROUP
    lb = eb - gidx * EXPERTS_PER_GROUP
    cls = gidx * PAIRS_PER_GROUP + la * (2 * EXPERTS_PER_GROUP - 1 - la) * 0.5 + (lb - la - 1.0)

    onehot = lane == cls
    oh = jnp.where(onehot, 1.0, 0.0)
    strict_b = jnp.where(_iota2((tm, tm), 0) > _iota2((tm, tm), 1), 1.0, 0.0).astype(BF16)
    before = cnt_scr[...] + _dot(strict_b, oh.astype(BF16))
    rank = jnp.sum(jnp.where(onehot, before, 0.0), axis=-1, keepdims=True)
    cnt_scr[...] += jnp.sum(oh, axis=0, keepdims=True)
    cnt_ref[...] = cnt_scr[...]
    route = jnp.where(lane == RT_CLS, cls, jnp.where(lane == RT_RANK, rank, jnp.where(
        lane == RT_WA, wa, jnp.where(lane == RT_WB, wb, 0.0))))
    rt_ref[...] = route
    xr_ref[:, d:d + LANES] = route


def _outproj(og, of, x, wo, norm_w, wr, br, cnt0):
    n, d = x.shape
    tm = _pick(n, (512, 384, 256, 128))
    rowblk = lambda i: (i, 0)
    return pl.pallas_call(
        _outproj_kernel,
        grid=(n // tm,),
        in_specs=[
            pl.BlockSpec((tm, WIDTH), rowblk),
            pl.BlockSpec((tm, WIDTH), rowblk),
            pl.BlockSpec((tm, d), rowblk),
            pl.BlockSpec((2 * WIDTH, d), lambda i: (0, 0)),
            pl.BlockSpec((1, d), lambda i: (0, 0)),
            pl.BlockSpec((2, d, LANES), lambda i: (0, 0, 0)),
            pl.BlockSpec((1, LANES), lambda i: (0, 0)),
            pl.BlockSpec((1, LANES), lambda i: (0, 0)),
        ],
        out_specs=[
            pl.BlockSpec((tm, d + LANES), rowblk),
            pl.BlockSpec((tm, LANES), rowblk),
            pl.BlockSpec((1, LANES), lambda i: (0, 0)),
        ],
        out_shape=[
            jax.ShapeDtypeStruct((n, d + LANES), F32),
            jax.ShapeDtypeStruct((n, LANES), F32),
            jax.ShapeDtypeStruct((1, LANES), F32),
        ],
        scratch_shapes=[pltpu.VMEM((1, LANES), F32)],
        compiler_params=_cparams(("arbitrary",)),
        name="outproj_router",
    )(og, of, x, wo, norm_w, wr, br, cnt0)


def _pos_rows(tm):
    return -(-(-(-tm // LANES)) // 8) * 8


def _tile_positions(pos, tm):
    nt = pos.shape[0] // tm
    rows = _pos_rows(tm)
    p = jnp.pad(pos.reshape(nt, tm), ((0, 0), (0, rows * LANES - tm)))
    return p.reshape(nt * rows, LANES)


def _row_copy_loops(tm, pos_smem, make_copy):
    def start_row(r, carry):
        base = pl.multiple_of(r * LANES, LANES)
        for c in range(LANES):
            make_copy(base + c, pos_smem[r, c]).start()
        return carry

    def wait(t, carry):
        make_copy(0, 0).wait()
        return carry

    lax.fori_loop(0, tm // LANES, start_row, 0)
    lax.fori_loop(0, tm, wait, 0, unroll=8)


def _dispatch_kernel(ends_ref, xr_ref, pos_hbm, *rest, zero_fill):
    if zero_fill:
        xs_hbm, pos_smem, zbuf, sem_idx, sem_fill, sem_rows = rest
    else:
        _, xs_hbm, pos_smem, sem_idx, sem_rows = rest
    i = pl.program_id(0)
    tm = xr_ref.shape[0]
    rows = pos_smem.shape[0]
    idx_cp = pltpu.make_async_copy(pos_hbm.at[pl.ds(pl.multiple_of(i * rows, 8), rows)], pos_smem, sem_idx)
    idx_cp.start()

    if zero_fill:
        @pl.when(i == 0)
        def _():
            zbuf[...] = jnp.zeros(zbuf.shape, F32)

            def fill(start):
                def body(c, carry):
                    lo = jnp.where(c == 0, 0, ends_ref[jnp.maximum(c - 1, 0)])
                    hi = ends_ref[c]

                    @pl.when(hi > lo)
                    def _():
                        cp = pltpu.make_async_copy(
                            zbuf, xs_hbm.at[pl.ds(pl.multiple_of(hi - MOE_TS, MOE_TS), MOE_TS)], sem_fill)
                        if start:
                            cp.start()
                        else:
                            cp.wait()
                    return carry
                lax.fori_loop(0, N_CLASSES, body, 0)

            fill(True)
            fill(False)

    idx_cp.wait()
    _row_copy_loops(tm, pos_smem, lambda t, p: pltpu.make_async_copy(
        xr_ref.at[pl.ds(t, 1)], xs_hbm.at[pl.ds(p, 1)], sem_rows))


def _dispatch(ends, xr, pos, xs, *, n_sorted):
    n, dw = xr.shape
    tm = _pick(n, (1024, 1152, 512, 384, 256, 128))
    rows = _pos_rows(tm)
    zero_fill = xs is None
    any_spec = pl.BlockSpec(memory_space=pl.ANY)
    in_specs = [pl.BlockSpec((tm, dw), lambda i, ends_r: (i, 0)), any_spec]
    args = [xr, _tile_positions(pos, tm)]
    scratch = [pltpu.SMEM((rows, LANES), jnp.int32)]
    if zero_fill:
        scratch += [pltpu.VMEM((MOE_TS, dw), F32), pltpu.SemaphoreType.DMA, pltpu.SemaphoreType.DMA,
                    pltpu.SemaphoreType.DMA]
        aliases = {}
    else:
        in_specs.append(any_spec)
        args.append(xs)
        scratch += [pltpu.SemaphoreType.DMA, pltpu.SemaphoreType.DMA]
        aliases = {3: 0}
    return pl.pallas_call(
        functools.partial(_dispatch_kernel, zero_fill=zero_fill),
        grid_spec=pltpu.PrefetchScalarGridSpec(
            num_scalar_prefetch=1, grid=(n // tm,), in_specs=in_specs, out_specs=any_spec,
            scratch_shapes=scratch),
        out_shape=jax.ShapeDtypeStruct((n_sorted, dw), F32),
        input_output_aliases=aliases,
        compiler_params=pltpu.CompilerParams(dimension_semantics=("arbitrary",), vmem_limit_bytes=VMEM_LIMIT,
                                             has_side_effects=True),
        name="moe_dispatch",
    )(ends, *args)


def _ffn_kernel(ta_ref, tb_ref, nv_ref, xs_ref, w1a_ref, w3a_ref, w2a_ref, w1b_ref, w3b_ref, w2b_ref,
                nffn_ref, nfin_ref, ys_ref):
    @pl.when(pl.program_id(0) < nv_ref[0])
    def _():
        d = ys_ref.shape[1]
        x1 = xs_ref[:, 0:d]
        route = xs_ref[:, d:d + LANES]
        xn = (x1 * lax.rsqrt(jnp.mean(x1 * x1, axis=-1, keepdims=True) + EPS) * nffn_ref[...]).astype(BF16)

        ga, gb = _dot(xn, w1a_ref[0]), _dot(xn, w1b_ref[0])
        ua, ub = _dot(xn, w3a_ref[0]), _dot(xn, w3b_ref[0])
        ha = (_silu(ga) * ua * route[:, RT_WA:RT_WA + 1]).astype(BF16)
        hb = (_silu(gb) * ub * route[:, RT_WB:RT_WB + 1]).astype(BF16)
        x2 = x1 + _dot(ha, w2a_ref[0]) + _dot(hb, w2b_ref[0])
        ys_ref[...] = x2 * lax.rsqrt(jnp.mean(x2 * x2, axis=-1, keepdims=True) + EPS) * nfin_ref[...]


def _ffn(tile_a, tile_b, n_valid, xs, w1, w3, w2, nffn, nfin):
    ns, dw = xs.shape
    d = dw - LANES
    de = w1.shape[2]
    row = lambda t, ta, tb, nv: (jnp.minimum(t, nv[0] - 1), 0)
    wa = lambda t, ta, tb, nv: (ta[t], 0, 0)
    wb = lambda t, ta, tb, nv: (tb[t], 0, 0)
    const = lambda t, ta, tb, nv: (0, 0)
    wspec = pl.BlockSpec
    return pl.pallas_call(
        _ffn_kernel,
        grid_spec=pltpu.PrefetchScalarGridSpec(
            num_scalar_prefetch=3,
            grid=(ns // MOE_TS,),
            in_specs=[
                pl.BlockSpec((MOE_TS, dw), row),
                wspec((1, d, de), wa), wspec((1, d, de), wa), wspec((1, de, d), wa),
                wspec((1, d, de), wb), wspec((1, d, de), wb), wspec((1, de, d), wb),
                pl.BlockSpec((1, d), const), pl.BlockSpec((1, d), const),
            ],
            out_specs=pl.BlockSpec((MOE_TS, d), row),
        ),
        out_shape=jax.ShapeDtypeStruct((ns, d), F32),
        compiler_params=_cparams(("arbitrary",)),
        name="moe_ffn",
    )(tile_a, tile_b, n_valid, xs, w1, w3, w2, w1, w3, w2, nffn, nfin)


def _unsort_kernel(pos_hbm, ys_hbm, y_ref, pos_smem, sem_idx, sem_rows):
    i = pl.program_id(0)
    tm = y_ref.shape[0]
    rows = pos_smem.shape[0]
    idx_cp = pltpu.make_async_copy(pos_hbm.at[pl.ds(pl.multiple_of(i * rows, 8), rows)], pos_smem, sem_idx)
    idx_cp.start()
    idx_cp.wait()
    _row_copy_loops(tm, pos_smem, lambda t, p: pltpu.make_async_copy(
        ys_hbm.at[pl.ds(p, 1)], y_ref.at[pl.ds(t, 1)], sem_rows))


def _unsort(ys, pos):
    n = pos.shape[0]
    d = ys.shape[1]
    tm = _pick(n, (1024, 1152, 512, 384, 256, 128))
    any_spec = pl.BlockSpec(memory_space=pl.ANY)
    return pl.pallas_call(
        _unsort_kernel,
        grid=(n // tm,),
        in_specs=[any_spec, any_spec],
        out_specs=pl.BlockSpec((tm, d), lambda i: (i, 0)),
        out_shape=jax.ShapeDtypeStruct((n, d), F32),
        scratch_shapes=[pltpu.SMEM((_pos_rows(tm), LANES), jnp.int32), pltpu.SemaphoreType.DMA,
                        pltpu.SemaphoreType.DMA],
        compiler_params=_cparams(("arbitrary",)),
        name="moe_unsort",
    )(_tile_positions(pos, tm), ys)


def _pair_tables():
    a, b = [], []
    for g in range(N_GROUPS):
        for la in range(EXPERTS_PER_GROUP):
            for lb in range(la + 1, EXPERTS_PER_GROUP):
                a.append(g * EXPERTS_PER_GROUP + la)
                b.append(g * EXPERTS_PER_GROUP + lb)
    return jnp.asarray(a, jnp.int32), jnp.asarray(b, jnp.int32)


def _moe(xr_list, rt_list, cnt, w1, w3, w2, nffn, nfin):
    n_total = sum(x.shape[0] for x in xr_list)
    n_sorted = (-(-n_total // MOE_TS) + N_CLASSES) * MOE_TS
    counts = cnt[0, :N_CLASSES].astype(jnp.int32)
    padded = (counts + MOE_TS - 1) // MOE_TS * MOE_TS
    ends = jnp.cumsum(padded)
    offs = ends - padded
    classes = jnp.arange(N_CLASSES, dtype=jnp.int32)

    def position(rt):
        cls = rt[:, RT_CLS].astype(jnp.int32)
        return jnp.sum(jnp.where(cls[:, None] == classes, offs, 0), axis=1) + rt[:, RT_RANK].astype(jnp.int32)

    pos_list = [position(rt) for rt in rt_list]
    tile_start = jnp.arange(n_sorted // MOE_TS, dtype=jnp.int32) * MOE_TS
    tile_cls = jnp.minimum(jnp.sum((ends <= tile_start[:, None]).astype(jnp.int32), axis=1), N_CLASSES - 1)
    pair_a, pair_b = _pair_tables()
    n_valid = (ends[N_CLASSES - 1] // MOE_TS).reshape(1)
    xs = None
    for xr, pos in zip(xr_list, pos_list):
        xs = _dispatch(ends, xr, pos, xs, n_sorted=n_sorted)
    tile_is = tile_cls[:, None] == classes
    tile_a = jnp.sum(jnp.where(tile_is, pair_a, 0), axis=1)
    tile_b = jnp.sum(jnp.where(tile_is, pair_b, 0), axis=1)
    ys = _ffn(tile_a, tile_b, n_valid, xs, w1, w3, w2, nffn, nfin)
    return [_unsort(ys, pos) for pos in pos_list]


def kernel(x_prompt, x_sample, cache_fox_k, cache_fox_v, cache_fox_logf, state_gdn, state_gdn_conv, meta_tokens, norm_mix_w, w_in, gdn_conv_w, gdn_A_log, gdn_dt_bias, gdn_norm_w, fox_q_norm_w, fox_k_norm_w, fox_f_bias, w_out, norm_ffn_w, w_router_group, b_router_group, w_router_expert, b_router_expert, w_gate, w_up, w_down, norm_final_w):
    B, S, D = x_prompt.shape
    BS, LS, _ = x_sample.shape
    C = GDN_CHUNK
    assert w_in.shape[0] == 1, "single-layer step only"
    assert S % C == 0 and LS <= C and N_META <= C and meta_tokens.shape[0] == N_META

    wi = w_in[0]
    o = 0
    parts = {}
    for name, size in (("g_qkv", 3 * WIDTH), ("g_z", WIDTH), ("g_a", HEADS), ("g_b", HEADS),
                       ("f_qkv", 3 * WIDTH), ("f_og", WIDTH), ("f_f", HEADS)):
        parts[name] = wi[:, o:o + size]
        o += size
    wbig = jnp.concatenate([parts[k].astype(BF16) for k in ("g_qkv", "f_qkv", "g_z", "f_og")], axis=1)
    wsm_cols = jnp.concatenate([parts["f_f"], parts["g_a"], parts["g_b"]], axis=1)
    wsm = jnp.pad(wsm_cols, ((0, 0), (0, LANES - 3 * HEADS))).astype(BF16)
    wsmt = jnp.pad(wsm_cols.T, ((0, 32 - 3 * HEADS), (0, 0))).astype(BF16)
    wo = w_out[0].astype(BF16)
    wr32 = jnp.pad(jnp.concatenate([w_router_group[0], w_router_expert[0]], axis=1),
                   ((0, 0), (0, LANES - N_GROUPS - N_EXPERTS)))
    wr_hi = wr32.astype(BF16)
    wr = jnp.stack([wr_hi, (wr32 - wr_hi.astype(F32)).astype(BF16)])
    br = jnp.pad(jnp.concatenate([b_router_group[0], b_router_expert[0]]),
                 (0, LANES - N_GROUPS - N_EXPERTS)).reshape(1, LANES)
    w1 = w_gate[0].astype(BF16)
    w3 = w_up[0].astype(BF16)
    w2 = w_down[0].astype(BF16)
    nmix = norm_mix_w[0].reshape(1, D)
    nffn = norm_ffn_w[0].reshape(1, D)
    nfin = norm_final_w.reshape(1, D)

    x_small = jnp.concatenate([
        jnp.pad(meta_tokens.astype(F32), ((0, C - N_META), (0, 0))),
        jnp.pad(x_sample, ((0, 0), (0, C - LS), (0, 0))).reshape(BS * C, D)], axis=0)
    xp = x_prompt.reshape(B * S, D)

    p_s, psm_s, psmt_s = _inproj(x_small, nmix, wbig, wsm, wsmt)
    p_p, psm_p, psmt_p = _inproj(xp, nmix, wbig, wsm, wsmt)

    s0_s = jnp.concatenate([jnp.zeros((1,) + state_gdn.shape[2:], F32), state_gdn[0]], axis=0)
    cb_s = jnp.concatenate([jnp.zeros((1,) + state_gdn_conv.shape[2:], F32), state_gdn_conv[0]], axis=0)
    gdn_args = (gdn_conv_w[0], gdn_A_log[0], gdn_dt_bias[0], gdn_norm_w[0])
    og_s, st_s, cv_s = _gdn(p_s, psm_s, psmt_s, s0_s, cb_s, *gdn_args, n_seq=1 + BS, l_valid=LS,
                            bcast_state=False)
    og_p, st_p, cv_p = _gdn(p_p, psm_p, psmt_p, st_s[0:1], cv_s[0:1], *gdn_args, n_seq=B, l_valid=C,
                            bcast_state=True)

    fox_args = (fox_q_norm_w[0], fox_k_norm_w[0], fox_f_bias[0])
    zc = jnp.zeros((1, 1, LANES), F32)
    zr = jnp.zeros((1, HEADS, 1), F32)
    k32_s, v32_s, lfc_s, lfr_s, fc_s, fr_s, qt_s, kcat_s, vt_s, q_s, kb_s = _foxprep(
        p_s, psm_s, psmt_s, zc, zr, *fox_args, n_seq=1 + BS, bcast_carry=True, row_major=True)
    c0c = fc_s[N_META - 1:N_META, :].reshape(1, 1, LANES)
    c0r = fr_s[:, N_META - 1:N_META].reshape(1, HEADS, 1)
    k32_p, v32_p, lfc_p, lfr_p, fc_p, fr_p, qt_p, kcat_p, vt_p = _foxprep(
        p_p, psm_p, psmt_p, c0c, c0r, *fox_args, n_seq=B, bcast_carry=True, row_major=False, lead=N_META)
    of_p = _fox_prompt(qt_p, kcat_p, vt_p, p_p, kcat_s, vt_s, n_seq=B)
    of_s = _fox_sample(q_s, kb_s, p_s, lfc_s, lfr_s, cache_fox_k[0], cache_fox_v[0], cache_fox_logf[0],
                       l_valid=LS)

    xr_p, rt_p, cnt_p = _outproj(og_p, of_p, xp, wo, nffn, wr, br, jnp.zeros((1, LANES), F32))
    xr_s, rt_s, cnt = _outproj(og_s[C:], of_s, x_small[C:], wo, nffn, wr, br, cnt_p)
    y_p, y_s = _moe([xr_p, xr_s], [rt_p, rt_s], cnt, w1, w3, w2, nffn, nfin)

    def prompt_kv(tab_p, tab_s):
        tab = _fill_lead(tab_p, tab_s[:N_META * HEADS], n_seq=B)
        return tab.reshape(1, B, N_META + S, HEADS, HEAD_DIM)

    def sample_kv(tab_s):
        return tab_s.reshape(1 + BS, C, HEADS, HEAD_DIM)[1:, :LS][None]

    y_prompt = y_p.reshape(B, S, D)
    y_sample = y_s.reshape(BS, C, D)[:, :LS]
    fk_p = prompt_kv(k32_p, k32_s)
    fv_p = prompt_kv(v32_p, v32_s)
    lf_p = _assemble_logf(lfc_s, lfc_p, n_seq=B)[None]
    fk_s = sample_kv(k32_s)
    fv_s = sample_kv(v32_s)
    lf_s = lfc_s[C:, SM_F:SM_F + HEADS].reshape(BS, C, HEADS)[:, :LS][None]
    return (y_prompt, y_sample, fk_p, fv_p, lf_p, st_p[None], cv_p[None],
            fk_s, fv_s, lf_s, st_s[1:][None], cv_s[1:][None])
```

```python
import functools

import jax
import jax.numpy as jnp
from jax import lax
from jax.experimental import pallas as pl
from jax.experimental.pallas import tpu as pltpu

F32 = jnp.float32
BF16 = jnp.bfloat16
EPS = 1e-6
NEG = -1e30
LOG2E = 1.4426950408889634

N_META = 16
HEADS = 8
HEAD_DIM = 128
WIDTH = HEADS * HEAD_DIM
CONV_WIDTH = 4
N_GROUPS = 4
EXPERTS_PER_GROUP = 8
N_EXPERTS = N_GROUPS * EXPERTS_PER_GROUP
GDN_CHUNK = 128
LANES = 128
SM_F, SM_A, SM_B = 0, 8, 16
PB_FQ, PB_FK, PB_FV, PB_GZ, PB_FOG = 3, 4, 5, 6, 7
RT_E0 = N_GROUPS
PAIRS_PER_GROUP = EXPERTS_PER_GROUP * (EXPERTS_PER_GROUP - 1) // 2
N_CLASSES = N_GROUPS * PAIRS_PER_GROUP
RT_CLS, RT_RANK, RT_WA, RT_WB = 0, 1, 2, 3
MOE_TS = 256

VMEM_LIMIT = 56 * 1024 * 1024


def _cparams(sem):
    return pltpu.CompilerParams(dimension_semantics=sem, vmem_limit_bytes=VMEM_LIMIT)


def _pick(n, prefs):
    for p in prefs:
        if n % p == 0:
            return p
    raise ValueError(f"no tile in {prefs} divides {n}")


def _dot(a, b):
    return jnp.dot(a, b, preferred_element_type=F32)


def _dot_nt(a, b):
    return lax.dot_general(a, b, (((1,), (1,)), ((), ())), preferred_element_type=F32)


def _dot_tn(a, b):
    return lax.dot_general(a, b, (((0,), (0,)), ((), ())), preferred_element_type=F32)


def _split3(x):
    x1 = x.astype(BF16)
    r1 = x - x1.astype(F32)
    x2 = r1.astype(BF16)
    x3 = (r1 - x2.astype(F32)).astype(BF16)
    return x1, x2, x3


def _mask_dot(mask_bf16, x):
    x1, x2, x3 = _split3(x)
    return _dot(mask_bf16, x1) + _dot(mask_bf16, x2) + _dot(mask_bf16, x3)


def _dot_mask(x, mask_bf16):
    x1, x2, x3 = _split3(x)
    return _dot(x1, mask_bf16) + _dot(x2, mask_bf16) + _dot(x3, mask_bf16)


def _softplus(x):
    return jnp.maximum(x, 0.0) + jnp.log1p(jnp.exp(-jnp.abs(x)))


def _sigmoid(x):
    return 1.0 / (1.0 + jnp.exp(-x))


def _silu(x):
    return x * _sigmoid(x)


def _iota2(shape, dim):
    return lax.broadcasted_iota(jnp.int32, shape, dim)


def _inproj_kernel(x_ref, nw_ref, wbig_ref, wsm_ref, wsmt_ref, p_ref, psm_ref, psmt_ref, xn_scr):
    @pl.when(pl.program_id(1) == 0)
    def _():
        x = x_ref[...]
        xn = x * lax.rsqrt(jnp.mean(x * x, axis=-1, keepdims=True) + EPS) * nw_ref[...]
        xnb = xn.astype(BF16)
        xn_scr[...] = xnb
        psm_ref[...] = _dot(xnb, wsm_ref[...])
        psmt_ref[...] = _dot_nt(wsmt_ref[...], xnb)

    p_ref[...] = _dot(xn_scr[...], wbig_ref[...]).astype(BF16)


def _inproj(x, norm_w, wbig, wsm, wsmt):
    n, d = x.shape
    tm = _pick(n, (1024, 1152, 512, 384, 256, 128))
    tn = 2048
    nproj = wbig.shape[1]
    return pl.pallas_call(
        _inproj_kernel,
        grid=(n // tm, nproj // tn),
        in_specs=[
            pl.BlockSpec((tm, d), lambda i, j: (i, 0)),
            pl.BlockSpec((1, d), lambda i, j: (0, 0)),
            pl.BlockSpec((d, tn), lambda i, j: (0, j)),
            pl.BlockSpec((d, LANES), lambda i, j: (0, 0)),
            pl.BlockSpec((32, d), lambda i, j: (0, 0)),
        ],
        out_specs=[
            pl.BlockSpec((tm, tn), lambda i, j: (i, j)),
            pl.BlockSpec((tm, LANES), lambda i, j: (i, 0)),
            pl.BlockSpec((32, tm), lambda i, j: (0, i)),
        ],
        out_shape=[
            jax.ShapeDtypeStruct((n, nproj), BF16),
            jax.ShapeDtypeStruct((n, LANES), F32),
            jax.ShapeDtypeStruct((32, n), F32),
        ],
        scratch_shapes=[pltpu.VMEM((tm, d), BF16)],
        compiler_params=_cparams(("arbitrary", "arbitrary")),
        name="inproj",
    )(x, norm_w, wbig, wsm, wsmt)


def _gdn_kernel(qkv_ref, z_ref, sm_ref, smt_ref, s0_ref, cb_ref, convw_ref, alog_c_ref, dtb_c_ref,
                alog_r_ref, dtb_r_ref, normw_ref, o_ref, snew_ref, cnew_ref, s_scr, xp_scr, conv_scr, *,
                l_valid):
    C = GDN_CHUNK
    n_sub = qkv_ref.shape[0] // C

    @pl.when(pl.program_id(1) == 0)
    def _():
        s_scr[...] = s0_ref[0]
        xp_scr[5:8, :] = cb_ref[0]

    row = _iota2((C, C), 0)
    col = _iota2((C, C), 1)
    incl = row >= col
    strict = row > col
    tril_b = jnp.where(incl, 1.0, 0.0).astype(BF16)
    triu_b = jnp.where(row <= col, 1.0, 0.0).astype(BF16)
    shift_b = jnp.concatenate(
        [jnp.where(col == row - i, 1.0, 0.0).astype(BF16) for i in (1, 2, 3)], axis=0)
    inv_levels = []
    sh = 1
    while (1 << sh) < C:
        inv_levels.append((((row >> sh) & 1) == 1) & ((col >> (sh + 1)) == (row >> (sh + 1)))
                          & (((col >> sh) & 1) == 0))
        sh += 1
    m1 = ((row & 1) == 1) & (col == row - 1)
    eye = jnp.where(row == col, 1.0, 0.0)
    w = convw_ref[...]
    normw = normw_ref[...]

    act, gl_col, gl_row, beta_col = [], [], [], []
    for sub in range(n_sub):
        rows = slice(sub * C, (sub + 1) * C)
        xb = qkv_ref[rows, :]
        x = xb.astype(F32)
        xp_scr[8:16, :] = x[0:8]
        shifted = _dot(shift_b, xb)
        conv_scr[rows, :] = (w[3:4] * x + w[2:3] * shifted[0:C] + w[1:2] * shifted[C:2 * C]
                             + w[0:1] * shifted[2 * C:3 * C])
        conv_scr[sub * C:sub * C + 8, :] = (w[0:1] * xp_scr[5:13, :] + w[1:2] * xp_scr[6:14, :]
                                            + w[2:3] * xp_scr[7:15, :] + w[3:4] * x[0:8])
        act.append(_silu(conv_scr[rows, :]))
        tail = x[l_valid - 3:l_valid, :]
        xp_scr[5:8, :] = tail
        cnew_ref[0] = tail

        sm = sm_ref[rows, :]
        g_col = -jnp.exp(alog_c_ref[...]) * _softplus(sm + dtb_c_ref[...])
        g_row = -jnp.exp(alog_r_ref[...]) * _softplus(smt_ref[:, rows] + dtb_r_ref[...])
        b_col = _sigmoid(sm)
        if l_valid < C:
            g_col = jnp.where(_iota2((C, LANES), 0) < l_valid, g_col, 0.0)
            b_col = jnp.where(_iota2((C, LANES), 0) < l_valid, b_col, 0.0)
            g_row = jnp.where(_iota2((HEADS, C), 1) < l_valid, g_row, 0.0)
        beta_col.append(b_col)
        gl_col.append(_mask_dot(tril_b, g_col))
        gl_row.append(_dot_mask(g_row, triu_b))

    hs = [slice(h * HEAD_DIM, (h + 1) * HEAD_DIM) for h in range(HEADS)]
    I = [(sub, h) for sub in range(n_sub) for h in range(HEADS)]
    gc = {(c, h): gl_col[c][:, SM_A + h:SM_A + h + 1] for c, h in I}
    beta = {(c, h): beta_col[c][:, SM_B + h:SM_B + h + 1] for c, h in I}
    qn, kn, kb, knb = {}, {}, {}, {}
    for c, h in I:
        q = act[c][:, hs[h]]
        k = act[c][:, WIDTH + h * HEAD_DIM:WIDTH + (h + 1) * HEAD_DIM]
        qn[c, h] = q * lax.rsqrt(jnp.sum(q * q, axis=-1, keepdims=True) + EPS) * (HEAD_DIM ** -0.5)
        kn[c, h] = k * lax.rsqrt(jnp.sum(k * k, axis=-1, keepdims=True) + EPS)
        kb[c, h] = kn[c, h] * beta[c, h]
        knb[c, h] = kn[c, h].astype(BF16)
    decay = {i: jnp.exp(jnp.where(incl, gc[i] - gl_row[i[0]][i[1]:i[1] + 1, :], NEG)) for i in I}
    a_mat = {i: jnp.where(strict, _dot_nt(kb[i].astype(BF16), knb[i]) * decay[i], 0.0) for i in I}
    qk = {i: (_dot_nt(qn[i].astype(BF16), knb[i]) * decay[i]).astype(BF16) for i in I}

    t = {i: eye - jnp.where(m1, a_mat[i], 0.0) for i in I}
    for m in inv_levels:
        tb = {i: t[i].astype(BF16) for i in I}
        y = {i: _dot(tb[i], jnp.where(m, a_mat[i], 0.0).astype(BF16)).astype(BF16) for i in I}
        t = {i: t[i] - _dot(y[i], tb[i]) for i in I}
    tb = {i: t[i].astype(BF16) for i in I}
    eg = {i: jnp.exp(gc[i]) for i in I}
    g_last = {i: gc[i][C - 1:C, :] for i in I}

    for c in range(n_sub):
        H = [(c, h) for h in range(HEADS)]
        s = {i: s_scr[i[1]] for i in H}
        sb = {i: s[i].astype(BF16) for i in H}
        r = {i: (act[c][:, 2 * WIDTH + i[1] * HEAD_DIM:2 * WIDTH + (i[1] + 1) * HEAD_DIM] * beta[i]
                 - _dot((kb[i] * eg[i]).astype(BF16), sb[i])).astype(BF16) for i in H}
        ub = {i: _dot(tb[i], r[i]).astype(BF16) for i in H}
        o = {i: _dot((qn[i] * eg[i]).astype(BF16), sb[i]) + _dot(qk[i], ub[i]) for i in H}
        for i in H:
            k_dec = (kn[i] * jnp.exp(g_last[i] - gc[i])).astype(BF16)
            s_scr[i[1]] = s[i] * jnp.exp(g_last[i]) + _dot_tn(k_dec, ub[i])
        for i in H:
            on = o[i] * lax.rsqrt(jnp.mean(o[i] * o[i], axis=-1, keepdims=True) + EPS) * normw
            z = z_ref[c * C:(c + 1) * C, hs[i[1]]].astype(F32)
            o_ref[c * C:(c + 1) * C, hs[i[1]]] = (on * _silu(z)).astype(BF16)

    snew_ref[0] = s_scr[...]


def _gdn(p, psm, psmt, s0, cb, conv_w, alog, dtb, norm_w, *, n_seq, l_valid, bcast_state):
    n = p.shape[0]
    C = GDN_CHUNK
    n_sub = _pick(n // (n_seq * C), (2, 1))
    R = n_sub * C
    nc = n // (n_seq * R)
    alog_c = jnp.zeros((1, LANES), F32).at[0, SM_A:SM_A + HEADS].set(alog)
    dtb_c = jnp.zeros((1, LANES), F32).at[0, SM_A:SM_A + HEADS].set(dtb)
    st_idx = (lambda b, c: (0, 0, 0, 0)) if bcast_state else (lambda b, c: (b, 0, 0, 0))
    cb_idx = (lambda b, c: (0, 0, 0)) if bcast_state else (lambda b, c: (b, 0, 0))
    return pl.pallas_call(
        functools.partial(_gdn_kernel, l_valid=l_valid),
        grid=(n_seq, nc),
        in_specs=[
            pl.BlockSpec((R, 3 * WIDTH), lambda b, c: (b * nc + c, 0)),
            pl.BlockSpec((R, WIDTH), lambda b, c: (b * nc + c, PB_GZ)),
            pl.BlockSpec((R, LANES), lambda b, c: (b * nc + c, 0)),
            pl.BlockSpec((HEADS, R), lambda b, c: (SM_A // HEADS, b * nc + c)),
            pl.BlockSpec((1, HEADS, HEAD_DIM, HEAD_DIM), st_idx),
            pl.BlockSpec((1, CONV_WIDTH - 1, 3 * WIDTH), cb_idx),
            pl.BlockSpec((CONV_WIDTH, 3 * WIDTH), lambda b, c: (0, 0)),
            pl.BlockSpec((1, LANES), lambda b, c: (0, 0)),
            pl.BlockSpec((1, LANES), lambda b, c: (0, 0)),
            pl.BlockSpec((HEADS, 1), lambda b, c: (0, 0)),
            pl.BlockSpec((HEADS, 1), lambda b, c: (0, 0)),
            pl.BlockSpec((1, HEAD_DIM), lambda b, c: (0, 0)),
        ],
        out_specs=[
            pl.BlockSpec((R, WIDTH), lambda b, c: (b * nc + c, 0)),
            pl.BlockSpec((1, HEADS, HEAD_DIM, HEAD_DIM), lambda b, c: (b, 0, 0, 0)),
            pl.BlockSpec((1, CONV_WIDTH - 1, 3 * WIDTH), lambda b, c: (b, 0, 0)),
        ],
        out_shape=[
            jax.ShapeDtypeStruct((n, WIDTH), BF16),
            jax.ShapeDtypeStruct((n_seq, HEADS, HEAD_DIM, HEAD_DIM), F32),
            jax.ShapeDtypeStruct((n_seq, CONV_WIDTH - 1, 3 * WIDTH), F32),
        ],
        scratch_shapes=[
            pltpu.VMEM((HEADS, HEAD_DIM, HEAD_DIM), F32),
            pltpu.VMEM((16, 3 * WIDTH), F32),
            pltpu.VMEM((R, 3 * WIDTH), F32),
        ],
        compiler_params=_cparams(("arbitrary", "arbitrary")),
        name="gdn",
    )(p, p, psm, psmt, s0, cb, conv_w, alog_c, dtb_c, alog.reshape(HEADS, 1), dtb.reshape(HEADS, 1),
      norm_w.reshape(1, HEAD_DIM))


def _foxprep_kernel(qkv_ref, sm_ref, smt_ref, c0c_ref, c0r_ref, qw_ref, kw_ref, fb_c_ref, fb_r_ref,
                    k32_ref, v32_ref, lfc_ref, lfr_ref, fc_ref, fr_ref, qt_ref, kcat_ref, vt_ref, *rest,
                    row_major):
    if row_major:
        q_ref, kb_ref, cc_scr, cr_scr = rest
    else:
        cc_scr, cr_scr = rest
    tm = qkv_ref.shape[0]

    @pl.when(pl.program_id(1) == 0)
    def _():
        cc_scr[...] = c0c_ref[0]
        cr_scr[...] = c0r_ref[0]

    lf_col = -_softplus(-(sm_ref[...] + fb_c_ref[...]))
    lf_row = -_softplus(-(smt_ref[...] + fb_r_ref[...]))
    lfc_ref[...] = lf_col
    lfr_ref[...] = lf_row
    row = _iota2((tm, tm), 0)
    col = _iota2((tm, tm), 1)
    tril_b = jnp.where(row >= col, 1.0, 0.0).astype(BF16)
    triu_b = jnp.where(row <= col, 1.0, 0.0).astype(BF16)
    f_col = cc_scr[...] + _mask_dot(tril_b, lf_col)
    f_row = cr_scr[...] + _dot_mask(lf_row, triu_b)
    fc_ref[...] = f_col
    fr_ref[...] = f_row
    cc_scr[...] = f_col[tm - 1:tm, :]
    cr_scr[...] = f_row[:, tm - 1:tm]

    qw = qw_ref[...]
    kw = kw_ref[...]
    lane = _iota2((tm, HEAD_DIM), 1)
    sub = _iota2((16, tm), 0)
    zeros_t = jnp.zeros((HEAD_DIM - 16, tm), BF16)
    for h in range(HEADS):
        hs = slice(h * HEAD_DIM, (h + 1) * HEAD_DIM)
        q = qkv_ref[:, hs].astype(F32)
        k = qkv_ref[:, WIDTH + h * HEAD_DIM:WIDTH + (h + 1) * HEAD_DIM].astype(F32)
        v = qkv_ref[:, 2 * WIDTH + h * HEAD_DIM:2 * WIDTH + (h + 1) * HEAD_DIM].astype(F32)
        qn = q * lax.rsqrt(jnp.mean(q * q, axis=-1, keepdims=True) + EPS) * qw
        kn = k * lax.rsqrt(jnp.mean(k * k, axis=-1, keepdims=True) + EPS) * kw
        k32_ref[pl.ds(h, tm, stride=HEADS), :] = kn
        v32_ref[pl.ds(h, tm, stride=HEADS), :] = v
        if row_major:
            q_ref[:, hs] = (qn * (HEAD_DIM ** -0.5)).astype(BF16)
            kb_ref[:, hs] = kn.astype(BF16)
        base = 2 * h * HEAD_DIM
        qt_ref[base:base + HEAD_DIM, :] = (qn * (LOG2E * HEAD_DIM ** -0.5)).T.astype(BF16)
        r1, r2, r3 = _split3(f_row[h:h + 1, :] * LOG2E)
        aug_q = jnp.where(sub == 0, r1.astype(F32), jnp.where(sub == 1, r2.astype(F32), jnp.where(
            sub == 2, r3.astype(F32), jnp.where(sub < 6, 1.0, 0.0))))
        qt_ref[base + HEAD_DIM:base + HEAD_DIM + 16, :] = aug_q.astype(BF16)
        qt_ref[base + HEAD_DIM + 16:base + 2 * HEAD_DIM, :] = zeros_t
        kcat_ref[:, base:base + HEAD_DIM] = kn.astype(BF16)
        c1, c2, c3 = _split3(f_col[:, SM_F + h:SM_F + h + 1] * LOG2E)
        aug_k = jnp.where(lane < 3, 1.0, jnp.where(lane == 3, -c1.astype(F32), jnp.where(
            lane == 4, -c2.astype(F32), jnp.where(lane == 5, -c3.astype(F32), 0.0))))
        kcat_ref[:, base + HEAD_DIM:base + 2 * HEAD_DIM] = aug_k.astype(BF16)
        vt_ref[hs, :] = v.T.astype(BF16)


def _foxprep(p, psm, psmt, c0c, c0r, q_norm_w, k_norm_w, f_bias, *, n_seq, bcast_carry, row_major, lead=0):
    n = p.shape[0]
    rows = n // n_seq
    tm = _pick(rows, (512, 256, 128))
    nt = rows // tm
    fb_c = jnp.zeros((1, LANES), F32).at[0, SM_F:SM_F + HEADS].set(f_bias)
    c_idx = (lambda b, t: (0, 0, 0)) if bcast_carry else (lambda b, t: (b, 0, 0))
    rowblk = lambda b, t: (b * nt + t, 0)
    colblk = lambda b, t: (0, b * nt + t)
    kv_spec = pl.BlockSpec((pl.Element(tm * HEADS), pl.Element(HEAD_DIM)),
                           lambda b, t: ((b * (rows + lead) + lead + t * tm) * HEADS, 0))
    kv_shape = jax.ShapeDtypeStruct((n_seq * (rows + lead) * HEADS, HEAD_DIM), F32)
    out_specs = [
        kv_spec,
        kv_spec,
        pl.BlockSpec((tm, LANES), rowblk),
        pl.BlockSpec((HEADS, tm), colblk),
        pl.BlockSpec((tm, LANES), rowblk),
        pl.BlockSpec((HEADS, tm), colblk),
        pl.BlockSpec((2 * WIDTH, tm), colblk),
        pl.BlockSpec((tm, 2 * WIDTH), rowblk),
        pl.BlockSpec((WIDTH, tm), colblk),
    ]
    out_shape = [
        kv_shape,
        kv_shape,
        jax.ShapeDtypeStruct((n, LANES), F32),
        jax.ShapeDtypeStruct((HEADS, n), F32),
        jax.ShapeDtypeStruct((n, LANES), F32),
        jax.ShapeDtypeStruct((HEADS, n), F32),
        jax.ShapeDtypeStruct((2 * WIDTH, n), BF16),
        jax.ShapeDtypeStruct((n, 2 * WIDTH), BF16),
        jax.ShapeDtypeStruct((WIDTH, n), BF16),
    ]
    if row_major:
        out_specs += [pl.BlockSpec((tm, WIDTH), rowblk), pl.BlockSpec((tm, WIDTH), rowblk)]
        out_shape += [jax.ShapeDtypeStruct((n, WIDTH), BF16),
                      jax.ShapeDtypeStruct((n, WIDTH), BF16)]
    return pl.pallas_call(
        functools.partial(_foxprep_kernel, row_major=row_major),
        grid=(n_seq, nt),
        in_specs=[
            pl.BlockSpec((tm, 3 * WIDTH), lambda b, t: (b * nt + t, 1)),
            pl.BlockSpec((tm, LANES), rowblk),
            pl.BlockSpec((HEADS, tm), lambda b, t: (SM_F // HEADS, b * nt + t)),
            pl.BlockSpec((1, 1, LANES), c_idx),
            pl.BlockSpec((1, HEADS, 1), c_idx),
            pl.BlockSpec((1, HEAD_DIM), lambda b, t: (0, 0)),
            pl.BlockSpec((1, HEAD_DIM), lambda b, t: (0, 0)),
            pl.BlockSpec((1, LANES), lambda b, t: (0, 0)),
            pl.BlockSpec((HEADS, 1), lambda b, t: (0, 0)),
        ],
        out_specs=out_specs,
        out_shape=out_shape,
        scratch_shapes=[pltpu.VMEM((1, LANES), F32), pltpu.VMEM((HEADS, 1), F32)],
        compiler_params=_cparams(("arbitrary", "arbitrary")),
        name="foxprep",
    )(p, psm, psmt, c0c, c0r, q_norm_w.reshape(1, HEAD_DIM), k_norm_w.reshape(1, HEAD_DIM), fb_c,
      f_bias.reshape(HEADS, 1))


def _fill_lead_kernel(src_ref, big_ref, o_ref):
    o_ref[...] = src_ref[...]


def _fill_lead(big, src, *, n_seq):
    r = src.shape[0]
    seq_rows = big.shape[0] // n_seq
    assert seq_rows % r == 0
    return pl.pallas_call(
        _fill_lead_kernel,
        grid=(n_seq,),
        in_specs=[pl.BlockSpec((r, HEAD_DIM), lambda b: (0, 0)), pl.BlockSpec(memory_space=pl.ANY)],
        out_specs=pl.BlockSpec((r, HEAD_DIM), lambda b: (b * (seq_rows // r), 0)),
        out_shape=jax.ShapeDtypeStruct(big.shape, big.dtype),
        input_output_aliases={1: 0},
        compiler_params=_cparams(("arbitrary",)),
        name="fill_lead",
    )(src, big)


def _logf_kernel(meta_ref, lf_ref, o_ref):
    o_ref[0, 0:N_META, :] = meta_ref[0:N_META, SM_F:SM_F + HEADS]
    o_ref[0, N_META:, :] = lf_ref[:, SM_F:SM_F + HEADS]


def _assemble_logf(lf_small, lf_prompt, *, n_seq):
    s = lf_prompt.shape[0] // n_seq
    return pl.pallas_call(
        _logf_kernel,
        grid=(n_seq,),
        in_specs=[pl.BlockSpec((GDN_CHUNK, LANES), lambda b: (0, 0)), pl.BlockSpec((s, LANES), lambda b: (b, 0))],
        out_specs=pl.BlockSpec((1, N_META + s, HEADS), lambda b: (b, 0, 0)),
        out_shape=jax.ShapeDtypeStruct((n_seq, N_META + s, HEADS), F32),
        compiler_params=_cparams(("arbitrary",)),
        name="assemble_logf",
    )(lf_small, lf_prompt)


FOX_UNIT = 256

def _fox_kernel(qi_ref, ki_ref, qt_ref, kcat_ref, vt_ref, og_ref, km_ref, vtm_ref, o_ref, m_scr, l_scr, acc_scr):
    pair = pl.program_id(1)
    qi = qi_ref[pair]
    ki = ki_ref[pair]
    tq = qt_ref.shape[1]
    tk = kcat_ref.shape[0]
    U = FOX_UNIT
    H = range(HEADS)

    def unit(qh, k_rows, vt_cols, mask):
        qs = slice(qh * U, (qh + 1) * U)
        s = [_dot(k_rows(h), qt_ref[2 * h * HEAD_DIM:2 * (h + 1) * HEAD_DIM, qs]) for h in H]
        if mask is not None:
            s = [jnp.where(mask, s[h], NEG) for h in H]
        m_old = [m_scr[h:h + 1, qs] for h in H]
        m_new = [jnp.maximum(m_old[h], jnp.max(s[h], axis=0, keepdims=True)) for h in H]
        alpha = [jnp.exp2(m_old[h] - m_new[h]) for h in H]
        p = [jnp.exp2(s[h] - m_new[h]) for h in H]
        for h in H:
            l_scr[h:h + 1, qs] = alpha[h] * l_scr[h:h + 1, qs] + jnp.sum(p[h], axis=0, keepdims=True)
            m_scr[h:h + 1, qs] = m_new[h]
        for h in H:
            acc_scr[h, :, qs] = alpha[h] * acc_scr[h, :, qs] + _dot(vt_cols(h), p[h].astype(BF16))

    def k_unit(ku):
        return lambda h: kcat_ref[ku * U:(ku + 1) * U, 2 * h * HEAD_DIM:2 * (h + 1) * HEAD_DIM]

    def vt_unit(ku):
        return lambda h: vt_ref[h * HEAD_DIM:(h + 1) * HEAD_DIM, ku * U:(ku + 1) * U]

    @pl.when(ki == 0)
    def _():
        m_scr[...] = jnp.full(m_scr.shape, NEG, F32)
        l_scr[...] = jnp.zeros(l_scr.shape, F32)
        acc_scr[...] = jnp.zeros(acc_scr.shape, F32)
        tmeta = km_ref.shape[0]
        mask = _iota2((tmeta, U), 0) < N_META
        for qh in range(tq // U):
            unit(qh, lambda h: km_ref[:, 2 * h * HEAD_DIM:2 * (h + 1) * HEAD_DIM],
                 lambda h: vtm_ref[h * HEAD_DIM:(h + 1) * HEAD_DIM, :], mask)

    @pl.when(ki < qi)
    def _():
        for qh in range(tq // U):
            for ku in range(tk // U):
                unit(qh, k_unit(ku), vt_unit(ku), None)

    @pl.when(ki == qi)
    def _():
        diag = _iota2((U, U), 0) <= _iota2((U, U), 1)
        for qh in range(tq // U):
            for ku in range(qh + 1):
                unit(qh, k_unit(ku), vt_unit(ku), diag if ku == qh else None)
        for h in H:
            hs = slice(h * HEAD_DIM, (h + 1) * HEAD_DIM)
            gate = _sigmoid(og_ref[:, hs].astype(F32))
            o_ref[:, hs] = ((acc_scr[h] / l_scr[h:h + 1, :]).T * gate).astype(BF16)


def _fox_prompt(qt, kcat, vt, p, kcat_small, vt_small, *, n_seq):
    n = p.shape[0]
    rows = n // n_seq
    tq = _pick(rows, (512, 256))
    nq = rows // tq
    pairs = [(i, j) for i in range(nq) for j in range(i + 1)]
    qi = jnp.asarray([a for a, _ in pairs], jnp.int32)
    ki = jnp.asarray([b for _, b in pairs], jnp.int32)
    grid_spec = pltpu.PrefetchScalarGridSpec(
        num_scalar_prefetch=2,
        grid=(n_seq, len(pairs)),
        in_specs=[
            pl.BlockSpec((2 * WIDTH, tq), lambda b, t, qi_r, ki_r: (0, b * nq + qi_r[t])),
            pl.BlockSpec((tq, 2 * WIDTH), lambda b, t, qi_r, ki_r: (b * nq + ki_r[t], 0)),
            pl.BlockSpec((WIDTH, tq), lambda b, t, qi_r, ki_r: (0, b * nq + ki_r[t])),
            pl.BlockSpec((tq, WIDTH), lambda b, t, qi_r, ki_r: (b * nq + qi_r[t], PB_FOG)),
            pl.BlockSpec((GDN_CHUNK, 2 * WIDTH), lambda b, t, qi_r, ki_r: (0, 0)),
            pl.BlockSpec((WIDTH, GDN_CHUNK), lambda b, t, qi_r, ki_r: (0, 0)),
        ],
        out_specs=pl.BlockSpec((tq, WIDTH), lambda b, t, qi_r, ki_r: (b * nq + qi_r[t], 0)),
        scratch_shapes=[
            pltpu.VMEM((HEADS, tq), F32),
            pltpu.VMEM((HEADS, tq), F32),
            pltpu.VMEM((HEADS, HEAD_DIM, tq), F32),
        ],
    )
    return pl.pallas_call(
        _fox_kernel,
        grid_spec=grid_spec,
        out_shape=jax.ShapeDtypeStruct((n, WIDTH), BF16),
        compiler_params=_cparams(("arbitrary", "arbitrary")),
        name="fox_prompt",
    )(qi, ki, qt, kcat, vt, p, kcat_small, vt_small)


def _fox_sample_kernel(q_ref, kn_ref, vn_ref, og_ref, lfc_ref, lfr_ref, ck_ref, cv_ref, clc_ref, clr_ref,
                       o_ref, *, l_valid):
    C = q_ref.shape[0]
    P = ck_ref.shape[1] // HEADS
    rowp = _iota2((P, P), 0)
    colp = _iota2((P, P), 1)
    triu_p = jnp.where(rowp <= colp, 1.0, 0.0).astype(BF16)
    row = _iota2((C, C), 0)
    col = _iota2((C, C), 1)
    tril_b = jnp.where(row >= col, 1.0, 0.0).astype(BF16)
    triu_b = jnp.where(row <= col, 1.0, 0.0).astype(BF16)

    f_cache = _dot_mask(clr_ref[0], triu_p)
    carry_r = f_cache[:, P - 1:P]
    carry_c = jnp.sum(clc_ref[0], axis=0, keepdims=True)
    f_new_r = carry_r + _dot_mask(lfr_ref[...], triu_b)
    f_new_c = carry_c + _mask_dot(tril_b, lfc_ref[...])[:, SM_F:SM_F + HEADS]
    mask_new = (col <= row) & (col < l_valid)

    for h in range(HEADS):
        hs = slice(h * HEAD_DIM, (h + 1) * HEAD_DIM)
        q = q_ref[:, hs]
        fq = f_new_c[:, h:h + 1]
        ck = ck_ref[0, pl.ds(h, P, stride=HEADS), :].astype(BF16)
        cv = cv_ref[0, pl.ds(h, P, stride=HEADS), :].astype(BF16)
        s_c = _dot_nt(q, ck) + (fq - f_cache[h:h + 1, :])
        s_n = jnp.where(mask_new, _dot_nt(q, kn_ref[:, hs]) + (fq - f_new_r[h:h + 1, :]), NEG)
        m = jnp.maximum(jnp.max(s_c, axis=-1, keepdims=True), jnp.max(s_n, axis=-1, keepdims=True))
        p_c = jnp.exp(s_c - m)
        p_n = jnp.exp(s_n - m)
        l = jnp.sum(p_c, axis=-1, keepdims=True) + jnp.sum(p_n, axis=-1, keepdims=True)
        o = _dot(p_c.astype(BF16), cv) + _dot(p_n.astype(BF16), vn_ref[:, hs])
        gate = _sigmoid(og_ref[:, hs].astype(F32))
        o_ref[:, hs] = (o / l * gate).astype(BF16)


def _fox_sample(q, kb, p, lf_col, lf_row, cache_k, cache_v, cache_lf, *, l_valid):
    bs, past = cache_k.shape[0], cache_k.shape[1]
    C = GDN_CHUNK
    n = q.shape[0]
    blk = lambda b: (b + 1, 0)
    return pl.pallas_call(
        functools.partial(_fox_sample_kernel, l_valid=l_valid),
        grid=(bs,),
        in_specs=[
            pl.BlockSpec((C, WIDTH), blk),
            pl.BlockSpec((C, WIDTH), blk),
            pl.BlockSpec((C, WIDTH), lambda b: (b + 1, PB_FV)),
            pl.BlockSpec((C, WIDTH), lambda b: (b + 1, PB_FOG)),
            pl.BlockSpec((C, LANES), blk),
            pl.BlockSpec((HEADS, C), lambda b: (0, b + 1)),
            pl.BlockSpec((1, past * HEADS, HEAD_DIM), lambda b: (b, 0, 0)),
            pl.BlockSpec((1, past * HEADS, HEAD_DIM), lambda b: (b, 0, 0)),
            pl.BlockSpec((1, past, HEADS), lambda b: (b, 0, 0)),
            pl.BlockSpec((1, HEADS, past), lambda b: (b, 0, 0)),
        ],
        out_specs=pl.BlockSpec((C, WIDTH), lambda b: (b, 0)),
        out_shape=jax.ShapeDtypeStruct((bs * C, WIDTH), BF16),
        compiler_params=_cparams(("arbitrary",)),
        name="fox_sample",
    )(q, kb, p, p, lf_col, lf_row, cache_k.reshape(bs, past * HEADS, HEAD_DIM),
      cache_v.reshape(bs, past * HEADS, HEAD_DIM), cache_lf, jnp.swapaxes(cache_lf, 1, 2))


def _outproj_kernel(og_ref, of_ref, x_ref, wo_ref, nw_ref, wr_ref, br_ref, cnt0_ref, xr_ref, rt_ref, cnt_ref,
                    cnt_scr):
    d = x_ref.shape[1]

    @pl.when(pl.program_id(0) == 0)
    def _():
        cnt_scr[...] = cnt0_ref[...]

    half = og_ref.shape[1]
    h = _dot(og_ref[...], wo_ref[0:half, :]) + _dot(of_ref[...], wo_ref[half:2 * half, :])
    x1 = x_ref[...] + h
    xr_ref[:, 0:d] = x1
    xn = x1 * lax.rsqrt(jnp.mean(x1 * x1, axis=-1, keepdims=True) + EPS) * nw_ref[...]

    x_hi = xn.astype(BF16)
    x_lo = (xn - x_hi.astype(F32)).astype(BF16)
    hi = _dot(x_hi, wr_ref[...])
    logits = (hi[:, 0:LANES] + hi[:, LANES:2 * LANES] + _dot(x_lo, wr_ref[:, 0:LANES])) + br_ref[...]

    tm = logits.shape[0]
    lane = _iota2((tm, LANES), 1).astype(F32)
    big = float(LANES)
    gl = jnp.where(lane < N_GROUPS, logits, NEG)
    gmax = jnp.max(gl, axis=-1, keepdims=True)
    gidx = jnp.min(jnp.where(gl == gmax, lane, big), axis=-1, keepdims=True)
    p_top = 1.0 / jnp.sum(jnp.exp(gl - gmax), axis=-1, keepdims=True)
    e = lane - RT_E0
    sel = (e >= 0) & (e < N_EXPERTS) & (jnp.floor(e * (1.0 / EXPERTS_PER_GROUP)) == gidx)
    el = jnp.where(sel, logits, NEG)
    v1 = jnp.max(el, axis=-1, keepdims=True)
    i1 = jnp.min(jnp.where(el == v1, lane, big), axis=-1, keepdims=True)
    el2 = jnp.where(lane == i1, NEG, el)
    v2 = jnp.max(el2, axis=-1, keepdims=True)
    i2 = jnp.min(jnp.where(el2 == v2, lane, big), axis=-1, keepdims=True)
    e2 = jnp.exp(v2 - v1)
    w1 = p_top / (1.0 + e2)
    w2 = p_top * e2 / (1.0 + e2)

    ex1 = i1 - RT_E0
    ex2 = i2 - RT_E0
    first = ex1 < ex2
    ea = jnp.where(first, ex1, ex2)
    eb = jnp.where(first, ex2, ex1)
    wa = jnp.where(first, w1, w2)
    wb = jnp.where(first, w2, w1)
    la = ea - gidx * EXPERTS_PER_GROUP
    lb = eb - gidx * EXPERTS_PER_GROUP
    cls = gidx * PAIRS_PER_GROUP + la * (2 * EXPERTS_PER_GROUP - 1 - la) * 0.5 + (lb - la - 1.0)

    onehot = lane == cls
    oh = jnp.where(onehot, 1.0, 0.0)
    strict_b = jnp.where(_iota2((tm, tm), 0) > _iota2((tm, tm), 1), 1.0, 0.0).astype(BF16)
    before = cnt_scr[...] + _dot(strict_b, oh.astype(BF16))
    rank = jnp.sum(jnp.where(onehot, before, 0.0), axis=-1, keepdims=True)
    cnt_scr[...] += jnp.sum(oh, axis=0, keepdims=True)
    cnt_ref[...] = cnt_scr[...]
    route = jnp.where(lane == RT_CLS, cls, jnp.where(lane == RT_RANK, rank, jnp.where(
        lane == RT_WA, wa, jnp.where(lane == RT_WB, wb, 0.0))))
    rt_ref[...] = route
    xr_ref[:, d:d + LANES] = route


def _outproj(og, of, x, wo, norm_w, wr, br, cnt0):
    n, d = x.shape
    tm = _pick(n, (512, 384, 256, 128))
    rowblk = lambda i: (i, 0)
    return pl.pallas_call(
        _outproj_kernel,
        grid=(n // tm,),
        in_specs=[
            pl.BlockSpec((tm, WIDTH), rowblk),
            pl.BlockSpec((tm, WIDTH), rowblk),
            pl.BlockSpec((tm, d), rowblk),
            pl.BlockSpec((2 * WIDTH, d), lambda i: (0, 0)),
            pl.BlockSpec((1, d), lambda i: (0, 0)),
            pl.BlockSpec((d, 2 * LANES), lambda i: (0, 0)),
            pl.BlockSpec((1, LANES), lambda i: (0, 0)),
            pl.BlockSpec((1, LANES), lambda i: (0, 0)),
        ],
        out_specs=[
            pl.BlockSpec((tm, d + LANES), rowblk),
            pl.BlockSpec((tm, LANES), rowblk),
            pl.BlockSpec((1, LANES), lambda i: (0, 0)),
        ],
        out_shape=[
            jax.ShapeDtypeStruct((n, d + LANES), F32),
            jax.ShapeDtypeStruct((n, LANES), F32),
            jax.ShapeDtypeStruct((1, LANES), F32),
        ],
        scratch_shapes=[pltpu.VMEM((1, LANES), F32)],
        compiler_params=_cparams(("arbitrary",)),
        name="outproj_router",
    )(og, of, x, wo, norm_w, wr, br, cnt0)


def _pos_rows(tm):
    return -(-(-(-tm // LANES)) // 8) * 8


def _tile_positions(pos, tm):
    nt = pos.shape[0] // tm
    rows = _pos_rows(tm)
    p = jnp.pad(pos.reshape(nt, tm), ((0, 0), (0, rows * LANES - tm)))
    return p.reshape(nt * rows, LANES)


def _row_copy_loops(tm, pos_smem, make_copy):
    def start_row(r, carry):
        base = pl.multiple_of(r * LANES, LANES)
        for c in range(LANES):
            make_copy(base + c, pos_smem[r, c]).start()
        return carry

    def wait(t, carry):
        make_copy(0, 0).wait()
        return carry

    lax.fori_loop(0, tm // LANES, start_row, 0)
    lax.fori_loop(0, tm, wait, 0, unroll=8)


def _dispatch_kernel(ends_ref, xr_ref, pos_hbm, *rest, zero_fill):
    if zero_fill:
        xs_hbm, pos_smem, zbuf, sem_idx, sem_fill, sem_rows = rest
    else:
        _, xs_hbm, pos_smem, sem_idx, sem_rows = rest
    i = pl.program_id(0)
    tm = xr_ref.shape[0]
    rows = pos_smem.shape[0]
    idx_cp = pltpu.make_async_copy(pos_hbm.at[pl.ds(pl.multiple_of(i * rows, 8), rows)], pos_smem, sem_idx)
    idx_cp.start()

    if zero_fill:
        @pl.when(i == 0)
        def _():
            zbuf[...] = jnp.zeros(zbuf.shape, F32)

            def fill(start):
                def body(c, carry):
                    lo = jnp.where(c == 0, 0, ends_ref[jnp.maximum(c - 1, 0)])
                    hi = ends_ref[c]

                    @pl.when(hi > lo)
                    def _():
                        cp = pltpu.make_async_copy(
                            zbuf, xs_hbm.at[pl.ds(pl.multiple_of(hi - MOE_TS, MOE_TS), MOE_TS)], sem_fill)
                        if start:
                            cp.start()
                        else:
                            cp.wait()
                    return carry
                lax.fori_loop(0, N_CLASSES, body, 0)

            fill(True)
            fill(False)

            last = ends_ref[N_CLASSES - 1]

            @pl.when((last // MOE_TS) % 2 == 1)
            def _():
                cp = pltpu.make_async_copy(zbuf, xs_hbm.at[pl.ds(pl.multiple_of(last, MOE_TS), MOE_TS)], sem_fill)
                cp.start()
                cp.wait()

    idx_cp.wait()
    _row_copy_loops(tm, pos_smem, lambda t, p: pltpu.make_async_copy(
        xr_ref.at[pl.ds(t, 1)], xs_hbm.at[pl.ds(p, 1)], sem_rows))


def _dispatch(ends, xr, pos, xs, *, n_sorted):
    n, dw = xr.shape
    tm = _pick(n, (1024, 1152, 512, 384, 256, 128))
    rows = _pos_rows(tm)
    zero_fill = xs is None
    any_spec = pl.BlockSpec(memory_space=pl.ANY)
    in_specs = [pl.BlockSpec((tm, dw), lambda i, ends_r: (i, 0)), any_spec]
    args = [xr, _tile_positions(pos, tm)]
    scratch = [pltpu.SMEM((rows, LANES), jnp.int32)]
    if zero_fill:
        scratch += [pltpu.VMEM((MOE_TS, dw), F32), pltpu.SemaphoreType.DMA, pltpu.SemaphoreType.DMA,
                    pltpu.SemaphoreType.DMA]
        aliases = {}
    else:
        in_specs.append(any_spec)
        args.append(xs)
        scratch += [pltpu.SemaphoreType.DMA, pltpu.SemaphoreType.DMA]
        aliases = {3: 0}
    return pl.pallas_call(
        functools.partial(_dispatch_kernel, zero_fill=zero_fill),
        grid_spec=pltpu.PrefetchScalarGridSpec(
            num_scalar_prefetch=1, grid=(n // tm,), in_specs=in_specs, out_specs=any_spec,
            scratch_shapes=scratch),
        out_shape=jax.ShapeDtypeStruct((n_sorted, dw), F32),
        input_output_aliases=aliases,
        compiler_params=pltpu.CompilerParams(dimension_semantics=("arbitrary",), vmem_limit_bytes=VMEM_LIMIT,
                                             has_side_effects=True),
        name="moe_dispatch",
    )(ends, *args)


def _ffn_kernel(ta_ref, tb_ref, nv_ref, xs_ref, *refs):
    w_refs, (nffn_ref, nfin_ref, ys_ref) = refs[:12], refs[12:]

    @pl.when(pl.program_id(0) < nv_ref[0])
    def _():
        d = ys_ref.shape[1]
        halves = range(2)
        rows = [slice(i * MOE_TS, (i + 1) * MOE_TS) for i in halves]
        w1a, w3a, w2a, w1b, w3b, w2b = ([w_refs[6 * i + j] for i in halves] for j in range(6))
        x1 = [xs_ref[rows[i], 0:d] for i in halves]
        route = [xs_ref[rows[i], d:d + LANES] for i in halves]
        xn = [(x1[i] * lax.rsqrt(jnp.mean(x1[i] * x1[i], axis=-1, keepdims=True) + EPS)
               * nffn_ref[...]).astype(BF16) for i in halves]
        ga = [_dot(xn[i], w1a[i][0]) for i in halves]
        gb = [_dot(xn[i], w1b[i][0]) for i in halves]
        ua = [_dot(xn[i], w3a[i][0]) for i in halves]
        ub = [_dot(xn[i], w3b[i][0]) for i in halves]
        ha = [(_silu(ga[i]) * ua[i] * route[i][:, RT_WA:RT_WA + 1]).astype(BF16) for i in halves]
        hb = [(_silu(gb[i]) * ub[i] * route[i][:, RT_WB:RT_WB + 1]).astype(BF16) for i in halves]
        x2 = [x1[i] + _dot(ha[i], w2a[i][0]) + _dot(hb[i], w2b[i][0]) for i in halves]
        for i in halves:
            ys_ref[rows[i], :] = (x2[i] * lax.rsqrt(jnp.mean(x2[i] * x2[i], axis=-1, keepdims=True) + EPS)
                                  * nfin_ref[...])


def _ffn(tile_a, tile_b, n_valid, xs, w1, w3, w2, nffn, nfin):
    ns, dw = xs.shape
    d = dw - LANES
    de = w1.shape[2]
    row = lambda t, ta, tb, nv: (jnp.minimum(t, nv[0] - 1), 0)
    const = lambda t, ta, tb, nv: (0, 0)
    w_specs = []
    for i in range(2):
        wa = lambda t, ta, tb, nv, i=i: (ta[2 * t + i], 0, 0)
        wb = lambda t, ta, tb, nv, i=i: (tb[2 * t + i], 0, 0)
        w_specs += [pl.BlockSpec((1, d, de), wa), pl.BlockSpec((1, d, de), wa), pl.BlockSpec((1, de, d), wa),
                    pl.BlockSpec((1, d, de), wb), pl.BlockSpec((1, d, de), wb), pl.BlockSpec((1, de, d), wb)]
    return pl.pallas_call(
        _ffn_kernel,
        grid_spec=pltpu.PrefetchScalarGridSpec(
            num_scalar_prefetch=3,
            grid=(ns // (2 * MOE_TS),),
            in_specs=[pl.BlockSpec((2 * MOE_TS, dw), row)] + w_specs
            + [pl.BlockSpec((1, d), const), pl.BlockSpec((1, d), const)],
            out_specs=pl.BlockSpec((2 * MOE_TS, d), row),
        ),
        out_shape=jax.ShapeDtypeStruct((ns, d), F32),
        compiler_params=_cparams(("arbitrary",)),
        name="moe_ffn",
    )(tile_a, tile_b, n_valid, xs, *([w1, w3, w2, w1, w3, w2] * 2), nffn, nfin)


def _unsort_kernel(pos_hbm, ys_hbm, y_ref, pos_smem, sem_idx, sem_rows):
    i = pl.program_id(0)
    tm = y_ref.shape[0]
    rows = pos_smem.shape[0]
    idx_cp = pltpu.make_async_copy(pos_hbm.at[pl.ds(pl.multiple_of(i * rows, 8), rows)], pos_smem, sem_idx)
    idx_cp.start()
    idx_cp.wait()
    _row_copy_loops(tm, pos_smem, lambda t, p: pltpu.make_async_copy(
        ys_hbm.at[pl.ds(p, 1)], y_ref.at[pl.ds(t, 1)], sem_rows))


def _unsort(ys, pos):
    n = pos.shape[0]
    d = ys.shape[1]
    tm = _pick(n, (1024, 1152, 512, 384, 256, 128))
    any_spec = pl.BlockSpec(memory_space=pl.ANY)
    return pl.pallas_call(
        _unsort_kernel,
        grid=(n // tm,),
        in_specs=[any_spec, any_spec],
        out_specs=pl.BlockSpec((tm, d), lambda i: (i, 0)),
        out_shape=jax.ShapeDtypeStruct((n, d), F32),
        scratch_shapes=[pltpu.SMEM((_pos_rows(tm), LANES), jnp.int32), pltpu.SemaphoreType.DMA,
                        pltpu.SemaphoreType.DMA],
        compiler_params=_cparams(("arbitrary",)),
        name="moe_unsort",
    )(_tile_positions(pos, tm), ys)


def _pair_tables():
    a, b = [], []
    for g in range(N_GROUPS):
        for la in range(EXPERTS_PER_GROUP):
            for lb in range(la + 1, EXPERTS_PER_GROUP):
                a.append(g * EXPERTS_PER_GROUP + la)
                b.append(g * EXPERTS_PER_GROUP + lb)
    return jnp.asarray(a, jnp.int32), jnp.asarray(b, jnp.int32)


def _moe(xr_list, rt_list, cnt, w1, w3, w2, nffn, nfin):
    n_total = sum(x.shape[0] for x in xr_list)
    n_sorted = (-(-n_total // MOE_TS) + N_CLASSES + 2) // 2 * 2 * MOE_TS
    counts = cnt[0, :N_CLASSES].astype(jnp.int32)
    padded = (counts + MOE_TS - 1) // MOE_TS * MOE_TS
    ends = jnp.cumsum(padded)
    offs = ends - padded
    classes = jnp.arange(N_CLASSES, dtype=jnp.int32)

    def position(rt):
        cls = rt[:, RT_CLS].astype(jnp.int32)
        return jnp.sum(jnp.where(cls[:, None] == classes, offs, 0), axis=1) + rt[:, RT_RANK].astype(jnp.int32)

    pos_list = [position(rt) for rt in rt_list]
    tile_start = jnp.arange(n_sorted // MOE_TS, dtype=jnp.int32) * MOE_TS
    tile_cls = jnp.minimum(jnp.sum((ends <= tile_start[:, None]).astype(jnp.int32), axis=1), N_CLASSES - 1)
    pair_a, pair_b = _pair_tables()
    n_valid = ((ends[N_CLASSES - 1] // MOE_TS + 1) // 2).reshape(1)
    xs = None
    for xr, pos in zip(xr_list, pos_list):
        xs = _dispatch(ends, xr, pos, xs, n_sorted=n_sorted)
    tile_is = tile_cls[:, None] == classes
    tile_a = jnp.sum(jnp.where(tile_is, pair_a, 0), axis=1)
    tile_b = jnp.sum(jnp.where(tile_is, pair_b, 0), axis=1)
    ys = _ffn(tile_a, tile_b, n_valid, xs, w1, w3, w2, nffn, nfin)
    return [_unsort(ys, pos) for pos in pos_list]


def kernel(x_prompt, x_sample, cache_fox_k, cache_fox_v, cache_fox_logf, state_gdn, state_gdn_conv, meta_tokens, norm_mix_w, w_in, gdn_conv_w, gdn_A_log, gdn_dt_bias, gdn_norm_w, fox_q_norm_w, fox_k_norm_w, fox_f_bias, w_out, norm_ffn_w, w_router_group, b_router_group, w_router_expert, b_router_expert, w_gate, w_up, w_down, norm_final_w):
    B, S, D = x_prompt.shape
    BS, LS, _ = x_sample.shape
    C = GDN_CHUNK
    assert w_in.shape[0] == 1, "single-layer step only"
    assert S % C == 0 and LS <= C and N_META <= C and meta_tokens.shape[0] == N_META

    wi = w_in[0]
    o = 0
    parts = {}
    for name, size in (("g_qkv", 3 * WIDTH), ("g_z", WIDTH), ("g_a", HEADS), ("g_b", HEADS),
                       ("f_qkv", 3 * WIDTH), ("f_og", WIDTH), ("f_f", HEADS)):
        parts[name] = wi[:, o:o + size]
        o += size
    wbig = jnp.concatenate([parts[k].astype(BF16) for k in ("g_qkv", "f_qkv", "g_z", "f_og")], axis=1)
    wsm_cols = jnp.concatenate([parts["f_f"], parts["g_a"], parts["g_b"]], axis=1)
    wsm = jnp.pad(wsm_cols, ((0, 0), (0, LANES - 3 * HEADS))).astype(BF16)
    wsmt = jnp.pad(wsm_cols.T, ((0, 32 - 3 * HEADS), (0, 0))).astype(BF16)
    wo = w_out[0].astype(BF16)
    wr32 = jnp.pad(jnp.concatenate([w_router_group[0], w_router_expert[0]], axis=1),
                   ((0, 0), (0, LANES - N_GROUPS - N_EXPERTS)))
    wr_hi = wr32.astype(BF16)
    wr = jnp.concatenate([wr_hi, (wr32 - wr_hi.astype(F32)).astype(BF16)], axis=1)
    br = jnp.pad(jnp.concatenate([b_router_group[0], b_router_expert[0]]),
                 (0, LANES - N_GROUPS - N_EXPERTS)).reshape(1, LANES)
    w1 = w_gate[0].astype(BF16)
    w3 = w_up[0].astype(BF16)
    w2 = w_down[0].astype(BF16)
    nmix = norm_mix_w[0].reshape(1, D)
    nffn = norm_ffn_w[0].reshape(1, D)
    nfin = norm_final_w.reshape(1, D)

    x_small = jnp.concatenate([
        jnp.pad(meta_tokens.astype(F32), ((0, C - N_META), (0, 0))),
        jnp.pad(x_sample, ((0, 0), (0, C - LS), (0, 0))).reshape(BS * C, D)], axis=0)
    xp = x_prompt.reshape(B * S, D)

    p_s, psm_s, psmt_s = _inproj(x_small, nmix, wbig, wsm, wsmt)
    p_p, psm_p, psmt_p = _inproj(xp, nmix, wbig, wsm, wsmt)

    s0_s = jnp.concatenate([jnp.zeros((1,) + state_gdn.shape[2:], F32), state_gdn[0]], axis=0)
    cb_s = jnp.concatenate([jnp.zeros((1,) + state_gdn_conv.shape[2:], F32), state_gdn_conv[0]], axis=0)
    gdn_args = (gdn_conv_w[0], gdn_A_log[0], gdn_dt_bias[0], gdn_norm_w[0])
    og_s, st_s, cv_s = _gdn(p_s, psm_s, psmt_s, s0_s, cb_s, *gdn_args, n_seq=1 + BS, l_valid=LS,
                            bcast_state=False)
    og_p, st_p, cv_p = _gdn(p_p, psm_p, psmt_p, st_s[0:1], cv_s[0:1], *gdn_args, n_seq=B, l_valid=C,
                            bcast_state=True)

    fox_args = (fox_q_norm_w[0], fox_k_norm_w[0], fox_f_bias[0])
    zc = jnp.zeros((1, 1, LANES), F32)
    zr = jnp.zeros((1, HEADS, 1), F32)
    k32_s, v32_s, lfc_s, lfr_s, fc_s, fr_s, qt_s, kcat_s, vt_s, q_s, kb_s = _foxprep(
        p_s, psm_s, psmt_s, zc, zr, *fox_args, n_seq=1 + BS, bcast_carry=True, row_major=True)
    c0c = fc_s[N_META - 1:N_META, :].reshape(1, 1, LANES)
    c0r = fr_s[:, N_META - 1:N_META].reshape(1, HEADS, 1)
    k32_p, v32_p, lfc_p, lfr_p, fc_p, fr_p, qt_p, kcat_p, vt_p = _foxprep(
        p_p, psm_p, psmt_p, c0c, c0r, *fox_args, n_seq=B, bcast_carry=True, row_major=False, lead=N_META)
    of_p = _fox_prompt(qt_p, kcat_p, vt_p, p_p, kcat_s, vt_s, n_seq=B)
    of_s = _fox_sample(q_s, kb_s, p_s, lfc_s, lfr_s, cache_fox_k[0], cache_fox_v[0], cache_fox_logf[0],
                       l_valid=LS)

    xr_p, rt_p, cnt_p = _outproj(og_p, of_p, xp, wo, nffn, wr, br, jnp.zeros((1, LANES), F32))
    xr_s, rt_s, cnt = _outproj(og_s[C:], of_s, x_small[C:], wo, nffn, wr, br, cnt_p)
    y_p, y_s = _moe([xr_p, xr_s], [rt_p, rt_s], cnt, w1, w3, w2, nffn, nfin)

    def prompt_kv(tab_p, tab_s):
        tab = _fill_lead(tab_p, tab_s[:N_META * HEADS], n_seq=B)
        return tab.reshape(1, B, N_META + S, HEADS, HEAD_DIM)

    def sample_kv(tab_s):
        return tab_s.reshape(1 + BS, C, HEADS, HEAD_DIM)[1:, :LS][None]

    y_prompt = y_p.reshape(B, S, D)
    y_sample = y_s.reshape(BS, C, D)[:, :LS]
    fk_p = prompt_kv(k32_p, k32_s)
    fv_p = prompt_kv(v32_p, v32_s)
    lf_p = _assemble_logf(lfc_s, lfc_p, n_seq=B)[None]
    fk_s = sample_kv(k32_s)
    fv_s = sample_kv(v32_s)
    lf_s = lfc_s[C:, SM_F:SM_F + HEADS].reshape(BS, C, HEADS)[:, :LS][None]
    return (y_prompt, y_sample, fk_p, fv_p, lf_p, st_p[None], cv_p[None],
            fk_s, fv_s, lf_s, st_s[1:][None], cv_s[1:][None])
```

```python
import functools

import jax
import jax.numpy as jnp
from jax import lax
from jax.experimental import pallas as pl
from jax.experimental.pallas import tpu as pltpu

F32 = jnp.float32
BF16 = jnp.bfloat16
EPS = 1e-6
NEG = -1e30
LOG2E = 1.4426950408889634

N_META = 16
HEADS = 8
HEAD_DIM = 128
WIDTH = HEADS * HEAD_DIM
CONV_WIDTH = 4
N_GROUPS = 4
EXPERTS_PER_GROUP = 8
N_EXPERTS = N_GROUPS * EXPERTS_PER_GROUP
GDN_CHUNK = 128
LANES = 128
SM_F, SM_A, SM_B = 0, 8, 16
PB_FQ, PB_FK, PB_FV, PB_GZ, PB_FOG = 3, 4, 5, 6, 7
RT_E0 = N_GROUPS
PAIRS_PER_GROUP = EXPERTS_PER_GROUP * (EXPERTS_PER_GROUP - 1) // 2
N_CLASSES = N_GROUPS * PAIRS_PER_GROUP
RT_CLS, RT_RANK, RT_WA, RT_WB = 0, 1, 2, 3
MOE_TS = 256

VMEM_LIMIT = 56 * 1024 * 1024


def _cparams(sem):
    return pltpu.CompilerParams(dimension_semantics=sem, vmem_limit_bytes=VMEM_LIMIT)


def _pick(n, prefs):
    for p in prefs:
        if n % p == 0:
            return p
    raise ValueError(f"no tile in {prefs} divides {n}")


def _dot(a, b):
    return jnp.dot(a, b, preferred_element_type=F32)


def _dot_nt(a, b):
    return lax.dot_general(a, b, (((1,), (1,)), ((), ())), preferred_element_type=F32)


def _dot_tn(a, b):
    return lax.dot_general(a, b, (((0,), (0,)), ((), ())), preferred_element_type=F32)


def _split3(x):
    x1 = x.astype(BF16)
    r1 = x - x1.astype(F32)
    x2 = r1.astype(BF16)
    x3 = (r1 - x2.astype(F32)).astype(BF16)
    return x1, x2, x3


def _mask_dot(mask_bf16, x):
    x1, x2, x3 = _split3(x)
    return _dot(mask_bf16, x1) + _dot(mask_bf16, x2) + _dot(mask_bf16, x3)


def _dot_mask(x, mask_bf16):
    x1, x2, x3 = _split3(x)
    return _dot(x1, mask_bf16) + _dot(x2, mask_bf16) + _dot(x3, mask_bf16)


def _softplus(x):
    return jnp.maximum(x, 0.0) + jnp.log1p(jnp.exp(-jnp.abs(x)))


def _sigmoid(x):
    return 1.0 / (1.0 + jnp.exp(-x))


def _silu(x):
    return x * _sigmoid(x)


def _iota2(shape, dim):
    return lax.broadcasted_iota(jnp.int32, shape, dim)


def _inproj_kernel(x_ref, nw_ref, wbig_ref, wsm_ref, p_ref, psm_ref, psmt_ref, xn_scr):
    @pl.when(pl.program_id(1) == 0)
    def _():
        x = x_ref[...]
        xn = x * lax.rsqrt(jnp.mean(x * x, axis=-1, keepdims=True) + EPS) * nw_ref[...]
        xnb = xn.astype(BF16)
        xn_scr[...] = xnb
        psm = _dot(xnb, wsm_ref[...])
        psm_ref[...] = psm
        psmt_ref[...] = psm.T[0:psmt_ref.shape[0], :]

    p_ref[...] = _dot(xn_scr[...], wbig_ref[...]).astype(BF16)


def _inproj(x, norm_w, wbig, wsm):
    n, d = x.shape
    tm = _pick(n, (1024, 1152, 512, 384, 256, 128))
    tn = 2048
    nproj = wbig.shape[1]
    return pl.pallas_call(
        _inproj_kernel,
        grid=(n // tm, nproj // tn),
        in_specs=[
            pl.BlockSpec((tm, d), lambda i, j: (i, 0)),
            pl.BlockSpec((1, d), lambda i, j: (0, 0)),
            pl.BlockSpec((d, tn), lambda i, j: (0, j)),
            pl.BlockSpec((d, LANES), lambda i, j: (0, 0)),
        ],
        out_specs=[
            pl.BlockSpec((tm, tn), lambda i, j: (i, j)),
            pl.BlockSpec((tm, LANES), lambda i, j: (i, 0)),
            pl.BlockSpec((32, tm), lambda i, j: (0, i)),
        ],
        out_shape=[
            jax.ShapeDtypeStruct((n, nproj), BF16),
            jax.ShapeDtypeStruct((n, LANES), F32),
            jax.ShapeDtypeStruct((32, n), F32),
        ],
        scratch_shapes=[pltpu.VMEM((tm, d), BF16)],
        compiler_params=_cparams(("arbitrary", "arbitrary")),
        name="inproj",
    )(x, norm_w, wbig, wsm)


def _gdn_kernel(qkv_ref, z_ref, sm_ref, smt_ref, s0_ref, cb_ref, convw_ref, alog_c_ref, dtb_c_ref,
                alog_r_ref, dtb_r_ref, normw_ref, o_ref, snew_ref, cnew_ref, s_scr, xp_scr, conv_scr, *,
                l_valid):
    C = GDN_CHUNK
    n_sub = qkv_ref.shape[0] // C

    @pl.when(pl.program_id(1) == 0)
    def _():
        s_scr[...] = s0_ref[0]
        xp_scr[5:8, :] = cb_ref[0]

    row = _iota2((C, C), 0)
    col = _iota2((C, C), 1)
    incl = row >= col
    strict = row > col
    tril_b = jnp.where(incl, 1.0, 0.0).astype(BF16)
    triu_b = jnp.where(row <= col, 1.0, 0.0).astype(BF16)
    shift_b = jnp.concatenate(
        [jnp.where(col == row - i, 1.0, 0.0).astype(BF16) for i in (1, 2, 3)], axis=0)
    inv_levels = []
    sh = 1
    while (1 << sh) < C:
        inv_levels.append((((row >> sh) & 1) == 1) & ((col >> (sh + 1)) == (row >> (sh + 1)))
                          & (((col >> sh) & 1) == 0))
        sh += 1
    m1 = ((row & 1) == 1) & (col == row - 1)
    eye = jnp.where(row == col, 1.0, 0.0)
    w = convw_ref[...]
    normw = normw_ref[...]

    act, gl_col, gl_row, beta_col = [], [], [], []
    for sub in range(n_sub):
        rows = slice(sub * C, (sub + 1) * C)
        xb = qkv_ref[rows, :]
        x = xb.astype(F32)
        xp_scr[8:16, :] = x[0:8]
        shifted = _dot(shift_b, xb)
        conv_scr[rows, :] = (w[3:4] * x + w[2:3] * shifted[0:C] + w[1:2] * shifted[C:2 * C]
                             + w[0:1] * shifted[2 * C:3 * C])
        conv_scr[sub * C:sub * C + 8, :] = (w[0:1] * xp_scr[5:13, :] + w[1:2] * xp_scr[6:14, :]
                                            + w[2:3] * xp_scr[7:15, :] + w[3:4] * x[0:8])
        act.append(_silu(conv_scr[rows, :]))
        tail = x[l_valid - 3:l_valid, :]
        xp_scr[5:8, :] = tail
        cnew_ref[0] = tail

        sm = sm_ref[rows, :]
        g_col = -jnp.exp(alog_c_ref[...]) * _softplus(sm + dtb_c_ref[...])
        g_row = -jnp.exp(alog_r_ref[...]) * _softplus(smt_ref[:, rows] + dtb_r_ref[...])
        b_col = _sigmoid(sm)
        if l_valid < C:
            g_col = jnp.where(_iota2((C, LANES), 0) < l_valid, g_col, 0.0)
            b_col = jnp.where(_iota2((C, LANES), 0) < l_valid, b_col, 0.0)
            g_row = jnp.where(_iota2((HEADS, C), 1) < l_valid, g_row, 0.0)
        beta_col.append(b_col)
        gl_col.append(_mask_dot(tril_b, g_col))
        gl_row.append(_dot_mask(g_row, triu_b))

    hs = [slice(h * HEAD_DIM, (h + 1) * HEAD_DIM) for h in range(HEADS)]
    I = [(sub, h) for sub in range(n_sub) for h in range(HEADS)]
    gc = {(c, h): gl_col[c][:, SM_A + h:SM_A + h + 1] for c, h in I}
    beta = {(c, h): beta_col[c][:, SM_B + h:SM_B + h + 1] for c, h in I}
    qn, kn, kb, knb = {}, {}, {}, {}
    for c, h in I:
        q = act[c][:, hs[h]]
        k = act[c][:, WIDTH + h * HEAD_DIM:WIDTH + (h + 1) * HEAD_DIM]
        qn[c, h] = q * lax.rsqrt(jnp.sum(q * q, axis=-1, keepdims=True) + EPS) * (HEAD_DIM ** -0.5)
        kn[c, h] = k * lax.rsqrt(jnp.sum(k * k, axis=-1, keepdims=True) + EPS)
        kb[c, h] = kn[c, h] * beta[c, h]
        knb[c, h] = kn[c, h].astype(BF16)
    decay = {i: jnp.exp(jnp.where(incl, gc[i] - gl_row[i[0]][i[1]:i[1] + 1, :], NEG)) for i in I}
    a_mat = {i: jnp.where(strict, _dot_nt(kb[i].astype(BF16), knb[i]) * decay[i], 0.0) for i in I}
    qk = {i: (_dot_nt(qn[i].astype(BF16), knb[i]) * decay[i]).astype(BF16) for i in I}

    t = {i: eye - jnp.where(m1, a_mat[i], 0.0) for i in I}
    for m in inv_levels:
        tb = {i: t[i].astype(BF16) for i in I}
        y = {i: _dot(tb[i], jnp.where(m, a_mat[i], 0.0).astype(BF16)).astype(BF16) for i in I}
        t = {i: t[i] - _dot(y[i], tb[i]) for i in I}
    tb = {i: t[i].astype(BF16) for i in I}
    eg = {i: jnp.exp(gc[i]) for i in I}
    g_last = {i: gc[i][C - 1:C, :] for i in I}

    for c in range(n_sub):
        H = [(c, h) for h in range(HEADS)]
        s = {i: s_scr[i[1]] for i in H}
        sb = {i: s[i].astype(BF16) for i in H}
        r = {i: (act[c][:, 2 * WIDTH + i[1] * HEAD_DIM:2 * WIDTH + (i[1] + 1) * HEAD_DIM] * beta[i]
                 - _dot((kb[i] * eg[i]).astype(BF16), sb[i])).astype(BF16) for i in H}
        ub = {i: _dot(tb[i], r[i]).astype(BF16) for i in H}
        o = {i: _dot((qn[i] * eg[i]).astype(BF16), sb[i]) + _dot(qk[i], ub[i]) for i in H}
        for i in H:
            k_dec = (kn[i] * jnp.exp(g_last[i] - gc[i])).astype(BF16)
            s_scr[i[1]] = s[i] * jnp.exp(g_last[i]) + _dot_tn(k_dec, ub[i])
        for i in H:
            on = o[i] * lax.rsqrt(jnp.mean(o[i] * o[i], axis=-1, keepdims=True) + EPS) * normw
            z = z_ref[c * C:(c + 1) * C, hs[i[1]]].astype(F32)
            o_ref[c * C:(c + 1) * C, hs[i[1]]] = (on * _silu(z)).astype(BF16)

    snew_ref[0] = s_scr[...]


def _gdn(p, psm, psmt, s0, cb, conv_w, alog, dtb, norm_w, *, n_seq, l_valid, bcast_state):
    n = p.shape[0]
    C = GDN_CHUNK
    n_sub = _pick(n // (n_seq * C), (2, 1))
    R = n_sub * C
    nc = n // (n_seq * R)
    alog_c = jnp.zeros((1, LANES), F32).at[0, SM_A:SM_A + HEADS].set(alog)
    dtb_c = jnp.zeros((1, LANES), F32).at[0, SM_A:SM_A + HEADS].set(dtb)
    st_idx = (lambda b, c: (0, 0, 0, 0)) if bcast_state else (lambda b, c: (b, 0, 0, 0))
    cb_idx = (lambda b, c: (0, 0, 0)) if bcast_state else (lambda b, c: (b, 0, 0))
    return pl.pallas_call(
        functools.partial(_gdn_kernel, l_valid=l_valid),
        grid=(n_seq, nc),
        in_specs=[
            pl.BlockSpec((R, 3 * WIDTH), lambda b, c: (b * nc + c, 0)),
            pl.BlockSpec((R, WIDTH), lambda b, c: (b * nc + c, PB_GZ)),
            pl.BlockSpec((R, LANES), lambda b, c: (b * nc + c, 0)),
            pl.BlockSpec((HEADS, R), lambda b, c: (SM_A // HEADS, b * nc + c)),
            pl.BlockSpec((1, HEADS, HEAD_DIM, HEAD_DIM), st_idx),
            pl.BlockSpec((1, CONV_WIDTH - 1, 3 * WIDTH), cb_idx),
            pl.BlockSpec((CONV_WIDTH, 3 * WIDTH), lambda b, c: (0, 0)),
            pl.BlockSpec((1, LANES), lambda b, c: (0, 0)),
            pl.BlockSpec((1, LANES), lambda b, c: (0, 0)),
            pl.BlockSpec((HEADS, 1), lambda b, c: (0, 0)),
            pl.BlockSpec((HEADS, 1), lambda b, c: (0, 0)),
            pl.BlockSpec((1, HEAD_DIM), lambda b, c: (0, 0)),
        ],
        out_specs=[
            pl.BlockSpec((R, WIDTH), lambda b, c: (b * nc + c, 0)),
            pl.BlockSpec((1, HEADS, HEAD_DIM, HEAD_DIM), lambda b, c: (b, 0, 0, 0)),
            pl.BlockSpec((1, CONV_WIDTH - 1, 3 * WIDTH), lambda b, c: (b, 0, 0)),
        ],
        out_shape=[
            jax.ShapeDtypeStruct((n, WIDTH), BF16),
            jax.ShapeDtypeStruct((n_seq, HEADS, HEAD_DIM, HEAD_DIM), F32),
            jax.ShapeDtypeStruct((n_seq, CONV_WIDTH - 1, 3 * WIDTH), F32),
        ],
        scratch_shapes=[
            pltpu.VMEM((HEADS, HEAD_DIM, HEAD_DIM), F32),
            pltpu.VMEM((16, 3 * WIDTH), F32),
            pltpu.VMEM((R, 3 * WIDTH), F32),
        ],
        compiler_params=_cparams(("arbitrary", "arbitrary")),
        name="gdn",
    )(p, p, psm, psmt, s0, cb, conv_w, alog_c, dtb_c, alog.reshape(HEADS, 1), dtb.reshape(HEADS, 1),
      norm_w.reshape(1, HEAD_DIM))


def _foxprep_kernel(qkv_ref, sm_ref, smt_ref, c0c_ref, c0r_ref, qw_ref, kw_ref, fb_c_ref, fb_r_ref,
                    k32_ref, v32_ref, lfc_ref, lfr_ref, fc_ref, fr_ref, qt_ref, kcat_ref, vt_ref, *rest,
                    row_major):
    if row_major:
        q_ref, kb_ref, cc_scr, cr_scr = rest
    else:
        cc_scr, cr_scr = rest
    tm = qkv_ref.shape[0]

    @pl.when(pl.program_id(1) == 0)
    def _():
        cc_scr[...] = c0c_ref[0]
        cr_scr[...] = c0r_ref[0]

    lf_col = -_softplus(-(sm_ref[...] + fb_c_ref[...]))
    lf_row = -_softplus(-(smt_ref[...] + fb_r_ref[...]))
    lfc_ref[...] = lf_col
    lfr_ref[...] = lf_row
    row = _iota2((tm, tm), 0)
    col = _iota2((tm, tm), 1)
    tril_b = jnp.where(row >= col, 1.0, 0.0).astype(BF16)
    triu_b = jnp.where(row <= col, 1.0, 0.0).astype(BF16)
    f_col = cc_scr[...] + _mask_dot(tril_b, lf_col)
    f_row = cr_scr[...] + _dot_mask(lf_row, triu_b)
    fc_ref[...] = f_col
    fr_ref[...] = f_row
    cc_scr[...] = f_col[tm - 1:tm, :]
    cr_scr[...] = f_row[:, tm - 1:tm]

    qw = qw_ref[...]
    kw = kw_ref[...]
    lane = _iota2((tm, HEAD_DIM), 1)
    sub = _iota2((16, tm), 0)
    zeros_t = jnp.zeros((HEAD_DIM - 16, tm), BF16)
    for h in range(HEADS):
        hs = slice(h * HEAD_DIM, (h + 1) * HEAD_DIM)
        q = qkv_ref[:, hs].astype(F32)
        k = qkv_ref[:, WIDTH + h * HEAD_DIM:WIDTH + (h + 1) * HEAD_DIM].astype(F32)
        v = qkv_ref[:, 2 * WIDTH + h * HEAD_DIM:2 * WIDTH + (h + 1) * HEAD_DIM].astype(F32)
        qn = q * lax.rsqrt(jnp.mean(q * q, axis=-1, keepdims=True) + EPS) * qw
        kn = k * lax.rsqrt(jnp.mean(k * k, axis=-1, keepdims=True) + EPS) * kw
        k32_ref[pl.ds(h, tm, stride=HEADS), :] = kn
        v32_ref[pl.ds(h, tm, stride=HEADS), :] = v
        if row_major:
            q_ref[:, hs] = (qn * (HEAD_DIM ** -0.5)).astype(BF16)
            kb_ref[:, hs] = kn.astype(BF16)
        base = 2 * h * HEAD_DIM
        qt_ref[base:base + HEAD_DIM, :] = (qn * (LOG2E * HEAD_DIM ** -0.5)).T.astype(BF16)
        r1, r2, r3 = _split3(f_row[h:h + 1, :] * LOG2E)
        aug_q = jnp.where(sub == 0, r1.astype(F32), jnp.where(sub == 1, r2.astype(F32), jnp.where(
            sub == 2, r3.astype(F32), jnp.where(sub < 6, 1.0, 0.0))))
        qt_ref[base + HEAD_DIM:base + HEAD_DIM + 16, :] = aug_q.astype(BF16)
        qt_ref[base + HEAD_DIM + 16:base + 2 * HEAD_DIM, :] = zeros_t
        kcat_ref[:, base:base + HEAD_DIM] = kn.astype(BF16)
        c1, c2, c3 = _split3(f_col[:, SM_F + h:SM_F + h + 1] * LOG2E)
        aug_k = jnp.where(lane < 3, 1.0, jnp.where(lane == 3, -c1.astype(F32), jnp.where(
            lane == 4, -c2.astype(F32), jnp.where(lane == 5, -c3.astype(F32), 0.0))))
        kcat_ref[:, base + HEAD_DIM:base + 2 * HEAD_DIM] = aug_k.astype(BF16)
        vt_ref[hs, :] = v.T.astype(BF16)


def _foxprep(p, psm, psmt, c0c, c0r, q_norm_w, k_norm_w, f_bias, *, n_seq, bcast_carry, row_major, lead=0):
    n = p.shape[0]
    rows = n // n_seq
    tm = _pick(rows, (512, 256, 128))
    nt = rows // tm
    fb_c = jnp.zeros((1, LANES), F32).at[0, SM_F:SM_F + HEADS].set(f_bias)
    c_idx = (lambda b, t: (0, 0, 0)) if bcast_carry else (lambda b, t: (b, 0, 0))
    rowblk = lambda b, t: (b * nt + t, 0)
    colblk = lambda b, t: (0, b * nt + t)
    kv_spec = pl.BlockSpec((pl.Element(tm * HEADS), pl.Element(HEAD_DIM)),
                           lambda b, t: ((b * (rows + lead) + lead + t * tm) * HEADS, 0))
    kv_shape = jax.ShapeDtypeStruct((n_seq * (rows + lead) * HEADS, HEAD_DIM), F32)
    out_specs = [
        kv_spec,
        kv_spec,
        pl.BlockSpec((tm, LANES), rowblk),
        pl.BlockSpec((HEADS, tm), colblk),
        pl.BlockSpec((tm, LANES), rowblk),
        pl.BlockSpec((HEADS, tm), colblk),
        pl.BlockSpec((2 * WIDTH, tm), colblk),
        pl.BlockSpec((tm, 2 * WIDTH), rowblk),
        pl.BlockSpec((WIDTH, tm), colblk),
    ]
    out_shape = [
        kv_shape,
        kv_shape,
        jax.ShapeDtypeStruct((n, LANES), F32),
        jax.ShapeDtypeStruct((HEADS, n), F32),
        jax.ShapeDtypeStruct((n, LANES), F32),
        jax.ShapeDtypeStruct((HEADS, n), F32),
        jax.ShapeDtypeStruct((2 * WIDTH, n), BF16),
        jax.ShapeDtypeStruct((n, 2 * WIDTH), BF16),
        jax.ShapeDtypeStruct((WIDTH, n), BF16),
    ]
    if row_major:
        out_specs += [pl.BlockSpec((tm, WIDTH), rowblk), pl.BlockSpec((tm, WIDTH), rowblk)]
        out_shape += [jax.ShapeDtypeStruct((n, WIDTH), BF16),
                      jax.ShapeDtypeStruct((n, WIDTH), BF16)]
    return pl.pallas_call(
        functools.partial(_foxprep_kernel, row_major=row_major),
        grid=(n_seq, nt),
        in_specs=[
            pl.BlockSpec((tm, 3 * WIDTH), lambda b, t: (b * nt + t, 1)),
            pl.BlockSpec((tm, LANES), rowblk),
            pl.BlockSpec((HEADS, tm), lambda b, t: (SM_F // HEADS, b * nt + t)),
            pl.BlockSpec((1, 1, LANES), c_idx),
            pl.BlockSpec((1, HEADS, 1), c_idx),
            pl.BlockSpec((1, HEAD_DIM), lambda b, t: (0, 0)),
            pl.BlockSpec((1, HEAD_DIM), lambda b, t: (0, 0)),
            pl.BlockSpec((1, LANES), lambda b, t: (0, 0)),
            pl.BlockSpec((HEADS, 1), lambda b, t: (0, 0)),
        ],
        out_specs=out_specs,
        out_shape=out_shape,
        scratch_shapes=[pltpu.VMEM((1, LANES), F32), pltpu.VMEM((HEADS, 1), F32)],
        compiler_params=_cparams(("arbitrary", "arbitrary")),
        name="foxprep",
    )(p, psm, psmt, c0c, c0r, q_norm_w.reshape(1, HEAD_DIM), k_norm_w.reshape(1, HEAD_DIM), fb_c,
      f_bias.reshape(HEADS, 1))


def _fill_lead_kernel(src_ref, big_ref, o_ref):
    o_ref[...] = src_ref[...]


def _fill_lead(big, src, *, n_seq):
    r = src.shape[0]
    seq_rows = big.shape[0] // n_seq
    assert seq_rows % r == 0
    return pl.pallas_call(
        _fill_lead_kernel,
        grid=(n_seq,),
        in_specs=[pl.BlockSpec((r, HEAD_DIM), lambda b: (0, 0)), pl.BlockSpec(memory_space=pl.ANY)],
        out_specs=pl.BlockSpec((r, HEAD_DIM), lambda b: (b * (seq_rows // r), 0)),
        out_shape=jax.ShapeDtypeStruct(big.shape, big.dtype),
        input_output_aliases={1: 0},
        compiler_params=_cparams(("arbitrary",)),
        name="fill_lead",
    )(src, big)


def _logf_kernel(meta_ref, lf_ref, o_ref):
    o_ref[0, 0:N_META, :] = meta_ref[0:N_META, SM_F:SM_F + HEADS]
    o_ref[0, N_META:, :] = lf_ref[:, SM_F:SM_F + HEADS]


def _assemble_logf(lf_small, lf_prompt, *, n_seq):
    s = lf_prompt.shape[0] // n_seq
    return pl.pallas_call(
        _logf_kernel,
        grid=(n_seq,),
        in_specs=[pl.BlockSpec((GDN_CHUNK, LANES), lambda b: (0, 0)), pl.BlockSpec((s, LANES), lambda b: (b, 0))],
        out_specs=pl.BlockSpec((1, N_META + s, HEADS), lambda b: (b, 0, 0)),
        out_shape=jax.ShapeDtypeStruct((n_seq, N_META + s, HEADS), F32),
        compiler_params=_cparams(("arbitrary",)),
        name="assemble_logf",
    )(lf_small, lf_prompt)


FOX_UNIT = 256
FOX_SUM_ROWS = 16

def _fox_kernel(qi_ref, ki_ref, qt_ref, kcat_ref, vt_ref, og_ref, km_ref, vtm_ref, o_ref, m_scr, acc_scr):
    pair = pl.program_id(1)
    qi = qi_ref[pair]
    ki = ki_ref[pair]
    tq = qt_ref.shape[1]
    tk = kcat_ref.shape[0]
    U = FOX_UNIT
    H = range(HEADS)

    def unit(qh, k_rows, vt_cols, mask):
        qs = slice(qh * U, (qh + 1) * U)
        s = [_dot(k_rows(h), qt_ref[2 * h * HEAD_DIM:2 * (h + 1) * HEAD_DIM, qs]) for h in H]
        if mask is not None:
            s = [jnp.where(mask, s[h], NEG) for h in H]
        m_old = [m_scr[h:h + 1, qs] for h in H]
        m_new = [jnp.maximum(m_old[h], jnp.max(s[h], axis=0, keepdims=True)) for h in H]
        alpha = [jnp.exp2(m_old[h] - m_new[h]) for h in H]
        p = [jnp.exp2((s[h] - m_new[h]).astype(BF16)) for h in H]
        for h in H:
            m_scr[h:h + 1, qs] = m_new[h]
        for h in H:
            lhs = jnp.concatenate([vt_cols(h), jnp.ones((FOX_SUM_ROWS, p[h].shape[0]), BF16)], axis=0)
            acc_scr[h, :, qs] = alpha[h] * acc_scr[h, :, qs] + _dot(lhs, p[h])

    def k_unit(ku):
        return lambda h: kcat_ref[ku * U:(ku + 1) * U, 2 * h * HEAD_DIM:2 * (h + 1) * HEAD_DIM]

    def vt_unit(ku):
        return lambda h: vt_ref[h * HEAD_DIM:(h + 1) * HEAD_DIM, ku * U:(ku + 1) * U]

    @pl.when(ki == 0)
    def _():
        m_scr[...] = jnp.full(m_scr.shape, NEG, F32)
        acc_scr[...] = jnp.zeros(acc_scr.shape, F32)
        tmeta = km_ref.shape[0]
        mask = _iota2((tmeta, U), 0) < N_META
        for qh in range(tq // U):
            unit(qh, lambda h: km_ref[:, 2 * h * HEAD_DIM:2 * (h + 1) * HEAD_DIM],
                 lambda h: vtm_ref[h * HEAD_DIM:(h + 1) * HEAD_DIM, :], mask)

    @pl.when(ki < qi)
    def _():
        for qh in range(tq // U):
            for ku in range(tk // U):
                unit(qh, k_unit(ku), vt_unit(ku), None)

    @pl.when(ki == qi)
    def _():
        diag = _iota2((U, U), 0) <= _iota2((U, U), 1)
        for qh in range(tq // U):
            for ku in range(qh + 1):
                unit(qh, k_unit(ku), vt_unit(ku), diag if ku == qh else None)
        for h in H:
            hs = slice(h * HEAD_DIM, (h + 1) * HEAD_DIM)
            gate = _sigmoid(og_ref[:, hs].astype(F32))
            o_ref[:, hs] = ((acc_scr[h, 0:HEAD_DIM, :] / acc_scr[h, HEAD_DIM:HEAD_DIM + 1, :]).T * gate).astype(BF16)


def _fox_prompt(qt, kcat, vt, p, kcat_small, vt_small, *, n_seq):
    n = p.shape[0]
    rows = n // n_seq
    tq = _pick(rows, (512, 256))
    nq = rows // tq
    pairs = [(i, j) for i in range(nq) for j in range(i + 1)]
    qi = jnp.asarray([a for a, _ in pairs], jnp.int32)
    ki = jnp.asarray([b for _, b in pairs], jnp.int32)
    grid_spec = pltpu.PrefetchScalarGridSpec(
        num_scalar_prefetch=2,
        grid=(n_seq, len(pairs)),
        in_specs=[
            pl.BlockSpec((2 * WIDTH, tq), lambda b, t, qi_r, ki_r: (0, b * nq + qi_r[t])),
            pl.BlockSpec((tq, 2 * WIDTH), lambda b, t, qi_r, ki_r: (b * nq + ki_r[t], 0)),
            pl.BlockSpec((WIDTH, tq), lambda b, t, qi_r, ki_r: (0, b * nq + ki_r[t])),
            pl.BlockSpec((tq, WIDTH), lambda b, t, qi_r, ki_r: (b * nq + qi_r[t], PB_FOG)),
            pl.BlockSpec((GDN_CHUNK, 2 * WIDTH), lambda b, t, qi_r, ki_r: (0, 0)),
            pl.BlockSpec((WIDTH, GDN_CHUNK), lambda b, t, qi_r, ki_r: (0, 0)),
        ],
        out_specs=pl.BlockSpec((tq, WIDTH), lambda b, t, qi_r, ki_r: (b * nq + qi_r[t], 0)),
        scratch_shapes=[
            pltpu.VMEM((HEADS, tq), F32),
            pltpu.VMEM((HEADS, HEAD_DIM + FOX_SUM_ROWS, tq), F32),
        ],
    )
    return pl.pallas_call(
        _fox_kernel,
        grid_spec=grid_spec,
        out_shape=jax.ShapeDtypeStruct((n, WIDTH), BF16),
        compiler_params=_cparams(("arbitrary", "arbitrary")),
        name="fox_prompt",
    )(qi, ki, qt, kcat, vt, p, kcat_small, vt_small)


def _fox_sample_kernel(q_ref, kn_ref, vn_ref, og_ref, lfc_ref, lfr_ref, ck_ref, cv_ref, clc_ref, clr_ref,
                       o_ref, *, l_valid):
    C = q_ref.shape[0]
    P = ck_ref.shape[1] // HEADS
    rowp = _iota2((P, P), 0)
    colp = _iota2((P, P), 1)
    triu_p = jnp.where(rowp <= colp, 1.0, 0.0).astype(BF16)
    row = _iota2((C, C), 0)
    col = _iota2((C, C), 1)
    tril_b = jnp.where(row >= col, 1.0, 0.0).astype(BF16)
    triu_b = jnp.where(row <= col, 1.0, 0.0).astype(BF16)

    f_cache = _dot_mask(clr_ref[0], triu_p)
    carry_r = f_cache[:, P - 1:P]
    carry_c = jnp.sum(clc_ref[0], axis=0, keepdims=True)
    f_new_r = carry_r + _dot_mask(lfr_ref[...], triu_b)
    f_new_c = carry_c + _mask_dot(tril_b, lfc_ref[...])[:, SM_F:SM_F + HEADS]
    mask_new = (col <= row) & (col < l_valid)

    for h in range(HEADS):
        hs = slice(h * HEAD_DIM, (h + 1) * HEAD_DIM)
        q = q_ref[:, hs]
        fq = f_new_c[:, h:h + 1]
        ck = ck_ref[0, pl.ds(h, P, stride=HEADS), :].astype(BF16)
        cv = cv_ref[0, pl.ds(h, P, stride=HEADS), :].astype(BF16)
        s_c = _dot_nt(q, ck) + (fq - f_cache[h:h + 1, :])
        s_n = jnp.where(mask_new, _dot_nt(q, kn_ref[:, hs]) + (fq - f_new_r[h:h + 1, :]), NEG)
        m = jnp.maximum(jnp.max(s_c, axis=-1, keepdims=True), jnp.max(s_n, axis=-1, keepdims=True))
        p_c = jnp.exp(s_c - m)
        p_n = jnp.exp(s_n - m)
        l = jnp.sum(p_c, axis=-1, keepdims=True) + jnp.sum(p_n, axis=-1, keepdims=True)
        o = _dot(p_c.astype(BF16), cv) + _dot(p_n.astype(BF16), vn_ref[:, hs])
        gate = _sigmoid(og_ref[:, hs].astype(F32))
        o_ref[:, hs] = (o / l * gate).astype(BF16)


def _fox_sample(q, kb, p, lf_col, lf_row, cache_k, cache_v, cache_lf, *, l_valid):
    bs, past = cache_k.shape[0], cache_k.shape[1]
    C = GDN_CHUNK
    n = q.shape[0]
    blk = lambda b: (b + 1, 0)
    return pl.pallas_call(
        functools.partial(_fox_sample_kernel, l_valid=l_valid),
        grid=(bs,),
        in_specs=[
            pl.BlockSpec((C, WIDTH), blk),
            pl.BlockSpec((C, WIDTH), blk),
            pl.BlockSpec((C, WIDTH), lambda b: (b + 1, PB_FV)),
            pl.BlockSpec((C, WIDTH), lambda b: (b + 1, PB_FOG)),
            pl.BlockSpec((C, LANES), blk),
            pl.BlockSpec((HEADS, C), lambda b: (0, b + 1)),
            pl.BlockSpec((1, past * HEADS, HEAD_DIM), lambda b: (b, 0, 0)),
            pl.BlockSpec((1, past * HEADS, HEAD_DIM), lambda b: (b, 0, 0)),
            pl.BlockSpec((1, past, HEADS), lambda b: (b, 0, 0)),
            pl.BlockSpec((1, HEADS, past), lambda b: (b, 0, 0)),
        ],
        out_specs=pl.BlockSpec((C, WIDTH), lambda b: (b, 0)),
        out_shape=jax.ShapeDtypeStruct((bs * C, WIDTH), BF16),
        compiler_params=_cparams(("arbitrary",)),
        name="fox_sample",
    )(q, kb, p, p, lf_col, lf_row, cache_k.reshape(bs, past * HEADS, HEAD_DIM),
      cache_v.reshape(bs, past * HEADS, HEAD_DIM), cache_lf, jnp.swapaxes(cache_lf, 1, 2))


def _outproj_kernel(og_ref, of_ref, x_ref, wo_ref, nw_ref, wr_ref, br_ref, cnt0_ref, xr_ref, rt_ref, cnt_ref,
                    cnt_scr):
    d = x_ref.shape[1]

    @pl.when(pl.program_id(0) == 0)
    def _():
        cnt_scr[...] = cnt0_ref[...]

    half = og_ref.shape[1]
    h = _dot(og_ref[...], wo_ref[0:half, :]) + _dot(of_ref[...], wo_ref[half:2 * half, :])
    x1 = x_ref[...] + h
    xr_ref[:, 0:d] = x1
    xn = x1 * lax.rsqrt(jnp.mean(x1 * x1, axis=-1, keepdims=True) + EPS) * nw_ref[...]

    x_hi = xn.astype(BF16)
    x_lo = (xn - x_hi.astype(F32)).astype(BF16)
    hi = _dot(x_hi, wr_ref[...])
    logits = (hi[:, 0:LANES] + hi[:, LANES:2 * LANES] + _dot(x_lo, wr_ref[:, 0:LANES])) + br_ref[...]

    tm = logits.shape[0]
    lane = _iota2((tm, LANES), 1).astype(F32)
    big = float(LANES)
    gl = jnp.where(lane < N_GROUPS, logits, NEG)
    gmax = jnp.max(gl, axis=-1, keepdims=True)
    gidx = jnp.min(jnp.where(gl == gmax, lane, big), axis=-1, keepdims=True)
    p_top = 1.0 / jnp.sum(jnp.exp(gl - gmax), axis=-1, keepdims=True)
    e = lane - RT_E0
    sel = (e >= 0) & (e < N_EXPERTS) & (jnp.floor(e * (1.0 / EXPERTS_PER_GROUP)) == gidx)
    el = jnp.where(sel, logits, NEG)
    v1 = jnp.max(el, axis=-1, keepdims=True)
    i1 = jnp.min(jnp.where(el == v1, lane, big), axis=-1, keepdims=True)
    el2 = jnp.where(lane == i1, NEG, el)
    v2 = jnp.max(el2, axis=-1, keepdims=True)
    i2 = jnp.min(jnp.where(el2 == v2, lane, big), axis=-1, keepdims=True)
    e2 = jnp.exp(v2 - v1)
    w1 = p_top / (1.0 + e2)
    w2 = p_top * e2 / (1.0 + e2)

    ex1 = i1 - RT_E0
    ex2 = i2 - RT_E0
    first = ex1 < ex2
    ea = jnp.where(first, ex1, ex2)
    eb = jnp.where(first, ex2, ex1)
    wa = jnp.where(first, w1, w2)
    wb = jnp.where(first, w2, w1)
    la = ea - gidx * EXPERTS_PER_GROUP
    lb = eb - gidx * EXPERTS_PER_GROUP
    cls = gidx * PAIRS_PER_GROUP + la * (2 * EXPERTS_PER_GROUP - 1 - la) * 0.5 + (lb - la - 1.0)

    onehot = lane == cls
    oh = jnp.where(onehot, 1.0, 0.0)
    strict_b = jnp.where(_iota2((tm, tm), 0) > _iota2((tm, tm), 1), 1.0, 0.0).astype(BF16)
    before = cnt_scr[...] + _dot(strict_b, oh.astype(BF16))
    rank = jnp.sum(jnp.where(onehot, before, 0.0), axis=-1, keepdims=True)
    cnt_scr[...] += jnp.sum(oh, axis=0, keepdims=True)
    cnt_ref[...] = cnt_scr[...]
    route = jnp.where(lane == RT_CLS, cls, jnp.where(lane == RT_RANK, rank, jnp.where(
        lane == RT_WA, wa, jnp.where(lane == RT_WB, wb, 0.0))))
    rt_ref[...] = route
    xr_ref[:, d:d + LANES] = route


def _outproj(og, of, x, wo, norm_w, wr, br, cnt0):
    n, d = x.shape
    tm = _pick(n, (512, 384, 256, 128))
    rowblk = lambda i: (i, 0)
    return pl.pallas_call(
        _outproj_kernel,
        grid=(n // tm,),
        in_specs=[
            pl.BlockSpec((tm, WIDTH), rowblk),
            pl.BlockSpec((tm, WIDTH), rowblk),
            pl.BlockSpec((tm, d), rowblk),
            pl.BlockSpec((2 * WIDTH, d), lambda i: (0, 0)),
            pl.BlockSpec((1, d), lambda i: (0, 0)),
            pl.BlockSpec((d, 2 * LANES), lambda i: (0, 0)),
            pl.BlockSpec((1, LANES), lambda i: (0, 0)),
            pl.BlockSpec((1, LANES), lambda i: (0, 0)),
        ],
        out_specs=[
            pl.BlockSpec((tm, d + LANES), rowblk),
            pl.BlockSpec((tm, LANES), rowblk),
            pl.BlockSpec((1, LANES), lambda i: (0, 0)),
        ],
        out_shape=[
            jax.ShapeDtypeStruct((n, d + LANES), F32),
            jax.ShapeDtypeStruct((n, LANES), F32),
            jax.ShapeDtypeStruct((1, LANES), F32),
        ],
        scratch_shapes=[pltpu.VMEM((1, LANES), F32)],
        compiler_params=_cparams(("arbitrary",)),
        name="outproj_router",
    )(og, of, x, wo, norm_w, wr, br, cnt0)


def _pos_rows(tm):
    return -(-(-(-tm // LANES)) // 8) * 8


def _tile_positions(pos, tm):
    nt = pos.shape[0] // tm
    rows = _pos_rows(tm)
    p = jnp.pad(pos.reshape(nt, tm), ((0, 0), (0, rows * LANES - tm)))
    return p.reshape(nt * rows, LANES)


def _row_copy_loops(tm, pos_smem, make_copy):
    def start_row(r, carry):
        base = pl.multiple_of(r * LANES, LANES)
        for c in range(LANES):
            make_copy(base + c, pos_smem[r, c]).start()
        return carry

    def wait(t, carry):
        make_copy(0, 0).wait()
        return carry

    lax.fori_loop(0, tm // LANES, start_row, 0)
    lax.fori_loop(0, tm, wait, 0, unroll=8)


def _dispatch_kernel(ends_ref, xr_ref, pos_hbm, *rest, zero_fill):
    if zero_fill:
        xs_hbm, pos_smem, zbuf, sem_idx, sem_fill, sem_rows = rest
    else:
        _, xs_hbm, pos_smem, sem_idx, sem_rows = rest
    i = pl.program_id(0)
    tm = xr_ref.shape[0]
    rows = pos_smem.shape[0]
    idx_cp = pltpu.make_async_copy(pos_hbm.at[pl.ds(pl.multiple_of(i * rows, 8), rows)], pos_smem, sem_idx)
    idx_cp.start()

    if zero_fill:
        @pl.when(i == 0)
        def _():
            zbuf[...] = jnp.zeros(zbuf.shape, F32)

            def fill(start):
                def body(c, carry):
                    lo = jnp.where(c == 0, 0, ends_ref[jnp.maximum(c - 1, 0)])
                    hi = ends_ref[c]

                    @pl.when(hi > lo)
                    def _():
                        cp = pltpu.make_async_copy(
                            zbuf, xs_hbm.at[pl.ds(pl.multiple_of(hi - MOE_TS, MOE_TS), MOE_TS)], sem_fill)
                        if start:
                            cp.start()
                        else:
                            cp.wait()
                    return carry
                lax.fori_loop(0, N_CLASSES, body, 0)

            fill(True)
            fill(False)

            last = ends_ref[N_CLASSES - 1]

            @pl.when((last // MOE_TS) % 2 == 1)
            def _():
                cp = pltpu.make_async_copy(zbuf, xs_hbm.at[pl.ds(pl.multiple_of(last, MOE_TS), MOE_TS)], sem_fill)
                cp.start()
                cp.wait()

    idx_cp.wait()
    _row_copy_loops(tm, pos_smem, lambda t, p: pltpu.make_async_copy(
        xr_ref.at[pl.ds(t, 1)], xs_hbm.at[pl.ds(p, 1)], sem_rows))


def _dispatch(ends, xr, pos, xs, *, n_sorted):
    n, dw = xr.shape
    tm = _pick(n, (1024, 1152, 512, 384, 256, 128))
    rows = _pos_rows(tm)
    zero_fill = xs is None
    any_spec = pl.BlockSpec(memory_space=pl.ANY)
    in_specs = [pl.BlockSpec((tm, dw), lambda i, ends_r: (i, 0)), any_spec]
    args = [xr, _tile_positions(pos, tm)]
    scratch = [pltpu.SMEM((rows, LANES), jnp.int32)]
    if zero_fill:
        scratch += [pltpu.VMEM((MOE_TS, dw), F32), pltpu.SemaphoreType.DMA, pltpu.SemaphoreType.DMA,
                    pltpu.SemaphoreType.DMA]
        aliases = {}
    else:
        in_specs.append(any_spec)
        args.append(xs)
        scratch += [pltpu.SemaphoreType.DMA, pltpu.SemaphoreType.DMA]
        aliases = {3: 0}
    return pl.pallas_call(
        functools.partial(_dispatch_kernel, zero_fill=zero_fill),
        grid_spec=pltpu.PrefetchScalarGridSpec(
            num_scalar_prefetch=1, grid=(n // tm,), in_specs=in_specs, out_specs=any_spec,
            scratch_shapes=scratch),
        out_shape=jax.ShapeDtypeStruct((n_sorted, dw), F32),
        input_output_aliases=aliases,
        compiler_params=pltpu.CompilerParams(dimension_semantics=("arbitrary",), vmem_limit_bytes=VMEM_LIMIT,
                                             has_side_effects=True),
        name="moe_dispatch",
    )(ends, *args)


def _ffn_kernel(ta_ref, tb_ref, nv_ref, xs_ref, *refs):
    w_refs, (nffn_ref, nfin_ref, ys_ref) = refs[:12], refs[12:]

    @pl.when(pl.program_id(0) < nv_ref[0])
    def _():
        d = ys_ref.shape[1]
        halves = range(2)
        rows = [slice(i * MOE_TS, (i + 1) * MOE_TS) for i in halves]
        w1a, w3a, w2a, w1b, w3b, w2b = ([w_refs[6 * i + j] for i in halves] for j in range(6))
        x1 = [xs_ref[rows[i], 0:d] for i in halves]
        route = [xs_ref[rows[i], d:d + LANES] for i in halves]
        xn = [(x1[i] * lax.rsqrt(jnp.mean(x1[i] * x1[i], axis=-1, keepdims=True) + EPS)
               * nffn_ref[...]).astype(BF16) for i in halves]
        ga = [_dot(xn[i], w1a[i][0]) for i in halves]
        gb = [_dot(xn[i], w1b[i][0]) for i in halves]
        ua = [_dot(xn[i], w3a[i][0]) for i in halves]
        ub = [_dot(xn[i], w3b[i][0]) for i in halves]
        ha = [(_silu(ga[i]) * ua[i] * route[i][:, RT_WA:RT_WA + 1]).astype(BF16) for i in halves]
        hb = [(_silu(gb[i]) * ub[i] * route[i][:, RT_WB:RT_WB + 1]).astype(BF16) for i in halves]
        x2 = [x1[i] + _dot(ha[i], w2a[i][0]) + _dot(hb[i], w2b[i][0]) for i in halves]
        for i in halves:
            ys_ref[rows[i], :] = (x2[i] * lax.rsqrt(jnp.mean(x2[i] * x2[i], axis=-1, keepdims=True) + EPS)
                                  * nfin_ref[...])


def _ffn(tile_a, tile_b, n_valid, xs, w1, w3, w2, nffn, nfin):
    ns, dw = xs.shape
    d = dw - LANES
    de = w1.shape[2]
    row = lambda t, ta, tb, nv: (jnp.minimum(t, nv[0] - 1), 0)
    const = lambda t, ta, tb, nv: (0, 0)
    w_specs = []
    for i in range(2):
        wa = lambda t, ta, tb, nv, i=i: (ta[2 * t + i], 0, 0)
        wb = lambda t, ta, tb, nv, i=i: (tb[2 * t + i], 0, 0)
        w_specs += [pl.BlockSpec((1, d, de), wa), pl.BlockSpec((1, d, de), wa), pl.BlockSpec((1, de, d), wa),
                    pl.BlockSpec((1, d, de), wb), pl.BlockSpec((1, d, de), wb), pl.BlockSpec((1, de, d), wb)]
    return pl.pallas_call(
        _ffn_kernel,
        grid_spec=pltpu.PrefetchScalarGridSpec(
            num_scalar_prefetch=3,
            grid=(ns // (2 * MOE_TS),),
            in_specs=[pl.BlockSpec((2 * MOE_TS, dw), row)] + w_specs
            + [pl.BlockSpec((1, d), const), pl.BlockSpec((1, d), const)],
            out_specs=pl.BlockSpec((2 * MOE_TS, d), row),
        ),
        out_shape=jax.ShapeDtypeStruct((ns, d), F32),
        compiler_params=_cparams(("arbitrary",)),
        name="moe_ffn",
    )(tile_a, tile_b, n_valid, xs, *([w1, w3, w2, w1, w3, w2] * 2), nffn, nfin)


def _unsort_kernel(pos_hbm, ys_hbm, y_ref, pos_smem, sem_idx, sem_rows):
    i = pl.program_id(0)
    tm = y_ref.shape[0]
    rows = pos_smem.shape[0]
    idx_cp = pltpu.make_async_copy(pos_hbm.at[pl.ds(pl.multiple_of(i * rows, 8), rows)], pos_smem, sem_idx)
    idx_cp.start()
    idx_cp.wait()
    _row_copy_loops(tm, pos_smem, lambda t, p: pltpu.make_async_copy(
        ys_hbm.at[pl.ds(p, 1)], y_ref.at[pl.ds(t, 1)], sem_rows))


def _unsort(ys, pos):
    n = pos.shape[0]
    d = ys.shape[1]
    tm = _pick(n, (1024, 1152, 512, 384, 256, 128))
    any_spec = pl.BlockSpec(memory_space=pl.ANY)
    return pl.pallas_call(
        _unsort_kernel,
        grid=(n // tm,),
        in_specs=[any_spec, any_spec],
        out_specs=pl.BlockSpec((tm, d), lambda i: (i, 0)),
        out_shape=jax.ShapeDtypeStruct((n, d), F32),
        scratch_shapes=[pltpu.SMEM((_pos_rows(tm), LANES), jnp.int32), pltpu.SemaphoreType.DMA,
                        pltpu.SemaphoreType.DMA],
        compiler_params=_cparams(("arbitrary",)),
        name="moe_unsort",
    )(_tile_positions(pos, tm), ys)


def _pair_tables():
    a, b = [], []
    for g in range(N_GROUPS):
        for la in range(EXPERTS_PER_GROUP):
            for lb in range(la + 1, EXPERTS_PER_GROUP):
                a.append(g * EXPERTS_PER_GROUP + la)
                b.append(g * EXPERTS_PER_GROUP + lb)
    return jnp.asarray(a, jnp.int32), jnp.asarray(b, jnp.int32)


def _moe(xr_list, rt_list, cnt, w1, w3, w2, nffn, nfin):
    n_total = sum(x.shape[0] for x in xr_list)
    n_sorted = (-(-n_total // MOE_TS) + N_CLASSES + 2) // 2 * 2 * MOE_TS
    counts = cnt[0, :N_CLASSES].astype(jnp.int32)
    padded = (counts + MOE_TS - 1) // MOE_TS * MOE_TS
    ends = jnp.cumsum(padded)
    offs = ends - padded
    classes = jnp.arange(N_CLASSES, dtype=jnp.int32)

    def position(rt):
        cls = rt[:, RT_CLS].astype(jnp.int32)
        return jnp.sum(jnp.where(cls[:, None] == classes, offs, 0), axis=1) + rt[:, RT_RANK].astype(jnp.int32)

    pos_list = [position(rt) for rt in rt_list]
    tile_start = jnp.arange(n_sorted // MOE_TS, dtype=jnp.int32) * MOE_TS
    tile_cls = jnp.minimum(jnp.sum((ends <= tile_start[:, None]).astype(jnp.int32), axis=1), N_CLASSES - 1)
    pair_a, pair_b = _pair_tables()
    n_valid = ((ends[N_CLASSES - 1] // MOE_TS + 1) // 2).reshape(1)
    xs = None
    for xr, pos in zip(xr_list, pos_list):
        xs = _dispatch(ends, xr, pos, xs, n_sorted=n_sorted)
    tile_is = tile_cls[:, None] == classes
    tile_a = jnp.sum(jnp.where(tile_is, pair_a, 0), axis=1)
    tile_b = jnp.sum(jnp.where(tile_is, pair_b, 0), axis=1)
    ys = _ffn(tile_a, tile_b, n_valid, xs, w1, w3, w2, nffn, nfin)
    return [_unsort(ys, pos) for pos in pos_list]


def kernel(x_prompt, x_sample, cache_fox_k, cache_fox_v, cache_fox_logf, state_gdn, state_gdn_conv, meta_tokens, norm_mix_w, w_in, gdn_conv_w, gdn_A_log, gdn_dt_bias, gdn_norm_w, fox_q_norm_w, fox_k_norm_w, fox_f_bias, w_out, norm_ffn_w, w_router_group, b_router_group, w_router_expert, b_router_expert, w_gate, w_up, w_down, norm_final_w):
    B, S, D = x_prompt.shape
    BS, LS, _ = x_sample.shape
    C = GDN_CHUNK
    assert w_in.shape[0] == 1, "single-layer step only"
    assert S % C == 0 and LS <= C and N_META <= C and meta_tokens.shape[0] == N_META

    wi = w_in[0]
    o = 0
    parts = {}
    for name, size in (("g_qkv", 3 * WIDTH), ("g_z", WIDTH), ("g_a", HEADS), ("g_b", HEADS),
                       ("f_qkv", 3 * WIDTH), ("f_og", WIDTH), ("f_f", HEADS)):
        parts[name] = wi[:, o:o + size]
        o += size
    wbig = jnp.concatenate([parts[k].astype(BF16) for k in ("g_qkv", "f_qkv", "g_z", "f_og")], axis=1)
    wsm_cols = jnp.concatenate([parts["f_f"], parts["g_a"], parts["g_b"]], axis=1)
    wsm = jnp.pad(wsm_cols, ((0, 0), (0, LANES - 3 * HEADS))).astype(BF16)
    wo = w_out[0].astype(BF16)
    wr32 = jnp.pad(jnp.concatenate([w_router_group[0], w_router_expert[0]], axis=1),
                   ((0, 0), (0, LANES - N_GROUPS - N_EXPERTS)))
    wr_hi = wr32.astype(BF16)
    wr = jnp.concatenate([wr_hi, (wr32 - wr_hi.astype(F32)).astype(BF16)], axis=1)
    br = jnp.pad(jnp.concatenate([b_router_group[0], b_router_expert[0]]),
                 (0, LANES - N_GROUPS - N_EXPERTS)).reshape(1, LANES)
    w1 = w_gate[0].astype(BF16)
    w3 = w_up[0].astype(BF16)
    w2 = w_down[0].astype(BF16)
    nmix = norm_mix_w[0].reshape(1, D)
    nffn = norm_ffn_w[0].reshape(1, D)
    nfin = norm_final_w.reshape(1, D)

    x_small = jnp.concatenate([
        jnp.pad(meta_tokens.astype(F32), ((0, C - N_META), (0, 0))),
        jnp.pad(x_sample, ((0, 0), (0, C - LS), (0, 0))).reshape(BS * C, D)], axis=0)
    xp = x_prompt.reshape(B * S, D)

    p_s, psm_s, psmt_s = _inproj(x_small, nmix, wbig, wsm)
    p_p, psm_p, psmt_p = _inproj(xp, nmix, wbig, wsm)

    s0_s = jnp.concatenate([jnp.zeros((1,) + state_gdn.shape[2:], F32), state_gdn[0]], axis=0)
    cb_s = jnp.concatenate([jnp.zeros((1,) + state_gdn_conv.shape[2:], F32), state_gdn_conv[0]], axis=0)
    gdn_args = (gdn_conv_w[0], gdn_A_log[0], gdn_dt_bias[0], gdn_norm_w[0])
    og_s, st_s, cv_s = _gdn(p_s, psm_s, psmt_s, s0_s, cb_s, *gdn_args, n_seq=1 + BS, l_valid=LS,
                            bcast_state=False)
    og_p, st_p, cv_p = _gdn(p_p, psm_p, psmt_p, st_s[0:1], cv_s[0:1], *gdn_args, n_seq=B, l_valid=C,
                            bcast_state=True)

    fox_args = (fox_q_norm_w[0], fox_k_norm_w[0], fox_f_bias[0])
    zc = jnp.zeros((1, 1, LANES), F32)
    zr = jnp.zeros((1, HEADS, 1), F32)
    k32_s, v32_s, lfc_s, lfr_s, fc_s, fr_s, qt_s, kcat_s, vt_s, q_s, kb_s = _foxprep(
        p_s, psm_s, psmt_s, zc, zr, *fox_args, n_seq=1 + BS, bcast_carry=True, row_major=True)
    c0c = fc_s[N_META - 1:N_META, :].reshape(1, 1, LANES)
    c0r = fr_s[:, N_META - 1:N_META].reshape(1, HEADS, 1)
    k32_p, v32_p, lfc_p, lfr_p, fc_p, fr_p, qt_p, kcat_p, vt_p = _foxprep(
        p_p, psm_p, psmt_p, c0c, c0r, *fox_args, n_seq=B, bcast_carry=True, row_major=False, lead=N_META)
    of_p = _fox_prompt(qt_p, kcat_p, vt_p, p_p, kcat_s, vt_s, n_seq=B)
    of_s = _fox_sample(q_s, kb_s, p_s, lfc_s, lfr_s, cache_fox_k[0], cache_fox_v[0], cache_fox_logf[0],
                       l_valid=LS)

    xr_p, rt_p, cnt_p = _outproj(og_p, of_p, xp, wo, nffn, wr, br, jnp.zeros((1, LANES), F32))
    xr_s, rt_s, cnt = _outproj(og_s[C:], of_s, x_small[C:], wo, nffn, wr, br, cnt_p)
    y_p, y_s = _moe([xr_p, xr_s], [rt_p, rt_s], cnt, w1, w3, w2, nffn, nfin)

    def prompt_kv(tab_p, tab_s):
        tab = _fill_lead(tab_p, tab_s[:N_META * HEADS], n_seq=B)
        return tab.reshape(1, B, N_META + S, HEADS, HEAD_DIM)

    def sample_kv(tab_s):
        return tab_s.reshape(1 + BS, C, HEADS, HEAD_DIM)[1:, :LS][None]

    y_prompt = y_p.reshape(B, S, D)
    y_sample = y_s.reshape(BS, C, D)[:, :LS]
    fk_p = prompt_kv(k32_p, k32_s)
    fv_p = prompt_kv(v32_p, v32_s)
    lf_p = _assemble_logf(lfc_s, lfc_p, n_seq=B)[None]
    fk_s = sample_kv(k32_s)
    fv_s = sample_kv(v32_s)
    lf_s = lfc_s[C:, SM_F:SM_F + HEADS].reshape(BS, C, HEADS)[:, :LS][None]
    return (y_prompt, y_sample, fk_p, fv_p, lf_p, st_p[None], cv_p[None],
            fk_s, fv_s, lf_s, st_s[1:][None], cv_s[1:][None])
```

```python
import functools

import jax
import jax.numpy as jnp
from jax import lax
from jax.experimental import pallas as pl
from jax.experimental.pallas import tpu as pltpu

F32 = jnp.float32
BF16 = jnp.bfloat16
EPS = 1e-6
NEG = -1e30
LOG2E = 1.4426950408889634

N_META = 16
HEADS = 8
HEAD_DIM = 128
WIDTH = HEADS * HEAD_DIM
CONV_WIDTH = 4
N_GROUPS = 4
EXPERTS_PER_GROUP = 8
N_EXPERTS = N_GROUPS * EXPERTS_PER_GROUP
GDN_CHUNK = 128
LANES = 128
SM_F, SM_A, SM_B = 0, 8, 16
PB_FQ, PB_FK, PB_FV, PB_GZ, PB_FOG = 3, 4, 5, 6, 7
RT_E0 = N_GROUPS
PAIRS_PER_GROUP = EXPERTS_PER_GROUP * (EXPERTS_PER_GROUP - 1) // 2
N_CLASSES = N_GROUPS * PAIRS_PER_GROUP
RT_CLS, RT_RANK, RT_WA, RT_WB = 0, 1, 2, 3
MOE_TS = 256

VMEM_LIMIT = 56 * 1024 * 1024


def _cparams(sem):
    return pltpu.CompilerParams(dimension_semantics=sem, vmem_limit_bytes=VMEM_LIMIT)


def _pick(n, prefs):
    for p in prefs:
        if n % p == 0:
            return p
    raise ValueError(f"no tile in {prefs} divides {n}")


def _dot(a, b):
    return jnp.dot(a, b, preferred_element_type=F32)


def _dot_nt(a, b):
    return lax.dot_general(a, b, (((1,), (1,)), ((), ())), preferred_element_type=F32)


def _dot_tn(a, b):
    return lax.dot_general(a, b, (((0,), (0,)), ((), ())), preferred_element_type=F32)


def _split3(x):
    x1 = x.astype(BF16)
    r1 = x - x1.astype(F32)
    x2 = r1.astype(BF16)
    x3 = (r1 - x2.astype(F32)).astype(BF16)
    return x1, x2, x3


def _mask_dot(mask_bf16, x):
    x1, x2, x3 = _split3(x)
    return _dot(mask_bf16, x1) + _dot(mask_bf16, x2) + _dot(mask_bf16, x3)


def _dot_mask(x, mask_bf16):
    x1, x2, x3 = _split3(x)
    return _dot(x1, mask_bf16) + _dot(x2, mask_bf16) + _dot(x3, mask_bf16)


def _softplus(x):
    return jnp.maximum(x, 0.0) + jnp.log1p(jnp.exp(-jnp.abs(x)))


def _sigmoid(x):
    return 1.0 / (1.0 + jnp.exp(-x))


def _silu(x):
    return x * _sigmoid(x)


def _iota2(shape, dim):
    return lax.broadcasted_iota(jnp.int32, shape, dim)


def _inproj_kernel(x_ref, nw_ref, wbig_ref, wsm_ref, p_ref, psm_ref, psmt_ref, xn_scr):
    @pl.when(pl.program_id(1) == 0)
    def _():
        x = x_ref[...]
        xn = x * lax.rsqrt(jnp.mean(x * x, axis=-1, keepdims=True) + EPS) * nw_ref[...]
        xnb = xn.astype(BF16)
        xn_scr[...] = xnb
        psm = _dot(xnb, wsm_ref[...])
        psm_ref[...] = psm
        psmt_ref[...] = psm.T[0:psmt_ref.shape[0], :]

    p_ref[...] = _dot(xn_scr[...], wbig_ref[...]).astype(BF16)


def _inproj(x, norm_w, wbig, wsm):
    n, d = x.shape
    tm = _pick(n, (1024, 1152, 512, 384, 256, 128))
    tn = 2048
    nproj = wbig.shape[1]
    return pl.pallas_call(
        _inproj_kernel,
        grid=(n // tm, nproj // tn),
        in_specs=[
            pl.BlockSpec((tm, d), lambda i, j: (i, 0)),
            pl.BlockSpec((1, d), lambda i, j: (0, 0)),
            pl.BlockSpec((d, tn), lambda i, j: (0, j)),
            pl.BlockSpec((d, LANES), lambda i, j: (0, 0)),
        ],
        out_specs=[
            pl.BlockSpec((tm, tn), lambda i, j: (i, j)),
            pl.BlockSpec((tm, LANES), lambda i, j: (i, 0)),
            pl.BlockSpec((32, tm), lambda i, j: (0, i)),
        ],
        out_shape=[
            jax.ShapeDtypeStruct((n, nproj), BF16),
            jax.ShapeDtypeStruct((n, LANES), F32),
            jax.ShapeDtypeStruct((32, n), F32),
        ],
        scratch_shapes=[pltpu.VMEM((tm, d), BF16)],
        compiler_params=_cparams(("arbitrary", "arbitrary")),
        name="inproj",
    )(x, norm_w, wbig, wsm)


def _gdn_kernel(qkv_ref, z_ref, sm_ref, smt_ref, s0_ref, cb_ref, convw_ref, alog_c_ref, dtb_c_ref,
                alog_r_ref, dtb_r_ref, normw_ref, o_ref, snew_ref, cnew_ref, s_scr, xp_scr, conv_scr, *,
                l_valid):
    C = GDN_CHUNK
    n_sub = qkv_ref.shape[0] // C

    @pl.when(pl.program_id(1) == 0)
    def _():
        s_scr[...] = s0_ref[0]
        xp_scr[5:8, :] = cb_ref[0]

    row = _iota2((C, C), 0)
    col = _iota2((C, C), 1)
    incl = row >= col
    strict = row > col
    tril_b = jnp.where(incl, 1.0, 0.0).astype(BF16)
    triu_b = jnp.where(row <= col, 1.0, 0.0).astype(BF16)
    shift_b = jnp.concatenate(
        [jnp.where(col == row - i, 1.0, 0.0).astype(BF16) for i in (1, 2, 3)], axis=0)
    inv_levels = []
    sh = 1
    while (1 << sh) < C:
        inv_levels.append((((row >> sh) & 1) == 1) & ((col >> (sh + 1)) == (row >> (sh + 1)))
                          & (((col >> sh) & 1) == 0))
        sh += 1
    m1 = ((row & 1) == 1) & (col == row - 1)
    eye = jnp.where(row == col, 1.0, 0.0)
    w = convw_ref[...]
    normw = normw_ref[...]

    act, gl_col, gl_row, beta_col = [], [], [], []
    for sub in range(n_sub):
        rows = slice(sub * C, (sub + 1) * C)
        xb = qkv_ref[rows, :]
        x = xb.astype(F32)
        xp_scr[8:16, :] = x[0:8]
        shifted = _dot(shift_b, xb)
        conv_scr[rows, :] = (w[3:4] * x + w[2:3] * shifted[0:C] + w[1:2] * shifted[C:2 * C]
                             + w[0:1] * shifted[2 * C:3 * C])
        conv_scr[sub * C:sub * C + 8, :] = (w[0:1] * xp_scr[5:13, :] + w[1:2] * xp_scr[6:14, :]
                                            + w[2:3] * xp_scr[7:15, :] + w[3:4] * x[0:8])
        act.append(_silu(conv_scr[rows, :]))
        tail = x[l_valid - 3:l_valid, :]
        xp_scr[5:8, :] = tail
        cnew_ref[0] = tail

        sm = sm_ref[rows, :]
        g_col = -jnp.exp(alog_c_ref[...]) * _softplus(sm + dtb_c_ref[...])
        g_row = -jnp.exp(alog_r_ref[...]) * _softplus(smt_ref[:, rows] + dtb_r_ref[...])
        b_col = _sigmoid(sm)
        if l_valid < C:
            g_col = jnp.where(_iota2((C, LANES), 0) < l_valid, g_col, 0.0)
            b_col = jnp.where(_iota2((C, LANES), 0) < l_valid, b_col, 0.0)
            g_row = jnp.where(_iota2((HEADS, C), 1) < l_valid, g_row, 0.0)
        beta_col.append(b_col)
        gl_col.append(_mask_dot(tril_b, g_col))
        gl_row.append(_dot_mask(g_row, triu_b))

    hs = [slice(h * HEAD_DIM, (h + 1) * HEAD_DIM) for h in range(HEADS)]
    I = [(sub, h) for sub in range(n_sub) for h in range(HEADS)]
    gc = {(c, h): gl_col[c][:, SM_A + h:SM_A + h + 1] for c, h in I}
    beta = {(c, h): beta_col[c][:, SM_B + h:SM_B + h + 1] for c, h in I}
    qn, kn, kb, knb = {}, {}, {}, {}
    for c, h in I:
        q = act[c][:, hs[h]]
        k = act[c][:, WIDTH + h * HEAD_DIM:WIDTH + (h + 1) * HEAD_DIM]
        qn[c, h] = q * lax.rsqrt(jnp.sum(q * q, axis=-1, keepdims=True) + EPS) * (HEAD_DIM ** -0.5)
        kn[c, h] = k * lax.rsqrt(jnp.sum(k * k, axis=-1, keepdims=True) + EPS)
        kb[c, h] = kn[c, h] * beta[c, h]
        knb[c, h] = kn[c, h].astype(BF16)
    decay = {i: jnp.exp(jnp.where(incl, gc[i] - gl_row[i[0]][i[1]:i[1] + 1, :], NEG)) for i in I}
    a_mat = {i: jnp.where(strict, _dot_nt(kb[i].astype(BF16), knb[i]) * decay[i], 0.0) for i in I}
    qk = {i: (_dot_nt(qn[i].astype(BF16), knb[i]) * decay[i]).astype(BF16) for i in I}

    t = {i: eye - jnp.where(m1, a_mat[i], 0.0) for i in I}
    for m in inv_levels:
        tb = {i: t[i].astype(BF16) for i in I}
        y = {i: _dot(tb[i], jnp.where(m, a_mat[i], 0.0).astype(BF16)).astype(BF16) for i in I}
        t = {i: t[i] - _dot(y[i], tb[i]) for i in I}
    tb = {i: t[i].astype(BF16) for i in I}
    eg = {i: jnp.exp(gc[i]) for i in I}
    g_last = {i: gc[i][C - 1:C, :] for i in I}

    for c in range(n_sub):
        H = [(c, h) for h in range(HEADS)]
        s = {i: s_scr[i[1]] for i in H}
        sb = {i: s[i].astype(BF16) for i in H}
        r = {i: (act[c][:, 2 * WIDTH + i[1] * HEAD_DIM:2 * WIDTH + (i[1] + 1) * HEAD_DIM] * beta[i]
                 - _dot((kb[i] * eg[i]).astype(BF16), sb[i])).astype(BF16) for i in H}
        ub = {i: _dot(tb[i], r[i]).astype(BF16) for i in H}
        o = {i: _dot((qn[i] * eg[i]).astype(BF16), sb[i]) + _dot(qk[i], ub[i]) for i in H}
        for i in H:
            k_dec = (kn[i] * jnp.exp(g_last[i] - gc[i])).astype(BF16)
            s_scr[i[1]] = s[i] * jnp.exp(g_last[i]) + _dot_tn(k_dec, ub[i])
        for i in H:
            on = o[i] * lax.rsqrt(jnp.mean(o[i] * o[i], axis=-1, keepdims=True) + EPS) * normw
            z = z_ref[c * C:(c + 1) * C, hs[i[1]]].astype(F32)
            o_ref[c * C:(c + 1) * C, hs[i[1]]] = (on * _silu(z)).astype(BF16)

    snew_ref[0] = s_scr[...]


def _gdn(p, psm, psmt, s0, cb, conv_w, alog, dtb, norm_w, *, n_seq, l_valid, bcast_state):
    n = p.shape[0]
    C = GDN_CHUNK
    n_sub = _pick(n // (n_seq * C), (2, 1))
    R = n_sub * C
    nc = n // (n_seq * R)
    alog_c = jnp.zeros((1, LANES), F32).at[0, SM_A:SM_A + HEADS].set(alog)
    dtb_c = jnp.zeros((1, LANES), F32).at[0, SM_A:SM_A + HEADS].set(dtb)
    st_idx = (lambda b, c: (0, 0, 0, 0)) if bcast_state else (lambda b, c: (b, 0, 0, 0))
    cb_idx = (lambda b, c: (0, 0, 0)) if bcast_state else (lambda b, c: (b, 0, 0))
    return pl.pallas_call(
        functools.partial(_gdn_kernel, l_valid=l_valid),
        grid=(n_seq, nc),
        in_specs=[
            pl.BlockSpec((R, 3 * WIDTH), lambda b, c: (b * nc + c, 0)),
            pl.BlockSpec((R, WIDTH), lambda b, c: (b * nc + c, PB_GZ)),
            pl.BlockSpec((R, LANES), lambda b, c: (b * nc + c, 0)),
            pl.BlockSpec((HEADS, R), lambda b, c: (SM_A // HEADS, b * nc + c)),
            pl.BlockSpec((1, HEADS, HEAD_DIM, HEAD_DIM), st_idx),
            pl.BlockSpec((1, CONV_WIDTH - 1, 3 * WIDTH), cb_idx),
            pl.BlockSpec((CONV_WIDTH, 3 * WIDTH), lambda b, c: (0, 0)),
            pl.BlockSpec((1, LANES), lambda b, c: (0, 0)),
            pl.BlockSpec((1, LANES), lambda b, c: (0, 0)),
            pl.BlockSpec((HEADS, 1), lambda b, c: (0, 0)),
            pl.BlockSpec((HEADS, 1), lambda b, c: (0, 0)),
            pl.BlockSpec((1, HEAD_DIM), lambda b, c: (0, 0)),
        ],
        out_specs=[
            pl.BlockSpec((R, WIDTH), lambda b, c: (b * nc + c, 0)),
            pl.BlockSpec((1, HEADS, HEAD_DIM, HEAD_DIM), lambda b, c: (b, 0, 0, 0)),
            pl.BlockSpec((1, CONV_WIDTH - 1, 3 * WIDTH), lambda b, c: (b, 0, 0)),
        ],
        out_shape=[
            jax.ShapeDtypeStruct((n, WIDTH), BF16),
            jax.ShapeDtypeStruct((n_seq, HEADS, HEAD_DIM, HEAD_DIM), F32),
            jax.ShapeDtypeStruct((n_seq, CONV_WIDTH - 1, 3 * WIDTH), F32),
        ],
        scratch_shapes=[
            pltpu.VMEM((HEADS, HEAD_DIM, HEAD_DIM), F32),
            pltpu.VMEM((16, 3 * WIDTH), F32),
            pltpu.VMEM((R, 3 * WIDTH), F32),
        ],
        compiler_params=_cparams(("arbitrary", "arbitrary")),
        name="gdn",
    )(p, p, psm, psmt, s0, cb, conv_w, alog_c, dtb_c, alog.reshape(HEADS, 1), dtb.reshape(HEADS, 1),
      norm_w.reshape(1, HEAD_DIM))


def _foxprep_kernel(qkv_ref, sm_ref, smt_ref, c0c_ref, c0r_ref, qw_ref, kw_ref, fb_c_ref, fb_r_ref,
                    k32_ref, v32_ref, lfc_ref, lfr_ref, fc_ref, fr_ref, qt_ref, kcat_ref, vt_ref, *rest,
                    row_major):
    if row_major:
        q_ref, kb_ref, cc_scr, cr_scr = rest
    else:
        cc_scr, cr_scr = rest
    tm = qkv_ref.shape[0]

    @pl.when(pl.program_id(1) == 0)
    def _():
        cc_scr[...] = c0c_ref[0]
        cr_scr[...] = c0r_ref[0]

    lf_col = -_softplus(-(sm_ref[...] + fb_c_ref[...]))
    lf_row = -_softplus(-(smt_ref[...] + fb_r_ref[...]))
    lfc_ref[...] = lf_col
    lfr_ref[...] = lf_row
    row = _iota2((tm, tm), 0)
    col = _iota2((tm, tm), 1)
    tril_b = jnp.where(row >= col, 1.0, 0.0).astype(BF16)
    triu_b = jnp.where(row <= col, 1.0, 0.0).astype(BF16)
    f_col = cc_scr[...] + _mask_dot(tril_b, lf_col)
    f_row = cr_scr[...] + _dot_mask(lf_row, triu_b)
    fc_ref[...] = f_col
    fr_ref[...] = f_row
    cc_scr[...] = f_col[tm - 1:tm, :]
    cr_scr[...] = f_row[:, tm - 1:tm]

    qw = qw_ref[...]
    kw = kw_ref[...]
    lane = _iota2((tm, HEAD_DIM), 1)
    sub = _iota2((16, tm), 0)
    zeros_t = jnp.zeros((HEAD_DIM - 16, tm), BF16)
    for h in range(HEADS):
        hs = slice(h * HEAD_DIM, (h + 1) * HEAD_DIM)
        q = qkv_ref[:, hs].astype(F32)
        k = qkv_ref[:, WIDTH + h * HEAD_DIM:WIDTH + (h + 1) * HEAD_DIM].astype(F32)
        v = qkv_ref[:, 2 * WIDTH + h * HEAD_DIM:2 * WIDTH + (h + 1) * HEAD_DIM].astype(F32)
        qn = q * lax.rsqrt(jnp.mean(q * q, axis=-1, keepdims=True) + EPS) * qw
        kn = k * lax.rsqrt(jnp.mean(k * k, axis=-1, keepdims=True) + EPS) * kw
        k32_ref[pl.ds(h, tm, stride=HEADS), :] = kn
        v32_ref[pl.ds(h, tm, stride=HEADS), :] = v
        if row_major:
            q_ref[:, hs] = (qn * (HEAD_DIM ** -0.5)).astype(BF16)
            kb_ref[:, hs] = kn.astype(BF16)
        base = 2 * h * HEAD_DIM
        qt_ref[base:base + HEAD_DIM, :] = (qn * (LOG2E * HEAD_DIM ** -0.5)).T.astype(BF16)
        r1, r2, r3 = _split3(f_row[h:h + 1, :] * LOG2E)
        aug_q = jnp.where(sub == 0, r1.astype(F32), jnp.where(sub == 1, r2.astype(F32), jnp.where(
            sub == 2, r3.astype(F32), jnp.where(sub < 6, 1.0, 0.0))))
        qt_ref[base + HEAD_DIM:base + HEAD_DIM + 16, :] = aug_q.astype(BF16)
        qt_ref[base + HEAD_DIM + 16:base + 2 * HEAD_DIM, :] = zeros_t
        kcat_ref[:, base:base + HEAD_DIM] = kn.astype(BF16)
        c1, c2, c3 = _split3(f_col[:, SM_F + h:SM_F + h + 1] * LOG2E)
        aug_k = jnp.where(lane < 3, 1.0, jnp.where(lane == 3, -c1.astype(F32), jnp.where(
            lane == 4, -c2.astype(F32), jnp.where(lane == 5, -c3.astype(F32), 0.0))))
        kcat_ref[:, base + HEAD_DIM:base + 2 * HEAD_DIM] = aug_k.astype(BF16)
        vt_ref[hs, :] = v.T.astype(BF16)


def _foxprep(p, psm, psmt, c0c, c0r, q_norm_w, k_norm_w, f_bias, *, n_seq, bcast_carry, row_major, lead=0):
    n = p.shape[0]
    rows = n // n_seq
    tm = _pick(rows, (512, 256, 128))
    nt = rows // tm
    fb_c = jnp.zeros((1, LANES), F32).at[0, SM_F:SM_F + HEADS].set(f_bias)
    c_idx = (lambda b, t: (0, 0, 0)) if bcast_carry else (lambda b, t: (b, 0, 0))
    rowblk = lambda b, t: (b * nt + t, 0)
    colblk = lambda b, t: (0, b * nt + t)
    kv_spec = pl.BlockSpec((pl.Element(tm * HEADS), pl.Element(HEAD_DIM)),
                           lambda b, t: ((b * (rows + lead) + lead + t * tm) * HEADS, 0))
    kv_shape = jax.ShapeDtypeStruct((n_seq * (rows + lead) * HEADS, HEAD_DIM), F32)
    out_specs = [
        kv_spec,
        kv_spec,
        pl.BlockSpec((tm, LANES), rowblk),
        pl.BlockSpec((HEADS, tm), colblk),
        pl.BlockSpec((tm, LANES), rowblk),
        pl.BlockSpec((HEADS, tm), colblk),
        pl.BlockSpec((2 * WIDTH, tm), colblk),
        pl.BlockSpec((tm, 2 * WIDTH), rowblk),
        pl.BlockSpec((WIDTH, tm), colblk),
    ]
    out_shape = [
        kv_shape,
        kv_shape,
        jax.ShapeDtypeStruct((n, LANES), F32),
        jax.ShapeDtypeStruct((HEADS, n), F32),
        jax.ShapeDtypeStruct((n, LANES), F32),
        jax.ShapeDtypeStruct((HEADS, n), F32),
        jax.ShapeDtypeStruct((2 * WIDTH, n), BF16),
        jax.ShapeDtypeStruct((n, 2 * WIDTH), BF16),
        jax.ShapeDtypeStruct((WIDTH, n), BF16),
    ]
    if row_major:
        out_specs += [pl.BlockSpec((tm, WIDTH), rowblk), pl.BlockSpec((tm, WIDTH), rowblk)]
        out_shape += [jax.ShapeDtypeStruct((n, WIDTH), BF16),
                      jax.ShapeDtypeStruct((n, WIDTH), BF16)]
    return pl.pallas_call(
        functools.partial(_foxprep_kernel, row_major=row_major),
        grid=(n_seq, nt),
        in_specs=[
            pl.BlockSpec((tm, 3 * WIDTH), lambda b, t: (b * nt + t, 1)),
            pl.BlockSpec((tm, LANES), rowblk),
            pl.BlockSpec((HEADS, tm), lambda b, t: (SM_F // HEADS, b * nt + t)),
            pl.BlockSpec((1, 1, LANES), c_idx),
            pl.BlockSpec((1, HEADS, 1), c_idx),
            pl.BlockSpec((1, HEAD_DIM), lambda b, t: (0, 0)),
            pl.BlockSpec((1, HEAD_DIM), lambda b, t: (0, 0)),
            pl.BlockSpec((1, LANES), lambda b, t: (0, 0)),
            pl.BlockSpec((HEADS, 1), lambda b, t: (0, 0)),
        ],
        out_specs=out_specs,
        out_shape=out_shape,
        scratch_shapes=[pltpu.VMEM((1, LANES), F32), pltpu.VMEM((HEADS, 1), F32)],
        compiler_params=_cparams(("arbitrary", "arbitrary")),
        name="foxprep",
    )(p, psm, psmt, c0c, c0r, q_norm_w.reshape(1, HEAD_DIM), k_norm_w.reshape(1, HEAD_DIM), fb_c,
      f_bias.reshape(HEADS, 1))


def _fill_lead_kernel(src_ref, big_ref, o_ref):
    o_ref[...] = src_ref[...]


def _fill_lead(big, src, *, n_seq):
    r = src.shape[0]
    seq_rows = big.shape[0] // n_seq
    assert seq_rows % r == 0
    return pl.pallas_call(
        _fill_lead_kernel,
        grid=(n_seq,),
        in_specs=[pl.BlockSpec((r, HEAD_DIM), lambda b: (0, 0)), pl.BlockSpec(memory_space=pl.ANY)],
        out_specs=pl.BlockSpec((r, HEAD_DIM), lambda b: (b * (seq_rows // r), 0)),
        out_shape=jax.ShapeDtypeStruct(big.shape, big.dtype),
        input_output_aliases={1: 0},
        compiler_params=_cparams(("arbitrary",)),
        name="fill_lead",
    )(src, big)


def _logf_kernel(meta_ref, lf_ref, o_ref):
    o_ref[0, 0:N_META, :] = meta_ref[0:N_META, SM_F:SM_F + HEADS]
    o_ref[0, N_META:, :] = lf_ref[:, SM_F:SM_F + HEADS]


def _assemble_logf(lf_small, lf_prompt, *, n_seq):
    s = lf_prompt.shape[0] // n_seq
    return pl.pallas_call(
        _logf_kernel,
        grid=(n_seq,),
        in_specs=[pl.BlockSpec((GDN_CHUNK, LANES), lambda b: (0, 0)), pl.BlockSpec((s, LANES), lambda b: (b, 0))],
        out_specs=pl.BlockSpec((1, N_META + s, HEADS), lambda b: (b, 0, 0)),
        out_shape=jax.ShapeDtypeStruct((n_seq, N_META + s, HEADS), F32),
        compiler_params=_cparams(("arbitrary",)),
        name="assemble_logf",
    )(lf_small, lf_prompt)


FOX_UNIT = 256
FOX_SUM_ROWS = 16

def _fox_kernel(qi_ref, ki_ref, qt_ref, kcat_ref, vt_ref, og_ref, km_ref, vtm_ref, o_ref, m_scr, acc_scr):
    pair = pl.program_id(1)
    qi = qi_ref[pair]
    ki = ki_ref[pair]
    tq = qt_ref.shape[1]
    tk = kcat_ref.shape[0]
    U = FOX_UNIT
    H = range(HEADS)

    def unit(qh, k_rows, vt_cols, mask):
        qs = slice(qh * U, (qh + 1) * U)
        s = [_dot(k_rows(h), qt_ref[2 * h * HEAD_DIM:2 * (h + 1) * HEAD_DIM, qs]) for h in H]
        if mask is not None:
            s = [jnp.where(mask, s[h], NEG) for h in H]
        m_old = [m_scr[h:h + 1, qs] for h in H]
        m_new = [jnp.maximum(m_old[h], jnp.max(s[h], axis=0, keepdims=True)) for h in H]
        alpha = [jnp.exp2(m_old[h] - m_new[h]) for h in H]
        p = [jnp.exp2((s[h] - m_new[h]).astype(BF16)) for h in H]
        for h in H:
            m_scr[h:h + 1, qs] = m_new[h]
        for h in H:
            lhs = jnp.concatenate([vt_cols(h), jnp.ones((FOX_SUM_ROWS, p[h].shape[0]), BF16)], axis=0)
            acc_scr[h, :, qs] = alpha[h] * acc_scr[h, :, qs] + _dot(lhs, p[h])

    def k_unit(r0):
        return lambda h: kcat_ref[r0:r0 + U, 2 * h * HEAD_DIM:2 * (h + 1) * HEAD_DIM]

    def vt_unit(r0):
        return lambda h: vt_ref[h * HEAD_DIM:(h + 1) * HEAD_DIM, r0:r0 + U]

    @pl.when(ki == 0)
    def _():
        m_scr[...] = jnp.full(m_scr.shape, NEG, F32)
        acc_scr[...] = jnp.zeros(acc_scr.shape, F32)
        tmeta = km_ref.shape[0]
        mask = _iota2((tmeta, U), 0) < N_META
        for qh in range(tq // U):
            unit(qh, lambda h: km_ref[:, 2 * h * HEAD_DIM:2 * (h + 1) * HEAD_DIM],
                 lambda h: vtm_ref[h * HEAD_DIM:(h + 1) * HEAD_DIM, :], mask)

    kb = tk // tq
    for j in range(kb):
        g = ki * kb + j

        @pl.when(g < qi)
        def _(j=j):
            for qh in range(tq // U):
                for ku in range(tq // U):
                    unit(qh, k_unit(j * tq + ku * U), vt_unit(j * tq + ku * U), None)

        @pl.when(g == qi)
        def _(j=j):
            diag = _iota2((U, U), 0) <= _iota2((U, U), 1)
            for qh in range(tq // U):
                for ku in range(qh + 1):
                    unit(qh, k_unit(j * tq + ku * U), vt_unit(j * tq + ku * U), diag if ku == qh else None)

    @pl.when(ki == qi // kb)
    def _():
        for h in H:
            hs = slice(h * HEAD_DIM, (h + 1) * HEAD_DIM)
            gate = _sigmoid(og_ref[:, hs].astype(F32))
            o_ref[:, hs] = ((acc_scr[h, 0:HEAD_DIM, :] / acc_scr[h, HEAD_DIM:HEAD_DIM + 1, :]).T * gate).astype(BF16)


def _fox_prompt(qt, kcat, vt, p, kcat_small, vt_small, *, n_seq):
    n = p.shape[0]
    rows = n // n_seq
    tq = _pick(rows, (512, 256))
    nq = rows // tq
    kb = _pick(nq, (2, 1))
    tk, nk = kb * tq, nq // kb
    pairs = [(i, j) for i in range(nq) for j in range(i // kb + 1)]
    qi = jnp.asarray([a for a, _ in pairs], jnp.int32)
    ki = jnp.asarray([b for _, b in pairs], jnp.int32)
    grid_spec = pltpu.PrefetchScalarGridSpec(
        num_scalar_prefetch=2,
        grid=(n_seq, len(pairs)),
        in_specs=[
            pl.BlockSpec((2 * WIDTH, tq), lambda b, t, qi_r, ki_r: (0, b * nq + qi_r[t])),
            pl.BlockSpec((tk, 2 * WIDTH), lambda b, t, qi_r, ki_r: (b * nk + ki_r[t], 0)),
            pl.BlockSpec((WIDTH, tk), lambda b, t, qi_r, ki_r: (0, b * nk + ki_r[t])),
            pl.BlockSpec((tq, WIDTH), lambda b, t, qi_r, ki_r: (b * nq + qi_r[t], PB_FOG)),
            pl.BlockSpec((GDN_CHUNK, 2 * WIDTH), lambda b, t, qi_r, ki_r: (0, 0)),
            pl.BlockSpec((WIDTH, GDN_CHUNK), lambda b, t, qi_r, ki_r: (0, 0)),
        ],
        out_specs=pl.BlockSpec((tq, WIDTH), lambda b, t, qi_r, ki_r: (b * nq + qi_r[t], 0)),
        scratch_shapes=[
            pltpu.VMEM((HEADS, tq), F32),
            pltpu.VMEM((HEADS, HEAD_DIM + FOX_SUM_ROWS, tq), F32),
        ],
    )
    return pl.pallas_call(
        _fox_kernel,
        grid_spec=grid_spec,
        out_shape=jax.ShapeDtypeStruct((n, WIDTH), BF16),
        compiler_params=_cparams(("arbitrary", "arbitrary")),
        name="fox_prompt",
    )(qi, ki, qt, kcat, vt, p, kcat_small, vt_small)


def _fox_sample_kernel(q_ref, kn_ref, vn_ref, og_ref, lfc_ref, lfr_ref, ck_ref, cv_ref, clc_ref, clr_ref,
                       o_ref, *, l_valid):
    C = q_ref.shape[0]
    P = ck_ref.shape[1] // HEADS
    rowp = _iota2((P, P), 0)
    colp = _iota2((P, P), 1)
    triu_p = jnp.where(rowp <= colp, 1.0, 0.0).astype(BF16)
    row = _iota2((C, C), 0)
    col = _iota2((C, C), 1)
    tril_b = jnp.where(row >= col, 1.0, 0.0).astype(BF16)
    triu_b = jnp.where(row <= col, 1.0, 0.0).astype(BF16)

    f_cache = _dot_mask(clr_ref[0], triu_p)
    carry_r = f_cache[:, P - 1:P]
    carry_c = jnp.sum(clc_ref[0], axis=0, keepdims=True)
    f_new_r = carry_r + _dot_mask(lfr_ref[...], triu_b)
    f_new_c = carry_c + _mask_dot(tril_b, lfc_ref[...])[:, SM_F:SM_F + HEADS]
    mask_new = (col <= row) & (col < l_valid)

    for h in range(HEADS):
        hs = slice(h * HEAD_DIM, (h + 1) * HEAD_DIM)
        q = q_ref[:, hs]
        fq = f_new_c[:, h:h + 1]
        ck = ck_ref[0, pl.ds(h, P, stride=HEADS), :].astype(BF16)
        cv = cv_ref[0, pl.ds(h, P, stride=HEADS), :].astype(BF16)
        s_c = _dot_nt(q, ck) + (fq - f_cache[h:h + 1, :])
        s_n = jnp.where(mask_new, _dot_nt(q, kn_ref[:, hs]) + (fq - f_new_r[h:h + 1, :]), NEG)
        m = jnp.maximum(jnp.max(s_c, axis=-1, keepdims=True), jnp.max(s_n, axis=-1, keepdims=True))
        p_c = jnp.exp(s_c - m)
        p_n = jnp.exp(s_n - m)
        l = jnp.sum(p_c, axis=-1, keepdims=True) + jnp.sum(p_n, axis=-1, keepdims=True)
        o = _dot(p_c.astype(BF16), cv) + _dot(p_n.astype(BF16), vn_ref[:, hs])
        gate = _sigmoid(og_ref[:, hs].astype(F32))
        o_ref[:, hs] = (o / l * gate).astype(BF16)


def _fox_sample(q, kb, p, lf_col, lf_row, cache_k, cache_v, cache_lf, *, l_valid):
    bs, past = cache_k.shape[0], cache_k.shape[1]
    C = GDN_CHUNK
    n = q.shape[0]
    blk = lambda b: (b + 1, 0)
    return pl.pallas_call(
        functools.partial(_fox_sample_kernel, l_valid=l_valid),
        grid=(bs,),
        in_specs=[
            pl.BlockSpec((C, WIDTH), blk),
            pl.BlockSpec((C, WIDTH), blk),
            pl.BlockSpec((C, WIDTH), lambda b: (b + 1, PB_FV)),
            pl.BlockSpec((C, WIDTH), lambda b: (b + 1, PB_FOG)),
            pl.BlockSpec((C, LANES), blk),
            pl.BlockSpec((HEADS, C), lambda b: (0, b + 1)),
            pl.BlockSpec((1, past * HEADS, HEAD_DIM), lambda b: (b, 0, 0)),
            pl.BlockSpec((1, past * HEADS, HEAD_DIM), lambda b: (b, 0, 0)),
            pl.BlockSpec((1, past, HEADS), lambda b: (b, 0, 0)),
            pl.BlockSpec((1, HEADS, past), lambda b: (b, 0, 0)),
        ],
        out_specs=pl.BlockSpec((C, WIDTH), lambda b: (b, 0)),
        out_shape=jax.ShapeDtypeStruct((bs * C, WIDTH), BF16),
        compiler_params=_cparams(("arbitrary",)),
        name="fox_sample",
    )(q, kb, p, p, lf_col, lf_row, cache_k.reshape(bs, past * HEADS, HEAD_DIM),
      cache_v.reshape(bs, past * HEADS, HEAD_DIM), cache_lf, jnp.swapaxes(cache_lf, 1, 2))


def _outproj_kernel(og_ref, of_ref, x_ref, wo_ref, nw_ref, wr_ref, br_ref, cnt0_ref, xr_ref, rt_ref, cnt_ref,
                    cnt_scr):
    d = x_ref.shape[1]

    @pl.when(pl.program_id(0) == 0)
    def _():
        cnt_scr[...] = cnt0_ref[...]

    half = og_ref.shape[1]
    h = _dot(og_ref[...], wo_ref[0:half, :]) + _dot(of_ref[...], wo_ref[half:2 * half, :])
    x1 = x_ref[...] + h
    xr_ref[:, 0:d] = x1
    xn = x1 * lax.rsqrt(jnp.mean(x1 * x1, axis=-1, keepdims=True) + EPS) * nw_ref[...]

    x_hi = xn.astype(BF16)
    x_lo = (xn - x_hi.astype(F32)).astype(BF16)
    hi = _dot(x_hi, wr_ref[...])
    logits = (hi[:, 0:LANES] + hi[:, LANES:2 * LANES] + _dot(x_lo, wr_ref[:, 0:LANES])) + br_ref[...]

    tm = logits.shape[0]
    lane = _iota2((tm, LANES), 1).astype(F32)
    big = float(LANES)
    gl = jnp.where(lane < N_GROUPS, logits, NEG)
    gmax = jnp.max(gl, axis=-1, keepdims=True)
    gidx = jnp.min(jnp.where(gl == gmax, lane, big), axis=-1, keepdims=True)
    p_top = 1.0 / jnp.sum(jnp.exp(gl - gmax), axis=-1, keepdims=True)
    e = lane - RT_E0
    sel = (e >= 0) & (e < N_EXPERTS) & (jnp.floor(e * (1.0 / EXPERTS_PER_GROUP)) == gidx)
    el = jnp.where(sel, logits, NEG)
    v1 = jnp.max(el, axis=-1, keepdims=True)
    i1 = jnp.min(jnp.where(el == v1, lane, big), axis=-1, keepdims=True)
    el2 = jnp.where(lane == i1, NEG, el)
    v2 = jnp.max(el2, axis=-1, keepdims=True)
    i2 = jnp.min(jnp.where(el2 == v2, lane, big), axis=-1, keepdims=True)
    e2 = jnp.exp(v2 - v1)
    w1 = p_top / (1.0 + e2)
    w2 = p_top * e2 / (1.0 + e2)

    ex1 = i1 - RT_E0
    ex2 = i2 - RT_E0
    first = ex1 < ex2
    ea = jnp.where(first, ex1, ex2)
    eb = jnp.where(first, ex2, ex1)
    wa = jnp.where(first, w1, w2)
    wb = jnp.where(first, w2, w1)
    la = ea - gidx * EXPERTS_PER_GROUP
    lb = eb - gidx * EXPERTS_PER_GROUP
    cls = gidx * PAIRS_PER_GROUP + la * (2 * EXPERTS_PER_GROUP - 1 - la) * 0.5 + (lb - la - 1.0)

    onehot = lane == cls
    oh = jnp.where(onehot, 1.0, 0.0)
    strict_b = jnp.where(_iota2((tm, tm), 0) > _iota2((tm, tm), 1), 1.0, 0.0).astype(BF16)
    before = cnt_scr[...] + _dot(strict_b, oh.astype(BF16))
    rank = jnp.sum(jnp.where(onehot, before, 0.0), axis=-1, keepdims=True)
    cnt_scr[...] += jnp.sum(oh, axis=0, keepdims=True)
    cnt_ref[...] = cnt_scr[...]
    route = jnp.where(lane == RT_CLS, cls, jnp.where(lane == RT_RANK, rank, jnp.where(
        lane == RT_WA, wa, jnp.where(lane == RT_WB, wb, 0.0))))
    rt_ref[...] = route
    xr_ref[:, d:d + LANES] = route


def _outproj(og, of, x, wo, norm_w, wr, br, cnt0):
    n, d = x.shape
    tm = _pick(n, (512, 384, 256, 128))
    rowblk = lambda i: (i, 0)
    return pl.pallas_call(
        _outproj_kernel,
        grid=(n // tm,),
        in_specs=[
            pl.BlockSpec((tm, WIDTH), rowblk),
            pl.BlockSpec((tm, WIDTH), rowblk),
            pl.BlockSpec((tm, d), rowblk),
            pl.BlockSpec((2 * WIDTH, d), lambda i: (0, 0)),
            pl.BlockSpec((1, d), lambda i: (0, 0)),
            pl.BlockSpec((d, 2 * LANES), lambda i: (0, 0)),
            pl.BlockSpec((1, LANES), lambda i: (0, 0)),
            pl.BlockSpec((1, LANES), lambda i: (0, 0)),
        ],
        out_specs=[
            pl.BlockSpec((tm, d + LANES), rowblk),
            pl.BlockSpec((tm, LANES), rowblk),
            pl.BlockSpec((1, LANES), lambda i: (0, 0)),
        ],
        out_shape=[
            jax.ShapeDtypeStruct((n, d + LANES), F32),
            jax.ShapeDtypeStruct((n, LANES), F32),
            jax.ShapeDtypeStruct((1, LANES), F32),
        ],
        scratch_shapes=[pltpu.VMEM((1, LANES), F32)],
        compiler_params=_cparams(("arbitrary",)),
        name="outproj_router",
    )(og, of, x, wo, norm_w, wr, br, cnt0)


def _pos_rows(tm):
    return -(-(-(-tm // LANES)) // 8) * 8


def _tile_positions(pos, tm):
    nt = pos.shape[0] // tm
    rows = _pos_rows(tm)
    p = jnp.pad(pos.reshape(nt, tm), ((0, 0), (0, rows * LANES - tm)))
    return p.reshape(nt * rows, LANES)


def _row_copy_loops(tm, pos_smem, make_copy):
    def start_row(r, carry):
        base = pl.multiple_of(r * LANES, LANES)
        for c in range(LANES):
            make_copy(base + c, pos_smem[r, c]).start()
        return carry

    def wait(t, carry):
        make_copy(0, 0).wait()
        return carry

    lax.fori_loop(0, tm // LANES, start_row, 0)
    lax.fori_loop(0, tm, wait, 0, unroll=8)


def _dispatch_kernel(ends_ref, xr_ref, pos_hbm, *rest, zero_fill):
    if zero_fill:
        xs_hbm, pos_smem, zbuf, sem_idx, sem_fill, sem_rows = rest
    else:
        _, xs_hbm, pos_smem, sem_idx, sem_rows = rest
    i = pl.program_id(0)
    tm = xr_ref.shape[0]
    rows = pos_smem.shape[0]
    idx_cp = pltpu.make_async_copy(pos_hbm.at[pl.ds(pl.multiple_of(i * rows, 8), rows)], pos_smem, sem_idx)
    idx_cp.start()

    if zero_fill:
        @pl.when(i == 0)
        def _():
            zbuf[...] = jnp.zeros(zbuf.shape, F32)

            def fill(start):
                def body(c, carry):
                    lo = jnp.where(c == 0, 0, ends_ref[jnp.maximum(c - 1, 0)])
                    hi = ends_ref[c]

                    @pl.when(hi > lo)
                    def _():
                        cp = pltpu.make_async_copy(
                            zbuf, xs_hbm.at[pl.ds(pl.multiple_of(hi - MOE_TS, MOE_TS), MOE_TS)], sem_fill)
                        if start:
                            cp.start()
                        else:
                            cp.wait()
                    return carry
                lax.fori_loop(0, N_CLASSES, body, 0)

            fill(True)
            fill(False)

            last = ends_ref[N_CLASSES - 1]

            @pl.when((last // MOE_TS) % 2 == 1)
            def _():
                cp = pltpu.make_async_copy(zbuf, xs_hbm.at[pl.ds(pl.multiple_of(last, MOE_TS), MOE_TS)], sem_fill)
                cp.start()
                cp.wait()

    idx_cp.wait()
    _row_copy_loops(tm, pos_smem, lambda t, p: pltpu.make_async_copy(
        xr_ref.at[pl.ds(t, 1)], xs_hbm.at[pl.ds(p, 1)], sem_rows))


def _dispatch(ends, xr, pos, xs, *, n_sorted):
    n, dw = xr.shape
    tm = _pick(n, (1024, 1152, 512, 384, 256, 128))
    rows = _pos_rows(tm)
    zero_fill = xs is None
    any_spec = pl.BlockSpec(memory_space=pl.ANY)
    in_specs = [pl.BlockSpec((tm, dw), lambda i, ends_r: (i, 0)), any_spec]
    args = [xr, _tile_positions(pos, tm)]
    scratch = [pltpu.SMEM((rows, LANES), jnp.int32)]
    if zero_fill:
        scratch += [pltpu.VMEM((MOE_TS, dw), F32), pltpu.SemaphoreType.DMA, pltpu.SemaphoreType.DMA,
                    pltpu.SemaphoreType.DMA]
        aliases = {}
    else:
        in_specs.append(any_spec)
        args.append(xs)
        scratch += [pltpu.SemaphoreType.DMA, pltpu.SemaphoreType.DMA]
        aliases = {3: 0}
    return pl.pallas_call(
        functools.partial(_dispatch_kernel, zero_fill=zero_fill),
        grid_spec=pltpu.PrefetchScalarGridSpec(
            num_scalar_prefetch=1, grid=(n // tm,), in_specs=in_specs, out_specs=any_spec,
            scratch_shapes=scratch),
        out_shape=jax.ShapeDtypeStruct((n_sorted, dw), F32),
        input_output_aliases=aliases,
        compiler_params=pltpu.CompilerParams(dimension_semantics=("arbitrary",), vmem_limit_bytes=VMEM_LIMIT,
                                             has_side_effects=True),
        name="moe_dispatch",
    )(ends, *args)


def _ffn_kernel(ta_ref, tb_ref, nv_ref, xs_ref, *refs):
    w_refs, (nffn_ref, nfin_ref, ys_ref) = refs[:12], refs[12:]

    @pl.when(pl.program_id(0) < nv_ref[0])
    def _():
        d = ys_ref.shape[1]
        halves = range(2)
        rows = [slice(i * MOE_TS, (i + 1) * MOE_TS) for i in halves]
        w1a, w3a, w2a, w1b, w3b, w2b = ([w_refs[6 * i + j] for i in halves] for j in range(6))
        x1 = [xs_ref[rows[i], 0:d] for i in halves]
        route = [xs_ref[rows[i], d:d + LANES] for i in halves]
        xn = [(x1[i] * lax.rsqrt(jnp.mean(x1[i] * x1[i], axis=-1, keepdims=True) + EPS)
               * nffn_ref[...]).astype(BF16) for i in halves]
        ga = [_dot(xn[i], w1a[i][0]) for i in halves]
        gb = [_dot(xn[i], w1b[i][0]) for i in halves]
        ua = [_dot(xn[i], w3a[i][0]) for i in halves]
        ub = [_dot(xn[i], w3b[i][0]) for i in halves]
        ha = [(_silu(ga[i]) * ua[i] * route[i][:, RT_WA:RT_WA + 1]).astype(BF16) for i in halves]
        hb = [(_silu(gb[i]) * ub[i] * route[i][:, RT_WB:RT_WB + 1]).astype(BF16) for i in halves]
        x2 = [x1[i] + _dot(ha[i], w2a[i][0]) + _dot(hb[i], w2b[i][0]) for i in halves]
        for i in halves:
            ys_ref[rows[i], :] = (x2[i] * lax.rsqrt(jnp.mean(x2[i] * x2[i], axis=-1, keepdims=True) + EPS)
                                  * nfin_ref[...])


def _ffn(tile_a, tile_b, n_valid, xs, w1, w3, w2, nffn, nfin):
    ns, dw = xs.shape
    d = dw - LANES
    de = w1.shape[2]
    row = lambda t, ta, tb, nv: (jnp.minimum(t, nv[0] - 1), 0)
    const = lambda t, ta, tb, nv: (0, 0)
    w_specs = []
    for i in range(2):
        wa = lambda t, ta, tb, nv, i=i: (ta[2 * t + i], 0, 0)
        wb = lambda t, ta, tb, nv, i=i: (tb[2 * t + i], 0, 0)
        w_specs += [pl.BlockSpec((1, d, de), wa), pl.BlockSpec((1, d, de), wa), pl.BlockSpec((1, de, d), wa),
                    pl.BlockSpec((1, d, de), wb), pl.BlockSpec((1, d, de), wb), pl.BlockSpec((1, de, d), wb)]
    return pl.pallas_call(
        _ffn_kernel,
        grid_spec=pltpu.PrefetchScalarGridSpec(
            num_scalar_prefetch=3,
            grid=(ns // (2 * MOE_TS),),
            in_specs=[pl.BlockSpec((2 * MOE_TS, dw), row)] + w_specs
            + [pl.BlockSpec((1, d), const), pl.BlockSpec((1, d), const)],
            out_specs=pl.BlockSpec((2 * MOE_TS, d), row),
        ),
        out_shape=jax.ShapeDtypeStruct((ns, d), F32),
        compiler_params=_cparams(("arbitrary",)),
        name="moe_ffn",
    )(tile_a, tile_b, n_valid, xs, *([w1, w3, w2, w1, w3, w2] * 2), nffn, nfin)


def _unsort_kernel(pos_hbm, ys_hbm, y_ref, pos_smem, sem_idx, sem_rows):
    i = pl.program_id(0)
    tm = y_ref.shape[0]
    rows = pos_smem.shape[0]
    idx_cp = pltpu.make_async_copy(pos_hbm.at[pl.ds(pl.multiple_of(i * rows, 8), rows)], pos_smem, sem_idx)
    idx_cp.start()
    idx_cp.wait()
    _row_copy_loops(tm, pos_smem, lambda t, p: pltpu.make_async_copy(
        ys_hbm.at[pl.ds(p, 1)], y_ref.at[pl.ds(t, 1)], sem_rows))


def _unsort(ys, pos):
    n = pos.shape[0]
    d = ys.shape[1]
    tm = _pick(n, (1024, 1152, 512, 384, 256, 128))
    any_spec = pl.BlockSpec(memory_space=pl.ANY)
    return pl.pallas_call(
        _unsort_kernel,
        grid=(n // tm,),
        in_specs=[any_spec, any_spec],
        out_specs=pl.BlockSpec((tm, d), lambda i: (i, 0)),
        out_shape=jax.ShapeDtypeStruct((n, d), F32),
        scratch_shapes=[pltpu.SMEM((_pos_rows(tm), LANES), jnp.int32), pltpu.SemaphoreType.DMA,
                        pltpu.SemaphoreType.DMA],
        compiler_params=_cparams(("arbitrary",)),
        name="moe_unsort",
    )(_tile_positions(pos, tm), ys)


def _pair_tables():
    a, b = [], []
    for g in range(N_GROUPS):
        for la in range(EXPERTS_PER_GROUP):
            for lb in range(la + 1, EXPERTS_PER_GROUP):
                a.append(g * EXPERTS_PER_GROUP + la)
                b.append(g * EXPERTS_PER_GROUP + lb)
    return jnp.asarray(a, jnp.int32), jnp.asarray(b, jnp.int32)


def _moe(xr_list, rt_list, cnt, w1, w3, w2, nffn, nfin):
    n_total = sum(x.shape[0] for x in xr_list)
    n_sorted = (-(-n_total // MOE_TS) + N_CLASSES + 2) // 2 * 2 * MOE_TS
    counts = cnt[0, :N_CLASSES].astype(jnp.int32)
    padded = (counts + MOE_TS - 1) // MOE_TS * MOE_TS
    ends = jnp.cumsum(padded)
    offs = ends - padded
    classes = jnp.arange(N_CLASSES, dtype=jnp.int32)

    def position(rt):
        cls = rt[:, RT_CLS].astype(jnp.int32)
        return jnp.sum(jnp.where(cls[:, None] == classes, offs, 0), axis=1) + rt[:, RT_RANK].astype(jnp.int32)

    pos_list = [position(rt) for rt in rt_list]
    tile_start = jnp.arange(n_sorted // MOE_TS, dtype=jnp.int32) * MOE_TS
    tile_cls = jnp.minimum(jnp.sum((ends <= tile_start[:, None]).astype(jnp.int32), axis=1), N_CLASSES - 1)
    pair_a, pair_b = _pair_tables()
    n_valid = ((ends[N_CLASSES - 1] // MOE_TS + 1) // 2).reshape(1)
    xs = None
    for xr, pos in zip(xr_list, pos_list):
        xs = _dispatch(ends, xr, pos, xs, n_sorted=n_sorted)
    tile_is = tile_cls[:, None] == classes
    tile_a = jnp.sum(jnp.where(tile_is, pair_a, 0), axis=1)
    tile_b = jnp.sum(jnp.where(tile_is, pair_b, 0), axis=1)
    ys = _ffn(tile_a, tile_b, n_valid, xs, w1, w3, w2, nffn, nfin)
    return [_unsort(ys, pos) for pos in pos_list]


def kernel(x_prompt, x_sample, cache_fox_k, cache_fox_v, cache_fox_logf, state_gdn, state_gdn_conv, meta_tokens, norm_mix_w, w_in, gdn_conv_w, gdn_A_log, gdn_dt_bias, gdn_norm_w, fox_q_norm_w, fox_k_norm_w, fox_f_bias, w_out, norm_ffn_w, w_router_group, b_router_group, w_router_expert, b_router_expert, w_gate, w_up, w_down, norm_final_w):
    B, S, D = x_prompt.shape
    BS, LS, _ = x_sample.shape
    C = GDN_CHUNK
    assert w_in.shape[0] == 1, "single-layer step only"
    assert S % C == 0 and LS <= C and N_META <= C and meta_tokens.shape[0] == N_META

    wi = w_in[0]
    o = 0
    parts = {}
    for name, size in (("g_qkv", 3 * WIDTH), ("g_z", WIDTH), ("g_a", HEADS), ("g_b", HEADS),
                       ("f_qkv", 3 * WIDTH), ("f_og", WIDTH), ("f_f", HEADS)):
        parts[name] = wi[:, o:o + size]
        o += size
    wbig = jnp.concatenate([parts[k].astype(BF16) for k in ("g_qkv", "f_qkv", "g_z", "f_og")], axis=1)
    wsm_cols = jnp.concatenate([parts["f_f"], parts["g_a"], parts["g_b"]], axis=1)
    wsm = jnp.pad(wsm_cols, ((0, 0), (0, LANES - 3 * HEADS))).astype(BF16)
    wo = w_out[0].astype(BF16)
    wr32 = jnp.pad(jnp.concatenate([w_router_group[0], w_router_expert[0]], axis=1),
                   ((0, 0), (0, LANES - N_GROUPS - N_EXPERTS)))
    wr_hi = wr32.astype(BF16)
    wr = jnp.concatenate([wr_hi, (wr32 - wr_hi.astype(F32)).astype(BF16)], axis=1)
    br = jnp.pad(jnp.concatenate([b_router_group[0], b_router_expert[0]]),
                 (0, LANES - N_GROUPS - N_EXPERTS)).reshape(1, LANES)
    w1 = w_gate[0].astype(BF16)
    w3 = w_up[0].astype(BF16)
    w2 = w_down[0].astype(BF16)
    nmix = norm_mix_w[0].reshape(1, D)
    nffn = norm_ffn_w[0].reshape(1, D)
    nfin = norm_final_w.reshape(1, D)

    x_small = jnp.concatenate([
        jnp.pad(meta_tokens.astype(F32), ((0, C - N_META), (0, 0))),
        jnp.pad(x_sample, ((0, 0), (0, C - LS), (0, 0))).reshape(BS * C, D)], axis=0)
    xp = x_prompt.reshape(B * S, D)

    p_s, psm_s, psmt_s = _inproj(x_small, nmix, wbig, wsm)
    p_p, psm_p, psmt_p = _inproj(xp, nmix, wbig, wsm)

    s0_s = jnp.concatenate([jnp.zeros((1,) + state_gdn.shape[2:], F32), state_gdn[0]], axis=0)
    cb_s = jnp.concatenate([jnp.zeros((1,) + state_gdn_conv.shape[2:], F32), state_gdn_conv[0]], axis=0)
    gdn_args = (gdn_conv_w[0], gdn_A_log[0], gdn_dt_bias[0], gdn_norm_w[0])
    og_s, st_s, cv_s = _gdn(p_s, psm_s, psmt_s, s0_s, cb_s, *gdn_args, n_seq=1 + BS, l_valid=LS,
                            bcast_state=False)
    og_p, st_p, cv_p = _gdn(p_p, psm_p, psmt_p, st_s[0:1], cv_s[0:1], *gdn_args, n_seq=B, l_valid=C,
                            bcast_state=True)

    fox_args = (fox_q_norm_w[0], fox_k_norm_w[0], fox_f_bias[0])
    zc = jnp.zeros((1, 1, LANES), F32)
    zr = jnp.zeros((1, HEADS, 1), F32)
    k32_s, v32_s, lfc_s, lfr_s, fc_s, fr_s, qt_s, kcat_s, vt_s, q_s, kb_s = _foxprep(
        p_s, psm_s, psmt_s, zc, zr, *fox_args, n_seq=1 + BS, bcast_carry=True, row_major=True)
    c0c = fc_s[N_META - 1:N_META, :].reshape(1, 1, LANES)
    c0r = fr_s[:, N_META - 1:N_META].reshape(1, HEADS, 1)
    k32_p, v32_p, lfc_p, lfr_p, fc_p, fr_p, qt_p, kcat_p, vt_p = _foxprep(
        p_p, psm_p, psmt_p, c0c, c0r, *fox_args, n_seq=B, bcast_carry=True, row_major=False, lead=N_META)
    of_p = _fox_prompt(qt_p, kcat_p, vt_p, p_p, kcat_s, vt_s, n_seq=B)
    of_s = _fox_sample(q_s, kb_s, p_s, lfc_s, lfr_s, cache_fox_k[0], cache_fox_v[0], cache_fox_logf[0],
                       l_valid=LS)

    xr_p, rt_p, cnt_p = _outproj(og_p, of_p, xp, wo, nffn, wr, br, jnp.zeros((1, LANES), F32))
    xr_s, rt_s, cnt = _outproj(og_s[C:], of_s, x_small[C:], wo, nffn, wr, br, cnt_p)
    y_p, y_s = _moe([xr_p, xr_s], [rt_p, rt_s], cnt, w1, w3, w2, nffn, nfin)

    def prompt_kv(tab_p, tab_s):
        tab = _fill_lead(tab_p, tab_s[:N_META * HEADS], n_seq=B)
        return tab.reshape(1, B, N_META + S, HEADS, HEAD_DIM)

    def sample_kv(tab_s):
        return tab_s.reshape(1 + BS, C, HEADS, HEAD_DIM)[1:, :LS][None]

    y_prompt = y_p.reshape(B, S, D)
    y_sample = y_s.reshape(BS, C, D)[:, :LS]
    fk_p = prompt_kv(k32_p, k32_s)
    fv_p = prompt_kv(v32_p, v32_s)
    lf_p = _assemble_logf(lfc_s, lfc_p, n_seq=B)[None]
    fk_s = sample_kv(k32_s)
    fv_s = sample_kv(v32_s)
    lf_s = lfc_s[C:, SM_F:SM_F + HEADS].reshape(BS, C, HEADS)[:, :LS][None]
    return (y_prompt, y_sample, fk_p, fv_p, lf_p, st_p[None], cv_p[None],
            fk_s, fv_s, lf_s, st_s[1:][None], cv_s[1:][None])
```

```python
import functools

import jax
import jax.numpy as jnp
from jax import lax
from jax.experimental import pallas as pl
from jax.experimental.pallas import tpu as pltpu

F32 = jnp.float32
BF16 = jnp.bfloat16
EPS = 1e-6
NEG = -1e30
LOG2E = 1.4426950408889634

N_META = 16
HEADS = 8
HEAD_DIM = 128
WIDTH = HEADS * HEAD_DIM
CONV_WIDTH = 4
N_GROUPS = 4
EXPERTS_PER_GROUP = 8
N_EXPERTS = N_GROUPS * EXPERTS_PER_GROUP
GDN_CHUNK = 128
LANES = 128
SM_F, SM_A, SM_B = 0, 8, 16
PB_FQ, PB_FK, PB_FV, PB_GZ, PB_FOG = 3, 4, 5, 6, 7
RT_E0 = N_GROUPS
PAIRS_PER_GROUP = EXPERTS_PER_GROUP * (EXPERTS_PER_GROUP - 1) // 2
N_CLASSES = N_GROUPS * PAIRS_PER_GROUP
RT_CLS, RT_RANK, RT_WA, RT_WB = 0, 1, 2, 3
MOE_TS = 256

VMEM_LIMIT = 56 * 1024 * 1024


def _cparams(sem):
    return pltpu.CompilerParams(dimension_semantics=sem, vmem_limit_bytes=VMEM_LIMIT)


def _pick(n, prefs):
    for p in prefs:
        if n % p == 0:
            return p
    raise ValueError(f"no tile in {prefs} divides {n}")


def _dot(a, b):
    return jnp.dot(a, b, preferred_element_type=F32)


def _dot_nt(a, b):
    return lax.dot_general(a, b, (((1,), (1,)), ((), ())), preferred_element_type=F32)


def _dot_tn(a, b):
    return lax.dot_general(a, b, (((0,), (0,)), ((), ())), preferred_element_type=F32)


def _split3(x):
    x1 = x.astype(BF16)
    r1 = x - x1.astype(F32)
    x2 = r1.astype(BF16)
    x3 = (r1 - x2.astype(F32)).astype(BF16)
    return x1, x2, x3


def _mask_dot(mask_bf16, x):
    x1, x2, x3 = _split3(x)
    return _dot(mask_bf16, x1) + _dot(mask_bf16, x2) + _dot(mask_bf16, x3)


def _dot_mask(x, mask_bf16):
    x1, x2, x3 = _split3(x)
    return _dot(x1, mask_bf16) + _dot(x2, mask_bf16) + _dot(x3, mask_bf16)


def _softplus(x):
    return jnp.maximum(x, 0.0) + jnp.log1p(jnp.exp(-jnp.abs(x)))


def _sigmoid(x):
    return 1.0 / (1.0 + jnp.exp(-x))


def _silu(x):
    return x * _sigmoid(x)


def _iota2(shape, dim):
    return lax.broadcasted_iota(jnp.int32, shape, dim)


def _inproj_kernel(x_ref, nw_ref, wbig_ref, wsm_ref, p_ref, psm_ref, psmt_ref, xn_scr):
    @pl.when(pl.program_id(1) == 0)
    def _():
        x = x_ref[...]
        xn = x * lax.rsqrt(jnp.mean(x * x, axis=-1, keepdims=True) + EPS) * nw_ref[...]
        xnb = xn.astype(BF16)
        xn_scr[...] = xnb
        psm = _dot(xnb, wsm_ref[...])
        psm_ref[...] = psm
        psmt_ref[...] = psm.T[0:psmt_ref.shape[0], :]

    p_ref[...] = _dot(xn_scr[...], wbig_ref[...]).astype(BF16)


def _inproj(x, norm_w, wbig, wsm):
    n, d = x.shape
    tm = _pick(n, (1024, 1152, 512, 384, 256, 128))
    tn = 2048
    nproj = wbig.shape[1]
    return pl.pallas_call(
        _inproj_kernel,
        grid=(n // tm, nproj // tn),
        in_specs=[
            pl.BlockSpec((tm, d), lambda i, j: (i, 0)),
            pl.BlockSpec((1, d), lambda i, j: (0, 0)),
            pl.BlockSpec((d, tn), lambda i, j: (0, j)),
            pl.BlockSpec((d, LANES), lambda i, j: (0, 0)),
        ],
        out_specs=[
            pl.BlockSpec((tm, tn), lambda i, j: (i, j)),
            pl.BlockSpec((tm, LANES), lambda i, j: (i, 0)),
            pl.BlockSpec((32, tm), lambda i, j: (0, i)),
        ],
        out_shape=[
            jax.ShapeDtypeStruct((n, nproj), BF16),
            jax.ShapeDtypeStruct((n, LANES), F32),
            jax.ShapeDtypeStruct((32, n), F32),
        ],
        scratch_shapes=[pltpu.VMEM((tm, d), BF16)],
        compiler_params=_cparams(("arbitrary", "arbitrary")),
        name="inproj",
    )(x, norm_w, wbig, wsm)


def _gdn_kernel(qkv_ref, z_ref, sm_ref, smt_ref, s0_ref, cb_ref, convw_ref, alog_c_ref, dtb_c_ref,
                alog_r_ref, dtb_r_ref, normw_ref, o_ref, snew_ref, cnew_ref, s_scr, xp_scr, conv_scr, *,
                l_valid):
    C = GDN_CHUNK
    n_sub = qkv_ref.shape[0] // C

    @pl.when(pl.program_id(1) == 0)
    def _():
        s_scr[...] = s0_ref[0]
        xp_scr[5:8, :] = cb_ref[0]

    row = _iota2((C, C), 0)
    col = _iota2((C, C), 1)
    incl = row >= col
    strict = row > col
    tril_b = jnp.where(incl, 1.0, 0.0).astype(BF16)
    triu_b = jnp.where(row <= col, 1.0, 0.0).astype(BF16)
    shift_b = jnp.concatenate(
        [jnp.where(col == row - i, 1.0, 0.0).astype(BF16) for i in (1, 2, 3)], axis=0)
    inv_levels = []
    sh = 1
    while (1 << sh) < C:
        inv_levels.append((((row >> sh) & 1) == 1) & ((col >> (sh + 1)) == (row >> (sh + 1)))
                          & (((col >> sh) & 1) == 0))
        sh += 1
    m1 = ((row & 1) == 1) & (col == row - 1)
    eye = jnp.where(row == col, 1.0, 0.0)
    w = convw_ref[...]
    normw = normw_ref[...]

    act, gl_col, gl_row, beta_col = [], [], [], []
    for sub in range(n_sub):
        rows = slice(sub * C, (sub + 1) * C)
        xb = qkv_ref[rows, :]
        x = xb.astype(F32)
        xp_scr[8:16, :] = x[0:8]
        shifted = _dot(shift_b, xb)
        conv_scr[rows, :] = (w[3:4] * x + w[2:3] * shifted[0:C] + w[1:2] * shifted[C:2 * C]
                             + w[0:1] * shifted[2 * C:3 * C])
        conv_scr[sub * C:sub * C + 8, :] = (w[0:1] * xp_scr[5:13, :] + w[1:2] * xp_scr[6:14, :]
                                            + w[2:3] * xp_scr[7:15, :] + w[3:4] * x[0:8])
        act.append(_silu(conv_scr[rows, :]))
        tail = x[l_valid - 3:l_valid, :]
        xp_scr[5:8, :] = tail
        cnew_ref[0] = tail

        sm = sm_ref[rows, :]
        g_col = -jnp.exp(alog_c_ref[...]) * _softplus(sm + dtb_c_ref[...])
        g_row = -jnp.exp(alog_r_ref[...]) * _softplus(smt_ref[:, rows] + dtb_r_ref[...])
        b_col = _sigmoid(sm)
        if l_valid < C:
            g_col = jnp.where(_iota2((C, LANES), 0) < l_valid, g_col, 0.0)
            b_col = jnp.where(_iota2((C, LANES), 0) < l_valid, b_col, 0.0)
            g_row = jnp.where(_iota2((HEADS, C), 1) < l_valid, g_row, 0.0)
        beta_col.append(b_col)
        gl_col.append(_mask_dot(tril_b, g_col))
        gl_row.append(_dot_mask(g_row, triu_b))

    hs = [slice(h * HEAD_DIM, (h + 1) * HEAD_DIM) for h in range(HEADS)]
    I = [(sub, h) for sub in range(n_sub) for h in range(HEADS)]
    gc = {(c, h): gl_col[c][:, SM_A + h:SM_A + h + 1] for c, h in I}
    beta = {(c, h): beta_col[c][:, SM_B + h:SM_B + h + 1] for c, h in I}
    qn, kn, kb, knb = {}, {}, {}, {}
    for c, h in I:
        q = act[c][:, hs[h]]
        k = act[c][:, WIDTH + h * HEAD_DIM:WIDTH + (h + 1) * HEAD_DIM]
        qn[c, h] = q * lax.rsqrt(jnp.sum(q * q, axis=-1, keepdims=True) + EPS) * (HEAD_DIM ** -0.5)
        kn[c, h] = k * lax.rsqrt(jnp.sum(k * k, axis=-1, keepdims=True) + EPS)
        kb[c, h] = kn[c, h] * beta[c, h]
        knb[c, h] = kn[c, h].astype(BF16)
    decay = {i: jnp.exp(jnp.where(incl, gc[i] - gl_row[i[0]][i[1]:i[1] + 1, :], NEG)) for i in I}
    a_mat = {i: jnp.where(strict, _dot_nt(kb[i].astype(BF16), knb[i]) * decay[i], 0.0) for i in I}
    qk = {i: (_dot_nt(qn[i].astype(BF16), knb[i]) * decay[i]).astype(BF16) for i in I}

    t = {i: eye - jnp.where(m1, a_mat[i], 0.0) for i in I}
    for m in inv_levels:
        tb = {i: t[i].astype(BF16) for i in I}
        y = {i: _dot(tb[i], jnp.where(m, a_mat[i], 0.0).astype(BF16)).astype(BF16) for i in I}
        t = {i: t[i] - _dot(y[i], tb[i]) for i in I}
    tb = {i: t[i].astype(BF16) for i in I}
    eg = {i: jnp.exp(gc[i]) for i in I}
    g_last = {i: gc[i][C - 1:C, :] for i in I}

    for c in range(n_sub):
        H = [(c, h) for h in range(HEADS)]
        s = {i: s_scr[i[1]] for i in H}
        sb = {i: s[i].astype(BF16) for i in H}
        r = {i: (act[c][:, 2 * WIDTH + i[1] * HEAD_DIM:2 * WIDTH + (i[1] + 1) * HEAD_DIM] * beta[i]
                 - _dot((kb[i] * eg[i]).astype(BF16), sb[i])).astype(BF16) for i in H}
        ub = {i: _dot(tb[i], r[i]).astype(BF16) for i in H}
        o = {i: _dot((qn[i] * eg[i]).astype(BF16), sb[i]) + _dot(qk[i], ub[i]) for i in H}
        for i in H:
            k_dec = (kn[i] * jnp.exp(g_last[i] - gc[i])).astype(BF16)
            s_scr[i[1]] = s[i] * jnp.exp(g_last[i]) + _dot_tn(k_dec, ub[i])
        for i in H:
            on = o[i] * lax.rsqrt(jnp.mean(o[i] * o[i], axis=-1, keepdims=True) + EPS) * normw
            z = z_ref[c * C:(c + 1) * C, hs[i[1]]].astype(F32)
            o_ref[c * C:(c + 1) * C, hs[i[1]]] = (on * _silu(z)).astype(BF16)

    snew_ref[0] = s_scr[...]


def _gdn(p, psm, psmt, s0, cb, conv_w, alog, dtb, norm_w, *, n_seq, l_valid, bcast_state):
    n = p.shape[0]
    C = GDN_CHUNK
    n_sub = _pick(n // (n_seq * C), (2, 1))
    R = n_sub * C
    nc = n // (n_seq * R)
    alog_c = jnp.zeros((1, LANES), F32).at[0, SM_A:SM_A + HEADS].set(alog)
    dtb_c = jnp.zeros((1, LANES), F32).at[0, SM_A:SM_A + HEADS].set(dtb)
    st_idx = (lambda b, c: (0, 0, 0, 0)) if bcast_state else (lambda b, c: (b, 0, 0, 0))
    cb_idx = (lambda b, c: (0, 0, 0)) if bcast_state else (lambda b, c: (b, 0, 0))
    return pl.pallas_call(
        functools.partial(_gdn_kernel, l_valid=l_valid),
        grid=(n_seq, nc),
        in_specs=[
            pl.BlockSpec((R, 3 * WIDTH), lambda b, c: (b * nc + c, 0)),
            pl.BlockSpec((R, WIDTH), lambda b, c: (b * nc + c, PB_GZ)),
            pl.BlockSpec((R, LANES), lambda b, c: (b * nc + c, 0)),
            pl.BlockSpec((HEADS, R), lambda b, c: (SM_A // HEADS, b * nc + c)),
            pl.BlockSpec((1, HEADS, HEAD_DIM, HEAD_DIM), st_idx),
            pl.BlockSpec((1, CONV_WIDTH - 1, 3 * WIDTH), cb_idx),
            pl.BlockSpec((CONV_WIDTH, 3 * WIDTH), lambda b, c: (0, 0)),
            pl.BlockSpec((1, LANES), lambda b, c: (0, 0)),
            pl.BlockSpec((1, LANES), lambda b, c: (0, 0)),
            pl.BlockSpec((HEADS, 1), lambda b, c: (0, 0)),
            pl.BlockSpec((HEADS, 1), lambda b, c: (0, 0)),
            pl.BlockSpec((1, HEAD_DIM), lambda b, c: (0, 0)),
        ],
        out_specs=[
            pl.BlockSpec((R, WIDTH), lambda b, c: (b * nc + c, 0)),
            pl.BlockSpec((1, HEADS, HEAD_DIM, HEAD_DIM), lambda b, c: (b, 0, 0, 0)),
            pl.BlockSpec((1, CONV_WIDTH - 1, 3 * WIDTH), lambda b, c: (b, 0, 0)),
        ],
        out_shape=[
            jax.ShapeDtypeStruct((n, WIDTH), BF16),
            jax.ShapeDtypeStruct((n_seq, HEADS, HEAD_DIM, HEAD_DIM), F32),
            jax.ShapeDtypeStruct((n_seq, CONV_WIDTH - 1, 3 * WIDTH), F32),
        ],
        scratch_shapes=[
            pltpu.VMEM((HEADS, HEAD_DIM, HEAD_DIM), F32),
            pltpu.VMEM((16, 3 * WIDTH), F32),
            pltpu.VMEM((R, 3 * WIDTH), F32),
        ],
        compiler_params=_cparams(("arbitrary", "arbitrary")),
        name="gdn",
    )(p, p, psm, psmt, s0, cb, conv_w, alog_c, dtb_c, alog.reshape(HEADS, 1), dtb.reshape(HEADS, 1),
      norm_w.reshape(1, HEAD_DIM))


def _foxprep_kernel(qkv_ref, sm_ref, smt_ref, c0c_ref, c0r_ref, qw_ref, kw_ref, fb_c_ref, fb_r_ref,
                    k32_ref, v32_ref, lfc_ref, lfr_ref, fc_ref, fr_ref, qt_ref, kcat_ref, vt_ref, *rest,
                    row_major):
    if row_major:
        q_ref, kb_ref, cc_scr, cr_scr = rest
    else:
        cc_scr, cr_scr = rest
    tm = qkv_ref.shape[0]

    @pl.when(pl.program_id(1) == 0)
    def _():
        cc_scr[...] = c0c_ref[0]
        cr_scr[...] = c0r_ref[0]

    lf_col = -_softplus(-(sm_ref[...] + fb_c_ref[...]))
    lf_row = -_softplus(-(smt_ref[...] + fb_r_ref[...]))
    lfc_ref[...] = lf_col
    lfr_ref[...] = lf_row
    row = _iota2((tm, tm), 0)
    col = _iota2((tm, tm), 1)
    tril_b = jnp.where(row >= col, 1.0, 0.0).astype(BF16)
    triu_b = jnp.where(row <= col, 1.0, 0.0).astype(BF16)
    f_col = cc_scr[...] + _mask_dot(tril_b, lf_col)
    f_row = cr_scr[...] + _dot_mask(lf_row, triu_b)
    fc_ref[...] = f_col
    fr_ref[...] = f_row
    cc_scr[...] = f_col[tm - 1:tm, :]
    cr_scr[...] = f_row[:, tm - 1:tm]

    er = _iota2((LANES, LANES), 0) - SM_F
    ec = _iota2((LANES, LANES), 1)
    bias_k = jnp.where((_iota2((1, LANES), 1) & (FOX_BIAS - 1)) < 3, 1.0, 0.0)
    for j, c in enumerate(_split3(f_col * LOG2E)):
        move = jnp.where((er >= 0) & (er < HEADS) & (ec == FOX_BIAS * er + 3 + j), -1.0, 0.0).astype(BF16)
        bias_k = bias_k + _dot(c, move)
    kcat_ref[:, WIDTH:WIDTH + LANES] = bias_k.astype(BF16)

    qw = qw_ref[...]
    kw = kw_ref[...]
    sub = _iota2((FOX_BIAS, tm), 0)
    zeros_t = jnp.zeros((HEAD_DIM, tm), BF16)
    for h in range(HEADS):
        hs = slice(h * HEAD_DIM, (h + 1) * HEAD_DIM)
        q = qkv_ref[:, hs].astype(F32)
        k = qkv_ref[:, WIDTH + h * HEAD_DIM:WIDTH + (h + 1) * HEAD_DIM].astype(F32)
        v = qkv_ref[:, 2 * WIDTH + h * HEAD_DIM:2 * WIDTH + (h + 1) * HEAD_DIM].astype(F32)
        qn = q * lax.rsqrt(jnp.mean(q * q, axis=-1, keepdims=True) + EPS) * qw
        kn = k * lax.rsqrt(jnp.mean(k * k, axis=-1, keepdims=True) + EPS) * kw
        k32_ref[pl.ds(h, tm, stride=HEADS), :] = kn
        v32_ref[pl.ds(h, tm, stride=HEADS), :] = v
        if row_major:
            q_ref[:, hs] = (qn * (HEAD_DIM ** -0.5)).astype(BF16)
            kb_ref[:, hs] = kn.astype(BF16)
        base = 2 * h * HEAD_DIM
        qt_ref[base:base + HEAD_DIM, :] = (qn * (LOG2E * HEAD_DIM ** -0.5)).T.astype(BF16)
        r1, r2, r3 = _split3(f_row[h:h + 1, :] * LOG2E)
        aug_q = jnp.where(sub == 0, r1.astype(F32), jnp.where(sub == 1, r2.astype(F32), jnp.where(
            sub == 2, r3.astype(F32), jnp.where(sub < 6, 1.0, 0.0))))
        qt_ref[base + HEAD_DIM:base + 2 * HEAD_DIM, :] = zeros_t
        qt_ref[base + HEAD_DIM + FOX_BIAS * h:base + HEAD_DIM + FOX_BIAS * (h + 1), :] = aug_q.astype(BF16)
        kcat_ref[:, hs] = kn.astype(BF16)
        vt_ref[hs, :] = v.T.astype(BF16)


def _foxprep(p, psm, psmt, c0c, c0r, q_norm_w, k_norm_w, f_bias, *, n_seq, bcast_carry, row_major, lead=0):
    n = p.shape[0]
    rows = n // n_seq
    tm = _pick(rows, (512, 256, 128))
    nt = rows // tm
    fb_c = jnp.zeros((1, LANES), F32).at[0, SM_F:SM_F + HEADS].set(f_bias)
    c_idx = (lambda b, t: (0, 0, 0)) if bcast_carry else (lambda b, t: (b, 0, 0))
    rowblk = lambda b, t: (b * nt + t, 0)
    colblk = lambda b, t: (0, b * nt + t)
    kv_spec = pl.BlockSpec((pl.Element(tm * HEADS), pl.Element(HEAD_DIM)),
                           lambda b, t: ((b * (rows + lead) + lead + t * tm) * HEADS, 0))
    kv_shape = jax.ShapeDtypeStruct((n_seq * (rows + lead) * HEADS, HEAD_DIM), F32)
    out_specs = [
        kv_spec,
        kv_spec,
        pl.BlockSpec((tm, LANES), rowblk),
        pl.BlockSpec((HEADS, tm), colblk),
        pl.BlockSpec((tm, LANES), rowblk),
        pl.BlockSpec((HEADS, tm), colblk),
        pl.BlockSpec((2 * WIDTH, tm), colblk),
        pl.BlockSpec((tm, WIDTH + LANES), rowblk),
        pl.BlockSpec((WIDTH, tm), colblk),
    ]
    out_shape = [
        kv_shape,
        kv_shape,
        jax.ShapeDtypeStruct((n, LANES), F32),
        jax.ShapeDtypeStruct((HEADS, n), F32),
        jax.ShapeDtypeStruct((n, LANES), F32),
        jax.ShapeDtypeStruct((HEADS, n), F32),
        jax.ShapeDtypeStruct((2 * WIDTH, n), BF16),
        jax.ShapeDtypeStruct((n, WIDTH + LANES), BF16),
        jax.ShapeDtypeStruct((WIDTH, n), BF16),
    ]
    if row_major:
        out_specs += [pl.BlockSpec((tm, WIDTH), rowblk), pl.BlockSpec((tm, WIDTH), rowblk)]
        out_shape += [jax.ShapeDtypeStruct((n, WIDTH), BF16),
                      jax.ShapeDtypeStruct((n, WIDTH), BF16)]
    return pl.pallas_call(
        functools.partial(_foxprep_kernel, row_major=row_major),
        grid=(n_seq, nt),
        in_specs=[
            pl.BlockSpec((tm, 3 * WIDTH), lambda b, t: (b * nt + t, 1)),
            pl.BlockSpec((tm, LANES), rowblk),
            pl.BlockSpec((HEADS, tm), lambda b, t: (SM_F // HEADS, b * nt + t)),
            pl.BlockSpec((1, 1, LANES), c_idx),
            pl.BlockSpec((1, HEADS, 1), c_idx),
            pl.BlockSpec((1, HEAD_DIM), lambda b, t: (0, 0)),
            pl.BlockSpec((1, HEAD_DIM), lambda b, t: (0, 0)),
            pl.BlockSpec((1, LANES), lambda b, t: (0, 0)),
            pl.BlockSpec((HEADS, 1), lambda b, t: (0, 0)),
        ],
        out_specs=out_specs,
        out_shape=out_shape,
        scratch_shapes=[pltpu.VMEM((1, LANES), F32), pltpu.VMEM((HEADS, 1), F32)],
        compiler_params=_cparams(("arbitrary", "arbitrary")),
        name="foxprep",
    )(p, psm, psmt, c0c, c0r, q_norm_w.reshape(1, HEAD_DIM), k_norm_w.reshape(1, HEAD_DIM), fb_c,
      f_bias.reshape(HEADS, 1))


def _fill_lead_kernel(src_ref, big_ref, o_ref):
    o_ref[...] = src_ref[...]


def _fill_lead(big, src, *, n_seq):
    r = src.shape[0]
    seq_rows = big.shape[0] // n_seq
    assert seq_rows % r == 0
    return pl.pallas_call(
        _fill_lead_kernel,
        grid=(n_seq,),
        in_specs=[pl.BlockSpec((r, HEAD_DIM), lambda b: (0, 0)), pl.BlockSpec(memory_space=pl.ANY)],
        out_specs=pl.BlockSpec((r, HEAD_DIM), lambda b: (b * (seq_rows // r), 0)),
        out_shape=jax.ShapeDtypeStruct(big.shape, big.dtype),
        input_output_aliases={1: 0},
        compiler_params=_cparams(("arbitrary",)),
        name="fill_lead",
    )(src, big)


def _logf_kernel(meta_ref, lf_ref, o_ref):
    o_ref[0, 0:N_META, :] = meta_ref[0:N_META, SM_F:SM_F + HEADS]
    o_ref[0, N_META:, :] = lf_ref[:, SM_F:SM_F + HEADS]


def _assemble_logf(lf_small, lf_prompt, *, n_seq):
    s = lf_prompt.shape[0] // n_seq
    return pl.pallas_call(
        _logf_kernel,
        grid=(n_seq,),
        in_specs=[pl.BlockSpec((GDN_CHUNK, LANES), lambda b: (0, 0)), pl.BlockSpec((s, LANES), lambda b: (b, 0))],
        out_specs=pl.BlockSpec((1, N_META + s, HEADS), lambda b: (b, 0, 0)),
        out_shape=jax.ShapeDtypeStruct((n_seq, N_META + s, HEADS), F32),
        compiler_params=_cparams(("arbitrary",)),
        name="assemble_logf",
    )(lf_small, lf_prompt)


FOX_UNIT = 256
FOX_BIAS = 16
FOX_SUM_ROWS = 16

def _fox_kernel(qi_ref, ki_ref, qt_ref, kcat_ref, vt_ref, og_ref, km_ref, vtm_ref, o_ref, m_scr, acc_scr):
    pair = pl.program_id(1)
    qi = qi_ref[pair]
    ki = ki_ref[pair]
    tq = qt_ref.shape[1]
    tk = kcat_ref.shape[0]
    U = FOX_UNIT
    H = range(HEADS)

    def unit(qh, k_rows, vt_cols, mask):
        qs = slice(qh * U, (qh + 1) * U)
        s = [_dot(k_rows(h), qt_ref[2 * h * HEAD_DIM:2 * (h + 1) * HEAD_DIM, qs]) for h in H]
        if mask is not None:
            s = [jnp.where(mask, s[h], NEG) for h in H]
        m_old = [m_scr[h:h + 1, qs] for h in H]
        m_new = [jnp.maximum(m_old[h], jnp.max(s[h], axis=0, keepdims=True)) for h in H]
        alpha = [jnp.exp2(m_old[h] - m_new[h]) for h in H]
        p = [jnp.exp2((s[h] - m_new[h]).astype(BF16)) for h in H]
        for h in H:
            m_scr[h:h + 1, qs] = m_new[h]
        for h in H:
            lhs = jnp.concatenate([vt_cols(h), jnp.ones((FOX_SUM_ROWS, p[h].shape[0]), BF16)], axis=0)
            acc_scr[h, :, qs] = alpha[h] * acc_scr[h, :, qs] + _dot(lhs, p[h])

    def k_unit(r0):
        return lambda h: jnp.concatenate([kcat_ref[r0:r0 + U, h * HEAD_DIM:(h + 1) * HEAD_DIM],
                                          kcat_ref[r0:r0 + U, WIDTH:WIDTH + LANES]], axis=1)

    def vt_unit(r0):
        return lambda h: vt_ref[h * HEAD_DIM:(h + 1) * HEAD_DIM, r0:r0 + U]

    @pl.when(ki == 0)
    def _():
        m_scr[...] = jnp.full(m_scr.shape, NEG, F32)
        acc_scr[...] = jnp.zeros(acc_scr.shape, F32)
        tmeta = km_ref.shape[0]
        mask = _iota2((tmeta, U), 0) < N_META
        for qh in range(tq // U):
            unit(qh, lambda h: jnp.concatenate([km_ref[:, h * HEAD_DIM:(h + 1) * HEAD_DIM],
                                                km_ref[:, WIDTH:WIDTH + LANES]], axis=1),
                 lambda h: vtm_ref[h * HEAD_DIM:(h + 1) * HEAD_DIM, :], mask)

    kb = tk // tq
    for j in range(kb):
        g = ki * kb + j

        @pl.when(g < qi)
        def _(j=j):
            for qh in range(tq // U):
                for ku in range(tq // U):
                    unit(qh, k_unit(j * tq + ku * U), vt_unit(j * tq + ku * U), None)

        @pl.when(g == qi)
        def _(j=j):
            diag = _iota2((U, U), 0) <= _iota2((U, U), 1)
            for qh in range(tq // U):
                for ku in range(qh + 1):
                    unit(qh, k_unit(j * tq + ku * U), vt_unit(j * tq + ku * U), diag if ku == qh else None)

    @pl.when(ki == qi // kb)
    def _():
        for h in H:
            hs = slice(h * HEAD_DIM, (h + 1) * HEAD_DIM)
            gate = _sigmoid(og_ref[:, hs].astype(F32))
            o_ref[:, hs] = ((acc_scr[h, 0:HEAD_DIM, :] / acc_scr[h, HEAD_DIM:HEAD_DIM + 1, :]).T * gate).astype(BF16)


def _fox_prompt(qt, kcat, vt, p, kcat_small, vt_small, *, n_seq):
    n = p.shape[0]
    rows = n // n_seq
    tq = _pick(rows, (512, 256))
    nq = rows // tq
    kb = _pick(nq, (2, 1))
    tk, nk = kb * tq, nq // kb
    pairs = [(i, j) for i in range(nq) for j in range(i // kb + 1)]
    qi = jnp.asarray([a for a, _ in pairs], jnp.int32)
    ki = jnp.asarray([b for _, b in pairs], jnp.int32)
    grid_spec = pltpu.PrefetchScalarGridSpec(
        num_scalar_prefetch=2,
        grid=(n_seq, len(pairs)),
        in_specs=[
            pl.BlockSpec((2 * WIDTH, tq), lambda b, t, qi_r, ki_r: (0, b * nq + qi_r[t])),
            pl.BlockSpec((tk, WIDTH + LANES), lambda b, t, qi_r, ki_r: (b * nk + ki_r[t], 0)),
            pl.BlockSpec((WIDTH, tk), lambda b, t, qi_r, ki_r: (0, b * nk + ki_r[t])),
            pl.BlockSpec((tq, WIDTH), lambda b, t, qi_r, ki_r: (b * nq + qi_r[t], PB_FOG)),
            pl.BlockSpec((GDN_CHUNK, WIDTH + LANES), lambda b, t, qi_r, ki_r: (0, 0)),
            pl.BlockSpec((WIDTH, GDN_CHUNK), lambda b, t, qi_r, ki_r: (0, 0)),
        ],
        out_specs=pl.BlockSpec((tq, WIDTH), lambda b, t, qi_r, ki_r: (b * nq + qi_r[t], 0)),
        scratch_shapes=[
            pltpu.VMEM((HEADS, tq), F32),
            pltpu.VMEM((HEADS, HEAD_DIM + FOX_SUM_ROWS, tq), F32),
        ],
    )
    return pl.pallas_call(
        _fox_kernel,
        grid_spec=grid_spec,
        out_shape=jax.ShapeDtypeStruct((n, WIDTH), BF16),
        compiler_params=_cparams(("arbitrary", "arbitrary")),
        name="fox_prompt",
    )(qi, ki, qt, kcat, vt, p, kcat_small, vt_small)


def _fox_sample_kernel(q_ref, kn_ref, vn_ref, og_ref, lfc_ref, lfr_ref, ck_ref, cv_ref, clc_ref, clr_ref,
                       o_ref, *, l_valid):
    C = q_ref.shape[0]
    P = ck_ref.shape[1] // HEADS
    rowp = _iota2((P, P), 0)
    colp = _iota2((P, P), 1)
    triu_p = jnp.where(rowp <= colp, 1.0, 0.0).astype(BF16)
    row = _iota2((C, C), 0)
    col = _iota2((C, C), 1)
    tril_b = jnp.where(row >= col, 1.0, 0.0).astype(BF16)
    triu_b = jnp.where(row <= col, 1.0, 0.0).astype(BF16)

    f_cache = _dot_mask(clr_ref[0], triu_p)
    carry_r = f_cache[:, P - 1:P]
    carry_c = jnp.sum(clc_ref[0], axis=0, keepdims=True)
    f_new_r = carry_r + _dot_mask(lfr_ref[...], triu_b)
    f_new_c = carry_c + _mask_dot(tril_b, lfc_ref[...])[:, SM_F:SM_F + HEADS]
    mask_new = (col <= row) & (col < l_valid)

    for h in range(HEADS):
        hs = slice(h * HEAD_DIM, (h + 1) * HEAD_DIM)
        q = q_ref[:, hs]
        fq = f_new_c[:, h:h + 1]
        ck = ck_ref[0, pl.ds(h, P, stride=HEADS), :].astype(BF16)
        cv = cv_ref[0, pl.ds(h, P, stride=HEADS), :].astype(BF16)
        s_c = _dot_nt(q, ck) + (fq - f_cache[h:h + 1, :])
        s_n = jnp.where(mask_new, _dot_nt(q, kn_ref[:, hs]) + (fq - f_new_r[h:h + 1, :]), NEG)
        m = jnp.maximum(jnp.max(s_c, axis=-1, keepdims=True), jnp.max(s_n, axis=-1, keepdims=True))
        p_c = jnp.exp(s_c - m)
        p_n = jnp.exp(s_n - m)
        l = jnp.sum(p_c, axis=-1, keepdims=True) + jnp.sum(p_n, axis=-1, keepdims=True)
        o = _dot(p_c.astype(BF16), cv) + _dot(p_n.astype(BF16), vn_ref[:, hs])
        gate = _sigmoid(og_ref[:, hs].astype(F32))
        o_ref[:, hs] = (o / l * gate).astype(BF16)


def _fox_sample(q, kb, p, lf_col, lf_row, cache_k, cache_v, cache_lf, *, l_valid):
    bs, past = cache_k.shape[0], cache_k.shape[1]
    C = GDN_CHUNK
    n = q.shape[0]
    blk = lambda b: (b + 1, 0)
    return pl.pallas_call(
        functools.partial(_fox_sample_kernel, l_valid=l_valid),
        grid=(bs,),
        in_specs=[
            pl.BlockSpec((C, WIDTH), blk),
            pl.BlockSpec((C, WIDTH), blk),
            pl.BlockSpec((C, WIDTH), lambda b: (b + 1, PB_FV)),
            pl.BlockSpec((C, WIDTH), lambda b: (b + 1, PB_FOG)),
            pl.BlockSpec((C, LANES), blk),
            pl.BlockSpec((HEADS, C), lambda b: (0, b + 1)),
            pl.BlockSpec((1, past * HEADS, HEAD_DIM), lambda b: (b, 0, 0)),
            pl.BlockSpec((1, past * HEADS, HEAD_DIM), lambda b: (b, 0, 0)),
            pl.BlockSpec((1, past, HEADS), lambda b: (b, 0, 0)),
            pl.BlockSpec((1, HEADS, past), lambda b: (b, 0, 0)),
        ],
        out_specs=pl.BlockSpec((C, WIDTH), lambda b: (b, 0)),
        out_shape=jax.ShapeDtypeStruct((bs * C, WIDTH), BF16),
        compiler_params=_cparams(("arbitrary",)),
        name="fox_sample",
    )(q, kb, p, p, lf_col, lf_row, cache_k.reshape(bs, past * HEADS, HEAD_DIM),
      cache_v.reshape(bs, past * HEADS, HEAD_DIM), cache_lf, jnp.swapaxes(cache_lf, 1, 2))


def _outproj_kernel(og_ref, of_ref, x_ref, wo_ref, nw_ref, wr_ref, br_ref, cnt0_ref, xr_ref, rt_ref, cnt_ref,
                    cnt_scr):
    d = x_ref.shape[1]

    @pl.when(pl.program_id(0) == 0)
    def _():
        cnt_scr[...] = cnt0_ref[...]

    half = og_ref.shape[1]
    h = _dot(og_ref[...], wo_ref[0:half, :]) + _dot(of_ref[...], wo_ref[half:2 * half, :])
    x1 = x_ref[...] + h
    xr_ref[:, 0:d] = x1
    xn = x1 * lax.rsqrt(jnp.mean(x1 * x1, axis=-1, keepdims=True) + EPS) * nw_ref[...]

    x_hi = xn.astype(BF16)
    x_lo = (xn - x_hi.astype(F32)).astype(BF16)
    hi = _dot(x_hi, wr_ref[...])
    logits = (hi[:, 0:LANES] + hi[:, LANES:2 * LANES] + _dot(x_lo, wr_ref[:, 0:LANES])) + br_ref[...]

    tm = logits.shape[0]
    lane = _iota2((tm, LANES), 1).astype(F32)
    big = float(LANES)
    gl = jnp.where(lane < N_GROUPS, logits, NEG)
    gmax = jnp.max(gl, axis=-1, keepdims=True)
    gidx = jnp.min(jnp.where(gl == gmax, lane, big), axis=-1, keepdims=True)
    p_top = 1.0 / jnp.sum(jnp.exp(gl - gmax), axis=-1, keepdims=True)
    e = lane - RT_E0
    sel = (e >= 0) & (e < N_EXPERTS) & (jnp.floor(e * (1.0 / EXPERTS_PER_GROUP)) == gidx)
    el = jnp.where(sel, logits, NEG)
    v1 = jnp.max(el, axis=-1, keepdims=True)
    i1 = jnp.min(jnp.where(el == v1, lane, big), axis=-1, keepdims=True)
    el2 = jnp.where(lane == i1, NEG, el)
    v2 = jnp.max(el2, axis=-1, keepdims=True)
    i2 = jnp.min(jnp.where(el2 == v2, lane, big), axis=-1, keepdims=True)
    e2 = jnp.exp(v2 - v1)
    w1 = p_top / (1.0 + e2)
    w2 = p_top * e2 / (1.0 + e2)

    ex1 = i1 - RT_E0
    ex2 = i2 - RT_E0
    first = ex1 < ex2
    ea = jnp.where(first, ex1, ex2)
    eb = jnp.where(first, ex2, ex1)
    wa = jnp.where(first, w1, w2)
    wb = jnp.where(first, w2, w1)
    la = ea - gidx * EXPERTS_PER_GROUP
    lb = eb - gidx * EXPERTS_PER_GROUP
    cls = gidx * PAIRS_PER_GROUP + la * (2 * EXPERTS_PER_GROUP - 1 - la) * 0.5 + (lb - la - 1.0)

    onehot = lane == cls
    oh = jnp.where(onehot, 1.0, 0.0)
    strict_b = jnp.where(_iota2((tm, tm), 0) > _iota2((tm, tm), 1), 1.0, 0.0).astype(BF16)
    before = cnt_scr[...] + _dot(strict_b, oh.astype(BF16))
    rank = jnp.sum(jnp.where(onehot, before, 0.0), axis=-1, keepdims=True)
    cnt_scr[...] += jnp.sum(oh, axis=0, keepdims=True)
    cnt_ref[...] = cnt_scr[...]
    route = jnp.where(lane == RT_CLS, cls, jnp.where(lane == RT_RANK, rank, jnp.where(
        lane == RT_WA, wa, jnp.where(lane == RT_WB, wb, 0.0))))
    rt_ref[...] = route
    xr_ref[:, d:d + LANES] = route


def _outproj(og, of, x, wo, norm_w, wr, br, cnt0):
    n, d = x.shape
    tm = _pick(n, (512, 384, 256, 128))
    rowblk = lambda i: (i, 0)
    return pl.pallas_call(
        _outproj_kernel,
        grid=(n // tm,),
        in_specs=[
            pl.BlockSpec((tm, WIDTH), rowblk),
            pl.BlockSpec((tm, WIDTH), rowblk),
            pl.BlockSpec((tm, d), rowblk),
            pl.BlockSpec((2 * WIDTH, d), lambda i: (0, 0)),
            pl.BlockSpec((1, d), lambda i: (0, 0)),
            pl.BlockSpec((d, 2 * LANES), lambda i: (0, 0)),
            pl.BlockSpec((1, LANES), lambda i: (0, 0)),
            pl.BlockSpec((1, LANES), lambda i: (0, 0)),
        ],
        out_specs=[
            pl.BlockSpec((tm, d + LANES), rowblk),
            pl.BlockSpec((tm, LANES), rowblk),
            pl.BlockSpec((1, LANES), lambda i: (0, 0)),
        ],
        out_shape=[
            jax.ShapeDtypeStruct((n, d + LANES), F32),
            jax.ShapeDtypeStruct((n, LANES), F32),
            jax.ShapeDtypeStruct((1, LANES), F32),
        ],
        scratch_shapes=[pltpu.VMEM((1, LANES), F32)],
        compiler_params=_cparams(("arbitrary",)),
        name="outproj_router",
    )(og, of, x, wo, norm_w, wr, br, cnt0)


def _pos_rows(tm):
    return -(-(-(-tm // LANES)) // 8) * 8


def _tile_positions(pos, tm):
    nt = pos.shape[0] // tm
    rows = _pos_rows(tm)
    p = jnp.pad(pos.reshape(nt, tm), ((0, 0), (0, rows * LANES - tm)))
    return p.reshape(nt * rows, LANES)


def _row_copy_loops(tm, pos_smem, make_copy):
    def start_row(r, carry):
        base = pl.multiple_of(r * LANES, LANES)
        for c in range(LANES):
            make_copy(base + c, pos_smem[r, c]).start()
        return carry

    def wait(t, carry):
        make_copy(0, 0).wait()
        return carry

    lax.fori_loop(0, tm // LANES, start_row, 0)
    lax.fori_loop(0, tm, wait, 0, unroll=8)


def _dispatch_kernel(ends_ref, xr_ref, pos_hbm, *rest, zero_fill):
    if zero_fill:
        xs_hbm, pos_smem, zbuf, sem_idx, sem_fill, sem_rows = rest
    else:
        _, xs_hbm, pos_smem, sem_idx, sem_rows = rest
    i = pl.program_id(0)
    tm = xr_ref.shape[0]
    rows = pos_smem.shape[0]
    idx_cp = pltpu.make_async_copy(pos_hbm.at[pl.ds(pl.multiple_of(i * rows, 8), rows)], pos_smem, sem_idx)
    idx_cp.start()

    if zero_fill:
        @pl.when(i == 0)
        def _():
            zbuf[...] = jnp.zeros(zbuf.shape, F32)

            def fill(start):
                def body(c, carry):
                    lo = jnp.where(c == 0, 0, ends_ref[jnp.maximum(c - 1, 0)])
                    hi = ends_ref[c]

                    @pl.when(hi > lo)
                    def _():
                        cp = pltpu.make_async_copy(
                            zbuf, xs_hbm.at[pl.ds(pl.multiple_of(hi - MOE_TS, MOE_TS), MOE_TS)], sem_fill)
                        if start:
                            cp.start()
                        else:
                            cp.wait()
                    return carry
                lax.fori_loop(0, N_CLASSES, body, 0)

            fill(True)
            fill(False)

            last = ends_ref[N_CLASSES - 1]

            @pl.when((last // MOE_TS) % 2 == 1)
            def _():
                cp = pltpu.make_async_copy(zbuf, xs_hbm.at[pl.ds(pl.multiple_of(last, MOE_TS), MOE_TS)], sem_fill)
                cp.start()
                cp.wait()

    idx_cp.wait()
    _row_copy_loops(tm, pos_smem, lambda t, p: pltpu.make_async_copy(
        xr_ref.at[pl.ds(t, 1)], xs_hbm.at[pl.ds(p, 1)], sem_rows))


def _dispatch(ends, xr, pos, xs, *, n_sorted):
    n, dw = xr.shape
    tm = _pick(n, (2048, 1024, 1152, 512, 384, 256, 128))
    rows = _pos_rows(tm)
    zero_fill = xs is None
    any_spec = pl.BlockSpec(memory_space=pl.ANY)
    in_specs = [pl.BlockSpec((tm, dw), lambda i, ends_r: (i, 0)), any_spec]
    args = [xr, _tile_positions(pos, tm)]
    scratch = [pltpu.SMEM((rows, LANES), jnp.int32)]
    if zero_fill:
        scratch += [pltpu.VMEM((MOE_TS, dw), F32), pltpu.SemaphoreType.DMA, pltpu.SemaphoreType.DMA,
                    pltpu.SemaphoreType.DMA]
        aliases = {}
    else:
        in_specs.append(any_spec)
        args.append(xs)
        scratch += [pltpu.SemaphoreType.DMA, pltpu.SemaphoreType.DMA]
        aliases = {3: 0}
    return pl.pallas_call(
        functools.partial(_dispatch_kernel, zero_fill=zero_fill),
        grid_spec=pltpu.PrefetchScalarGridSpec(
            num_scalar_prefetch=1, grid=(n // tm,), in_specs=in_specs, out_specs=any_spec,
            scratch_shapes=scratch),
        out_shape=jax.ShapeDtypeStruct((n_sorted, dw), F32),
        input_output_aliases=aliases,
        compiler_params=pltpu.CompilerParams(dimension_semantics=("arbitrary",), vmem_limit_bytes=VMEM_LIMIT,
                                             has_side_effects=True),
        name="moe_dispatch",
    )(ends, *args)


def _ffn_kernel(ta_ref, tb_ref, nv_ref, xs_ref, *refs):
    w_refs, (nffn_ref, nfin_ref, ys_ref) = refs[:12], refs[12:]

    @pl.when(pl.program_id(0) < nv_ref[0])
    def _():
        d = ys_ref.shape[1]
        halves = range(2)
        rows = [slice(i * MOE_TS, (i + 1) * MOE_TS) for i in halves]
        w1a, w3a, w2a, w1b, w3b, w2b = ([w_refs[6 * i + j] for i in halves] for j in range(6))
        x1 = [xs_ref[rows[i], 0:d] for i in halves]
        route = [xs_ref[rows[i], d:d + LANES] for i in halves]
        xn = [(x1[i] * lax.rsqrt(jnp.mean(x1[i] * x1[i], axis=-1, keepdims=True) + EPS)
               * nffn_ref[...]).astype(BF16) for i in halves]
        ga = [_dot(xn[i], w1a[i][0]) for i in halves]
        gb = [_dot(xn[i], w1b[i][0]) for i in halves]
        ua = [_dot(xn[i], w3a[i][0]) for i in halves]
        ub = [_dot(xn[i], w3b[i][0]) for i in halves]
        ha = [(_silu(ga[i]) * ua[i] * route[i][:, RT_WA:RT_WA + 1]).astype(BF16) for i in halves]
        hb = [(_silu(gb[i]) * ub[i] * route[i][:, RT_WB:RT_WB + 1]).astype(BF16) for i in halves]
        x2 = [x1[i] + _dot(ha[i], w2a[i][0]) + _dot(hb[i], w2b[i][0]) for i in halves]
        for i in halves:
            ys_ref[rows[i], :] = (x2[i] * lax.rsqrt(jnp.mean(x2[i] * x2[i], axis=-1, keepdims=True) + EPS)
                                  * nfin_ref[...])


def _ffn(tile_a, tile_b, n_valid, xs, w1, w3, w2, nffn, nfin):
    ns, dw = xs.shape
    d = dw - LANES
    de = w1.shape[2]
    row = lambda t, ta, tb, nv: (jnp.minimum(t, nv[0] - 1), 0)
    const = lambda t, ta, tb, nv: (0, 0)
    w_specs = []
    for i in range(2):
        wa = lambda t, ta, tb, nv, i=i: (ta[2 * t + i], 0, 0)
        wb = lambda t, ta, tb, nv, i=i: (tb[2 * t + i], 0, 0)
        w_specs += [pl.BlockSpec((1, d, de), wa), pl.BlockSpec((1, d, de), wa), pl.BlockSpec((1, de, d), wa),
                    pl.BlockSpec((1, d, de), wb), pl.BlockSpec((1, d, de), wb), pl.BlockSpec((1, de, d), wb)]
    return pl.pallas_call(
        _ffn_kernel,
        grid_spec=pltpu.PrefetchScalarGridSpec(
            num_scalar_prefetch=3,
            grid=(ns // (2 * MOE_TS),),
            in_specs=[pl.BlockSpec((2 * MOE_TS, dw), row)] + w_specs
            + [pl.BlockSpec((1, d), const), pl.BlockSpec((1, d), const)],
            out_specs=pl.BlockSpec((2 * MOE_TS, d), row),
        ),
        out_shape=jax.ShapeDtypeStruct((ns, d), F32),
        compiler_params=_cparams(("arbitrary",)),
        name="moe_ffn",
    )(tile_a, tile_b, n_valid, xs, *([w1, w3, w2, w1, w3, w2] * 2), nffn, nfin)


def _unsort_kernel(pos_hbm, ys_hbm, y_ref, pos_smem, sem_idx, sem_rows):
    i = pl.program_id(0)
    tm = y_ref.shape[0]
    rows = pos_smem.shape[0]
    idx_cp = pltpu.make_async_copy(pos_hbm.at[pl.ds(pl.multiple_of(i * rows, 8), rows)], pos_smem, sem_idx)
    idx_cp.start()
    idx_cp.wait()
    _row_copy_loops(tm, pos_smem, lambda t, p: pltpu.make_async_copy(
        ys_hbm.at[pl.ds(p, 1)], y_ref.at[pl.ds(t, 1)], sem_rows))


def _unsort(ys, pos):
    n = pos.shape[0]
    d = ys.shape[1]
    tm = _pick(n, (2048, 1024, 1152, 512, 384, 256, 128))
    any_spec = pl.BlockSpec(memory_space=pl.ANY)
    return pl.pallas_call(
        _unsort_kernel,
        grid=(n // tm,),
        in_specs=[any_spec, any_spec],
        out_specs=pl.BlockSpec((tm, d), lambda i: (i, 0)),
        out_shape=jax.ShapeDtypeStruct((n, d), F32),
        scratch_shapes=[pltpu.SMEM((_pos_rows(tm), LANES), jnp.int32), pltpu.SemaphoreType.DMA,
                        pltpu.SemaphoreType.DMA],
        compiler_params=_cparams(("arbitrary",)),
        name="moe_unsort",
    )(_tile_positions(pos, tm), ys)


def _pair_tables():
    a, b = [], []
    for g in range(N_GROUPS):
        for la in range(EXPERTS_PER_GROUP):
            for lb in range(la + 1, EXPERTS_PER_GROUP):
                a.append(g * EXPERTS_PER_GROUP + la)
                b.append(g * EXPERTS_PER_GROUP + lb)
    return jnp.asarray(a, jnp.int32), jnp.asarray(b, jnp.int32)


def _moe(xr_list, rt_list, cnt, w1, w3, w2, nffn, nfin):
    n_total = sum(x.shape[0] for x in xr_list)
    n_sorted = (-(-n_total // MOE_TS) + N_CLASSES + 2) // 2 * 2 * MOE_TS
    counts = cnt[0, :N_CLASSES].astype(jnp.int32)
    padded = (counts + MOE_TS - 1) // MOE_TS * MOE_TS
    ends = jnp.cumsum(padded)
    offs = ends - padded
    classes = jnp.arange(N_CLASSES, dtype=jnp.int32)

    def position(rt):
        cls = rt[:, RT_CLS].astype(jnp.int32)
        return jnp.sum(jnp.where(cls[:, None] == classes, offs, 0), axis=1) + rt[:, RT_RANK].astype(jnp.int32)

    pos_list = [position(rt) for rt in rt_list]
    tile_start = jnp.arange(n_sorted // MOE_TS, dtype=jnp.int32) * MOE_TS
    tile_cls = jnp.minimum(jnp.sum((ends <= tile_start[:, None]).astype(jnp.int32), axis=1), N_CLASSES - 1)
    pair_a, pair_b = _pair_tables()
    n_valid = ((ends[N_CLASSES - 1] // MOE_TS + 1) // 2).reshape(1)
    xs = None
    for xr, pos in zip(xr_list, pos_list):
        xs = _dispatch(ends, xr, pos, xs, n_sorted=n_sorted)
    tile_is = tile_cls[:, None] == classes
    tile_a = jnp.sum(jnp.where(tile_is, pair_a, 0), axis=1)
    tile_b = jnp.sum(jnp.where(tile_is, pair_b, 0), axis=1)
    ys = _ffn(tile_a, tile_b, n_valid, xs, w1, w3, w2, nffn, nfin)
    return [_unsort(ys, pos) for pos in pos_list]


def kernel(x_prompt, x_sample, cache_fox_k, cache_fox_v, cache_fox_logf, state_gdn, state_gdn_conv, meta_tokens, norm_mix_w, w_in, gdn_conv_w, gdn_A_log, gdn_dt_bias, gdn_norm_w, fox_q_norm_w, fox_k_norm_w, fox_f_bias, w_out, norm_ffn_w, w_router_group, b_router_group, w_router_expert, b_router_expert, w_gate, w_up, w_down, norm_final_w):
    B, S, D = x_prompt.shape
    BS, LS, _ = x_sample.shape
    C = GDN_CHUNK
    assert w_in.shape[0] == 1, "single-layer step only"
    assert S % C == 0 and LS <= C and N_META <= C and meta_tokens.shape[0] == N_META

    wi = w_in[0]
    o = 0
    parts = {}
    for name, size in (("g_qkv", 3 * WIDTH), ("g_z", WIDTH), ("g_a", HEADS), ("g_b", HEADS),
                       ("f_qkv", 3 * WIDTH), ("f_og", WIDTH), ("f_f", HEADS)):
        parts[name] = wi[:, o:o + size]
        o += size
    wbig = jnp.concatenate([parts[k].astype(BF16) for k in ("g_qkv", "f_qkv", "g_z", "f_og")], axis=1)
    wsm_cols = jnp.concatenate([parts["f_f"], parts["g_a"], parts["g_b"]], axis=1)
    wsm = jnp.pad(wsm_cols, ((0, 0), (0, LANES - 3 * HEADS))).astype(BF16)
    wo = w_out[0].astype(BF16)
    wr32 = jnp.pad(jnp.concatenate([w_router_group[0], w_router_expert[0]], axis=1),
                   ((0, 0), (0, LANES - N_GROUPS - N_EXPERTS)))
    wr_hi = wr32.astype(BF16)
    wr = jnp.concatenate([wr_hi, (wr32 - wr_hi.astype(F32)).astype(BF16)], axis=1)
    br = jnp.pad(jnp.concatenate([b_router_group[0], b_router_expert[0]]),
                 (0, LANES - N_GROUPS - N_EXPERTS)).reshape(1, LANES)
    w1 = w_gate[0].astype(BF16)
    w3 = w_up[0].astype(BF16)
    w2 = w_down[0].astype(BF16)
    nmix = norm_mix_w[0].reshape(1, D)
    nffn = norm_ffn_w[0].reshape(1, D)
    nfin = norm_final_w.reshape(1, D)

    x_small = jnp.concatenate([
        jnp.pad(meta_tokens.astype(F32), ((0, C - N_META), (0, 0))),
        jnp.pad(x_sample, ((0, 0), (0, C - LS), (0, 0))).reshape(BS * C, D)], axis=0)
    xp = x_prompt.reshape(B * S, D)

    p_s, psm_s, psmt_s = _inproj(x_small, nmix, wbig, wsm)
    p_p, psm_p, psmt_p = _inproj(xp, nmix, wbig, wsm)

    s0_s = jnp.concatenate([jnp.zeros((1,) + state_gdn.shape[2:], F32), state_gdn[0]], axis=0)
    cb_s = jnp.concatenate([jnp.zeros((1,) + state_gdn_conv.shape[2:], F32), state_gdn_conv[0]], axis=0)
    gdn_args = (gdn_conv_w[0], gdn_A_log[0], gdn_dt_bias[0], gdn_norm_w[0])
    og_s, st_s, cv_s = _gdn(p_s, psm_s, psmt_s, s0_s, cb_s, *gdn_args, n_seq=1 + BS, l_valid=LS,
                            bcast_state=False)
    og_p, st_p, cv_p = _gdn(p_p, psm_p, psmt_p, st_s[0:1], cv_s[0:1], *gdn_args, n_seq=B, l_valid=C,
                            bcast_state=True)

    fox_args = (fox_q_norm_w[0], fox_k_norm_w[0], fox_f_bias[0])
    zc = jnp.zeros((1, 1, LANES), F32)
    zr = jnp.zeros((1, HEADS, 1), F32)
    k32_s, v32_s, lfc_s, lfr_s, fc_s, fr_s, qt_s, kcat_s, vt_s, q_s, kb_s = _foxprep(
        p_s, psm_s, psmt_s, zc, zr, *fox_args, n_seq=1 + BS, bcast_carry=True, row_major=True)
    c0c = fc_s[N_META - 1:N_META, :].reshape(1, 1, LANES)
    c0r = fr_s[:, N_META - 1:N_META].reshape(1, HEADS, 1)
    k32_p, v32_p, lfc_p, lfr_p, fc_p, fr_p, qt_p, kcat_p, vt_p = _foxprep(
        p_p, psm_p, psmt_p, c0c, c0r, *fox_args, n_seq=B, bcast_carry=True, row_major=False, lead=N_META)
    of_p = _fox_prompt(qt_p, kcat_p, vt_p, p_p, kcat_s, vt_s, n_seq=B)
    of_s = _fox_sample(q_s, kb_s, p_s, lfc_s, lfr_s, cache_fox_k[0], cache_fox_v[0], cache_fox_logf[0],
                       l_valid=LS)

    xr_p, rt_p, cnt_p = _outproj(og_p, of_p, xp, wo, nffn, wr, br, jnp.zeros((1, LANES), F32))
    xr_s, rt_s, cnt = _outproj(og_s[C:], of_s, x_small[C:], wo, nffn, wr, br, cnt_p)
    y_p, y_s = _moe([xr_p, xr_s], [rt_p, rt_s], cnt, w1, w3, w2, nffn, nfin)

    def prompt_kv(tab_p, tab_s):
        tab = _fill_lead(tab_p, tab_s[:N_META * HEADS], n_seq=B)
        return tab.reshape(1, B, N_META + S, HEADS, HEAD_DIM)

    def sample_kv(tab_s):
        return tab_s.reshape(1 + BS, C, HEADS, HEAD_DIM)[1:, :LS][None]

    y_prompt = y_p.reshape(B, S, D)
    y_sample = y_s.reshape(BS, C, D)[:, :LS]
    fk_p = prompt_kv(k32_p, k32_s)
    fv_p = prompt_kv(v32_p, v32_s)
    lf_p = _assemble_logf(lfc_s, lfc_p, n_seq=B)[None]
    fk_s = sample_kv(k32_s)
    fv_s = sample_kv(v32_s)
    lf_s = lfc_s[C:, SM_F:SM_F + HEADS].reshape(BS, C, HEADS)[:, :LS][None]
    return (y_prompt, y_sample, fk_p, fv_p, lf_p, st_p[None], cv_p[None],
            fk_s, fv_s, lf_s, st_s[1:][None], cv_s[1:][None])
```

```python
import functools

import jax
import jax.numpy as jnp
from jax import lax
from jax.experimental import pallas as pl
from jax.experimental.pallas import tpu as pltpu

F32 = jnp.float32
BF16 = jnp.bfloat16
EPS = 1e-6
NEG = -1e30
LOG2E = 1.4426950408889634

N_META = 16
HEADS = 8
HEAD_DIM = 128
WIDTH = HEADS * HEAD_DIM
CONV_WIDTH = 4
N_GROUPS = 4
EXPERTS_PER_GROUP = 8
N_EXPERTS = N_GROUPS * EXPERTS_PER_GROUP
GDN_CHUNK = 128
LANES = 128
SM_F, SM_A, SM_B = 0, 8, 16
PB_FQ, PB_FK, PB_FV, PB_GZ, PB_FOG = 3, 4, 5, 6, 7
RT_E0 = N_GROUPS
PAIRS_PER_GROUP = EXPERTS_PER_GROUP * (EXPERTS_PER_GROUP - 1) // 2
N_CLASSES = N_GROUPS * PAIRS_PER_GROUP
RT_CLS, RT_RANK, RT_WA, RT_WB = 0, 1, 2, 3
MOE_TS = 256

VMEM_LIMIT = 56 * 1024 * 1024


def _cparams(sem):
    return pltpu.CompilerParams(dimension_semantics=sem, vmem_limit_bytes=VMEM_LIMIT)


def _pick(n, prefs):
    for p in prefs:
        if n % p == 0:
            return p
    raise ValueError(f"no tile in {prefs} divides {n}")


def _dot(a, b):
    return jnp.dot(a, b, preferred_element_type=F32)


def _dot_nt(a, b):
    return lax.dot_general(a, b, (((1,), (1,)), ((), ())), preferred_element_type=F32)


def _dot_tn(a, b):
    return lax.dot_general(a, b, (((0,), (0,)), ((), ())), preferred_element_type=F32)


def _split3(x):
    x1 = x.astype(BF16)
    r1 = x - x1.astype(F32)
    x2 = r1.astype(BF16)
    x3 = (r1 - x2.astype(F32)).astype(BF16)
    return x1, x2, x3


def _mask_dot(mask_bf16, x):
    x1, x2, x3 = _split3(x)
    return _dot(mask_bf16, x1) + _dot(mask_bf16, x2) + _dot(mask_bf16, x3)


def _dot_mask(x, mask_bf16):
    x1, x2, x3 = _split3(x)
    return _dot(x1, mask_bf16) + _dot(x2, mask_bf16) + _dot(x3, mask_bf16)


def _softplus(x):
    return jnp.maximum(x, 0.0) + jnp.log1p(jnp.exp(-jnp.abs(x)))


def _sigmoid(x):
    return 1.0 / (1.0 + jnp.exp(-x))


def _silu(x):
    return x * _sigmoid(x)


def _iota2(shape, dim):
    return lax.broadcasted_iota(jnp.int32, shape, dim)


def _inproj_kernel(x_ref, nw_ref, wbig_ref, wsm_ref, p_ref, psm_ref, psmt_ref, xn_scr):
    @pl.when(pl.program_id(1) == 0)
    def _():
        x = x_ref[...]
        xn = x * lax.rsqrt(jnp.mean(x * x, axis=-1, keepdims=True) + EPS) * nw_ref[...]
        xnb = xn.astype(BF16)
        xn_scr[...] = xnb
        psm = _dot(xnb, wsm_ref[...])
        psm_ref[...] = psm
        psmt_ref[...] = psm.T[0:psmt_ref.shape[0], :]

    p_ref[...] = _dot(xn_scr[...], wbig_ref[...]).astype(BF16)


def _inproj(x, norm_w, wbig, wsm):
    n, d = x.shape
    tm = _pick(n, (1024, 1152, 512, 384, 256, 128))
    tn = 2048
    nproj = wbig.shape[1]
    return pl.pallas_call(
        _inproj_kernel,
        grid=(n // tm, nproj // tn),
        in_specs=[
            pl.BlockSpec((tm, d), lambda i, j: (i, 0)),
            pl.BlockSpec((1, d), lambda i, j: (0, 0)),
            pl.BlockSpec((d, tn), lambda i, j: (0, j)),
            pl.BlockSpec((d, LANES), lambda i, j: (0, 0)),
        ],
        out_specs=[
            pl.BlockSpec((tm, tn), lambda i, j: (i, j)),
            pl.BlockSpec((tm, LANES), lambda i, j: (i, 0)),
            pl.BlockSpec((32, tm), lambda i, j: (0, i)),
        ],
        out_shape=[
            jax.ShapeDtypeStruct((n, nproj), BF16),
            jax.ShapeDtypeStruct((n, LANES), F32),
            jax.ShapeDtypeStruct((32, n), F32),
        ],
        scratch_shapes=[pltpu.VMEM((tm, d), BF16)],
        compiler_params=_cparams(("arbitrary", "arbitrary")),
        name="inproj",
    )(x, norm_w, wbig, wsm)


def _gdn_kernel(qkv_ref, z_ref, sm_ref, smt_ref, s0_ref, cb_ref, convw_ref, alog_c_ref, dtb_c_ref,
                alog_r_ref, dtb_r_ref, normw_ref, o_ref, snew_ref, cnew_ref, s_scr, xp_scr, conv_scr, *,
                l_valid):
    C = GDN_CHUNK
    n_sub = qkv_ref.shape[0] // C

    @pl.when(pl.program_id(1) == 0)
    def _():
        s_scr[...] = s0_ref[0]
        xp_scr[5:8, :] = cb_ref[0]

    row = _iota2((C, C), 0)
    col = _iota2((C, C), 1)
    incl = row >= col
    strict = row > col
    tril_b = jnp.where(incl, 1.0, 0.0).astype(BF16)
    triu_b = jnp.where(row <= col, 1.0, 0.0).astype(BF16)
    shift_b = jnp.concatenate(
        [jnp.where(col == row - i, 1.0, 0.0).astype(BF16) for i in (1, 2, 3)], axis=0)
    inv_levels = []
    sh = 1
    while (1 << sh) < C:
        inv_levels.append((((row >> sh) & 1) == 1) & ((col >> (sh + 1)) == (row >> (sh + 1)))
                          & (((col >> sh) & 1) == 0))
        sh += 1
    m1 = ((row & 1) == 1) & (col == row - 1)
    eye = jnp.where(row == col, 1.0, 0.0)
    w = convw_ref[...]
    normw = normw_ref[...]

    act, gl_col, gl_row, beta_col = [], [], [], []
    for sub in range(n_sub):
        rows = slice(sub * C, (sub + 1) * C)
        xb = qkv_ref[rows, :]
        x = xb.astype(F32)
        xp_scr[8:16, :] = x[0:8]
        shifted = _dot(shift_b, xb)
        conv_scr[rows, :] = (w[3:4] * x + w[2:3] * shifted[0:C] + w[1:2] * shifted[C:2 * C]
                             + w[0:1] * shifted[2 * C:3 * C])
        conv_scr[sub * C:sub * C + 8, :] = (w[0:1] * xp_scr[5:13, :] + w[1:2] * xp_scr[6:14, :]
                                            + w[2:3] * xp_scr[7:15, :] + w[3:4] * x[0:8])
        act.append(_silu(conv_scr[rows, :]))
        tail = x[l_valid - 3:l_valid, :]
        xp_scr[5:8, :] = tail
        cnew_ref[0] = tail

        sm = sm_ref[rows, :]
        g_col = -jnp.exp(alog_c_ref[...]) * _softplus(sm + dtb_c_ref[...])
        g_row = -jnp.exp(alog_r_ref[...]) * _softplus(smt_ref[:, rows] + dtb_r_ref[...])
        b_col = _sigmoid(sm)
        if l_valid < C:
            g_col = jnp.where(_iota2((C, LANES), 0) < l_valid, g_col, 0.0)
            b_col = jnp.where(_iota2((C, LANES), 0) < l_valid, b_col, 0.0)
            g_row = jnp.where(_iota2((HEADS, C), 1) < l_valid, g_row, 0.0)
        beta_col.append(b_col)
        gl_col.append(_mask_dot(tril_b, g_col))
        gl_row.append(_dot_mask(g_row, triu_b))

    hs = [slice(h * HEAD_DIM, (h + 1) * HEAD_DIM) for h in range(HEADS)]
    I = [(sub, h) for sub in range(n_sub) for h in range(HEADS)]
    gc = {(c, h): gl_col[c][:, SM_A + h:SM_A + h + 1] for c, h in I}
    beta = {(c, h): beta_col[c][:, SM_B + h:SM_B + h + 1] for c, h in I}
    qn, kn, kb, knb = {}, {}, {}, {}
    for c, h in I:
        q = act[c][:, hs[h]]
        k = act[c][:, WIDTH + h * HEAD_DIM:WIDTH + (h + 1) * HEAD_DIM]
        qn[c, h] = q * lax.rsqrt(jnp.sum(q * q, axis=-1, keepdims=True) + EPS) * (HEAD_DIM ** -0.5)
        kn[c, h] = k * lax.rsqrt(jnp.sum(k * k, axis=-1, keepdims=True) + EPS)
        kb[c, h] = kn[c, h] * beta[c, h]
        knb[c, h] = kn[c, h].astype(BF16)
    decay = {i: jnp.exp(jnp.where(incl, gc[i] - gl_row[i[0]][i[1]:i[1] + 1, :], NEG)) for i in I}
    a_mat = {i: jnp.where(strict, _dot_nt(kb[i].astype(BF16), knb[i]) * decay[i], 0.0) for i in I}
    qk = {i: (_dot_nt(qn[i].astype(BF16), knb[i]) * decay[i]).astype(BF16) for i in I}

    t = {i: eye - jnp.where(m1, a_mat[i], 0.0) for i in I}
    for m in inv_levels:
        tb = {i: t[i].astype(BF16) for i in I}
        y = {i: _dot(tb[i], jnp.where(m, a_mat[i], 0.0).astype(BF16)).astype(BF16) for i in I}
        t = {i: t[i] - _dot(y[i], tb[i]) for i in I}
    tb = {i: t[i].astype(BF16) for i in I}
    eg = {i: jnp.exp(gc[i]) for i in I}
    g_last = {i: gc[i][C - 1:C, :] for i in I}

    for c in range(n_sub):
        H = [(c, h) for h in range(HEADS)]
        s = {i: s_scr[i[1]] for i in H}
        sb = {i: s[i].astype(BF16) for i in H}
        r = {i: (act[c][:, 2 * WIDTH + i[1] * HEAD_DIM:2 * WIDTH + (i[1] + 1) * HEAD_DIM] * beta[i]
                 - _dot((kb[i] * eg[i]).astype(BF16), sb[i])).astype(BF16) for i in H}
        ub = {i: _dot(tb[i], r[i]).astype(BF16) for i in H}
        o = {i: _dot((qn[i] * eg[i]).astype(BF16), sb[i]) + _dot(qk[i], ub[i]) for i in H}
        for i in H:
            k_dec = (kn[i] * jnp.exp(g_last[i] - gc[i])).astype(BF16)
            s_scr[i[1]] = s[i] * jnp.exp(g_last[i]) + _dot_tn(k_dec, ub[i])
        for i in H:
            on = o[i] * lax.rsqrt(jnp.mean(o[i] * o[i], axis=-1, keepdims=True) + EPS) * normw
            z = z_ref[c * C:(c + 1) * C, hs[i[1]]].astype(F32)
            o_ref[c * C:(c + 1) * C, hs[i[1]]] = (on * _silu(z)).astype(BF16)

    snew_ref[0] = s_scr[...]


def _gdn(p, psm, psmt, s0, cb, conv_w, alog, dtb, norm_w, *, n_seq, l_valid, bcast_state):
    n = p.shape[0]
    C = GDN_CHUNK
    n_sub = _pick(n // (n_seq * C), (2, 1))
    R = n_sub * C
    nc = n // (n_seq * R)
    alog_c = jnp.zeros((1, LANES), F32).at[0, SM_A:SM_A + HEADS].set(alog)
    dtb_c = jnp.zeros((1, LANES), F32).at[0, SM_A:SM_A + HEADS].set(dtb)
    st_idx = (lambda b, c: (0, 0, 0, 0)) if bcast_state else (lambda b, c: (b, 0, 0, 0))
    cb_idx = (lambda b, c: (0, 0, 0)) if bcast_state else (lambda b, c: (b, 0, 0))
    return pl.pallas_call(
        functools.partial(_gdn_kernel, l_valid=l_valid),
        grid=(n_seq, nc),
        in_specs=[
            pl.BlockSpec((R, 3 * WIDTH), lambda b, c: (b * nc + c, 0)),
            pl.BlockSpec((R, WIDTH), lambda b, c: (b * nc + c, PB_GZ)),
            pl.BlockSpec((R, LANES), lambda b, c: (b * nc + c, 0)),
            pl.BlockSpec((HEADS, R), lambda b, c: (SM_A // HEADS, b * nc + c)),
            pl.BlockSpec((1, HEADS, HEAD_DIM, HEAD_DIM), st_idx),
            pl.BlockSpec((1, CONV_WIDTH - 1, 3 * WIDTH), cb_idx),
            pl.BlockSpec((CONV_WIDTH, 3 * WIDTH), lambda b, c: (0, 0)),
            pl.BlockSpec((1, LANES), lambda b, c: (0, 0)),
            pl.BlockSpec((1, LANES), lambda b, c: (0, 0)),
            pl.BlockSpec((HEADS, 1), lambda b, c: (0, 0)),
            pl.BlockSpec((HEADS, 1), lambda b, c: (0, 0)),
            pl.BlockSpec((1, HEAD_DIM), lambda b, c: (0, 0)),
        ],
        out_specs=[
            pl.BlockSpec((R, WIDTH), lambda b, c: (b * nc + c, 0)),
            pl.BlockSpec((1, HEADS, HEAD_DIM, HEAD_DIM), lambda b, c: (b, 0, 0, 0)),
            pl.BlockSpec((1, CONV_WIDTH - 1, 3 * WIDTH), lambda b, c: (b, 0, 0)),
        ],
        out_shape=[
            jax.ShapeDtypeStruct((n, WIDTH), BF16),
            jax.ShapeDtypeStruct((n_seq, HEADS, HEAD_DIM, HEAD_DIM), F32),
            jax.ShapeDtypeStruct((n_seq, CONV_WIDTH - 1, 3 * WIDTH), F32),
        ],
        scratch_shapes=[
            pltpu.VMEM((HEADS, HEAD_DIM, HEAD_DIM), F32),
            pltpu.VMEM((16, 3 * WIDTH), F32),
            pltpu.VMEM((R, 3 * WIDTH), F32),
        ],
        compiler_params=_cparams(("arbitrary", "arbitrary")),
        name="gdn",
    )(p, p, psm, psmt, s0, cb, conv_w, alog_c, dtb_c, alog.reshape(HEADS, 1), dtb.reshape(HEADS, 1),
      norm_w.reshape(1, HEAD_DIM))


def _foxprep_kernel(qkv_ref, sm_ref, smt_ref, c0c_ref, c0r_ref, qw_ref, kw_ref, fb_c_ref, fb_r_ref,
                    k32_ref, v32_ref, lfc_ref, lfr_ref, fc_ref, fr_ref, qt_ref, kcat_ref, vt_ref, *rest,
                    row_major):
    if row_major:
        q_ref, kb_ref, cc_scr, cr_scr = rest
    else:
        cc_scr, cr_scr = rest
    tm = qkv_ref.shape[0]

    @pl.when(pl.program_id(1) == 0)
    def _():
        cc_scr[...] = c0c_ref[0]
        cr_scr[...] = c0r_ref[0]

    lf_col = -_softplus(-(sm_ref[...] + fb_c_ref[...]))
    lf_row = -_softplus(-(smt_ref[...] + fb_r_ref[...]))
    lfc_ref[...] = lf_col
    lfr_ref[...] = lf_row
    row = _iota2((tm, tm), 0)
    col = _iota2((tm, tm), 1)
    tril_b = jnp.where(row >= col, 1.0, 0.0).astype(BF16)
    triu_b = jnp.where(row <= col, 1.0, 0.0).astype(BF16)
    f_col = cc_scr[...] + _mask_dot(tril_b, lf_col)
    f_row = cr_scr[...] + _dot_mask(lf_row, triu_b)
    fc_ref[...] = f_col
    fr_ref[...] = f_row
    cc_scr[...] = f_col[tm - 1:tm, :]
    cr_scr[...] = f_row[:, tm - 1:tm]

    er = _iota2((LANES, LANES), 0) - SM_F
    ec = _iota2((LANES, LANES), 1)
    bias_k = jnp.where((_iota2((1, LANES), 1) & (FOX_BIAS - 1)) < 3, 1.0, 0.0)
    for j, c in enumerate(_split3(f_col * LOG2E)):
        move = jnp.where((er >= 0) & (er < HEADS) & (ec == FOX_BIAS * er + 3 + j), -1.0, 0.0).astype(BF16)
        bias_k = bias_k + _dot(c, move)
    kcat_ref[:, WIDTH:WIDTH + LANES] = bias_k.astype(BF16)

    qw = qw_ref[...]
    kw = kw_ref[...]
    sub = _iota2((FOX_BIAS, tm), 0)
    zeros_t = jnp.zeros((HEAD_DIM, tm), BF16)
    for h in range(HEADS):
        hs = slice(h * HEAD_DIM, (h + 1) * HEAD_DIM)
        q = qkv_ref[:, hs].astype(F32)
        k = qkv_ref[:, WIDTH + h * HEAD_DIM:WIDTH + (h + 1) * HEAD_DIM].astype(F32)
        v = qkv_ref[:, 2 * WIDTH + h * HEAD_DIM:2 * WIDTH + (h + 1) * HEAD_DIM].astype(F32)
        qn = q * lax.rsqrt(jnp.mean(q * q, axis=-1, keepdims=True) + EPS) * qw
        kn = k * lax.rsqrt(jnp.mean(k * k, axis=-1, keepdims=True) + EPS) * kw
        k32_ref[pl.ds(h, tm, stride=HEADS), :] = kn
        v32_ref[pl.ds(h, tm, stride=HEADS), :] = v
        if row_major:
            q_ref[:, hs] = (qn * (HEAD_DIM ** -0.5)).astype(BF16)
            kb_ref[:, hs] = kn.astype(BF16)
        base = 2 * h * HEAD_DIM
        qt_ref[base:base + HEAD_DIM, :] = (qn * (LOG2E * HEAD_DIM ** -0.5)).T.astype(BF16)
        r1, r2, r3 = _split3(f_row[h:h + 1, :] * LOG2E)
        aug_q = jnp.where(sub == 0, r1.astype(F32), jnp.where(sub == 1, r2.astype(F32), jnp.where(
            sub == 2, r3.astype(F32), jnp.where(sub < 6, 1.0, 0.0))))
        qt_ref[base + HEAD_DIM:base + 2 * HEAD_DIM, :] = zeros_t
        qt_ref[base + HEAD_DIM + FOX_BIAS * h:base + HEAD_DIM + FOX_BIAS * (h + 1), :] = aug_q.astype(BF16)
        kcat_ref[:, hs] = kn.astype(BF16)
        vt_ref[hs, :] = v.T.astype(BF16)


def _foxprep(p, psm, psmt, c0c, c0r, q_norm_w, k_norm_w, f_bias, *, n_seq, bcast_carry, row_major, lead=0):
    n = p.shape[0]
    rows = n // n_seq
    tm = _pick(rows, (512, 256, 128))
    nt = rows // tm
    fb_c = jnp.zeros((1, LANES), F32).at[0, SM_F:SM_F + HEADS].set(f_bias)
    c_idx = (lambda b, t: (0, 0, 0)) if bcast_carry else (lambda b, t: (b, 0, 0))
    rowblk = lambda b, t: (b * nt + t, 0)
    colblk = lambda b, t: (0, b * nt + t)
    kv_spec = pl.BlockSpec((pl.Element(tm * HEADS), pl.Element(HEAD_DIM)),
                           lambda b, t: ((b * (rows + lead) + lead + t * tm) * HEADS, 0))
    kv_shape = jax.ShapeDtypeStruct((n_seq * (rows + lead) * HEADS, HEAD_DIM), F32)
    out_specs = [
        kv_spec,
        kv_spec,
        pl.BlockSpec((tm, LANES), rowblk),
        pl.BlockSpec((HEADS, tm), colblk),
        pl.BlockSpec((tm, LANES), rowblk),
        pl.BlockSpec((HEADS, tm), colblk),
        pl.BlockSpec((2 * WIDTH, tm), colblk),
        pl.BlockSpec((tm, WIDTH + LANES), rowblk),
        pl.BlockSpec((WIDTH, tm), colblk),
    ]
    out_shape = [
        kv_shape,
        kv_shape,
        jax.ShapeDtypeStruct((n, LANES), F32),
        jax.ShapeDtypeStruct((HEADS, n), F32),
        jax.ShapeDtypeStruct((n, LANES), F32),
        jax.ShapeDtypeStruct((HEADS, n), F32),
        jax.ShapeDtypeStruct((2 * WIDTH, n), BF16),
        jax.ShapeDtypeStruct((n, WIDTH + LANES), BF16),
        jax.ShapeDtypeStruct((WIDTH, n), BF16),
    ]
    if row_major:
        out_specs += [pl.BlockSpec((tm, WIDTH), rowblk), pl.BlockSpec((tm, WIDTH), rowblk)]
        out_shape += [jax.ShapeDtypeStruct((n, WIDTH), BF16),
                      jax.ShapeDtypeStruct((n, WIDTH), BF16)]
    return pl.pallas_call(
        functools.partial(_foxprep_kernel, row_major=row_major),
        grid=(n_seq, nt),
        in_specs=[
            pl.BlockSpec((tm, 3 * WIDTH), lambda b, t: (b * nt + t, 1)),
            pl.BlockSpec((tm, LANES), rowblk),
            pl.BlockSpec((HEADS, tm), lambda b, t: (SM_F // HEADS, b * nt + t)),
            pl.BlockSpec((1, 1, LANES), c_idx),
            pl.BlockSpec((1, HEADS, 1), c_idx),
            pl.BlockSpec((1, HEAD_DIM), lambda b, t: (0, 0)),
            pl.BlockSpec((1, HEAD_DIM), lambda b, t: (0, 0)),
            pl.BlockSpec((1, LANES), lambda b, t: (0, 0)),
            pl.BlockSpec((HEADS, 1), lambda b, t: (0, 0)),
        ],
        out_specs=out_specs,
        out_shape=out_shape,
        scratch_shapes=[pltpu.VMEM((1, LANES), F32), pltpu.VMEM((HEADS, 1), F32)],
        compiler_params=_cparams(("arbitrary", "arbitrary")),
        name="foxprep",
    )(p, psm, psmt, c0c, c0r, q_norm_w.reshape(1, HEAD_DIM), k_norm_w.reshape(1, HEAD_DIM), fb_c,
      f_bias.reshape(HEADS, 1))


def _fill_lead_kernel(src_ref, big_ref, o_ref):
    o_ref[...] = src_ref[...]


def _fill_lead(big, src, *, n_seq):
    r = src.shape[0]
    seq_rows = big.shape[0] // n_seq
    assert seq_rows % r == 0
    return pl.pallas_call(
        _fill_lead_kernel,
        grid=(n_seq,),
        in_specs=[pl.BlockSpec((r, HEAD_DIM), lambda b: (0, 0)), pl.BlockSpec(memory_space=pl.ANY)],
        out_specs=pl.BlockSpec((r, HEAD_DIM), lambda b: (b * (seq_rows // r), 0)),
        out_shape=jax.ShapeDtypeStruct(big.shape, big.dtype),
        input_output_aliases={1: 0},
        compiler_params=_cparams(("arbitrary",)),
        name="fill_lead",
    )(src, big)


def _logf_kernel(meta_ref, lf_ref, o_ref):
    o_ref[0, 0:N_META, :] = meta_ref[0:N_META, SM_F:SM_F + HEADS]
    o_ref[0, N_META:, :] = lf_ref[:, SM_F:SM_F + HEADS]


def _assemble_logf(lf_small, lf_prompt, *, n_seq):
    s = lf_prompt.shape[0] // n_seq
    return pl.pallas_call(
        _logf_kernel,
        grid=(n_seq,),
        in_specs=[pl.BlockSpec((GDN_CHUNK, LANES), lambda b: (0, 0)), pl.BlockSpec((s, LANES), lambda b: (b, 0))],
        out_specs=pl.BlockSpec((1, N_META + s, HEADS), lambda b: (b, 0, 0)),
        out_shape=jax.ShapeDtypeStruct((n_seq, N_META + s, HEADS), F32),
        compiler_params=_cparams(("arbitrary",)),
        name="assemble_logf",
    )(lf_small, lf_prompt)


FOX_UNIT = 256
FOX_BIAS = 16
FOX_SUM_ROWS = 16

def _fox_kernel(qi_ref, ki_ref, qt_ref, kcat_ref, vt_ref, og_ref, km_ref, vtm_ref, o_ref, m_scr, acc_scr):
    pair = pl.program_id(1)
    qi = qi_ref[pair]
    ki = ki_ref[pair]
    tq = qt_ref.shape[1]
    tk = kcat_ref.shape[0]
    U = FOX_UNIT
    H = range(HEADS)

    def unit(qh, k_rows, vt_cols, mask):
        qs = slice(qh * U, (qh + 1) * U)
        s = [_dot(k_rows(h), qt_ref[2 * h * HEAD_DIM:2 * (h + 1) * HEAD_DIM, qs]) for h in H]
        if mask is not None:
            s = [jnp.where(mask, s[h], NEG) for h in H]
        m_old = [m_scr[h:h + 1, qs] for h in H]
        m_new = [jnp.maximum(m_old[h], jnp.max(s[h], axis=0, keepdims=True)) for h in H]
        alpha = [jnp.exp2(m_old[h] - m_new[h]) for h in H]
        p = [jnp.exp2((s[h] - m_new[h]).astype(BF16)) for h in H]
        for h in H:
            m_scr[h:h + 1, qs] = m_new[h]
        for h in H:
            lhs = jnp.concatenate([vt_cols(h), jnp.ones((FOX_SUM_ROWS, p[h].shape[0]), BF16)], axis=0)
            acc_scr[h, :, qs] = alpha[h] * acc_scr[h, :, qs] + _dot(lhs, p[h])

    def k_unit(r0):
        return lambda h: jnp.concatenate([kcat_ref[r0:r0 + U, h * HEAD_DIM:(h + 1) * HEAD_DIM],
                                          kcat_ref[r0:r0 + U, WIDTH:WIDTH + LANES]], axis=1)

    def vt_unit(r0):
        return lambda h: vt_ref[h * HEAD_DIM:(h + 1) * HEAD_DIM, r0:r0 + U]

    @pl.when(ki == 0)
    def _():
        m_scr[...] = jnp.full(m_scr.shape, NEG, F32)
        acc_scr[...] = jnp.zeros(acc_scr.shape, F32)
        tmeta = km_ref.shape[0]
        mask = _iota2((tmeta, U), 0) < N_META
        for qh in range(tq // U):
            unit(qh, lambda h: jnp.concatenate([km_ref[:, h * HEAD_DIM:(h + 1) * HEAD_DIM],
                                                km_ref[:, WIDTH:WIDTH + LANES]], axis=1),
                 lambda h: vtm_ref[h * HEAD_DIM:(h + 1) * HEAD_DIM, :], mask)

    kb = tk // tq
    for j in range(kb):
        g = ki * kb + j

        @pl.when(g < qi)
        def _(j=j):
            for qh in range(tq // U):
                for ku in range(tq // U):
                    unit(qh, k_unit(j * tq + ku * U), vt_unit(j * tq + ku * U), None)

        @pl.when(g == qi)
        def _(j=j):
            diag = _iota2((U, U), 0) <= _iota2((U, U), 1)
            for qh in range(tq // U):
                for ku in range(qh + 1):
                    unit(qh, k_unit(j * tq + ku * U), vt_unit(j * tq + ku * U), diag if ku == qh else None)

    @pl.when(ki == qi // kb)
    def _():
        for h in H:
            hs = slice(h * HEAD_DIM, (h + 1) * HEAD_DIM)
            gate = _sigmoid(og_ref[:, hs].astype(F32))
            o_ref[:, hs] = ((acc_scr[h, 0:HEAD_DIM, :] / acc_scr[h, HEAD_DIM:HEAD_DIM + 1, :]).T * gate).astype(BF16)


def _fox_prompt(qt, kcat, vt, p, kcat_small, vt_small, *, n_seq):
    n = p.shape[0]
    rows = n // n_seq
    tq = _pick(rows, (512, 256))
    nq = rows // tq
    kb = _pick(nq, (2, 1))
    tk, nk = kb * tq, nq // kb
    pairs = [(i, j) for i in range(nq) for j in range(i // kb + 1)]
    qi = jnp.asarray([a for a, _ in pairs], jnp.int32)
    ki = jnp.asarray([b for _, b in pairs], jnp.int32)
    grid_spec = pltpu.PrefetchScalarGridSpec(
        num_scalar_prefetch=2,
        grid=(n_seq, len(pairs)),
        in_specs=[
            pl.BlockSpec((2 * WIDTH, tq), lambda b, t, qi_r, ki_r: (0, b * nq + qi_r[t])),
            pl.BlockSpec((tk, WIDTH + LANES), lambda b, t, qi_r, ki_r: (b * nk + ki_r[t], 0)),
            pl.BlockSpec((WIDTH, tk), lambda b, t, qi_r, ki_r: (0, b * nk + ki_r[t])),
            pl.BlockSpec((tq, WIDTH), lambda b, t, qi_r, ki_r: (b * nq + qi_r[t], PB_FOG)),
            pl.BlockSpec((GDN_CHUNK, WIDTH + LANES), lambda b, t, qi_r, ki_r: (0, 0)),
            pl.BlockSpec((WIDTH, GDN_CHUNK), lambda b, t, qi_r, ki_r: (0, 0)),
        ],
        out_specs=pl.BlockSpec((tq, WIDTH), lambda b, t, qi_r, ki_r: (b * nq + qi_r[t], 0)),
        scratch_shapes=[
            pltpu.VMEM((HEADS, tq), F32),
            pltpu.VMEM((HEADS, HEAD_DIM + FOX_SUM_ROWS, tq), F32),
        ],
    )
    return pl.pallas_call(
        _fox_kernel,
        grid_spec=grid_spec,
        out_shape=jax.ShapeDtypeStruct((n, WIDTH), BF16),
        compiler_params=_cparams(("arbitrary", "arbitrary")),
        name="fox_prompt",
    )(qi, ki, qt, kcat, vt, p, kcat_small, vt_small)


def _fox_sample_kernel(q_ref, kn_ref, vn_ref, og_ref, lfc_ref, lfr_ref, ck_ref, cv_ref, clc_ref, clr_ref,
                       o_ref, *, l_valid):
    C = q_ref.shape[0]
    P = ck_ref.shape[1] // HEADS
    rowp = _iota2((P, P), 0)
    colp = _iota2((P, P), 1)
    triu_p = jnp.where(rowp <= colp, 1.0, 0.0).astype(BF16)
    row = _iota2((C, C), 0)
    col = _iota2((C, C), 1)
    tril_b = jnp.where(row >= col, 1.0, 0.0).astype(BF16)
    triu_b = jnp.where(row <= col, 1.0, 0.0).astype(BF16)

    f_cache = _dot_mask(clr_ref[0], triu_p)
    carry_r = f_cache[:, P - 1:P]
    carry_c = jnp.sum(clc_ref[0], axis=0, keepdims=True)
    f_new_r = carry_r + _dot_mask(lfr_ref[...], triu_b)
    f_new_c = carry_c + _mask_dot(tril_b, lfc_ref[...])[:, SM_F:SM_F + HEADS]
    mask_new = (col <= row) & (col < l_valid)

    for h in range(HEADS):
        hs = slice(h * HEAD_DIM, (h + 1) * HEAD_DIM)
        q = q_ref[:, hs]
        fq = f_new_c[:, h:h + 1]
        ck = ck_ref[0, pl.ds(h, P, stride=HEADS), :].astype(BF16)
        cv = cv_ref[0, pl.ds(h, P, stride=HEADS), :].astype(BF16)
        s_c = _dot_nt(q, ck) + (fq - f_cache[h:h + 1, :])
        s_n = jnp.where(mask_new, _dot_nt(q, kn_ref[:, hs]) + (fq - f_new_r[h:h + 1, :]), NEG)
        m = jnp.maximum(jnp.max(s_c, axis=-1, keepdims=True), jnp.max(s_n, axis=-1, keepdims=True))
        p_c = jnp.exp(s_c - m)
        p_n = jnp.exp(s_n - m)
        l = jnp.sum(p_c, axis=-1, keepdims=True) + jnp.sum(p_n, axis=-1, keepdims=True)
        o = _dot(p_c.astype(BF16), cv) + _dot(p_n.astype(BF16), vn_ref[:, hs])
        gate = _sigmoid(og_ref[:, hs].astype(F32))
        o_ref[:, hs] = (o / l * gate).astype(BF16)


def _fox_sample(q, kb, p, lf_col, lf_row, cache_k, cache_v, cache_lf, *, l_valid):
    bs, past = cache_k.shape[0], cache_k.shape[1]
    C = GDN_CHUNK
    n = q.shape[0]
    blk = lambda b: (b + 1, 0)
    return pl.pallas_call(
        functools.partial(_fox_sample_kernel, l_valid=l_valid),
        grid=(bs,),
        in_specs=[
            pl.BlockSpec((C, WIDTH), blk),
            pl.BlockSpec((C, WIDTH), blk),
            pl.BlockSpec((C, WIDTH), lambda b: (b + 1, PB_FV)),
            pl.BlockSpec((C, WIDTH), lambda b: (b + 1, PB_FOG)),
            pl.BlockSpec((C, LANES), blk),
            pl.BlockSpec((HEADS, C), lambda b: (0, b + 1)),
            pl.BlockSpec((1, past * HEADS, HEAD_DIM), lambda b: (b, 0, 0)),
            pl.BlockSpec((1, past * HEADS, HEAD_DIM), lambda b: (b, 0, 0)),
            pl.BlockSpec((1, past, HEADS), lambda b: (b, 0, 0)),
            pl.BlockSpec((1, HEADS, past), lambda b: (b, 0, 0)),
        ],
        out_specs=pl.BlockSpec((C, WIDTH), lambda b: (b, 0)),
        out_shape=jax.ShapeDtypeStruct((bs * C, WIDTH), BF16),
        compiler_params=_cparams(("arbitrary",)),
        name="fox_sample",
    )(q, kb, p, p, lf_col, lf_row, cache_k.reshape(bs, past * HEADS, HEAD_DIM),
      cache_v.reshape(bs, past * HEADS, HEAD_DIM), cache_lf, jnp.swapaxes(cache_lf, 1, 2))


def _outproj_kernel(og_ref, of_ref, x_ref, wo_ref, nw_ref, wr_ref, br_ref, cnt0_ref, xr_ref, rt_ref, cnt_ref,
                    cnt_scr):
    d = x_ref.shape[1]

    @pl.when(pl.program_id(0) == 0)
    def _():
        cnt_scr[...] = cnt0_ref[...]

    half = og_ref.shape[1]
    h = _dot(og_ref[...], wo_ref[0:half, :]) + _dot(of_ref[...], wo_ref[half:2 * half, :])
    x1 = x_ref[...] + h
    xr_ref[:, 0:d] = x1
    xn = x1 * lax.rsqrt(jnp.mean(x1 * x1, axis=-1, keepdims=True) + EPS) * nw_ref[...]

    x_hi = xn.astype(BF16)
    x_lo = (xn - x_hi.astype(F32)).astype(BF16)
    hi = _dot(x_hi, wr_ref[...])
    logits = (hi[:, 0:LANES] + hi[:, LANES:2 * LANES] + _dot(x_lo, wr_ref[:, 0:LANES])) + br_ref[...]

    tm = logits.shape[0]
    lane = _iota2((tm, LANES), 1).astype(F32)
    big = float(LANES)
    gl = jnp.where(lane < N_GROUPS, logits, NEG)
    gmax = jnp.max(gl, axis=-1, keepdims=True)
    gidx = jnp.min(jnp.where(gl == gmax, lane, big), axis=-1, keepdims=True)
    p_top = 1.0 / jnp.sum(jnp.exp(gl - gmax), axis=-1, keepdims=True)
    e = lane - RT_E0
    sel = (e >= 0) & (e < N_EXPERTS) & (jnp.floor(e * (1.0 / EXPERTS_PER_GROUP)) == gidx)
    el = jnp.where(sel, logits, NEG)
    v1 = jnp.max(el, axis=-1, keepdims=True)
    i1 = jnp.min(jnp.where(el == v1, lane, big), axis=-1, keepdims=True)
    el2 = jnp.where(lane == i1, NEG, el)
    v2 = jnp.max(el2, axis=-1, keepdims=True)
    i2 = jnp.min(jnp.where(el2 == v2, lane, big), axis=-1, keepdims=True)
    e2 = jnp.exp(v2 - v1)
    w1 = p_top / (1.0 + e2)
    w2 = p_top * e2 / (1.0 + e2)

    ex1 = i1 - RT_E0
    ex2 = i2 - RT_E0
    first = ex1 < ex2
    ea = jnp.where(first, ex1, ex2)
    eb = jnp.where(first, ex2, ex1)
    wa = jnp.where(first, w1, w2)
    wb = jnp.where(first, w2, w1)
    la = ea - gidx * EXPERTS_PER_GROUP
    lb = eb - gidx * EXPERTS_PER_GROUP
    cls = gidx * PAIRS_PER_GROUP + la * (2 * EXPERTS_PER_GROUP - 1 - la) * 0.5 + (lb - la - 1.0)

    onehot = lane == cls
    oh = jnp.where(onehot, 1.0, 0.0)
    strict_b = jnp.where(_iota2((tm, tm), 0) > _iota2((tm, tm), 1), 1.0, 0.0).astype(BF16)
    before = cnt_scr[...] + _dot(strict_b, oh.astype(BF16))
    rank = jnp.sum(jnp.where(onehot, before, 0.0), axis=-1, keepdims=True)
    cnt_scr[...] += jnp.sum(oh, axis=0, keepdims=True)
    cnt_ref[...] = cnt_scr[...]
    route = jnp.where(lane == RT_CLS, cls, jnp.where(lane == RT_RANK, rank, jnp.where(
        lane == RT_WA, wa, jnp.where(lane == RT_WB, wb, 0.0))))
    rt_ref[...] = route
    xr_ref[:, d:d + LANES] = route


def _outproj(og, of, x, wo, norm_w, wr, br, cnt0):
    n, d = x.shape
    tm = _pick(n, (512, 384, 256, 128))
    rowblk = lambda i: (i, 0)
    return pl.pallas_call(
        _outproj_kernel,
        grid=(n // tm,),
        in_specs=[
            pl.BlockSpec((tm, WIDTH), rowblk),
            pl.BlockSpec((tm, WIDTH), rowblk),
            pl.BlockSpec((tm, d), rowblk),
            pl.BlockSpec((2 * WIDTH, d), lambda i: (0, 0)),
            pl.BlockSpec((1, d), lambda i: (0, 0)),
            pl.BlockSpec((d, 2 * LANES), lambda i: (0, 0)),
            pl.BlockSpec((1, LANES), lambda i: (0, 0)),
            pl.BlockSpec((1, LANES), lambda i: (0, 0)),
        ],
        out_specs=[
            pl.BlockSpec((tm, d + LANES), rowblk),
            pl.BlockSpec((tm, LANES), rowblk),
            pl.BlockSpec((1, LANES), lambda i: (0, 0)),
        ],
        out_shape=[
            jax.ShapeDtypeStruct((n, d + LANES), F32),
            jax.ShapeDtypeStruct((n, LANES), F32),
            jax.ShapeDtypeStruct((1, LANES), F32),
        ],
        scratch_shapes=[pltpu.VMEM((1, LANES), F32)],
        compiler_params=_cparams(("arbitrary",)),
        name="outproj_router",
    )(og, of, x, wo, norm_w, wr, br, cnt0)


def _pos_rows(tm):
    return -(-(-(-tm // LANES)) // 8) * 8


def _tile_positions(pos, tm):
    nt = pos.shape[0] // tm
    rows = _pos_rows(tm)
    p = jnp.pad(pos.reshape(nt, tm), ((0, 0), (0, rows * LANES - tm)))
    return p.reshape(nt * rows, LANES)


def _row_copy_loops(tm, pos_smem, make_copy):
    def start_row(r, carry):
        base = pl.multiple_of(r * LANES, LANES)
        for c in range(LANES):
            make_copy(base + c, pos_smem[r, c]).start(priority=c % 2)
        return carry

    def wait(t, carry):
        make_copy(0, 0).wait()
        return carry

    lax.fori_loop(0, tm // LANES, start_row, 0)
    lax.fori_loop(0, tm, wait, 0, unroll=8)


def _dispatch_kernel(ends_ref, xr_ref, pos_hbm, *rest, zero_fill):
    if zero_fill:
        xs_hbm, pos_smem, zbuf, sem_idx, sem_fill, sem_rows = rest
    else:
        _, xs_hbm, pos_smem, sem_idx, sem_rows = rest
    i = pl.program_id(0)
    tm = xr_ref.shape[0]
    rows = pos_smem.shape[0]
    idx_cp = pltpu.make_async_copy(pos_hbm.at[pl.ds(pl.multiple_of(i * rows, 8), rows)], pos_smem, sem_idx)
    idx_cp.start()

    if zero_fill:
        @pl.when(i == 0)
        def _():
            zbuf[...] = jnp.zeros(zbuf.shape, F32)

            def fill(start):
                def body(c, carry):
                    lo = jnp.where(c == 0, 0, ends_ref[jnp.maximum(c - 1, 0)])
                    hi = ends_ref[c]

                    @pl.when(hi > lo)
                    def _():
                        cp = pltpu.make_async_copy(
                            zbuf, xs_hbm.at[pl.ds(pl.multiple_of(hi - MOE_TS, MOE_TS), MOE_TS)], sem_fill)
                        if start:
                            cp.start()
                        else:
                            cp.wait()
                    return carry
                lax.fori_loop(0, N_CLASSES, body, 0)

            fill(True)
            fill(False)

            last = ends_ref[N_CLASSES - 1]

            @pl.when((last // MOE_TS) % 2 == 1)
            def _():
                cp = pltpu.make_async_copy(zbuf, xs_hbm.at[pl.ds(pl.multiple_of(last, MOE_TS), MOE_TS)], sem_fill)
                cp.start()
                cp.wait()

    idx_cp.wait()
    _row_copy_loops(tm, pos_smem, lambda t, p: pltpu.make_async_copy(
        xr_ref.at[pl.ds(t, 1)], xs_hbm.at[pl.ds(p, 1)], sem_rows))


def _dispatch(ends, xr, pos, xs, *, n_sorted):
    n, dw = xr.shape
    tm = _pick(n, (2048, 1024, 1152, 512, 384, 256, 128))
    rows = _pos_rows(tm)
    zero_fill = xs is None
    any_spec = pl.BlockSpec(memory_space=pl.ANY)
    in_specs = [pl.BlockSpec((tm, dw), lambda i, ends_r: (i, 0)), any_spec]
    args = [xr, _tile_positions(pos, tm)]
    scratch = [pltpu.SMEM((rows, LANES), jnp.int32)]
    if zero_fill:
        scratch += [pltpu.VMEM((MOE_TS, dw), F32), pltpu.SemaphoreType.DMA, pltpu.SemaphoreType.DMA,
                    pltpu.SemaphoreType.DMA]
        aliases = {}
    else:
        in_specs.append(any_spec)
        args.append(xs)
        scratch += [pltpu.SemaphoreType.DMA, pltpu.SemaphoreType.DMA]
        aliases = {3: 0}
    return pl.pallas_call(
        functools.partial(_dispatch_kernel, zero_fill=zero_fill),
        grid_spec=pltpu.PrefetchScalarGridSpec(
            num_scalar_prefetch=1, grid=(n // tm,), in_specs=in_specs, out_specs=any_spec,
            scratch_shapes=scratch),
        out_shape=jax.ShapeDtypeStruct((n_sorted, dw), F32),
        input_output_aliases=aliases,
        compiler_params=pltpu.CompilerParams(dimension_semantics=("arbitrary",), vmem_limit_bytes=VMEM_LIMIT,
                                             has_side_effects=True),
        name="moe_dispatch",
    )(ends, *args)


def _ffn_kernel(ta_ref, tb_ref, nv_ref, xs_ref, *refs):
    w_refs, (nffn_ref, nfin_ref, ys_ref) = refs[:12], refs[12:]

    @pl.when(pl.program_id(0) < nv_ref[0])
    def _():
        d = ys_ref.shape[1]
        halves = range(2)
        rows = [slice(i * MOE_TS, (i + 1) * MOE_TS) for i in halves]
        w1a, w3a, w2a, w1b, w3b, w2b = ([w_refs[6 * i + j] for i in halves] for j in range(6))
        x1 = [xs_ref[rows[i], 0:d] for i in halves]
        route = [xs_ref[rows[i], d:d + LANES] for i in halves]
        xn = [(x1[i] * lax.rsqrt(jnp.mean(x1[i] * x1[i], axis=-1, keepdims=True) + EPS)
               * nffn_ref[...]).astype(BF16) for i in halves]
        ga = [_dot(xn[i], w1a[i][0]) for i in halves]
        gb = [_dot(xn[i], w1b[i][0]) for i in halves]
        ua = [_dot(xn[i], w3a[i][0]) for i in halves]
        ub = [_dot(xn[i], w3b[i][0]) for i in halves]
        ha = [(_silu(ga[i]) * ua[i] * route[i][:, RT_WA:RT_WA + 1]).astype(BF16) for i in halves]
        hb = [(_silu(gb[i]) * ub[i] * route[i][:, RT_WB:RT_WB + 1]).astype(BF16) for i in halves]
        x2 = [x1[i] + _dot(ha[i], w2a[i][0]) + _dot(hb[i], w2b[i][0]) for i in halves]
        for i in halves:
            ys_ref[rows[i], :] = (x2[i] * lax.rsqrt(jnp.mean(x2[i] * x2[i], axis=-1, keepdims=True) + EPS)
                                  * nfin_ref[...])


def _ffn(tile_a, tile_b, n_valid, xs, w1, w3, w2, nffn, nfin):
    ns, dw = xs.shape
    d = dw - LANES
    de = w1.shape[2]
    row = lambda t, ta, tb, nv: (jnp.minimum(t, nv[0] - 1), 0)
    const = lambda t, ta, tb, nv: (0, 0)
    w_specs = []
    for i in range(2):
        wa = lambda t, ta, tb, nv, i=i: (ta[2 * t + i], 0, 0)
        wb = lambda t, ta, tb, nv, i=i: (tb[2 * t + i], 0, 0)
        w_specs += [pl.BlockSpec((1, d, de), wa), pl.BlockSpec((1, d, de), wa), pl.BlockSpec((1, de, d), wa),
                    pl.BlockSpec((1, d, de), wb), pl.BlockSpec((1, d, de), wb), pl.BlockSpec((1, de, d), wb)]
    return pl.pallas_call(
        _ffn_kernel,
        grid_spec=pltpu.PrefetchScalarGridSpec(
            num_scalar_prefetch=3,
            grid=(ns // (2 * MOE_TS),),
            in_specs=[pl.BlockSpec((2 * MOE_TS, dw), row)] + w_specs
            + [pl.BlockSpec((1, d), const), pl.BlockSpec((1, d), const)],
            out_specs=pl.BlockSpec((2 * MOE_TS, d), row),
        ),
        out_shape=jax.ShapeDtypeStruct((ns, d), F32),
        compiler_params=_cparams(("arbitrary",)),
        name="moe_ffn",
    )(tile_a, tile_b, n_valid, xs, *([w1, w3, w2, w1, w3, w2] * 2), nffn, nfin)


def _unsort_kernel(pos_hbm, ys_hbm, y_ref, pos_smem, sem_idx, sem_rows):
    i = pl.program_id(0)
    tm = y_ref.shape[0]
    rows = pos_smem.shape[0]
    idx_cp = pltpu.make_async_copy(pos_hbm.at[pl.ds(pl.multiple_of(i * rows, 8), rows)], pos_smem, sem_idx)
    idx_cp.start()
    idx_cp.wait()
    _row_copy_loops(tm, pos_smem, lambda t, p: pltpu.make_async_copy(
        ys_hbm.at[pl.ds(p, 1)], y_ref.at[pl.ds(t, 1)], sem_rows))


def _unsort(ys, pos):
    n = pos.shape[0]
    d = ys.shape[1]
    tm = _pick(n, (2048, 1024, 1152, 512, 384, 256, 128))
    any_spec = pl.BlockSpec(memory_space=pl.ANY)
    return pl.pallas_call(
        _unsort_kernel,
        grid=(n // tm,),
        in_specs=[any_spec, any_spec],
        out_specs=pl.BlockSpec((tm, d), lambda i: (i, 0)),
        out_shape=jax.ShapeDtypeStruct((n, d), F32),
        scratch_shapes=[pltpu.SMEM((_pos_rows(tm), LANES), jnp.int32), pltpu.SemaphoreType.DMA,
                        pltpu.SemaphoreType.DMA],
        compiler_params=_cparams(("arbitrary",)),
        name="moe_unsort",
    )(_tile_positions(pos, tm), ys)


def _pair_tables():
    a, b = [], []
    for g in range(N_GROUPS):
        for la in range(EXPERTS_PER_GROUP):
            for lb in range(la + 1, EXPERTS_PER_GROUP):
                a.append(g * EXPERTS_PER_GROUP + la)
                b.append(g * EXPERTS_PER_GROUP + lb)
    return jnp.asarray(a, jnp.int32), jnp.asarray(b, jnp.int32)


def _moe(xr_list, rt_list, cnt, w1, w3, w2, nffn, nfin):
    n_total = sum(x.shape[0] for x in xr_list)
    n_sorted = (-(-n_total // MOE_TS) + N_CLASSES + 2) // 2 * 2 * MOE_TS
    counts = cnt[0, :N_CLASSES].astype(jnp.int32)
    padded = (counts + MOE_TS - 1) // MOE_TS * MOE_TS
    ends = jnp.cumsum(padded)
    offs = ends - padded
    classes = jnp.arange(N_CLASSES, dtype=jnp.int32)

    def position(rt):
        cls = rt[:, RT_CLS].astype(jnp.int32)
        return jnp.sum(jnp.where(cls[:, None] == classes, offs, 0), axis=1) + rt[:, RT_RANK].astype(jnp.int32)

    pos_list = [position(rt) for rt in rt_list]
    tile_start = jnp.arange(n_sorted // MOE_TS, dtype=jnp.int32) * MOE_TS
    tile_cls = jnp.minimum(jnp.sum((ends <= tile_start[:, None]).astype(jnp.int32), axis=1), N_CLASSES - 1)
    pair_a, pair_b = _pair_tables()
    n_valid = ((ends[N_CLASSES - 1] // MOE_TS + 1) // 2).reshape(1)
    xs = None
    for xr, pos in zip(xr_list, pos_list):
        xs = _dispatch(ends, xr, pos, xs, n_sorted=n_sorted)
    tile_is = tile_cls[:, None] == classes
    tile_a = jnp.sum(jnp.where(tile_is, pair_a, 0), axis=1)
    tile_b = jnp.sum(jnp.where(tile_is, pair_b, 0), axis=1)
    ys = _ffn(tile_a, tile_b, n_valid, xs, w1, w3, w2, nffn, nfin)
    return [_unsort(ys, pos) for pos in pos_list]


def kernel(x_prompt, x_sample, cache_fox_k, cache_fox_v, cache_fox_logf, state_gdn, state_gdn_conv, meta_tokens, norm_mix_w, w_in, gdn_conv_w, gdn_A_log, gdn_dt_bias, gdn_norm_w, fox_q_norm_w, fox_k_norm_w, fox_f_bias, w_out, norm_ffn_w, w_router_group, b_router_group, w_router_expert, b_router_expert, w_gate, w_up, w_down, norm_final_w):
    B, S, D = x_prompt.shape
    BS, LS, _ = x_sample.shape
    C = GDN_CHUNK
    assert w_in.shape[0] == 1, "single-layer step only"
    assert S % C == 0 and LS <= C and N_META <= C and meta_tokens.shape[0] == N_META

    wi = w_in[0]
    o = 0
    parts = {}
    for name, size in (("g_qkv", 3 * WIDTH), ("g_z", WIDTH), ("g_a", HEADS), ("g_b", HEADS),
                       ("f_qkv", 3 * WIDTH), ("f_og", WIDTH), ("f_f", HEADS)):
        parts[name] = wi[:, o:o + size]
        o += size
    wbig = jnp.concatenate([parts[k].astype(BF16) for k in ("g_qkv", "f_qkv", "g_z", "f_og")], axis=1)
    wsm_cols = jnp.concatenate([parts["f_f"], parts["g_a"], parts["g_b"]], axis=1)
    wsm = jnp.pad(wsm_cols, ((0, 0), (0, LANES - 3 * HEADS))).astype(BF16)
    wo = w_out[0].astype(BF16)
    wr32 = jnp.pad(jnp.concatenate([w_router_group[0], w_router_expert[0]], axis=1),
                   ((0, 0), (0, LANES - N_GROUPS - N_EXPERTS)))
    wr_hi = wr32.astype(BF16)
    wr = jnp.concatenate([wr_hi, (wr32 - wr_hi.astype(F32)).astype(BF16)], axis=1)
    br = jnp.pad(jnp.concatenate([b_router_group[0], b_router_expert[0]]),
                 (0, LANES - N_GROUPS - N_EXPERTS)).reshape(1, LANES)
    w1 = w_gate[0].astype(BF16)
    w3 = w_up[0].astype(BF16)
    w2 = w_down[0].astype(BF16)
    nmix = norm_mix_w[0].reshape(1, D)
    nffn = norm_ffn_w[0].reshape(1, D)
    nfin = norm_final_w.reshape(1, D)

    x_small = jnp.concatenate([
        jnp.pad(meta_tokens.astype(F32), ((0, C - N_META), (0, 0))),
        jnp.pad(x_sample, ((0, 0), (0, C - LS), (0, 0))).reshape(BS * C, D)], axis=0)
    xp = x_prompt.reshape(B * S, D)

    p_s, psm_s, psmt_s = _inproj(x_small, nmix, wbig, wsm)
    p_p, psm_p, psmt_p = _inproj(xp, nmix, wbig, wsm)

    s0_s = jnp.concatenate([jnp.zeros((1,) + state_gdn.shape[2:], F32), state_gdn[0]], axis=0)
    cb_s = jnp.concatenate([jnp.zeros((1,) + state_gdn_conv.shape[2:], F32), state_gdn_conv[0]], axis=0)
    gdn_args = (gdn_conv_w[0], gdn_A_log[0], gdn_dt_bias[0], gdn_norm_w[0])
    og_s, st_s, cv_s = _gdn(p_s, psm_s, psmt_s, s0_s, cb_s, *gdn_args, n_seq=1 + BS, l_valid=LS,
                            bcast_state=False)
    og_p, st_p, cv_p = _gdn(p_p, psm_p, psmt_p, st_s[0:1], cv_s[0:1], *gdn_args, n_seq=B, l_valid=C,
                            bcast_state=True)

    fox_args = (fox_q_norm_w[0], fox_k_norm_w[0], fox_f_bias[0])
    zc = jnp.zeros((1, 1, LANES), F32)
    zr = jnp.zeros((1, HEADS, 1), F32)
    k32_s, v32_s, lfc_s, lfr_s, fc_s, fr_s, qt_s, kcat_s, vt_s, q_s, kb_s = _foxprep(
        p_s, psm_s, psmt_s, zc, zr, *fox_args, n_seq=1 + BS, bcast_carry=True, row_major=True)
    c0c = fc_s[N_META - 1:N_META, :].reshape(1, 1, LANES)
    c0r = fr_s[:, N_META - 1:N_META].reshape(1, HEADS, 1)
    k32_p, v32_p, lfc_p, lfr_p, fc_p, fr_p, qt_p, kcat_p, vt_p = _foxprep(
        p_p, psm_p, psmt_p, c0c, c0r, *fox_args, n_seq=B, bcast_carry=True, row_major=False, lead=N_META)
    of_p = _fox_prompt(qt_p, kcat_p, vt_p, p_p, kcat_s, vt_s, n_seq=B)
    of_s = _fox_sample(q_s, kb_s, p_s, lfc_s, lfr_s, cache_fox_k[0], cache_fox_v[0], cache_fox_logf[0],
                       l_valid=LS)

    xr_p, rt_p, cnt_p = _outproj(og_p, of_p, xp, wo, nffn, wr, br, jnp.zeros((1, LANES), F32))
    xr_s, rt_s, cnt = _outproj(og_s[C:], of_s, x_small[C:], wo, nffn, wr, br, cnt_p)
    y_p, y_s = _moe([xr_p, xr_s], [rt_p, rt_s], cnt, w1, w3, w2, nffn, nfin)

    def prompt_kv(tab_p, tab_s):
        tab = _fill_lead(tab_p, tab_s[:N_META * HEADS], n_seq=B)
        return tab.reshape(1, B, N_META + S, HEADS, HEAD_DIM)

    def sample_kv(tab_s):
        return tab_s.reshape(1 + BS, C, HEADS, HEAD_DIM)[1:, :LS][None]

    y_prompt = y_p.reshape(B, S, D)
    y_sample = y_s.reshape(BS, C, D)[:, :LS]
    fk_p = prompt_kv(k32_p, k32_s)
    fv_p = prompt_kv(v32_p, v32_s)
    lf_p = _assemble_logf(lfc_s, lfc_p, n_seq=B)[None]
    fk_s = sample_kv(k32_s)
    fv_s = sample_kv(v32_s)
    lf_s = lfc_s[C:, SM_F:SM_F + HEADS].reshape(BS, C, HEADS)[:, :LS][None]
    return (y_prompt, y_sample, fk_p, fv_p, lf_p, st_p[None], cv_p[None],
            fk_s, fv_s, lf_s, st_s[1:][None], cv_s[1:][None])
```
